```python
import math
import jax, jax.numpy as jnp
from jax import lax
import numpy as np

D_MODEL = 2048
BATCH = 16
SEQ = 2048
DEPTH = 1
DEC_BATCH = 4
DEC_SEQ = 4096
PAST_LEN = 128

N_META = 16
CHUNK = 128
ML_HEADS = 8
ML_DV = D_MODEL // ML_HEADS
ML_DQK = ML_DV // 2
AT_DH = 128
AT_HEADS = D_MODEL // AT_DH
AT_KV_HEADS = AT_HEADS // 4
WINDOW = 128
REL_BUCKETS = 32
REL_MAX_DIST = 128
N_EXPERTS = 16
EXPERT_FF = D_MODEL
CAPACITY_FACTOR = 2
ALPHA = (2.0 * DEPTH) ** 0.25
BETA = (8.0 * DEPTH) ** -0.25
LN_EPS = 1e-5
M_INIT = -1e30

IN_SIZES = (ML_HEADS * ML_DQK, ML_HEADS * ML_DQK, ML_HEADS * ML_DV, ML_HEADS * ML_DV, 4 * ML_HEADS,
            AT_HEADS * AT_DH, AT_KV_HEADS * AT_DH, AT_KV_HEADS * AT_DH, D_MODEL, D_MODEL)
IN_WIDTH = sum(IN_SIZES)
VALUE_BLOCKS = (2, 7)

kernel_name = "hybrid_mlstm_swa_ecmoe_encoder"


def layer_norm(x, g, b):
    xf = x.astype(jnp.float32)
    mu = xf.mean(-1, keepdims=True)
    var = jnp.mean(jnp.square(xf - mu), -1, keepdims=True)
    return ((xf - mu) * lax.rsqrt(var + LN_EPS) * g + b).astype(x.dtype)


def split_columns(proj):
    out, start = [], 0
    for size in IN_SIZES:
        out.append(proj[..., start:start + size])
        start += size
    return out


def t5_bucket(rel):
    half = REL_BUCKETS // 2
    max_exact = half // 2
    n = jnp.abs(rel)
    nf = jnp.maximum(n, 1).astype(jnp.float32)
    large = max_exact + (jnp.log(nf / max_exact) / math.log(REL_MAX_DIST / max_exact)
                         * (half - max_exact)).astype(jnp.int32)
    large = jnp.minimum(large, half - 1)
    return jnp.where(rel > 0, half, 0) + jnp.where(n < max_exact, n, large)


def mlstm_chunk(state, inp):
    c_prev, n_prev, m_prev = state
    q, k, v, ig, lf = inp
    L = q.shape[2]
    b = jnp.cumsum(lf, axis=-1)
    causal = jnp.tril(jnp.ones((L, L), bool))
    d = jnp.where(causal, b[..., :, None] - b[..., None, :] + ig[..., None, :], -jnp.inf)
    inter = b + m_prev[..., None]
    m_comb = jnp.maximum(inter, d.max(-1))
    w = jnp.exp(d - m_comb[..., None])
    w_inter = jnp.exp(inter - m_comb)
    s = jnp.einsum('bhtd,bhsd->bhts', q, k) * w
    num = jnp.einsum('bhts,bhsv->bhtv', s, v) + w_inter[..., None] * jnp.einsum('bhvd,bhtd->bhtv', c_prev, q)
    den = s.sum(-1) + w_inter * jnp.einsum('bhd,bhtd->bht', n_prev, q)
    h = num / jnp.maximum(jnp.abs(den), jnp.exp(-m_comb))[..., None]
    dec = b[..., -1:] - b + ig
    inter_end = b[..., -1] + m_prev
    m_new = jnp.maximum(inter_end, dec.max(-1))
    w_end = jnp.exp(dec - m_new[..., None])
    w_prev = jnp.exp(inter_end - m_new)
    c_new = w_prev[..., None, None] * c_prev + jnp.einsum('bhs,bhsv,bhsd->bhvd', w_end, v, k)
    n_new = w_prev[..., None] * n_prev + jnp.einsum('bhs,bhsd->bhd', w_end, k)
    return (c_new, n_new, m_new), h


def split_chunks(a):
    B, H, S = a.shape[:3]
    a = a.reshape((B, H, S // CHUNK, CHUNK) + a.shape[3:])
    return jnp.moveaxis(a, 2, 0)


def merge_chunks(a):
    a = jnp.moveaxis(a, 0, 2)
    return a.reshape(a.shape[:2] + (-1,) + a.shape[4:])


def init_state(B):
    return (jnp.zeros((B, ML_HEADS, ML_DV, ML_DQK), jnp.float32),
            jnp.zeros((B, ML_HEADS, ML_DQK), jnp.float32),
            jnp.full((B, ML_HEADS), M_INIT, jnp.float32))


def mlstm_forward_dir(q, k, v, ig, lf):
    arrs = (q, k, v, ig, lf)
    meta = tuple(a[:, :, :N_META] for a in arrs)
    real = tuple(split_chunks(a[:, :, N_META:]) for a in arrs)
    state, h_meta = mlstm_chunk(init_state(q.shape[0]), meta)
    _, h_real = lax.scan(mlstm_chunk, state, real)
    return jnp.concatenate([h_meta, merge_chunks(h_real)], axis=2)


def mlstm_backward_dir(q, k, v, ig, lf):
    arrs = (q, k, v, ig, lf)
    meta = tuple(jnp.flip(a[:, :, :N_META], axis=2) for a in arrs)
    real = tuple(split_chunks(jnp.flip(a[:, :, N_META:], axis=2)) for a in arrs)
    state, h_real = lax.scan(mlstm_chunk, init_state(q.shape[0]), real)
    _, h_meta = mlstm_chunk(state, meta)
    return jnp.concatenate([jnp.flip(h_meta, axis=2), jnp.flip(merge_chunks(h_real), axis=2)], axis=2)


def mlstm_mixer(q, k, v, o, gates, norm_g):
    B, L = q.shape[:2]
    dt = q.dtype
    to_heads = lambda a, d: a.reshape(B, L, ML_HEADS, d).transpose(0, 2, 1, 3).astype(jnp.float32)
    qh = to_heads(q, ML_DQK) * (ML_DQK ** -0.5)
    kh = to_heads(k, ML_DQK)
    vh = to_heads(v, ML_DV)
    ig_f, f_f, ig_b, f_b = jnp.split(gates.transpose(0, 2, 1), 4, axis=1)
    h = (mlstm_forward_dir(qh, kh, vh, ig_f, jax.nn.log_sigmoid(f_f))
         + mlstm_backward_dir(qh, kh, vh, ig_b, jax.nn.log_sigmoid(f_b)))
    mu = h.mean(-1, keepdims=True)
    var = jnp.mean(jnp.square(h - mu), -1, keepdims=True)
    h = (h - mu) * lax.rsqrt(var + LN_EPS)
    h = h.transpose(0, 2, 1, 3).reshape(B, L, ML_HEADS * ML_DV) * norm_g
    return (h * jax.nn.sigmoid(o.astype(jnp.float32))).astype(dt)


def sink_probs(scores, mask, sink):
    s = sink.astype(jnp.float32).reshape(AT_KV_HEADS, -1, 1, 1)
    scores = jnp.where(mask, scores, -jnp.inf)
    m = jnp.maximum(scores.max(-1, keepdims=True), s)
    p = jnp.exp(scores - m)
    return p / (p.sum(-1, keepdims=True) + jnp.exp(s - m))


def windowed_attention(q, k, v, rel_bias, sink):
    B, L = q.shape[:2]
    S = L - N_META
    nb = S // CHUNK
    G = AT_HEADS // AT_KV_HEADS
    q = q.reshape(B, L, AT_KV_HEADS, G, AT_DH) * (AT_DH ** -0.5)
    k = k.reshape(B, L, AT_KV_HEADS, AT_DH)
    v = v.reshape(B, L, AT_KV_HEADS, AT_DH)
    qm, qr = q[:, :N_META], q[:, N_META:]
    km, kr = k[:, :N_META], k[:, N_META:]
    vm, vr = v[:, :N_META], v[:, N_META:]

    def neighbours(a):
        a = a.reshape(B, nb, CHUNK, AT_KV_HEADS, AT_DH)
        a = jnp.pad(a, ((0, 0), (1, 1), (0, 0), (0, 0), (0, 0)))
        return jnp.concatenate([a[:, :-2], a[:, 1:-1], a[:, 2:]], axis=2)
    meta_b = lambda a: jnp.broadcast_to(a[:, None], (B, nb, N_META, AT_KV_HEADS, AT_DH))
    kb = jnp.concatenate([meta_b(km), neighbours(kr)], axis=2)
    vb = jnp.concatenate([meta_b(vm), neighbours(vr)], axis=2)
    qb = qr.reshape(B, nb, CHUNK, AT_KV_HEADS, G, AT_DH)

    blk = jnp.arange(nb)
    q_pos = N_META + blk[:, None] * CHUNK + jnp.arange(CHUNK)[None]
    nb_pos = N_META + (blk[:, None] - 1) * CHUNK + jnp.arange(3 * CHUNK)[None]
    k_pos = jnp.concatenate([jnp.broadcast_to(jnp.arange(N_META)[None], (nb, N_META)), nb_pos], axis=1)
    is_meta = jnp.concatenate([jnp.ones((nb, N_META), bool), jnp.zeros((nb, 3 * CHUNK), bool)], axis=1)
    valid = is_meta | ((k_pos >= N_META) & (k_pos < L))
    rel = k_pos[:, None, :] - q_pos[:, :, None]
    mask = valid[:, None, :] & ((jnp.abs(rel) <= WINDOW) | is_meta[:, None, :])
    bias = rel_bias[t5_bucket(rel)]
    bias = bias.reshape(nb, CHUNK, -1, AT_KV_HEADS, G).transpose(0, 3, 4, 1, 2)
    scores = jnp.einsum('bnqhgd,bnkhd->bnhgqk', qb, kb).astype(jnp.float32) + bias[None]
    p = sink_probs(scores, mask[None, :, None, None], sink)
    out_r = jnp.einsum('bnhgqk,bnkhd->bnqhgd', p.astype(vb.dtype), vb).reshape(B, S, AT_HEADS * AT_DH)

    km2 = jnp.concatenate([km, kr[:, :WINDOW]], axis=1)
    vm2 = jnp.concatenate([vm, vr[:, :WINDOW]], axis=1)
    mq_pos = jnp.arange(N_META)
    mk_pos = jnp.arange(N_META + WINDOW)
    rel_m = mk_pos[None] - mq_pos[:, None]
    mask_m = (jnp.abs(rel_m) <= WINDOW) | (mk_pos < N_META)[None]
    bias_m = rel_bias[t5_bucket(rel_m)].reshape(N_META, -1, AT_KV_HEADS, G).transpose(2, 3, 0, 1)
    scores_m = jnp.einsum('bqhgd,bkhd->bhgqk', qm, km2).astype(jnp.float32) + bias_m[None]
    p_m = sink_probs(scores_m, mask_m, sink)
    out_m = jnp.einsum('bhgqk,bkhd->bqhgd', p_m.astype(vm2.dtype), vm2).reshape(B, N_META, AT_HEADS * AT_DH)
    return jnp.concatenate([out_m, out_r], axis=1)


def expert_choice_moe(x, w_router, w_gate, w_up, w_down):
    T, D = x.shape
    cap = CAPACITY_FACTOR * T // N_EXPERTS
    affinity = jax.nn.softmax((x @ w_router).astype(jnp.float32), axis=-1)
    gate, idx = lax.top_k(affinity.T, cap)
    xs = x[idx]
    hid = jax.nn.silu(jnp.einsum('ecd,edf->ecf', xs, w_gate)) * jnp.einsum('ecd,edf->ecf', xs, w_up)
    ye = jnp.einsum('ecf,efd->ecd', hid, w_down) * gate[..., None].astype(x.dtype)
    return jnp.zeros_like(x).at[idx.reshape(-1)].add(ye.reshape(-1, D))


def encoder_layer(x, rel_bias, sink, w_in, b_gate, ml_norm_g, w_branch_a, w_branch_b, w_out,
                  ln1_g, ln1_b, w_router, w_gate, w_up, w_down, ln2_g, ln2_b):
    B, L, D = x.shape
    proj = x @ w_in
    mq, mk, mv, mo, mg, aq, ak, av, g_a, g_b = split_columns(proj)
    h_a = mlstm_mixer(mq, mk, mv, mo, mg.astype(jnp.float32) + b_gate, ml_norm_g)
    h_b = windowed_attention(aq, ak, av, rel_bias, sink)
    mix = jax.nn.sigmoid(g_a) * (h_a @ w_branch_a) + jax.nn.sigmoid(g_b) * (h_b @ w_branch_b)
    x = layer_norm(ALPHA * x + mix @ w_out, ln1_g, ln1_b)
    moe = expert_choice_moe(x.reshape(B * L, D), w_router, w_gate, w_up, w_down).reshape(B, L, D)
    return layer_norm(ALPHA * x + moe, ln2_g, ln2_b)


def setup_inputs(seed: int = 0) -> dict:
    key = jax.random.key(seed)
    ks = jax.random.split(key, 24)
    f32 = jnp.float32
    nrm = lambda k, shape, scale: jax.random.normal(k, shape, f32) * scale
    col_scale = jnp.concatenate([jnp.full((s,), BETA if i in VALUE_BLOCKS else 1.0, f32)
                                 for i, s in enumerate(IN_SIZES)])
    f_bias = jnp.linspace(3.0, 6.0, ML_HEADS, dtype=f32)
    zeros_h = jnp.zeros((ML_HEADS,), f32)
    gate_base = jnp.concatenate([zeros_h, f_bias, zeros_h, f_bias])
    mix_w = ML_HEADS * ML_DV
    att_w = AT_HEADS * AT_DH
    return {
        "x_prompt": nrm(ks[0], (BATCH, SEQ, D_MODEL), 1.0),
        "x_sample": nrm(ks[1], (DEC_BATCH, DEC_SEQ, D_MODEL), 1.0),
        "meta_tokens": nrm(ks[2], (N_META, D_MODEL), 1.0),
        "ln0_g": 1.0 + nrm(ks[3], (D_MODEL,), 0.01),
        "ln0_b": nrm(ks[4], (D_MODEL,), 0.01),
        "rel_bias": nrm(ks[5], (REL_BUCKETS, AT_HEADS), 0.1),
        "attn_sink": nrm(ks[6], (DEPTH, AT_HEADS), 0.5),
        "w_in": nrm(ks[7], (DEPTH, D_MODEL, IN_WIDTH), D_MODEL ** -0.5) * col_scale,
        "b_gate": gate_base + nrm(ks[8], (DEPTH, 4 * ML_HEADS), 0.01),
        "ml_norm_g": 1.0 + nrm(ks[9], (DEPTH, mix_w), 0.01),
        "w_branch_a": nrm(ks[10], (DEPTH, mix_w, D_MODEL), BETA * mix_w ** -0.5),
        "w_branch_b": nrm(ks[11], (DEPTH, att_w, D_MODEL), BETA * att_w ** -0.5),
        "w_out": nrm(ks[12], (DEPTH, D_MODEL, D_MODEL), BETA * D_MODEL ** -0.5),
        "ln1_g": 1.0 + nrm(ks[13], (DEPTH, D_MODEL), 0.01),
        "ln1_b": nrm(ks[14], (DEPTH, D_MODEL), 0.01),
        "w_router": nrm(ks[15], (DEPTH, D_MODEL, N_EXPERTS), D_MODEL ** -0.5),
        "w_gate": nrm(ks[16], (DEPTH, N_EXPERTS, D_MODEL, EXPERT_FF), D_MODEL ** -0.5),
        "w_up": nrm(ks[17], (DEPTH, N_EXPERTS, D_MODEL, EXPERT_FF), D_MODEL ** -0.5),
        "w_down": nrm(ks[18], (DEPTH, N_EXPERTS, EXPERT_FF, D_MODEL), BETA * EXPERT_FF ** -0.5),
        "ln2_g": 1.0 + nrm(ks[19], (DEPTH, D_MODEL), 0.01),
        "ln2_b": nrm(ks[20], (DEPTH, D_MODEL), 0.01),
    }


def reference(x_prompt, x_sample, meta_tokens, ln0_g, ln0_b, rel_bias, attn_sink, w_in, b_gate,
              ml_norm_g, w_branch_a, w_branch_b, w_out, ln1_g, ln1_b, w_router, w_gate, w_up,
              w_down, ln2_g, ln2_b):
    def trunk(x):
        B = x.shape[0]
        meta = jnp.broadcast_to(meta_tokens[None].astype(x.dtype), (B, N_META, D_MODEL))
        h = layer_norm(jnp.concatenate([meta, x], axis=1), ln0_g, ln0_b)
        for l in range(DEPTH):
            h = encoder_layer(h, rel_bias, attn_sink[l], w_in[l], b_gate[l], ml_norm_g[l],
                              w_branch_a[l], w_branch_b[l], w_out[l], ln1_g[l], ln1_b[l],
                              w_router[l], w_gate[l], w_up[l], w_down[l], ln2_g[l], ln2_b[l])
        return h[:, N_META:]

    y_prompt = trunk(x_prompt)
    y_sample = trunk(x_sample)
    return (y_prompt, y_sample)
```

```python
import functools
import math

import numpy as np
import jax
import jax.numpy as jnp
from jax import lax
from jax.experimental import pallas as pl
from jax.experimental.pallas import tpu as pltpu

F32 = jnp.float32
BF16 = jnp.bfloat16
I32 = jnp.int32

D_MODEL = 2048
N_META = 16
CHUNK = 128
ML_HEADS = 8
ML_DV = D_MODEL // ML_HEADS
ML_DQK = ML_DV // 2
AT_DH = 128
AT_HEADS = D_MODEL // AT_DH
AT_KV_HEADS = AT_HEADS // 4
AT_GROUP = AT_HEADS // AT_KV_HEADS
WINDOW = 128
REL_BUCKETS = 32
REL_MAX_DIST = 128
N_EXPERTS = 16
EXPERT_FF = D_MODEL
CAPACITY_FACTOR = 2
DEPTH = 1
ALPHA = (2.0 * DEPTH) ** 0.25
LN_EPS = 1e-5
M_INIT = -1e30
NEG = -1e30

LANES = 128
BF16_ROWS = 16
VMEM_LIMIT = 56 * 1024 * 1024

PM_MV, PM_MO, PM_AQ, PM_GA, PM_GB, PM_MQ, PM_AK, PM_AV = 0, 2048, 4096, 6144, 8192, 10240, 11264, 11776
PM_WIDTH = 12288
PM_TN = 1024
CT_W = ML_DV + LANES

NT_DIMS = (((1,), (1,)), ((), ()))


def _cparams(sem, vmem=VMEM_LIMIT):
    return pltpu.CompilerParams(dimension_semantics=sem, vmem_limit_bytes=vmem)


def _dot(a, b):
    return jnp.dot(a, b, preferred_element_type=F32)


def _dot_nt(a, b):
    return lax.dot_general(a, b, NT_DIMS, preferred_element_type=F32)


def _ln(x, g, b):
    mu = jnp.mean(x, axis=-1, keepdims=True)
    xc = x - mu
    var = jnp.mean(xc * xc, axis=-1, keepdims=True)
    return xc * lax.rsqrt(var + LN_EPS) * g + b


def _split3(x):
    hi = x.astype(BF16)
    r1 = x - hi.astype(F32)
    mid = r1.astype(BF16)
    lo = (r1 - mid.astype(F32)).astype(BF16)
    return hi, mid, lo


def _bias_kernel(rb_ref, bk_ref, o_ref):
    h = pl.program_id(0)
    bk = bk_ref[...]
    acc = jnp.full(bk.shape, NEG, F32)
    for b in range(REL_BUCKETS):
        acc = jnp.where(bk == b, rb_ref[b, h], acc)
    o_ref[0] = acc


def _bias_table(rel_bias, bucket):
    r, c = bucket.shape
    return pl.pallas_call(
        _bias_kernel,
        grid_spec=pltpu.PrefetchScalarGridSpec(
            num_scalar_prefetch=1, grid=(AT_HEADS,),
            in_specs=[pl.BlockSpec((r, c), lambda h, rb: (0, 0))],
            out_specs=pl.BlockSpec((1, r, c), lambda h, rb: (h, 0, 0))),
        out_shape=jax.ShapeDtypeStruct((AT_HEADS, r, c), F32),
        compiler_params=_cparams(("arbitrary",)),
        name="bias_table",
    )(rel_bias, bucket)


def _t5_bucket(rel):
    half = REL_BUCKETS // 2
    max_exact = half // 2
    n = jnp.abs(rel)
    nf = jnp.maximum(n, 1).astype(jnp.float32)
    large = max_exact + (jnp.log(nf / max_exact) / math.log(REL_MAX_DIST / max_exact)
                         * (half - max_exact)).astype(jnp.int32)
    large = jnp.minimum(large, half - 1)
    return jnp.where(rel > 0, half, 0) + jnp.where(n < max_exact, n, large)


def _t5_bucket_np(rel):
    half = REL_BUCKETS // 2
    max_exact = half // 2
    n = np.abs(rel)
    nf = np.maximum(n, 1).astype(np.float64)
    large = max_exact + (np.log(nf / max_exact) / math.log(REL_MAX_DIST / max_exact) * (half - max_exact)).astype(np.int64)
    large = np.minimum(large, half - 1)
    return np.where(rel > 0, half, 0) + np.where(n < max_exact, n, large)


def _bucket_tables(nb_max):
    i = np.arange(CHUNK)[:, None]
    m = np.arange(N_META)[None, :]
    ref_tab = _t5_bucket_np(m - (N_META + CHUNK + i))
    for j in range(1, nb_max):
        assert np.array_equal(_t5_bucket_np(m - (N_META + j * CHUNK + i)), ref_tab)
    qi = jnp.arange(CHUNK, dtype=I32)[:, None]
    c = jnp.arange(4 * CHUNK, dtype=I32)[None, :]
    rel_meta = c - (N_META + CHUNK + qi)
    rel_nb = (c - CHUNK) - CHUNK - qi
    vis_nb = (c >= CHUNK) & (jnp.abs(rel_nb) <= WINDOW)
    main = jnp.where(c < N_META, _t5_bucket(rel_meta), jnp.where(vis_nb, _t5_bucket(rel_nb), -1))
    c1 = jnp.arange(CHUNK, dtype=I32)[None, :]
    meta0 = jnp.where(c1 < N_META, _t5_bucket(c1 - (N_META + qi)), -1)
    mi = jnp.arange(N_META, dtype=I32)[:, None]
    c2 = jnp.arange(2 * CHUNK, dtype=I32)[None, :]
    rel_r = N_META + (c2 - CHUNK) - mi
    vis_r = (c2 >= CHUNK) & (jnp.abs(rel_r) <= WINDOW)
    mq = jnp.where(c2 < N_META, _t5_bucket(c2 - mi), jnp.where(vis_r, _t5_bucket(rel_r), -1))
    return main.astype(I32), meta0.astype(I32), mq.astype(I32)


def _proj_kernel(x_ref, g0_ref, b0_ref, w_ref, wkt_ref, wg_ref, wgt_ref, bg_ref, bgt_ref,
                 pm_ref, kt_ref, gc_ref, gt_ref, xs_ref):
    n = pl.program_id(1)

    @pl.when(n == 0)
    def _():
        xn = _ln(x_ref[...], g0_ref[...], b0_ref[...]).astype(BF16)
        xs_ref[...] = xn
        kt_ref[...] = _dot_nt(wkt_ref[...], xn).astype(BF16)
        gc_ref[...] = _dot(xn, wg_ref[...]) + bg_ref[...]
        gt_ref[...] = _dot_nt(wgt_ref[...], xn) + bgt_ref[...]

    y = _dot(xs_ref[...], w_ref[...])
    lo1, hi1 = PM_MO // PM_TN, PM_AQ // PM_TN
    lo2, hi2 = PM_GA // PM_TN, PM_MQ // PM_TN
    is_sig = ((n >= lo1) & (n < hi1)) | ((n >= lo2) & (n < hi2))

    @pl.when(is_sig)
    def _():
        pm_ref[...] = jax.nn.sigmoid(y).astype(BF16)

    @pl.when(jnp.logical_not(is_sig))
    def _():
        pm_ref[...] = y.astype(BF16)


def _ln_proj(x2, g0, b0, w_main, w_kt, w_g, w_gt, bg, bgt, tm):
    t = x2.shape[0]
    grid = (t // tm, PM_WIDTH // PM_TN)
    const = lambda i, n: (0, 0)
    return pl.pallas_call(
        _proj_kernel,
        grid=grid,
        in_specs=[
            pl.BlockSpec((tm, D_MODEL), lambda i, n: (i, 0)),
            pl.BlockSpec((1, D_MODEL), const),
            pl.BlockSpec((1, D_MODEL), const),
            pl.BlockSpec((D_MODEL, PM_TN), lambda i, n: (0, n)),
            pl.BlockSpec((ML_HEADS * ML_DQK, D_MODEL), const),
            pl.BlockSpec((D_MODEL, LANES), const),
            pl.BlockSpec((LANES, D_MODEL), const),
            pl.BlockSpec((1, LANES), const),
            pl.BlockSpec((LANES, 1), const),
        ],
        out_specs=[
            pl.BlockSpec((tm, PM_TN), lambda i, n: (i, n)),
            pl.BlockSpec((ML_HEADS * ML_DQK, tm), lambda i, n: (0, i)),
            pl.BlockSpec((tm, LANES), lambda i, n: (i, 0)),
            pl.BlockSpec((LANES, tm), lambda i, n: (0, i)),
        ],
        out_shape=[
            jax.ShapeDtypeStruct((t, PM_WIDTH), BF16),
            jax.ShapeDtypeStruct((ML_HEADS * ML_DQK, t), BF16),
            jax.ShapeDtypeStruct((t, LANES), F32),
            jax.ShapeDtypeStruct((LANES, t), F32),
        ],
        scratch_shapes=[pltpu.VMEM((tm, D_MODEL), BF16)],
        compiler_params=_cparams(("arbitrary", "arbitrary")),
        name="ln_proj",
    )(x2, g0, b0, w_main, w_kt, w_g, w_gt, bg, bgt)


def _mlstm_chunk(L, reverse, q_ref, kt_ref, v_ref, gc_ref, gt_ref, ct_ref, m_ref, emit):
    row = lax.broadcasted_iota(I32, (L, L), 0)
    col = lax.broadcasted_iota(I32, (L, L), 1)
    if reverse:
        tri = row <= col
        tri_t = row >= col
    else:
        tri = row >= col
        tri_t = row <= col
    mask = tri
    tri_b = jnp.where(tri, 1.0, 0.0).astype(BF16)
    tri_tb = jnp.where(tri_t, 1.0, 0.0).astype(BF16)
    gc = gc_ref[...]
    gt = gt_ref[...]
    lfc = jax.nn.log_sigmoid(gc)
    lft = jax.nn.log_sigmoid(gt)
    bc_all = sum(_dot(tri_b, p) for p in _split3(lfc))
    br_all = sum(_dot(p, tri_tb) for p in _split3(lft))
    end = 0 if reverse else L - 1
    ci0, cf0 = (2 * ML_HEADS, 3 * ML_HEADS) if reverse else (0, ML_HEADS)
    scale = ML_DQK ** -0.5
    for h in range(ML_HEADS):
        ci, cf = ci0 + h, cf0 + h
        q = q_ref[:, h * ML_DQK:(h + 1) * ML_DQK]
        kt = kt_ref[h * ML_DQK:(h + 1) * ML_DQK, :]
        v = v_ref[:, h * ML_DV:(h + 1) * ML_DV]
        igc = gc[:, ci:ci + 1]
        igr = gt[ci:ci + 1, :]
        bc = bc_all[:, cf:cf + 1]
        br = br_all[cf:cf + 1, :]
        btot = bc_all[end:end + 1, cf:cf + 1]
        m_prev = m_ref[h:h + 1, 0:1]
        ct_prev = ct_ref[h]
        d = jnp.where(mask, bc - br + igr, NEG)
        inter = bc + m_prev
        m_comb = jnp.maximum(inter, jnp.max(d, axis=1, keepdims=True))
        w = jnp.exp(d - m_comb)
        w_inter = jnp.exp(inter - m_comb)
        s = _dot(q, kt) * w * scale
        qc = _dot(q, ct_prev.astype(BF16)) * scale
        num = _dot(s.astype(BF16), v) + w_inter * qc[:, :ML_DV]
        den = jnp.sum(s, axis=1, keepdims=True) + w_inter * qc[:, ML_DV:ML_DV + 1]
        hv = num * (1.0 / jnp.maximum(jnp.abs(den), jnp.exp(-m_comb)))
        emit(h, hv)
        dec = btot - bc + igc
        inter_end = btot + m_prev
        m_new = jnp.maximum(inter_end, jnp.max(dec, axis=0, keepdims=True))
        w_end = jnp.exp(dec - m_new)
        w_prev = jnp.exp(inter_end - m_new)
        wv = jnp.concatenate([(w_end * v.astype(F32)).astype(BF16),
                              jnp.broadcast_to(w_end, (L, LANES)).astype(BF16)], axis=1)
        ct_ref[h] = w_prev * ct_prev + _dot(kt, wv)
        m_ref[h:h + 1, :] = jnp.broadcast_to(m_new, (1, LANES))


def _mlstm_kernel(*refs, reverse, nb, merge):
    (q_ref, kt_ref, v_ref, gc_ref, gt_ref, qm_ref, ktm_ref, vm_ref, gcm_ref, gtm_ref) = refs[:10]
    if merge:
        o_ref, om_ref, hb_ref, hbm_ref, ng_ref, out_ref, outm_ref, ct_ref, m_ref = refs[10:]
    else:
        out_ref, outm_ref, ct_ref, m_ref = refs[10:]
    j = pl.program_id(1)

    @pl.when(j == 0)
    def _():
        ct_ref[...] = jnp.zeros(ct_ref.shape, F32)
        m_ref[...] = jnp.full(m_ref.shape, M_INIT, F32)

    def make_emit(dst_ref, other_ref, gate_ref):
        def emit(h, hv):
            sl = slice(h * ML_DV, (h + 1) * ML_DV)
            if not merge:
                dst_ref[:, sl] = hv
                return
            hs = hv + other_ref[:, sl]
            mu = jnp.mean(hs, axis=-1, keepdims=True)
            hc = hs - mu
            var = jnp.mean(hc * hc, axis=-1, keepdims=True)
            hn = hc * lax.rsqrt(var + LN_EPS) * ng_ref[:, sl]
            dst_ref[:, sl] = (hn * gate_ref[:, sl].astype(F32)).astype(dst_ref.dtype)
        return emit

    is_meta = (j == nb) if reverse else (j == 0)

    @pl.when(is_meta)
    def _():
        emit = make_emit(outm_ref, hbm_ref if merge else None, om_ref if merge else None)
        _mlstm_chunk(N_META, reverse, qm_ref, ktm_ref, vm_ref, gcm_ref, gtm_ref, ct_ref, m_ref, emit)

    @pl.when(jnp.logical_not(is_meta))
    def _():
        emit = make_emit(out_ref, hb_ref if merge else None, o_ref if merge else None)
        _mlstm_chunk(CHUNK, reverse, q_ref, kt_ref, v_ref, gc_ref, gt_ref, ct_ref, m_ref, emit)


def _mlstm(pm, kt, gc, gt, pm_m, kt_m, gc_m, gt_m, batch, nb, *, reverse, merged=None):
    t = pm.shape[0]
    if reverse:
        rblk = lambda b, j: b * nb + jnp.maximum(nb - 1 - j, 0)
    else:
        rblk = lambda b, j: b * nb + jnp.maximum(j - 1, 0)
    const = lambda b, j: (0, 0)
    in_specs = [
        pl.BlockSpec((CHUNK, ML_HEADS * ML_DQK), lambda b, j: (rblk(b, j), PM_MQ // (ML_HEADS * ML_DQK))),
        pl.BlockSpec((ML_HEADS * ML_DQK, CHUNK), lambda b, j: (0, rblk(b, j))),
        pl.BlockSpec((CHUNK, D_MODEL), lambda b, j: (rblk(b, j), PM_MV // D_MODEL)),
        pl.BlockSpec((CHUNK, LANES), lambda b, j: (rblk(b, j), 0)),
        pl.BlockSpec((LANES, CHUNK), lambda b, j: (0, rblk(b, j))),
        pl.BlockSpec((N_META, ML_HEADS * ML_DQK), lambda b, j: (0, PM_MQ // (ML_HEADS * ML_DQK))),
        pl.BlockSpec((ML_HEADS * ML_DQK, N_META), const),
        pl.BlockSpec((N_META, D_MODEL), lambda b, j: (0, PM_MV // D_MODEL)),
        pl.BlockSpec((N_META, LANES), const),
        pl.BlockSpec((LANES, N_META), const),
    ]
    args = [pm, kt, pm, gc, gt, pm_m, kt_m, pm_m, gc_m, gt_m]
    merge = merged is not None
    if merge:
        hb, hb_m, ng = merged
        in_specs += [
            pl.BlockSpec((CHUNK, D_MODEL), lambda b, j: (rblk(b, j), PM_MO // D_MODEL)),
            pl.BlockSpec((N_META, D_MODEL), lambda b, j: (0, PM_MO // D_MODEL)),
            pl.BlockSpec((CHUNK, D_MODEL), lambda b, j: (rblk(b, j), 0)),
            pl.BlockSpec((N_META, D_MODEL), lambda b, j: (b, 0)),
            pl.BlockSpec((1, D_MODEL), const),
        ]
        args += [pm, pm_m, hb, hb_m, ng]
    odt = BF16 if merge else F32
    return pl.pallas_call(
        functools.partial(_mlstm_kernel, reverse=reverse, nb=nb, merge=merge),
        grid=(batch, nb + 1),
        in_specs=in_specs,
        out_specs=[
            pl.BlockSpec((CHUNK, D_MODEL), lambda b, j: (rblk(b, j), 0)),
            pl.BlockSpec((N_META, D_MODEL), lambda b, j: (b, 0)),
        ],
        out_shape=[
            jax.ShapeDtypeStruct((t, D_MODEL), odt),
            jax.ShapeDtypeStruct((batch * N_META, D_MODEL), odt),
        ],
        scratch_shapes=[pltpu.VMEM((ML_HEADS, ML_DQK, CT_W), F32), pltpu.VMEM((ML_HEADS, LANES), F32)],
        compiler_params=_cparams(("arbitrary", "arbitrary")),
        name="mlstm_bwd" if reverse else "mlstm_fwd",
    )(*args)


def _softmax_pv(sg, snk, vcat):
    m = jnp.maximum(jnp.max(sg, axis=1, keepdims=True), snk)
    p = jnp.exp(sg - m)
    l = jnp.sum(p, axis=1, keepdims=True) + jnp.exp(snk - m)
    return _dot(p.astype(BF16), vcat) * (1.0 / l)


def _attn_kernel(q_ref, kp_ref, kc_ref, kn_ref, vp_ref, vc_ref, vn_ref, km_ref, vm_ref,
                 bias_ref, bias0_ref, sink_ref, o_ref, *, nb):
    j = pl.program_id(1)
    col = lax.broadcasted_iota(I32, (1, 4 * CHUNK), 1)
    dead = ((j == 0) & (col >= CHUNK) & (col < 2 * CHUNK)) | ((j == nb - 1) & (col >= 3 * CHUNK))
    dyn = jnp.where(dead, NEG, 0.0)
    pad = jnp.zeros((CHUNK - N_META, AT_DH), BF16)
    scale = AT_DH ** -0.5
    for c in range(AT_KV_HEADS):
        ks = slice(c * AT_DH, (c + 1) * AT_DH)
        kcat = jnp.concatenate([km_ref[:, ks], pad, kp_ref[:, ks], kc_ref[:, ks], kn_ref[:, ks]], axis=0)
        vcat = jnp.concatenate([vm_ref[:, ks], pad, vp_ref[:, ks], vc_ref[:, ks], vn_ref[:, ks]], axis=0)
        qc = jnp.concatenate(
            [q_ref[:, (c * AT_GROUP + g) * AT_DH:(c * AT_GROUP + g + 1) * AT_DH] for g in range(AT_GROUP)], axis=0)
        s = _dot_nt(qc, kcat) * scale
        for g in range(AT_GROUP):
            h = c * AT_GROUP + g
            b_main = bias_ref[h]
            b_first = jnp.where(j == 0, bias0_ref[h], b_main[:, :CHUNK])
            bias = jnp.concatenate([b_first, b_main[:, CHUNK:]], axis=1) + dyn
            sg = s[g * CHUNK:(g + 1) * CHUNK] + bias
            o = _softmax_pv(sg, sink_ref[h:h + 1, 0:1], vcat)
            o_ref[:, h * AT_DH:(h + 1) * AT_DH] = o.astype(o_ref.dtype)


def _attn_real(pm, pm_m, bias_main, bias_meta0, sink_b, batch, nb):
    t = pm.shape[0]
    kvw = AT_KV_HEADS * AT_DH
    cur = lambda b, j: b * nb + j
    prv = lambda b, j: b * nb + jnp.maximum(j - 1, 0)
    nxt = lambda b, j: b * nb + jnp.minimum(j + 1, nb - 1)
    kcol, vcol = PM_AK // kvw, PM_AV // kvw
    const3 = lambda b, j: (0, 0, 0)
    return pl.pallas_call(
        functools.partial(_attn_kernel, nb=nb),
        grid=(batch, nb),
        in_specs=[
            pl.BlockSpec((CHUNK, D_MODEL), lambda b, j: (cur(b, j), PM_AQ // D_MODEL)),
            pl.BlockSpec((CHUNK, kvw), lambda b, j: (prv(b, j), kcol)),
            pl.BlockSpec((CHUNK, kvw), lambda b, j: (cur(b, j), kcol)),
            pl.BlockSpec((CHUNK, kvw), lambda b, j: (nxt(b, j), kcol)),
            pl.BlockSpec((CHUNK, kvw), lambda b, j: (prv(b, j), vcol)),
            pl.BlockSpec((CHUNK, kvw), lambda b, j: (cur(b, j), vcol)),
            pl.BlockSpec((CHUNK, kvw), lambda b, j: (nxt(b, j), vcol)),
            pl.BlockSpec((N_META, kvw), lambda b, j: (0, kcol)),
            pl.BlockSpec((N_META, kvw), lambda b, j: (0, vcol)),
            pl.BlockSpec((AT_HEADS, CHUNK, 4 * CHUNK), const3),
            pl.BlockSpec((AT_HEADS, CHUNK, CHUNK), const3),
            pl.BlockSpec((AT_HEADS, LANES), lambda b, j: (0, 0)),
        ],
        out_specs=pl.BlockSpec((CHUNK, D_MODEL), lambda b, j: (cur(b, j), 0)),
        out_shape=jax.ShapeDtypeStruct((t, D_MODEL), BF16),
        compiler_params=_cparams(("arbitrary", "arbitrary")),
        name="attn_real",
    )(pm, pm, pm, pm, pm, pm, pm, pm_m, pm_m, bias_main, bias_meta0, sink_b)


def _attn_meta_kernel(q_ref, kr_ref, vr_ref, km_ref, vm_ref, bias_ref, sink_ref, o_ref):
    pad = jnp.zeros((CHUNK - N_META, AT_DH), BF16)
    scale = AT_DH ** -0.5
    for c in range(AT_KV_HEADS):
        ks = slice(c * AT_DH, (c + 1) * AT_DH)
        kcat = jnp.concatenate([km_ref[:, ks], pad, kr_ref[:, ks]], axis=0)
        vcat = jnp.concatenate([vm_ref[:, ks], pad, vr_ref[:, ks]], axis=0)
        qc = jnp.concatenate(
            [q_ref[:, (c * AT_GROUP + g) * AT_DH:(c * AT_GROUP + g + 1) * AT_DH] for g in range(AT_GROUP)], axis=0)
        s = _dot_nt(qc, kcat) * scale
        for g in range(AT_GROUP):
            h = c * AT_GROUP + g
            sg = s[g * N_META:(g + 1) * N_META] + bias_ref[h]
            o = _softmax_pv(sg, sink_ref[h:h + 1, 0:1], vcat)
            o_ref[:, h * AT_DH:(h + 1) * AT_DH] = o.astype(o_ref.dtype)


def _attn_meta(pm, pm_m, bias_mq, sink_b, batch, nb):
    kvw = AT_KV_HEADS * AT_DH
    kcol, vcol = PM_AK // kvw, PM_AV // kvw
    return pl.pallas_call(
        _attn_meta_kernel,
        grid=(batch,),
        in_specs=[
            pl.BlockSpec((N_META, D_MODEL), lambda b: (0, PM_AQ // D_MODEL)),
            pl.BlockSpec((CHUNK, kvw), lambda b: (b * nb, kcol)),
            pl.BlockSpec((CHUNK, kvw), lambda b: (b * nb, vcol)),
            pl.BlockSpec((N_META, kvw), lambda b: (0, kcol)),
            pl.BlockSpec((N_META, kvw), lambda b: (0, vcol)),
            pl.BlockSpec((AT_HEADS, N_META, 2 * CHUNK), lambda b: (0, 0, 0)),
            pl.BlockSpec((AT_HEADS, LANES), lambda b: (0, 0)),
        ],
        out_specs=pl.BlockSpec((N_META, D_MODEL), lambda b: (b, 0)),
        out_shape=jax.ShapeDtypeStruct((batch * N_META, D_MODEL), BF16),
        compiler_params=_cparams(("arbitrary",)),
        name="attn_meta",
    )(pm_m, pm, pm, pm_m, pm_m, bias_mq, sink_b)


def _mix_kernel(ha_ref, hb_ref, ga_ref, gb_ref, wa_ref, wb_ref, o_ref):
    a = _dot(ha_ref[...], wa_ref[...])
    b = _dot(hb_ref[...], wb_ref[...])
    o_ref[...] = (ga_ref[...].astype(F32) * a + gb_ref[...].astype(F32) * b).astype(o_ref.dtype)


def _mix(ha, hb, gates_src, wa, wb, tm, ga_col, gb_col):
    t = ha.shape[0]
    nt = D_MODEL // PM_TN
    return pl.pallas_call(
        _mix_kernel,
        grid=(t // tm, nt),
        in_specs=[
            pl.BlockSpec((tm, D_MODEL), lambda i, n: (i, 0)),
            pl.BlockSpec((tm, D_MODEL), lambda i, n: (i, 0)),
            pl.BlockSpec((tm, PM_TN), lambda i, n: (i, ga_col + n)),
            pl.BlockSpec((tm, PM_TN), lambda i, n: (i, gb_col + n)),
            pl.BlockSpec((D_MODEL, PM_TN), lambda i, n: (0, n)),
            pl.BlockSpec((D_MODEL, PM_TN), lambda i, n: (0, n)),
        ],
        out_specs=pl.BlockSpec((tm, PM_TN), lambda i, n: (i, n)),
        out_shape=jax.ShapeDtypeStruct((t, D_MODEL), BF16),
        compiler_params=_cparams(("arbitrary", "arbitrary")),
        name="branch_mix",
    )(ha, hb, gates_src, gates_src, wa, wb)


def _out_kernel(mix_ref, x_ref, g0_ref, b0_ref, g1_ref, b1_ref, wo_ref, wrt_ref, x1_ref, aff_ref, *, chunked):
    y = _dot(mix_ref[...], wo_ref[...])
    h0 = _ln(x_ref[...], g0_ref[...], b0_ref[...])
    x1 = _ln(ALPHA * h0 + y, g1_ref[...], b1_ref[...])
    x1_ref[...] = x1
    lt = _dot_nt(wrt_ref[...], x1.astype(BF16))
    e = jnp.exp(lt - jnp.max(lt, axis=0, keepdims=True))
    aff = e / jnp.sum(e, axis=0, keepdims=True)
    if chunked:
        for c in range(aff.shape[1] // CHUNK):
            aff_ref[c] = aff[:, c * CHUNK:(c + 1) * CHUNK]
    else:
        aff_ref[...] = aff


def _out_ln_router(mix, x2, g0, b0, g1, b1, wo, wrt, tm, chunked):
    t = mix.shape[0]
    const = lambda i: (0, 0)
    if chunked:
        aff_spec = pl.BlockSpec((tm // CHUNK, N_EXPERTS, CHUNK), lambda i: (i, 0, 0))
        aff_shape = jax.ShapeDtypeStruct((t // CHUNK, N_EXPERTS, CHUNK), F32)
    else:
        aff_spec = pl.BlockSpec((N_EXPERTS, tm), lambda i: (0, i))
        aff_shape = jax.ShapeDtypeStruct((N_EXPERTS, t), F32)
    return pl.pallas_call(
        functools.partial(_out_kernel, chunked=chunked),
        grid=(t // tm,),
        in_specs=[
            pl.BlockSpec((tm, D_MODEL), lambda i: (i, 0)),
            pl.BlockSpec((tm, D_MODEL), lambda i: (i, 0)),
            pl.BlockSpec((1, D_MODEL), const), pl.BlockSpec((1, D_MODEL), const),
            pl.BlockSpec((1, D_MODEL), const), pl.BlockSpec((1, D_MODEL), const),
            pl.BlockSpec((D_MODEL, D_MODEL), const),
            pl.BlockSpec((N_EXPERTS, D_MODEL), const),
        ],
        out_specs=[pl.BlockSpec((tm, D_MODEL), lambda i: (i, 0)), aff_spec],
        out_shape=[jax.ShapeDtypeStruct((t, D_MODEL), F32), aff_shape],
        compiler_params=_cparams(("arbitrary",)),
        name="out_ln1_router",
    )(mix, x2, g0, b0, g1, b1, wo, wrt)


def _route_select_kernel(aff_ref, rank_ref, base_ref, p_s, b_s, *, cap, nc):
    aff = aff_ref[...]
    bits = lax.bitcast_convert_type(aff, I32)

    def count(maskf):
        return jnp.sum(jnp.sum(maskf, axis=0), axis=1, keepdims=True)

    def search(it, lo):
        cand = lo | (jnp.int32(1) << (30 - it))
        cnt = count(jnp.where(bits >= cand[None], 1.0, 0.0))
        return jnp.where(cnt >= cap, cand, lo)

    thr = lax.fori_loop(0, 31, search, jnp.zeros((N_EXPERTS, 1), I32))
    gt = bits > thr[None]
    eq = bits == thr[None]
    need = cap - count(jnp.where(gt, 1.0, 0.0))
    r_i = lax.broadcasted_iota(I32, (CHUNK, CHUNK), 0)
    c_i = lax.broadcasted_iota(I32, (CHUNK, CHUNK), 1)
    upper = jnp.where(r_i <= c_i, 1.0, 0.0).astype(BF16)

    def prefix(maskb):
        p = _dot(jnp.where(maskb, 1.0, 0.0).astype(BF16).reshape(nc * N_EXPERTS, CHUNK), upper)
        p_s[...] = p.reshape(nc, N_EXPERTS, CHUNK)

        def step(c, carry):
            b_s[c] = jnp.broadcast_to(carry, (N_EXPERTS, CHUNK))
            return carry + p_s[c][:, CHUNK - 1:CHUNK]

        lax.fori_loop(0, nc, step, jnp.zeros((N_EXPERTS, 1), F32))

    prefix(eq)
    eq_rank = b_s[...] + p_s[...] - 1.0
    sel = gt | (eq & (eq_rank < need[None]))
    prefix(sel)
    rank_ref[...] = jnp.where(sel, p_s[...] - 1.0, -1.0).astype(I32)
    base_ref[...] = b_s[...].astype(I32)


def _route_select(aff, cap):
    nc = aff.shape[0]
    shp = (nc, N_EXPERTS, CHUNK)
    full = pl.BlockSpec(shp, lambda i: (0, 0, 0))
    return pl.pallas_call(
        functools.partial(_route_select_kernel, cap=cap, nc=nc),
        grid=(1,),
        in_specs=[full],
        out_specs=[full, full],
        out_shape=[jax.ShapeDtypeStruct(shp, I32), jax.ShapeDtypeStruct(shp, I32)],
        scratch_shapes=[pltpu.VMEM(shp, F32), pltpu.VMEM(shp, F32)],
        compiler_params=_cparams(("arbitrary",)),
        name="route_select",
    )(aff)


FFN_MAX_ROWS = 576
LIST_ROWS = 16


def _route_compact_kernel(base_ref, rank_ref, aff_ref, out_ref):
    c = pl.program_id(0)

    @pl.when(c == 0)
    def _():
        out_ref[...] = jnp.zeros(out_ref.shape, F32)

    lane = lax.broadcasted_iota(I32, (1, CHUNK), 1)
    tok = c * CHUNK + lane
    t_hi = (tok >> 8).astype(F32)
    t_lo = (tok & 255).astype(F32)
    r_io = lax.broadcasted_iota(I32, (CHUNK, CHUNK), 0)
    zeros = jnp.zeros((LIST_ROWS - 5, CHUNK), F32)
    for e in range(N_EXPERTS):
        rk = rank_ref[0, e:e + 1, :]
        onehot = jnp.where(r_io == rk, 1.0, 0.0).astype(BF16)
        a_hi, a_mid, a_lo = _split3(aff_ref[0, e:e + 1, :])
        vals = jnp.concatenate([t_hi, t_lo, a_hi.astype(F32), a_mid.astype(F32), a_lo.astype(F32), zeros],
                               axis=0).astype(BF16)
        comp = _dot_nt(vals, onehot)
        base = base_ref[c, e]
        blk = base // CHUNK
        off = base - blk * CHUNK
        rolled = pltpu.roll(comp, off, axis=1)
        keep_lo = lane >= off
        out_ref[e, blk] = jnp.where(keep_lo, rolled, out_ref[e, blk])
        out_ref[e, blk + 1] = jnp.where(keep_lo, out_ref[e, blk + 1], rolled)


def _route_compact(base_s, rank, aff, n_blk):
    nc = rank.shape[0]
    blk = pl.BlockSpec((1, N_EXPERTS, CHUNK), lambda c, b: (c, 0, 0))
    return pl.pallas_call(
        _route_compact_kernel,
        grid_spec=pltpu.PrefetchScalarGridSpec(
            num_scalar_prefetch=1, grid=(nc,),
            in_specs=[blk, blk],
            out_specs=pl.BlockSpec((N_EXPERTS, n_blk, LIST_ROWS, CHUNK), lambda c, b: (0, 0, 0, 0))),
        out_shape=jax.ShapeDtypeStruct((N_EXPERTS, n_blk, LIST_ROWS, CHUNK), F32),
        compiler_params=_cparams(("arbitrary",)),
        name="route_compact",
    )(base_s, rank, aff)


def _ffn_kernel(idx_ref, gate_ref, x1_hbm, x1m_hbm, wg_ref, wu_ref, wd_ref, ye_ref,
                rows_ref, xb_ref, acc_ref, sem, *, tm, n_real, nf):
    f = pl.program_id(2)

    def row_copy(src, r_src, r_dst):
        return pltpu.make_async_copy(src.at[pl.ds(r_src, 1), :], rows_ref.at[pl.ds(r_dst, 1), :], sem.at[0])

    @pl.when(f == 0)
    def _():
        def issue(r, carry):
            t = idx_ref[0, 0, r]

            @pl.when(t < n_real)
            def _():
                row_copy(x1_hbm, t, r).start()

            @pl.when(t >= n_real)
            def _():
                row_copy(x1m_hbm, t - n_real, r).start()

            return carry

        lax.fori_loop(0, tm, issue, 0)

        def drain(r, carry):
            row_copy(x1_hbm, 0, r).wait()
            return carry

        lax.fori_loop(0, tm, drain, 0)
        xb_ref[...] = rows_ref[...].astype(BF16)

    xb = xb_ref[...]
    g = _dot(xb, wg_ref[0])
    u = _dot(xb, wu_ref[0])
    hid = (g * jax.nn.sigmoid(g) * u).astype(BF16)
    part = _dot(hid, wd_ref[0])

    @pl.when(f == 0)
    def _():
        acc_ref[...] = part

    @pl.when(f > 0)
    def _():
        acc_ref[...] += part

    @pl.when(f == nf - 1)
    def _():
        ye_ref[0] = (acc_ref[...] * gate_ref[0]).astype(ye_ref.dtype)


def _expert_ffn(idx3, gate3, x1, x1m, wg, wu, wd, cap_p, tm, tf):
    ns = cap_p // tm
    nf = EXPERT_FF // tf
    n_real = x1.shape[0]
    return pl.pallas_call(
        functools.partial(_ffn_kernel, tm=tm, n_real=n_real, nf=nf),
        grid=(N_EXPERTS, ns, nf),
        in_specs=[
            pl.BlockSpec((1, 1, tm), lambda e, s, f: (e * ns + s, 0, 0), memory_space=pltpu.SMEM),
            pl.BlockSpec((1, tm, 1), lambda e, s, f: (e, s, 0)),
            pl.BlockSpec(memory_space=pl.ANY),
            pl.BlockSpec(memory_space=pl.ANY),
            pl.BlockSpec((1, D_MODEL, tf), lambda e, s, f: (e, 0, f)),
            pl.BlockSpec((1, D_MODEL, tf), lambda e, s, f: (e, 0, f)),
            pl.BlockSpec((1, tf, D_MODEL), lambda e, s, f: (e, f, 0)),
        ],
        out_specs=pl.BlockSpec((1, tm, D_MODEL), lambda e, s, f: (e, s, 0)),
        out_shape=jax.ShapeDtypeStruct((N_EXPERTS, cap_p, D_MODEL), BF16),
        scratch_shapes=[
            pltpu.VMEM((tm, D_MODEL), F32),
            pltpu.VMEM((tm, D_MODEL), BF16),
            pltpu.VMEM((tm, D_MODEL), F32),
            pltpu.SemaphoreType.DMA((1,)),
        ],
        compiler_params=_cparams(("arbitrary", "arbitrary", "arbitrary")),
        name="expert_ffn",
    )(idx3, gate3, x1, x1m, wg, wu, wd)


COMB_ROWS = CHUNK + BF16_ROWS


def _combine_kernel(base_ref, x1_ref, rkt_ref, ye_hbm, g2_ref, b2_ref, y_ref, buf, sem, *, n_tiles, cap_p):
    i = pl.program_id(0)

    def start_of(c, e):
        b = base_ref[c, e]
        return jnp.minimum((b // BF16_ROWS) * BF16_ROWS, cap_p - COMB_ROWS)

    def copy(c, e, slot):
        st = pl.multiple_of(start_of(c, e), BF16_ROWS)
        return pltpu.make_async_copy(ye_hbm.at[e, pl.ds(st, COMB_ROWS), :], buf.at[slot, e], sem.at[slot, e])

    @pl.when(i == 0)
    def _():
        for e in range(N_EXPERTS):
            copy(0, e, 0).start()

    @pl.when(i + 1 < n_tiles)
    def _():
        for e in range(N_EXPERTS):
            copy(i + 1, e, (i + 1) % 2).start()

    slot = i % 2
    acc = ALPHA * x1_ref[...]
    rkt = rkt_ref[0]
    r_io = lax.broadcasted_iota(I32, (1, COMB_ROWS), 1)
    for e in range(N_EXPERTS):
        copy(i, e, slot).wait()
        rk = rkt[:, e:e + 1]
        srel = jnp.where(rk >= 0, rk + (base_ref[i, e] - start_of(i, e)), -1)
        onehot = jnp.where(srel == r_io, 1.0, 0.0).astype(BF16)
        acc = acc + _dot(onehot, buf[slot, e])
    y_ref[...] = _ln(acc, g2_ref[...], b2_ref[...])


def _combine(base_s, x1, rank_t, ye, g2, b2, cap_p):
    t = x1.shape[0]
    n_tiles = t // CHUNK
    const = lambda i, b: (0, 0)
    return pl.pallas_call(
        functools.partial(_combine_kernel, n_tiles=n_tiles, cap_p=cap_p),
        grid_spec=pltpu.PrefetchScalarGridSpec(
            num_scalar_prefetch=1, grid=(n_tiles,),
            in_specs=[
                pl.BlockSpec((CHUNK, D_MODEL), lambda i, b: (i, 0)),
                pl.BlockSpec((1, CHUNK, N_EXPERTS), lambda i, b: (i, 0, 0)),
                pl.BlockSpec(memory_space=pl.ANY),
                pl.BlockSpec((1, D_MODEL), const), pl.BlockSpec((1, D_MODEL), const),
            ],
            out_specs=pl.BlockSpec((CHUNK, D_MODEL), lambda i, b: (i, 0)),
            scratch_shapes=[
                pltpu.VMEM((2, N_EXPERTS, COMB_ROWS, D_MODEL), BF16),
                pltpu.SemaphoreType.DMA((2, N_EXPERTS)),
            ]),
        out_shape=jax.ShapeDtypeStruct((t, D_MODEL), F32),
        compiler_params=_cparams(("arbitrary",)),
        name="moe_combine_ln2",
    )(base_s, x1, rank_t, ye, g2, b2)


def _row(v):
    return v.reshape(1, -1).astype(F32)


def _pick_tile(n, pref):
    tm = pref
    while n % tm:
        tm //= 2
    return tm


def _trunk(x, p, meta):
    batch, seq, _ = x.shape
    nb = seq // CHUNK
    t = batch * seq
    bm = batch * N_META
    x2 = x.reshape(t, D_MODEL)
    pm_m, kt_m, gc_m, gt_m = meta
    pm, kt, gc, gt = _ln_proj(x2, p["g0"], p["b0"], p["w_main"], p["w_kt"], p["w_g"], p["w_gt"], p["bg"], p["bgt"],
                              _pick_tile(t, 512))
    hb, hb_m = _mlstm(pm, kt, gc, gt, pm_m, kt_m, gc_m, gt_m, batch, nb, reverse=True)
    ha, ha_m = _mlstm(pm, kt, gc, gt, pm_m, kt_m, gc_m, gt_m, batch, nb, reverse=False,
                      merged=(hb, hb_m, p["ng"]))
    at = _attn_real(pm, pm_m, p["bias_main"], p["bias_meta0"], p["sink_b"], batch, nb)
    at_m = _attn_meta(pm, pm_m, p["bias_mq"], p["sink_b"], batch, nb)

    mix = _mix(ha, at, pm, p["wa"], p["wb"], _pick_tile(t, 512), PM_GA // PM_TN, PM_GB // PM_TN)
    gates_m = jnp.tile(pm_m[:, PM_GA:PM_MQ], (batch, 1))
    mix_m = _mix(ha_m, at_m, gates_m, p["wa"], p["wb"], bm, 0, D_MODEL // PM_TN)

    x1, aff = _out_ln_router(mix, x2, p["g0"], p["b0"], p["g1"], p["b1"], p["wo"], p["wrt"],
                             _pick_tile(t, 256), True)
    xm = jnp.tile(p["meta_tokens"], (batch, 1))
    x1_m, aff_m = _out_ln_router(mix_m, xm, p["g0"], p["b0"], p["g1"], p["b1"], p["wo"], p["wrt"], bm, False)

    n_tok = t + bm
    nc = -(-n_tok // CHUNK)
    aff_m = jnp.pad(aff_m, ((0, 0), (0, nc * CHUNK - n_tok)), constant_values=-1.0)
    aff_m = aff_m.reshape(N_EXPERTS, -1, CHUNK).transpose(1, 0, 2)
    aff_all = jnp.concatenate([aff, aff_m], axis=0)
    cap = CAPACITY_FACTOR * n_tok // N_EXPERTS
    cap_p = -(-cap // CHUNK) * CHUNK
    rank, base = _route_select(aff_all, cap)
    base_s = base[:, :, 0]
    n_blk = cap_p // CHUNK + 2
    lists = _route_compact(base_s, rank, aff_all, n_blk)
    lists = lists.transpose(0, 2, 1, 3).reshape(N_EXPERTS, LIST_ROWS, n_blk * CHUNK)[:, :, :cap_p]
    idx = (lists[:, 0] * 256.0 + lists[:, 1]).astype(I32)
    gate = lists[:, 2] + lists[:, 3] + lists[:, 4]
    ns = next(n for n in range(1, cap_p) if cap_p % (n * BF16_ROWS) == 0 and cap_p // n <= FFN_MAX_ROWS)
    tm = cap_p // ns
    ye = _expert_ffn(idx.reshape(N_EXPERTS * ns, 1, tm), gate.reshape(N_EXPERTS, cap_p, 1), x1, x1_m,
                     p["wgate"], p["wup"], p["wdown"], cap_p, tm, 512)
    rank_t = rank[:t // CHUNK].transpose(0, 2, 1)
    y = _combine(base_s, x1, rank_t, ye, p["g2"], p["b2"], cap_p)
    return y.reshape(batch, seq, D_MODEL)


def kernel(x_prompt, x_sample, meta_tokens, ln0_g, ln0_b, rel_bias, attn_sink, w_in, b_gate, ml_norm_g,
           w_branch_a, w_branch_b, w_out, ln1_g, ln1_b, w_router, w_gate, w_up, w_down, ln2_g, ln2_b):
    assert w_in.shape[0] == DEPTH
    w = w_in[0]
    sizes = (ML_HEADS * ML_DQK, ML_HEADS * ML_DQK, ML_HEADS * ML_DV, ML_HEADS * ML_DV, 4 * ML_HEADS,
             AT_HEADS * AT_DH, AT_KV_HEADS * AT_DH, AT_KV_HEADS * AT_DH, D_MODEL, D_MODEL)
    offs = np.concatenate([[0], np.cumsum(sizes)])
    mq, mk, mv, mo, mg, aq, ak, av, ga, gb = [w[:, offs[i]:offs[i + 1]] for i in range(10)]
    w_g = jnp.pad(mg, ((0, 0), (0, LANES - 4 * ML_HEADS))).astype(BF16)
    bg = jnp.pad(b_gate[0].astype(F32), (0, LANES - 4 * ML_HEADS))
    nb_max = max(x_prompt.shape[1], x_sample.shape[1]) // CHUNK
    bk_main, bk_meta0, bk_mq = _bucket_tables(nb_max)
    rb = rel_bias.astype(F32)
    p = {
        "g0": _row(ln0_g), "b0": _row(ln0_b), "g1": _row(ln1_g[0]), "b1": _row(ln1_b[0]),
        "g2": _row(ln2_g[0]), "b2": _row(ln2_b[0]), "ng": _row(ml_norm_g[0]),
        "w_main": jnp.concatenate([mv, mo, aq, ga, gb, mq, ak, av], axis=1).astype(BF16),
        "w_kt": mk.T.astype(BF16), "w_g": w_g, "w_gt": w_g.T,
        "bg": bg.reshape(1, LANES), "bgt": bg.reshape(LANES, 1),
        "wa": w_branch_a[0].astype(BF16), "wb": w_branch_b[0].astype(BF16), "wo": w_out[0].astype(BF16),
        "wrt": w_router[0].T.astype(BF16),
        "wgate": w_gate[0].astype(BF16), "wup": w_up[0].astype(BF16), "wdown": w_down[0].astype(BF16),
        "bias_main": _bias_table(rb, bk_main), "bias_meta0": _bias_table(rb, bk_meta0),
        "bias_mq": _bias_table(rb, bk_mq),
        "sink_b": jnp.broadcast_to(attn_sink[0].astype(F32)[:, None], (AT_HEADS, LANES)),
        "meta_tokens": meta_tokens.astype(F32),
    }
    meta = _ln_proj(p["meta_tokens"], p["g0"], p["b0"], p["w_main"], p["w_kt"], p["w_g"], p["w_gt"],
                    p["bg"], p["bgt"], N_META)
    return (_trunk(x_prompt, p, meta), _trunk(x_sample, p, meta))
```

```python
import functools
import math

import numpy as np
import jax
import jax.numpy as jnp
from jax import lax
from jax.experimental import pallas as pl
from jax.experimental.pallas import tpu as pltpu

F32 = jnp.float32
BF16 = jnp.bfloat16
I32 = jnp.int32

D_MODEL = 2048
N_META = 16
CHUNK = 128
ML_HEADS = 8
ML_DV = D_MODEL // ML_HEADS
ML_DQK = ML_DV // 2
AT_DH = 128
AT_HEADS = D_MODEL // AT_DH
AT_KV_HEADS = AT_HEADS // 4
AT_GROUP = AT_HEADS // AT_KV_HEADS
WINDOW = 128
REL_BUCKETS = 32
REL_MAX_DIST = 128
N_EXPERTS = 16
EXPERT_FF = D_MODEL
CAPACITY_FACTOR = 2
DEPTH = 1
ALPHA = (2.0 * DEPTH) ** 0.25
LN_EPS = 1e-5
M_INIT = -1e30
NEG = -1e30

LANES = 128
BF16_ROWS = 16
VMEM_LIMIT = 56 * 1024 * 1024

PM_MV, PM_MO, PM_AQ, PM_GA, PM_GB, PM_MQ, PM_AK, PM_AV = 0, 2048, 4096, 6144, 8192, 10240, 11264, 11776
PM_WIDTH = 12288
PM_TN = 2048
MIX_TN = 1024
CT_W = ML_DV + LANES

NT_DIMS = (((1,), (1,)), ((), ()))


def _cparams(sem, vmem=VMEM_LIMIT):
    return pltpu.CompilerParams(dimension_semantics=sem, vmem_limit_bytes=vmem)


def _dot(a, b):
    return jnp.dot(a, b, preferred_element_type=F32)


def _dot_nt(a, b):
    return lax.dot_general(a, b, NT_DIMS, preferred_element_type=F32)


def _ln(x, g, b):
    mu = jnp.mean(x, axis=-1, keepdims=True)
    xc = x - mu
    var = jnp.mean(xc * xc, axis=-1, keepdims=True)
    return xc * lax.rsqrt(var + LN_EPS) * g + b


def _split3(x):
    hi = x.astype(BF16)
    r1 = x - hi.astype(F32)
    mid = r1.astype(BF16)
    lo = (r1 - mid.astype(F32)).astype(BF16)
    return hi, mid, lo


def _bias_kernel(rb_ref, bk_ref, o_ref):
    h = pl.program_id(0)
    bk = bk_ref[...]
    acc = jnp.full(bk.shape, NEG, F32)
    for b in range(REL_BUCKETS):
        acc = jnp.where(bk == b, rb_ref[b, h], acc)
    o_ref[0] = acc


def _bias_table(rel_bias, bucket):
    r, c = bucket.shape
    return pl.pallas_call(
        _bias_kernel,
        grid_spec=pltpu.PrefetchScalarGridSpec(
            num_scalar_prefetch=1, grid=(AT_HEADS,),
            in_specs=[pl.BlockSpec((r, c), lambda h, rb: (0, 0))],
            out_specs=pl.BlockSpec((1, r, c), lambda h, rb: (h, 0, 0))),
        out_shape=jax.ShapeDtypeStruct((AT_HEADS, r, c), F32),
        compiler_params=_cparams(("arbitrary",)),
        name="bias_table",
    )(rel_bias, bucket)


def _t5_bucket(rel):
    half = REL_BUCKETS // 2
    max_exact = half // 2
    n = jnp.abs(rel)
    nf = jnp.maximum(n, 1).astype(jnp.float32)
    large = max_exact + (jnp.log(nf / max_exact) / math.log(REL_MAX_DIST / max_exact)
                         * (half - max_exact)).astype(jnp.int32)
    large = jnp.minimum(large, half - 1)
    return jnp.where(rel > 0, half, 0) + jnp.where(n < max_exact, n, large)


def _t5_bucket_np(rel):
    half = REL_BUCKETS // 2
    max_exact = half // 2
    n = np.abs(rel)
    nf = np.maximum(n, 1).astype(np.float64)
    large = max_exact + (np.log(nf / max_exact) / math.log(REL_MAX_DIST / max_exact) * (half - max_exact)).astype(np.int64)
    large = np.minimum(large, half - 1)
    return np.where(rel > 0, half, 0) + np.where(n < max_exact, n, large)


def _bucket_tables(nb_max):
    i = np.arange(CHUNK)[:, None]
    m = np.arange(N_META)[None, :]
    ref_tab = _t5_bucket_np(m - (N_META + CHUNK + i))
    for j in range(1, nb_max):
        assert np.array_equal(_t5_bucket_np(m - (N_META + j * CHUNK + i)), ref_tab)
    qi = jnp.arange(CHUNK, dtype=I32)[:, None]
    c = jnp.arange(4 * CHUNK, dtype=I32)[None, :]
    rel_meta = c - (N_META + CHUNK + qi)
    rel_nb = (c - CHUNK) - CHUNK - qi
    vis_nb = (c >= CHUNK) & (jnp.abs(rel_nb) <= WINDOW)
    main = jnp.where(c < N_META, _t5_bucket(rel_meta), jnp.where(vis_nb, _t5_bucket(rel_nb), -1))
    c1 = jnp.arange(CHUNK, dtype=I32)[None, :]
    meta0 = jnp.where(c1 < N_META, _t5_bucket(c1 - (N_META + qi)), -1)
    mi = jnp.arange(N_META, dtype=I32)[:, None]
    c2 = jnp.arange(2 * CHUNK, dtype=I32)[None, :]
    rel_r = N_META + (c2 - CHUNK) - mi
    vis_r = (c2 >= CHUNK) & (jnp.abs(rel_r) <= WINDOW)
    mq = jnp.where(c2 < N_META, _t5_bucket(c2 - mi), jnp.where(vis_r, _t5_bucket(rel_r), -1))
    return main.astype(I32), meta0.astype(I32), mq.astype(I32)


def _proj_kernel(x_ref, g0_ref, b0_ref, w_ref, wkt_ref, wg_ref, wgt_ref, bg_ref, bgt_ref,
                 pm_ref, kt_ref, gc_ref, gt_ref, xs_ref):
    n = pl.program_id(1)

    @pl.when(n == 0)
    def _():
        xn = _ln(x_ref[...], g0_ref[...], b0_ref[...]).astype(BF16)
        xs_ref[...] = xn
        kt_ref[...] = _dot_nt(wkt_ref[...], xn).astype(BF16)
        gc_ref[...] = _dot(xn, wg_ref[...]) + bg_ref[...]
        gt_ref[...] = _dot_nt(wgt_ref[...], xn) + bgt_ref[...]

    y = _dot(xs_ref[...], w_ref[...])
    lo1, hi1 = PM_MO // PM_TN, PM_AQ // PM_TN
    lo2, hi2 = PM_GA // PM_TN, PM_MQ // PM_TN
    is_sig = ((n >= lo1) & (n < hi1)) | ((n >= lo2) & (n < hi2))

    @pl.when(is_sig)
    def _():
        pm_ref[...] = jax.nn.sigmoid(y).astype(BF16)

    @pl.when(jnp.logical_not(is_sig))
    def _():
        pm_ref[...] = y.astype(BF16)


def _ln_proj(x2, g0, b0, w_main, w_kt, w_g, w_gt, bg, bgt, tm):
    t = x2.shape[0]
    grid = (t // tm, PM_WIDTH // PM_TN)
    const = lambda i, n: (0, 0)
    return pl.pallas_call(
        _proj_kernel,
        grid=grid,
        in_specs=[
            pl.BlockSpec((tm, D_MODEL), lambda i, n: (i, 0)),
            pl.BlockSpec((1, D_MODEL), const),
            pl.BlockSpec((1, D_MODEL), const),
            pl.BlockSpec((D_MODEL, PM_TN), lambda i, n: (0, n)),
            pl.BlockSpec((ML_HEADS * ML_DQK, D_MODEL), const),
            pl.BlockSpec((D_MODEL, LANES), const),
            pl.BlockSpec((LANES, D_MODEL), const),
            pl.BlockSpec((1, LANES), const),
            pl.BlockSpec((LANES, 1), const),
        ],
        out_specs=[
            pl.BlockSpec((tm, PM_TN), lambda i, n: (i, n)),
            pl.BlockSpec((ML_HEADS * ML_DQK, tm), lambda i, n: (0, i)),
            pl.BlockSpec((tm, LANES), lambda i, n: (i, 0)),
            pl.BlockSpec((LANES, tm), lambda i, n: (0, i)),
        ],
        out_shape=[
            jax.ShapeDtypeStruct((t, PM_WIDTH), BF16),
            jax.ShapeDtypeStruct((ML_HEADS * ML_DQK, t), BF16),
            jax.ShapeDtypeStruct((t, LANES), F32),
            jax.ShapeDtypeStruct((LANES, t), F32),
        ],
        scratch_shapes=[pltpu.VMEM((tm, D_MODEL), BF16)],
        compiler_params=_cparams(("arbitrary", "arbitrary")),
        name="ln_proj",
    )(x2, g0, b0, w_main, w_kt, w_g, w_gt, bg, bgt)


def _mlstm_chunk(L, reverse, q_ref, kt_ref, v_ref, gc_ref, gt_ref, ct_ref, m_ref, emit):
    row = lax.broadcasted_iota(I32, (L, L), 0)
    col = lax.broadcasted_iota(I32, (L, L), 1)
    if reverse:
        tri = row <= col
        tri_t = row >= col
    else:
        tri = row >= col
        tri_t = row <= col
    mask = tri
    tri_b = jnp.where(tri, 1.0, 0.0).astype(BF16)
    tri_tb = jnp.where(tri_t, 1.0, 0.0).astype(BF16)
    gc = gc_ref[...]
    gt = gt_ref[...]
    lfc = jax.nn.log_sigmoid(gc)
    lft = jax.nn.log_sigmoid(gt)
    bc_all = sum(_dot(tri_b, p) for p in _split3(lfc))
    br_all = sum(_dot(p, tri_tb) for p in _split3(lft))
    end = 0 if reverse else L - 1
    ci0, cf0 = (2 * ML_HEADS, 3 * ML_HEADS) if reverse else (0, ML_HEADS)
    scale = ML_DQK ** -0.5
    for h in range(ML_HEADS):
        ci, cf = ci0 + h, cf0 + h
        q = q_ref[:, h * ML_DQK:(h + 1) * ML_DQK]
        kt = kt_ref[h * ML_DQK:(h + 1) * ML_DQK, :]
        v = v_ref[:, h * ML_DV:(h + 1) * ML_DV]
        igc = gc[:, ci:ci + 1]
        igr = gt[ci:ci + 1, :]
        bc = bc_all[:, cf:cf + 1]
        br = br_all[cf:cf + 1, :]
        btot = bc_all[end:end + 1, cf:cf + 1]
        m_prev = m_ref[h:h + 1, 0:1]
        ct_prev = ct_ref[h]
        d = jnp.where(mask, bc - br + igr, NEG)
        inter = bc + m_prev
        m_comb = jnp.maximum(inter, jnp.max(d, axis=1, keepdims=True))
        w = jnp.exp(d - m_comb)
        w_inter = jnp.exp(inter - m_comb)
        s = _dot(q, kt) * w * scale
        qc = _dot(q, ct_prev.astype(BF16)) * scale
        num = _dot(s.astype(BF16), v) + w_inter * qc[:, :ML_DV]
        den = jnp.sum(s, axis=1, keepdims=True) + w_inter * qc[:, ML_DV:ML_DV + 1]
        hv = num * (1.0 / jnp.maximum(jnp.abs(den), jnp.exp(-m_comb)))
        emit(h, hv)
        dec = btot - bc + igc
        inter_end = btot + m_prev
        m_new = jnp.maximum(inter_end, jnp.max(dec, axis=0, keepdims=True))
        w_end = jnp.exp(dec - m_new)
        w_prev = jnp.exp(inter_end - m_new)
        wv = jnp.concatenate([(w_end * v.astype(F32)).astype(BF16),
                              jnp.broadcast_to(w_end, (L, LANES)).astype(BF16)], axis=1)
        ct_ref[h] = w_prev * ct_prev + _dot(kt, wv)
        m_ref[h:h + 1, :] = jnp.broadcast_to(m_new, (1, LANES))


def _mlstm_kernel(*refs, reverse, nb, merge):
    (q_ref, kt_ref, v_ref, gc_ref, gt_ref, qm_ref, ktm_ref, vm_ref, gcm_ref, gtm_ref) = refs[:10]
    if merge:
        o_ref, om_ref, hb_ref, hbm_ref, ng_ref, out_ref, outm_ref, ct_ref, m_ref = refs[10:]
    else:
        out_ref, outm_ref, ct_ref, m_ref = refs[10:]
    j = pl.program_id(1)

    @pl.when(j == 0)
    def _():
        ct_ref[...] = jnp.zeros(ct_ref.shape, F32)
        m_ref[...] = jnp.full(m_ref.shape, M_INIT, F32)

    def make_emit(dst_ref, other_ref, gate_ref):
        def emit(h, hv):
            sl = slice(h * ML_DV, (h + 1) * ML_DV)
            if not merge:
                dst_ref[:, sl] = hv
                return
            hs = hv + other_ref[:, sl]
            mu = jnp.mean(hs, axis=-1, keepdims=True)
            hc = hs - mu
            var = jnp.mean(hc * hc, axis=-1, keepdims=True)
            hn = hc * lax.rsqrt(var + LN_EPS) * ng_ref[:, sl]
            dst_ref[:, sl] = (hn * gate_ref[:, sl].astype(F32)).astype(dst_ref.dtype)
        return emit

    is_meta = (j == nb) if reverse else (j == 0)

    @pl.when(is_meta)
    def _():
        emit = make_emit(outm_ref, hbm_ref if merge else None, om_ref if merge else None)
        _mlstm_chunk(N_META, reverse, qm_ref, ktm_ref, vm_ref, gcm_ref, gtm_ref, ct_ref, m_ref, emit)

    @pl.when(jnp.logical_not(is_meta))
    def _():
        emit = make_emit(out_ref, hb_ref if merge else None, o_ref if merge else None)
        _mlstm_chunk(CHUNK, reverse, q_ref, kt_ref, v_ref, gc_ref, gt_ref, ct_ref, m_ref, emit)


def _mlstm(pm, kt, gc, gt, pm_m, kt_m, gc_m, gt_m, batch, nb, *, reverse, merged=None):
    t = pm.shape[0]
    if reverse:
        rblk = lambda b, j: b * nb + jnp.maximum(nb - 1 - j, 0)
    else:
        rblk = lambda b, j: b * nb + jnp.maximum(j - 1, 0)
    const = lambda b, j: (0, 0)
    in_specs = [
        pl.BlockSpec((CHUNK, ML_HEADS * ML_DQK), lambda b, j: (rblk(b, j), PM_MQ // (ML_HEADS * ML_DQK))),
        pl.BlockSpec((ML_HEADS * ML_DQK, CHUNK), lambda b, j: (0, rblk(b, j))),
        pl.BlockSpec((CHUNK, D_MODEL), lambda b, j: (rblk(b, j), PM_MV // D_MODEL)),
        pl.BlockSpec((CHUNK, LANES), lambda b, j: (rblk(b, j), 0)),
        pl.BlockSpec((LANES, CHUNK), lambda b, j: (0, rblk(b, j))),
        pl.BlockSpec((N_META, ML_HEADS * ML_DQK), lambda b, j: (0, PM_MQ // (ML_HEADS * ML_DQK))),
        pl.BlockSpec((ML_HEADS * ML_DQK, N_META), const),
        pl.BlockSpec((N_META, D_MODEL), lambda b, j: (0, PM_MV // D_MODEL)),
        pl.BlockSpec((N_META, LANES), const),
        pl.BlockSpec((LANES, N_META), const),
    ]
    args = [pm, kt, pm, gc, gt, pm_m, kt_m, pm_m, gc_m, gt_m]
    merge = merged is not None
    if merge:
        hb, hb_m, ng = merged
        in_specs += [
            pl.BlockSpec((CHUNK, D_MODEL), lambda b, j: (rblk(b, j), PM_MO // D_MODEL)),
            pl.BlockSpec((N_META, D_MODEL), lambda b, j: (0, PM_MO // D_MODEL)),
            pl.BlockSpec((CHUNK, D_MODEL), lambda b, j: (rblk(b, j), 0)),
            pl.BlockSpec((N_META, D_MODEL), lambda b, j: (b, 0)),
            pl.BlockSpec((1, D_MODEL), const),
        ]
        args += [pm, pm_m, hb, hb_m, ng]
    odt = BF16 if merge else F32
    return pl.pallas_call(
        functools.partial(_mlstm_kernel, reverse=reverse, nb=nb, merge=merge),
        grid=(batch, nb + 1),
        in_specs=in_specs,
        out_specs=[
            pl.BlockSpec((CHUNK, D_MODEL), lambda b, j: (rblk(b, j), 0)),
            pl.BlockSpec((N_META, D_MODEL), lambda b, j: (b, 0)),
        ],
        out_shape=[
            jax.ShapeDtypeStruct((t, D_MODEL), odt),
            jax.ShapeDtypeStruct((batch * N_META, D_MODEL), odt),
        ],
        scratch_shapes=[pltpu.VMEM((ML_HEADS, ML_DQK, CT_W), F32), pltpu.VMEM((ML_HEADS, LANES), F32)],
        compiler_params=_cparams(("arbitrary", "arbitrary")),
        name="mlstm_bwd" if reverse else "mlstm_fwd",
    )(*args)


def _softmax_pv(sg, snk, vcat):
    m = jnp.maximum(jnp.max(sg, axis=1, keepdims=True), snk)
    p = jnp.exp(sg - m)
    l = jnp.sum(p, axis=1, keepdims=True) + jnp.exp(snk - m)
    return _dot(p.astype(BF16), vcat) * (1.0 / l)


def _attn_kernel(q_ref, kp_ref, kc_ref, kn_ref, vp_ref, vc_ref, vn_ref, km_ref, vm_ref,
                 bias_ref, bias0_ref, sink_ref, o_ref, *, nb):
    j = pl.program_id(1)
    col = lax.broadcasted_iota(I32, (1, 4 * CHUNK), 1)
    dead = ((j == 0) & (col >= CHUNK) & (col < 2 * CHUNK)) | ((j == nb - 1) & (col >= 3 * CHUNK))
    dyn = jnp.where(dead, NEG, 0.0)
    pad = jnp.zeros((CHUNK - N_META, AT_DH), BF16)
    scale = AT_DH ** -0.5
    for c in range(AT_KV_HEADS):
        ks = slice(c * AT_DH, (c + 1) * AT_DH)
        kcat = jnp.concatenate([km_ref[:, ks], pad, kp_ref[:, ks], kc_ref[:, ks], kn_ref[:, ks]], axis=0)
        vcat = jnp.concatenate([vm_ref[:, ks], pad, vp_ref[:, ks], vc_ref[:, ks], vn_ref[:, ks]], axis=0)
        qc = jnp.concatenate(
            [q_ref[:, (c * AT_GROUP + g) * AT_DH:(c * AT_GROUP + g + 1) * AT_DH] for g in range(AT_GROUP)], axis=0)
        s = _dot_nt(qc, kcat) * scale
        for g in range(AT_GROUP):
            h = c * AT_GROUP + g
            b_main = bias_ref[h]
            b_first = jnp.where(j == 0, bias0_ref[h], b_main[:, :CHUNK])
            bias = jnp.concatenate([b_first, b_main[:, CHUNK:]], axis=1) + dyn
            sg = s[g * CHUNK:(g + 1) * CHUNK] + bias
            o = _softmax_pv(sg, sink_ref[h:h + 1, 0:1], vcat)
            o_ref[:, h * AT_DH:(h + 1) * AT_DH] = o.astype(o_ref.dtype)


def _attn_real(pm, pm_m, bias_main, bias_meta0, sink_b, batch, nb):
    t = pm.shape[0]
    kvw = AT_KV_HEADS * AT_DH
    cur = lambda b, j: b * nb + j
    prv = lambda b, j: b * nb + jnp.maximum(j - 1, 0)
    nxt = lambda b, j: b * nb + jnp.minimum(j + 1, nb - 1)
    kcol, vcol = PM_AK // kvw, PM_AV // kvw
    const3 = lambda b, j: (0, 0, 0)
    return pl.pallas_call(
        functools.partial(_attn_kernel, nb=nb),
        grid=(batch, nb),
        in_specs=[
            pl.BlockSpec((CHUNK, D_MODEL), lambda b, j: (cur(b, j), PM_AQ // D_MODEL)),
            pl.BlockSpec((CHUNK, kvw), lambda b, j: (prv(b, j), kcol)),
            pl.BlockSpec((CHUNK, kvw), lambda b, j: (cur(b, j), kcol)),
            pl.BlockSpec((CHUNK, kvw), lambda b, j: (nxt(b, j), kcol)),
            pl.BlockSpec((CHUNK, kvw), lambda b, j: (prv(b, j), vcol)),
            pl.BlockSpec((CHUNK, kvw), lambda b, j: (cur(b, j), vcol)),
            pl.BlockSpec((CHUNK, kvw), lambda b, j: (nxt(b, j), vcol)),
            pl.BlockSpec((N_META, kvw), lambda b, j: (0, kcol)),
            pl.BlockSpec((N_META, kvw), lambda b, j: (0, vcol)),
            pl.BlockSpec((AT_HEADS, CHUNK, 4 * CHUNK), const3),
            pl.BlockSpec((AT_HEADS, CHUNK, CHUNK), const3),
            pl.BlockSpec((AT_HEADS, LANES), lambda b, j: (0, 0)),
        ],
        out_specs=pl.BlockSpec((CHUNK, D_MODEL), lambda b, j: (cur(b, j), 0)),
        out_shape=jax.ShapeDtypeStruct((t, D_MODEL), BF16),
        compiler_params=_cparams(("arbitrary", "arbitrary")),
        name="attn_real",
    )(pm, pm, pm, pm, pm, pm, pm, pm_m, pm_m, bias_main, bias_meta0, sink_b)


def _attn_meta_kernel(q_ref, kr_ref, vr_ref, km_ref, vm_ref, bias_ref, sink_ref, o_ref):
    pad = jnp.zeros((CHUNK - N_META, AT_DH), BF16)
    scale = AT_DH ** -0.5
    for c in range(AT_KV_HEADS):
        ks = slice(c * AT_DH, (c + 1) * AT_DH)
        kcat = jnp.concatenate([km_ref[:, ks], pad, kr_ref[:, ks]], axis=0)
        vcat = jnp.concatenate([vm_ref[:, ks], pad, vr_ref[:, ks]], axis=0)
        qc = jnp.concatenate(
            [q_ref[:, (c * AT_GROUP + g) * AT_DH:(c * AT_GROUP + g + 1) * AT_DH] for g in range(AT_GROUP)], axis=0)
        s = _dot_nt(qc, kcat) * scale
        for g in range(AT_GROUP):
            h = c * AT_GROUP + g
            sg = s[g * N_META:(g + 1) * N_META] + bias_ref[h]
            o = _softmax_pv(sg, sink_ref[h:h + 1, 0:1], vcat)
            o_ref[:, h * AT_DH:(h + 1) * AT_DH] = o.astype(o_ref.dtype)


def _attn_meta(pm, pm_m, bias_mq, sink_b, batch, nb):
    kvw = AT_KV_HEADS * AT_DH
    kcol, vcol = PM_AK // kvw, PM_AV // kvw
    return pl.pallas_call(
        _attn_meta_kernel,
        grid=(batch,),
        in_specs=[
            pl.BlockSpec((N_META, D_MODEL), lambda b: (0, PM_AQ // D_MODEL)),
            pl.BlockSpec((CHUNK, kvw), lambda b: (b * nb, kcol)),
            pl.BlockSpec((CHUNK, kvw), lambda b: (b * nb, vcol)),
            pl.BlockSpec((N_META, kvw), lambda b: (0, kcol)),
            pl.BlockSpec((N_META, kvw), lambda b: (0, vcol)),
            pl.BlockSpec((AT_HEADS, N_META, 2 * CHUNK), lambda b: (0, 0, 0)),
            pl.BlockSpec((AT_HEADS, LANES), lambda b: (0, 0)),
        ],
        out_specs=pl.BlockSpec((N_META, D_MODEL), lambda b: (b, 0)),
        out_shape=jax.ShapeDtypeStruct((batch * N_META, D_MODEL), BF16),
        compiler_params=_cparams(("arbitrary",)),
        name="attn_meta",
    )(pm_m, pm, pm, pm_m, pm_m, bias_mq, sink_b)


def _mix_kernel(ha_ref, hb_ref, ga_ref, gb_ref, wa_ref, wb_ref, o_ref):
    a = _dot(ha_ref[...], wa_ref[...])
    b = _dot(hb_ref[...], wb_ref[...])
    o_ref[...] = (ga_ref[...].astype(F32) * a + gb_ref[...].astype(F32) * b).astype(o_ref.dtype)


def _mix(ha, hb, gates_src, wa, wb, tm, ga_col, gb_col):
    t = ha.shape[0]
    nt = D_MODEL // MIX_TN
    return pl.pallas_call(
        _mix_kernel,
        grid=(t // tm, nt),
        in_specs=[
            pl.BlockSpec((tm, D_MODEL), lambda i, n: (i, 0)),
            pl.BlockSpec((tm, D_MODEL), lambda i, n: (i, 0)),
            pl.BlockSpec((tm, MIX_TN), lambda i, n: (i, ga_col + n)),
            pl.BlockSpec((tm, MIX_TN), lambda i, n: (i, gb_col + n)),
            pl.BlockSpec((D_MODEL, MIX_TN), lambda i, n: (0, n)),
            pl.BlockSpec((D_MODEL, MIX_TN), lambda i, n: (0, n)),
        ],
        out_specs=pl.BlockSpec((tm, MIX_TN), lambda i, n: (i, n)),
        out_shape=jax.ShapeDtypeStruct((t, D_MODEL), BF16),
        compiler_params=_cparams(("arbitrary", "arbitrary")),
        name="branch_mix",
    )(ha, hb, gates_src, gates_src, wa, wb)


HALF_D = D_MODEL // 2


def _pack_bf16_pairs(x):
    lo = lax.bitcast_convert_type(x[:, :HALF_D].astype(BF16).astype(F32), I32)
    hi = lax.bitcast_convert_type(x[:, HALF_D:].astype(BF16).astype(F32), I32)
    return lax.shift_right_logical(lo, 16) | (hi & jnp.int32(-65536))


def _unpack_bf16_pairs(u):
    lo = lax.bitcast_convert_type(u << 16, F32)
    hi = lax.bitcast_convert_type(u & jnp.int32(-65536), F32)
    return jnp.concatenate([lo, hi], axis=1).astype(BF16)


def _out_kernel(*refs, real):
    mix_ref, x_ref, g0_ref, b0_ref, g1_ref, b1_ref, wo_ref, wrt_ref = refs[:8]
    if real:
        x1_ref, x1p_ref, aff_ref = refs[8:]
    else:
        x1p_ref, aff_ref = refs[9:]

    def body():
        y = _dot(mix_ref[...], wo_ref[...])
        h0 = _ln(x_ref[...], g0_ref[...], b0_ref[...])
        x1 = _ln(ALPHA * h0 + y, g1_ref[...], b1_ref[...])
        x1p_ref[...] = _pack_bf16_pairs(x1)
        lt = _dot_nt(wrt_ref[...], x1.astype(BF16))
        e = jnp.exp(lt - jnp.max(lt, axis=0, keepdims=True))
        aff = e / jnp.sum(e, axis=0, keepdims=True)
        if real:
            x1_ref[...] = x1
            for c in range(aff.shape[1] // CHUNK):
                aff_ref[c] = aff[:, c * CHUNK:(c + 1) * CHUNK]
        else:
            aff_ref[...] = aff

    if not real:
        body()
        return
    last = pl.program_id(0) == pl.num_programs(0) - 1
    pl.when(jnp.logical_not(last))(body)

    @pl.when(last)
    def _():
        x1p_ref[...] = jnp.zeros(x1p_ref.shape, I32)


def _out_ln_router(mix, x2, g0, b0, g1, b1, wo, wrt, tm, n_meta_rows, packed=None):
    t = mix.shape[0]
    real = packed is None
    n = t // tm
    const = lambda i: (0, 0)
    row = lambda i: (jnp.minimum(i, n - 1), 0)
    in_specs = [
        pl.BlockSpec((tm, D_MODEL), row),
        pl.BlockSpec((tm, D_MODEL), row),
        pl.BlockSpec((1, D_MODEL), const), pl.BlockSpec((1, D_MODEL), const),
        pl.BlockSpec((1, D_MODEL), const), pl.BlockSpec((1, D_MODEL), const),
        pl.BlockSpec((D_MODEL, D_MODEL), const),
        pl.BlockSpec((N_EXPERTS, D_MODEL), const),
    ]
    args = [mix, x2, g0, b0, g1, b1, wo, wrt]
    if real:
        assert n_meta_rows <= tm
        out_specs = [
            pl.BlockSpec((tm, D_MODEL), row),
            pl.BlockSpec((tm, HALF_D), lambda i: (i, 0)),
            pl.BlockSpec((tm // CHUNK, N_EXPERTS, CHUNK), lambda i: (jnp.minimum(i, n - 1), 0, 0)),
        ]
        out_shape = [
            jax.ShapeDtypeStruct((t, D_MODEL), F32),
            jax.ShapeDtypeStruct((t + n_meta_rows, HALF_D), I32),
            jax.ShapeDtypeStruct((t // CHUNK, N_EXPERTS, CHUNK), F32),
        ]
        aliases = {}
    else:
        n_real = packed.shape[0] - t
        assert tm == t and n_real % t == 0
        in_specs.append(pl.BlockSpec(memory_space=pl.ANY))
        args.append(packed)
        out_specs = [
            pl.BlockSpec((t, HALF_D), lambda i: (n_real // t, 0)),
            pl.BlockSpec((N_EXPERTS, t), const),
        ]
        out_shape = [jax.ShapeDtypeStruct(packed.shape, I32), jax.ShapeDtypeStruct((N_EXPERTS, t), F32)]
        aliases = {8: 0}
    return pl.pallas_call(
        functools.partial(_out_kernel, real=real),
        grid=(n + 1 if real else n,),
        in_specs=in_specs,
        out_specs=out_specs,
        out_shape=out_shape,
        input_output_aliases=aliases,
        compiler_params=_cparams(("arbitrary",)),
        name="out_ln1_router" if real else "out_ln1_router_meta",
    )(*args)


def _route_select_kernel(aff_ref, rank_ref, base_ref, p_s, b_s, *, cap, nc):
    aff = aff_ref[...]
    bits = lax.bitcast_convert_type(aff, I32)

    def count(maskf):
        return jnp.sum(jnp.sum(maskf, axis=0), axis=1, keepdims=True)

    def search(it, lo):
        cand = lo | (jnp.int32(1) << (30 - it))
        cnt = count(jnp.where(bits >= cand[None], 1.0, 0.0))
        return jnp.where(cnt >= cap, cand, lo)

    thr = lax.fori_loop(0, 31, search, jnp.zeros((N_EXPERTS, 1), I32))
    gt = bits > thr[None]
    eq = bits == thr[None]
    need = cap - count(jnp.where(gt, 1.0, 0.0))
    r_i = lax.broadcasted_iota(I32, (CHUNK, CHUNK), 0)
    c_i = lax.broadcasted_iota(I32, (CHUNK, CHUNK), 1)
    upper = jnp.where(r_i <= c_i, 1.0, 0.0).astype(BF16)

    def prefix(maskb):
        p = _dot(jnp.where(maskb, 1.0, 0.0).astype(BF16).reshape(nc * N_EXPERTS, CHUNK), upper)
        p_s[...] = p.reshape(nc, N_EXPERTS, CHUNK)

        def step(c, carry):
            b_s[c] = jnp.broadcast_to(carry, (N_EXPERTS, CHUNK))
            return carry + p_s[c][:, CHUNK - 1:CHUNK]

        lax.fori_loop(0, nc, step, jnp.zeros((N_EXPERTS, 1), F32))

    prefix(eq)
    eq_rank = b_s[...] + p_s[...] - 1.0
    sel = gt | (eq & (eq_rank < need[None]))
    prefix(sel)
    rank_ref[...] = jnp.where(sel, p_s[...] - 1.0, -1.0).astype(I32)
    base_ref[...] = b_s[...].astype(I32)


def _route_select(aff, cap):
    nc = aff.shape[0]
    shp = (nc, N_EXPERTS, CHUNK)
    full = pl.BlockSpec(shp, lambda i: (0, 0, 0))
    return pl.pallas_call(
        functools.partial(_route_select_kernel, cap=cap, nc=nc),
        grid=(1,),
        in_specs=[full],
        out_specs=[full, full],
        out_shape=[jax.ShapeDtypeStruct(shp, I32), jax.ShapeDtypeStruct(shp, I32)],
        scratch_shapes=[pltpu.VMEM(shp, F32), pltpu.VMEM(shp, F32)],
        compiler_params=_cparams(("arbitrary",)),
        name="route_select",
    )(aff)


FFN_SLOT_TILES = 2
FFN_TF = 256
LIST_ROWS = 16


def _route_compact_kernel(base_ref, rank_ref, aff_ref, out_ref):
    c = pl.program_id(0)

    @pl.when(c == 0)
    def _():
        out_ref[...] = jnp.zeros(out_ref.shape, F32)

    lane = lax.broadcasted_iota(I32, (1, CHUNK), 1)
    tok = c * CHUNK + lane
    t_hi = (tok >> 8).astype(F32)
    t_lo = (tok & 255).astype(F32)
    r_io = lax.broadcasted_iota(I32, (CHUNK, CHUNK), 0)
    zeros = jnp.zeros((LIST_ROWS - 5, CHUNK), F32)
    for e in range(N_EXPERTS):
        rk = rank_ref[0, e:e + 1, :]
        onehot = jnp.where(r_io == rk, 1.0, 0.0).astype(BF16)
        a_hi, a_mid, a_lo = _split3(aff_ref[0, e:e + 1, :])
        vals = jnp.concatenate([t_hi, t_lo, a_hi.astype(F32), a_mid.astype(F32), a_lo.astype(F32), zeros],
                               axis=0).astype(BF16)
        comp = _dot_nt(vals, onehot)
        base = base_ref[c, e]
        blk = base // CHUNK
        off = base - blk * CHUNK
        rolled = pltpu.roll(comp, off, axis=1)
        keep_lo = lane >= off
        out_ref[e, blk] = jnp.where(keep_lo, rolled, out_ref[e, blk])
        out_ref[e, blk + 1] = jnp.where(keep_lo, out_ref[e, blk + 1], rolled)


def _route_compact(base_s, rank, aff, n_blk):
    nc = rank.shape[0]
    blk = pl.BlockSpec((1, N_EXPERTS, CHUNK), lambda c, b: (c, 0, 0))
    return pl.pallas_call(
        _route_compact_kernel,
        grid_spec=pltpu.PrefetchScalarGridSpec(
            num_scalar_prefetch=1, grid=(nc,),
            in_specs=[blk, blk],
            out_specs=pl.BlockSpec((N_EXPERTS, n_blk, LIST_ROWS, CHUNK), lambda c, b: (0, 0, 0, 0))),
        out_shape=jax.ShapeDtypeStruct((N_EXPERTS, n_blk, LIST_ROWS, CHUNK), F32),
        compiler_params=_cparams(("arbitrary",)),
        name="route_compact",
    )(base_s, rank, aff)


def _ffn_kernel(idx_ref, idxn_ref, gate_ref, xp_hbm, wg_ref, wu_ref, wd_ref, ye_ref,
                rows_ref, xb_ref, hid_ref, sem, *, tm, rs, nf, tf, n_tiles):
    e, s, j = pl.program_id(0), pl.program_id(1), pl.program_id(2)
    tile = e * pl.num_programs(1) + s
    n_sub = tm // rs
    per_slot = tm // (2 * nf * n_sub)

    def row_copy(iref, r):
        return pltpu.make_async_copy(xp_hbm.at[pl.ds(iref[0, 0, r], 1), :], rows_ref.at[pl.ds(r, 1), :], sem.at[0])

    def wait_rows():
        pltpu.make_async_copy(xp_hbm.at[pl.ds(0, tm), :], rows_ref, sem.at[0]).wait()

    def prefetch_next(r):
        first = (j * n_sub + r) * per_slot
        for k in range(per_slot):
            row_copy(idxn_ref, first + k).start()

    def rows(r):
        return pl.ds(pl.multiple_of(r * rs, BF16_ROWS), rs)

    @pl.when((tile == 0) & (j == 0))
    def _():
        def body(r, carry):
            row_copy(idx_ref, r).start()
            return carry

        lax.fori_loop(0, tm, body, 0)

    @pl.when(j == 0)
    def _():
        wait_rows()

        def unpack(r, carry):
            xb_ref[rows(r), :] = _unpack_bf16_pairs(rows_ref[rows(r), :])
            return carry

        lax.fori_loop(0, n_sub, unpack, 0)

    @pl.when(j < nf)
    def _():
        wg = wg_ref[0].astype(BF16)
        wu = wu_ref[0].astype(BF16)

        def body(r, carry):
            prefetch_next(r)
            x = xb_ref[rows(r), :]
            g = _dot(x, wg)
            u = _dot(x, wu)
            hid_ref[j, rows(r), :] = (g * jax.nn.sigmoid(g) * u).astype(BF16)
            return carry

        lax.fori_loop(0, n_sub, body, 0)

    @pl.when(j >= nf)
    def _():
        wd = wd_ref[0].astype(BF16)

        def body(r, carry):
            prefetch_next(r)
            acc = _dot(hid_ref[0, rows(r), :], wd[0:tf])
            for f in range(1, nf):
                acc = acc + _dot(hid_ref[f, rows(r), :], wd[f * tf:(f + 1) * tf])
            ye_ref[0, rows(r), :] = (acc * gate_ref[0, rows(r), :]).astype(ye_ref.dtype)
            return carry

        lax.fori_loop(0, n_sub, body, 0)

    @pl.when((tile == n_tiles - 1) & (j == 2 * nf - 1))
    def _():
        wait_rows()


def _expert_ffn(idx3, gate3, xp, wg, wu, wd, cap_p, tm, tf):
    ns = cap_p // tm
    nf = EXPERT_FF // tf
    n_tiles = N_EXPERTS * ns
    rs = tm // 4
    assert rs % BF16_ROWS == 0 and D_MODEL // tf == nf and tm % (2 * nf * (tm // rs)) == 0
    return pl.pallas_call(
        functools.partial(_ffn_kernel, tm=tm, rs=rs, nf=nf, tf=tf, n_tiles=n_tiles),
        grid=(N_EXPERTS, ns, 2 * nf),
        in_specs=[
            pl.BlockSpec((1, 1, tm), lambda e, s, j: (e * ns + s, 0, 0), memory_space=pltpu.SMEM),
            pl.BlockSpec((1, 1, tm), lambda e, s, j: (jnp.minimum(e * ns + s + 1, n_tiles - 1), 0, 0),
                         memory_space=pltpu.SMEM),
            pl.BlockSpec((1, tm, 1), lambda e, s, j: (e, s, 0)),
            pl.BlockSpec(memory_space=pl.ANY),
            pl.BlockSpec((1, D_MODEL, tf), lambda e, s, j: (e, 0, jnp.minimum(j, nf - 1))),
            pl.BlockSpec((1, D_MODEL, tf), lambda e, s, j: (e, 0, jnp.minimum(j, nf - 1))),
            pl.BlockSpec((1, EXPERT_FF, tf), lambda e, s, j: (e, 0, jnp.maximum(j - nf, 0))),
        ],
        out_specs=pl.BlockSpec((1, tm, tf), lambda e, s, j: (e, s, jnp.maximum(j - nf, 0))),
        out_shape=jax.ShapeDtypeStruct((N_EXPERTS, cap_p, D_MODEL), BF16),
        scratch_shapes=[
            pltpu.VMEM((tm, HALF_D), I32),
            pltpu.VMEM((tm, D_MODEL), BF16),
            pltpu.VMEM((nf, tm, tf), BF16),
            pltpu.SemaphoreType.DMA((1,)),
        ],
        compiler_params=_cparams(("arbitrary", "arbitrary", "arbitrary")),
        name="expert_ffn",
    )(idx3, idx3, gate3, xp, wg, wu, wd)


COMB_ROWS = CHUNK + BF16_ROWS


def _combine_kernel(base_ref, x1_ref, rkt_ref, ye_hbm, g2_ref, b2_ref, y_ref, buf, sem, *, n_tiles, cap_p):
    i = pl.program_id(0)

    def start_of(c, e):
        b = base_ref[c, e]
        return jnp.minimum((b // BF16_ROWS) * BF16_ROWS, cap_p - COMB_ROWS)

    def copy(c, e, slot):
        st = pl.multiple_of(start_of(c, e), BF16_ROWS)
        return pltpu.make_async_copy(ye_hbm.at[e, pl.ds(st, COMB_ROWS), :], buf.at[slot, e], sem.at[slot, e])

    @pl.when(i == 0)
    def _():
        for e in range(N_EXPERTS):
            copy(0, e, 0).start()

    @pl.when(i + 1 < n_tiles)
    def _():
        for e in range(N_EXPERTS):
            copy(i + 1, e, (i + 1) % 2).start()

    slot = i % 2
    acc = ALPHA * x1_ref[...]
    rkt = rkt_ref[0]
    r_io = lax.broadcasted_iota(I32, (1, COMB_ROWS), 1)
    for e in range(N_EXPERTS):
        copy(i, e, slot).wait()
        rk = rkt[:, e:e + 1]
        srel = jnp.where(rk >= 0, rk + (base_ref[i, e] - start_of(i, e)), -1)
        onehot = jnp.where(srel == r_io, 1.0, 0.0).astype(BF16)
        acc = acc + _dot(onehot, buf[slot, e])
    y_ref[...] = _ln(acc, g2_ref[...], b2_ref[...])


def _combine(base_s, x1, rank_t, ye, g2, b2, cap_p):
    t = x1.shape[0]
    n_tiles = t // CHUNK
    const = lambda i, b: (0, 0)
    return pl.pallas_call(
        functools.partial(_combine_kernel, n_tiles=n_tiles, cap_p=cap_p),
        grid_spec=pltpu.PrefetchScalarGridSpec(
            num_scalar_prefetch=1, grid=(n_tiles,),
            in_specs=[
                pl.BlockSpec((CHUNK, D_MODEL), lambda i, b: (i, 0)),
                pl.BlockSpec((1, CHUNK, N_EXPERTS), lambda i, b: (i, 0, 0)),
                pl.BlockSpec(memory_space=pl.ANY),
                pl.BlockSpec((1, D_MODEL), const), pl.BlockSpec((1, D_MODEL), const),
            ],
            out_specs=pl.BlockSpec((CHUNK, D_MODEL), lambda i, b: (i, 0)),
            scratch_shapes=[
                pltpu.VMEM((2, N_EXPERTS, COMB_ROWS, D_MODEL), BF16),
                pltpu.SemaphoreType.DMA((2, N_EXPERTS)),
            ]),
        out_shape=jax.ShapeDtypeStruct((t, D_MODEL), F32),
        compiler_params=_cparams(("arbitrary",)),
        name="moe_combine_ln2",
    )(base_s, x1, rank_t, ye, g2, b2)


def _row(v):
    return v.reshape(1, -1).astype(F32)


def _pick_tile(n, pref):
    tm = pref
    while n % tm:
        tm //= 2
    return tm


def _trunk(x, p, meta):
    batch, seq, _ = x.shape
    nb = seq // CHUNK
    t = batch * seq
    bm = batch * N_META
    x2 = x.reshape(t, D_MODEL)
    pm_m, kt_m, gc_m, gt_m = meta
    pm, kt, gc, gt = _ln_proj(x2, p["g0"], p["b0"], p["w_main"], p["w_kt"], p["w_g"], p["w_gt"], p["bg"], p["bgt"],
                              _pick_tile(t, 512))
    hb, hb_m = _mlstm(pm, kt, gc, gt, pm_m, kt_m, gc_m, gt_m, batch, nb, reverse=True)
    ha, ha_m = _mlstm(pm, kt, gc, gt, pm_m, kt_m, gc_m, gt_m, batch, nb, reverse=False,
                      merged=(hb, hb_m, p["ng"]))
    at = _attn_real(pm, pm_m, p["bias_main"], p["bias_meta0"], p["sink_b"], batch, nb)
    at_m = _attn_meta(pm, pm_m, p["bias_mq"], p["sink_b"], batch, nb)

    mix = _mix(ha, at, pm, p["wa"], p["wb"], _pick_tile(t, 512), PM_GA // MIX_TN, PM_GB // MIX_TN)
    gates_m = jnp.tile(pm_m[:, PM_GA:PM_MQ], (batch, 1))
    mix_m = _mix(ha_m, at_m, gates_m, p["wa"], p["wb"], bm, 0, D_MODEL // MIX_TN)

    ln_args = (p["g0"], p["b0"], p["g1"], p["b1"], p["wo"], p["wrt"])
    x1, xp, aff = _out_ln_router(mix, x2, *ln_args, _pick_tile(t, 256), bm)
    xm = jnp.tile(p["meta_tokens"], (batch, 1))
    xp, aff_m = _out_ln_router(mix_m, xm, *ln_args, bm, bm, packed=xp)

    n_tok = t + bm
    nc = -(-n_tok // CHUNK)
    aff_m = jnp.pad(aff_m, ((0, 0), (0, nc * CHUNK - n_tok)), constant_values=-1.0)
    aff_m = aff_m.reshape(N_EXPERTS, -1, CHUNK).transpose(1, 0, 2)
    aff_all = jnp.concatenate([aff, aff_m], axis=0)
    cap = CAPACITY_FACTOR * n_tok // N_EXPERTS
    cap_p = -(-cap // CHUNK) * CHUNK
    rank, base = _route_select(aff_all, cap)
    base_s = base[:, :, 0]
    n_blk = cap_p // CHUNK + 2
    lists = _route_compact(base_s, rank, aff_all, n_blk)
    lists = lists.transpose(0, 2, 1, 3).reshape(N_EXPERTS, LIST_ROWS, n_blk * CHUNK)[:, :, :cap_p]
    idx = (lists[:, 0] * 256.0 + lists[:, 1]).astype(I32)
    gate = lists[:, 2] + lists[:, 3] + lists[:, 4]
    ns = FFN_SLOT_TILES
    tm = cap_p // ns
    ye = _expert_ffn(idx.reshape(N_EXPERTS * ns, 1, tm), gate.reshape(N_EXPERTS, cap_p, 1), xp,
                     p["wgate"], p["wup"], p["wdown"], cap_p, tm, FFN_TF)
    rank_t = rank[:t // CHUNK].transpose(0, 2, 1)
    y = _combine(base_s, x1, rank_t, ye, p["g2"], p["b2"], cap_p)
    return y.reshape(batch, seq, D_MODEL)


def kernel(x_prompt, x_sample, meta_tokens, ln0_g, ln0_b, rel_bias, attn_sink, w_in, b_gate, ml_norm_g,
           w_branch_a, w_branch_b, w_out, ln1_g, ln1_b, w_router, w_gate, w_up, w_down, ln2_g, ln2_b):
    assert w_in.shape[0] == DEPTH
    w = w_in[0]
    sizes = (ML_HEADS * ML_DQK, ML_HEADS * ML_DQK, ML_HEADS * ML_DV, ML_HEADS * ML_DV, 4 * ML_HEADS,
             AT_HEADS * AT_DH, AT_KV_HEADS * AT_DH, AT_KV_HEADS * AT_DH, D_MODEL, D_MODEL)
    offs = np.concatenate([[0], np.cumsum(sizes)])
    mq, mk, mv, mo, mg, aq, ak, av, ga, gb = [w[:, offs[i]:offs[i + 1]] for i in range(10)]
    w_g = jnp.pad(mg, ((0, 0), (0, LANES - 4 * ML_HEADS))).astype(BF16)
    bg = jnp.pad(b_gate[0].astype(F32), (0, LANES - 4 * ML_HEADS))
    nb_max = max(x_prompt.shape[1], x_sample.shape[1]) // CHUNK
    bk_main, bk_meta0, bk_mq = _bucket_tables(nb_max)
    rb = rel_bias.astype(F32)
    p = {
        "g0": _row(ln0_g), "b0": _row(ln0_b), "g1": _row(ln1_g[0]), "b1": _row(ln1_b[0]),
        "g2": _row(ln2_g[0]), "b2": _row(ln2_b[0]), "ng": _row(ml_norm_g[0]),
        "w_main": jnp.concatenate([mv, mo, aq, ga, gb, mq, ak, av], axis=1).astype(BF16),
        "w_kt": mk.T.astype(BF16), "w_g": w_g, "w_gt": w_g.T,
        "bg": bg.reshape(1, LANES), "bgt": bg.reshape(LANES, 1),
        "wa": w_branch_a[0].astype(BF16), "wb": w_branch_b[0].astype(BF16), "wo": w_out[0].astype(BF16),
        "wrt": w_router[0].T.astype(BF16),
        "wgate": w_gate[0], "wup": w_up[0], "wdown": w_down[0],
        "bias_main": _bias_table(rb, bk_main), "bias_meta0": _bias_table(rb, bk_meta0),
        "bias_mq": _bias_table(rb, bk_mq),
        "sink_b": jnp.broadcast_to(attn_sink[0].astype(F32)[:, None], (AT_HEADS, LANES)),
        "meta_tokens": meta_tokens.astype(F32),
    }
    meta = _ln_proj(p["meta_tokens"], p["g0"], p["b0"], p["w_main"], p["w_kt"], p["w_g"], p["w_gt"],
                    p["bg"], p["bgt"], N_META)
    return (_trunk(x_prompt, p, meta), _trunk(x_sample, p, meta))
```

```python
import functools
import math

import numpy as np
import jax
import jax.numpy as jnp
from jax import lax
from jax.experimental import pallas as pl
from jax.experimental.pallas import tpu as pltpu

F32 = jnp.float32
BF16 = jnp.bfloat16
I32 = jnp.int32

D_MODEL = 2048
N_META = 16
CHUNK = 128
ML_HEADS = 8
ML_DV = D_MODEL // ML_HEADS
ML_DQK = ML_DV // 2
AT_DH = 128
AT_HEADS = D_MODEL // AT_DH
AT_KV_HEADS = AT_HEADS // 4
AT_GROUP = AT_HEADS // AT_KV_HEADS
WINDOW = 128
REL_BUCKETS = 32
REL_MAX_DIST = 128
N_EXPERTS = 16
EXPERT_FF = D_MODEL
CAPACITY_FACTOR = 2
DEPTH = 1
ALPHA = (2.0 * DEPTH) ** 0.25
LN_EPS = 1e-5
M_INIT = -1e30
NEG = -1e30

LANES = 128
BF16_ROWS = 16
VMEM_LIMIT = 56 * 1024 * 1024

PM_MV, PM_MO, PM_AQ, PM_GA, PM_GB, PM_MQ, PM_AK, PM_AV = 0, 2048, 4096, 6144, 8192, 10240, 11264, 11776
PM_WIDTH = 12288
PM_TN = 2048
MIX_TN = 1024
CT_W = ML_DV + LANES

NT_DIMS = (((1,), (1,)), ((), ()))


def _cparams(sem, vmem=VMEM_LIMIT):
    return pltpu.CompilerParams(dimension_semantics=sem, vmem_limit_bytes=vmem)


def _dot(a, b):
    return jnp.dot(a, b, preferred_element_type=F32)


def _dot_nt(a, b):
    return lax.dot_general(a, b, NT_DIMS, preferred_element_type=F32)


def _ln(x, g, b):
    mu = jnp.mean(x, axis=-1, keepdims=True)
    xc = x - mu
    var = jnp.mean(xc * xc, axis=-1, keepdims=True)
    return xc * lax.rsqrt(var + LN_EPS) * g + b


def _split3(x):
    hi = x.astype(BF16)
    r1 = x - hi.astype(F32)
    mid = r1.astype(BF16)
    lo = (r1 - mid.astype(F32)).astype(BF16)
    return hi, mid, lo


def _bias_kernel(rb_ref, bk_ref, o_ref):
    h = pl.program_id(0)
    bk = bk_ref[...]
    acc = jnp.full(bk.shape, NEG, F32)
    for b in range(REL_BUCKETS):
        acc = jnp.where(bk == b, rb_ref[b, h], acc)
    o_ref[0] = acc


def _bias_table(rel_bias, bucket):
    r, c = bucket.shape
    return pl.pallas_call(
        _bias_kernel,
        grid_spec=pltpu.PrefetchScalarGridSpec(
            num_scalar_prefetch=1, grid=(AT_HEADS,),
            in_specs=[pl.BlockSpec((r, c), lambda h, rb: (0, 0))],
            out_specs=pl.BlockSpec((1, r, c), lambda h, rb: (h, 0, 0))),
        out_shape=jax.ShapeDtypeStruct((AT_HEADS, r, c), F32),
        compiler_params=_cparams(("arbitrary",)),
        name="bias_table",
    )(rel_bias, bucket)


def _t5_bucket(rel):
    half = REL_BUCKETS // 2
    max_exact = half // 2
    n = jnp.abs(rel)
    nf = jnp.maximum(n, 1).astype(jnp.float32)
    large = max_exact + (jnp.log(nf / max_exact) / math.log(REL_MAX_DIST / max_exact)
                         * (half - max_exact)).astype(jnp.int32)
    large = jnp.minimum(large, half - 1)
    return jnp.where(rel > 0, half, 0) + jnp.where(n < max_exact, n, large)


def _t5_bucket_np(rel):
    half = REL_BUCKETS // 2
    max_exact = half // 2
    n = np.abs(rel)
    nf = np.maximum(n, 1).astype(np.float64)
    large = max_exact + (np.log(nf / max_exact) / math.log(REL_MAX_DIST / max_exact) * (half - max_exact)).astype(np.int64)
    large = np.minimum(large, half - 1)
    return np.where(rel > 0, half, 0) + np.where(n < max_exact, n, large)


def _bucket_tables(nb_max):
    i = np.arange(CHUNK)[:, None]
    m = np.arange(N_META)[None, :]
    ref_tab = _t5_bucket_np(m - (N_META + CHUNK + i))
    for j in range(1, nb_max):
        assert np.array_equal(_t5_bucket_np(m - (N_META + j * CHUNK + i)), ref_tab)
    qi = jnp.arange(CHUNK, dtype=I32)[:, None]
    c = jnp.arange(4 * CHUNK, dtype=I32)[None, :]
    rel_meta = c - (N_META + CHUNK + qi)
    rel_nb = (c - CHUNK) - CHUNK - qi
    vis_nb = (c >= CHUNK) & (jnp.abs(rel_nb) <= WINDOW)
    main = jnp.where(c < N_META, _t5_bucket(rel_meta), jnp.where(vis_nb, _t5_bucket(rel_nb), -1))
    c1 = jnp.arange(CHUNK, dtype=I32)[None, :]
    meta0 = jnp.where(c1 < N_META, _t5_bucket(c1 - (N_META + qi)), -1)
    mi = jnp.arange(N_META, dtype=I32)[:, None]
    c2 = jnp.arange(2 * CHUNK, dtype=I32)[None, :]
    rel_r = N_META + (c2 - CHUNK) - mi
    vis_r = (c2 >= CHUNK) & (jnp.abs(rel_r) <= WINDOW)
    mq = jnp.where(c2 < N_META, _t5_bucket(c2 - mi), jnp.where(vis_r, _t5_bucket(rel_r), -1))
    return main.astype(I32), meta0.astype(I32), mq.astype(I32)


def _proj_kernel(x_ref, g0_ref, b0_ref, w_ref, wkt_ref, wg_ref, wgt_ref, bg_ref, bgt_ref,
                 pm_ref, kt_ref, gc_ref, gt_ref, xs_ref):
    n = pl.program_id(1)

    @pl.when(n == 0)
    def _():
        xn = _ln(x_ref[...], g0_ref[...], b0_ref[...]).astype(BF16)
        xs_ref[...] = xn
        kt_ref[...] = _dot_nt(wkt_ref[...], xn).astype(BF16)
        gc_ref[...] = _dot(xn, wg_ref[...]) + bg_ref[...]
        gt_ref[...] = _dot_nt(wgt_ref[...], xn) + bgt_ref[...]

    y = _dot(xs_ref[...], w_ref[...])
    lo1, hi1 = PM_MO // PM_TN, PM_AQ // PM_TN
    lo2, hi2 = PM_GA // PM_TN, PM_MQ // PM_TN
    is_sig = ((n >= lo1) & (n < hi1)) | ((n >= lo2) & (n < hi2))

    @pl.when(is_sig)
    def _():
        pm_ref[...] = jax.nn.sigmoid(y).astype(BF16)

    @pl.when(jnp.logical_not(is_sig))
    def _():
        pm_ref[...] = y.astype(BF16)


def _ln_proj(x2, g0, b0, w_main, w_kt, w_g, w_gt, bg, bgt, tm):
    t = x2.shape[0]
    grid = (t // tm, PM_WIDTH // PM_TN)
    const = lambda i, n: (0, 0)
    return pl.pallas_call(
        _proj_kernel,
        grid=grid,
        in_specs=[
            pl.BlockSpec((tm, D_MODEL), lambda i, n: (i, 0)),
            pl.BlockSpec((1, D_MODEL), const),
            pl.BlockSpec((1, D_MODEL), const),
            pl.BlockSpec((D_MODEL, PM_TN), lambda i, n: (0, n)),
            pl.BlockSpec((ML_HEADS * ML_DQK, D_MODEL), const),
            pl.BlockSpec((D_MODEL, LANES), const),
            pl.BlockSpec((LANES, D_MODEL), const),
            pl.BlockSpec((1, LANES), const),
            pl.BlockSpec((LANES, 1), const),
        ],
        out_specs=[
            pl.BlockSpec((tm, PM_TN), lambda i, n: (i, n)),
            pl.BlockSpec((ML_HEADS * ML_DQK, tm), lambda i, n: (0, i)),
            pl.BlockSpec((tm, LANES), lambda i, n: (i, 0)),
            pl.BlockSpec((LANES, tm), lambda i, n: (0, i)),
        ],
        out_shape=[
            jax.ShapeDtypeStruct((t, PM_WIDTH), BF16),
            jax.ShapeDtypeStruct((ML_HEADS * ML_DQK, t), BF16),
            jax.ShapeDtypeStruct((t, LANES), F32),
            jax.ShapeDtypeStruct((LANES, t), F32),
        ],
        scratch_shapes=[pltpu.VMEM((tm, D_MODEL), BF16)],
        compiler_params=_cparams(("arbitrary", "arbitrary")),
        name="ln_proj",
    )(x2, g0, b0, w_main, w_kt, w_g, w_gt, bg, bgt)


def _mlstm_chunk(L, reverse, q_ref, kt_ref, v_ref, gc_ref, gt_ref, ct_ref, m_ref, emit):
    row = lax.broadcasted_iota(I32, (L, L), 0)
    col = lax.broadcasted_iota(I32, (L, L), 1)
    if reverse:
        tri = row <= col
        tri_t = row >= col
    else:
        tri = row >= col
        tri_t = row <= col
    mask = tri
    tri_b = jnp.where(tri, 1.0, 0.0).astype(BF16)
    tri_tb = jnp.where(tri_t, 1.0, 0.0).astype(BF16)
    gc = gc_ref[...]
    gt = gt_ref[...]
    lfc = jax.nn.log_sigmoid(gc)
    lft = jax.nn.log_sigmoid(gt)
    bc_all = sum(_dot(tri_b, p) for p in _split3(lfc))
    br_all = sum(_dot(p, tri_tb) for p in _split3(lft))
    end = 0 if reverse else L - 1
    ci0, cf0 = (2 * ML_HEADS, 3 * ML_HEADS) if reverse else (0, ML_HEADS)
    scale = ML_DQK ** -0.5
    for h in range(ML_HEADS):
        ci, cf = ci0 + h, cf0 + h
        q = q_ref[:, h * ML_DQK:(h + 1) * ML_DQK]
        kt = kt_ref[h * ML_DQK:(h + 1) * ML_DQK, :]
        v = v_ref[:, h * ML_DV:(h + 1) * ML_DV]
        igc = gc[:, ci:ci + 1]
        igr = gt[ci:ci + 1, :]
        bc = bc_all[:, cf:cf + 1]
        br = br_all[cf:cf + 1, :]
        btot = bc_all[end:end + 1, cf:cf + 1]
        m_prev = m_ref[h:h + 1, 0:1]
        ct_prev = ct_ref[h]
        d = jnp.where(mask, bc - br + igr, NEG)
        inter = bc + m_prev
        m_comb = jnp.maximum(inter, jnp.max(d, axis=1, keepdims=True))
        w = jnp.exp(d - m_comb)
        w_inter = jnp.exp(inter - m_comb)
        s = _dot(q, kt) * w * scale
        qc = _dot(q, ct_prev.astype(BF16)) * scale
        num = _dot(s.astype(BF16), v) + w_inter * qc[:, :ML_DV]
        den = jnp.sum(s, axis=1, keepdims=True) + w_inter * qc[:, ML_DV:ML_DV + 1]
        hv = num * (1.0 / jnp.maximum(jnp.abs(den), jnp.exp(-m_comb)))
        emit(h, hv)
        dec = btot - bc + igc
        inter_end = btot + m_prev
        m_new = jnp.maximum(inter_end, jnp.max(dec, axis=0, keepdims=True))
        w_end = jnp.exp(dec - m_new)
        w_prev = jnp.exp(inter_end - m_new)
        wv = jnp.concatenate([(w_end * v.astype(F32)).astype(BF16),
                              jnp.broadcast_to(w_end, (L, LANES)).astype(BF16)], axis=1)
        ct_ref[h] = w_prev * ct_prev + _dot(kt, wv)
        m_ref[h:h + 1, :] = jnp.broadcast_to(m_new, (1, LANES))


def _mlstm_kernel(*refs, reverse, nb, merge):
    (q_ref, kt_ref, v_ref, gc_ref, gt_ref, qm_ref, ktm_ref, vm_ref, gcm_ref, gtm_ref) = refs[:10]
    if merge:
        o_ref, om_ref, hb_ref, hbm_ref, ng_ref, out_ref, outm_ref, ct_ref, m_ref = refs[10:]
    else:
        out_ref, outm_ref, ct_ref, m_ref = refs[10:]
    j = pl.program_id(1)

    @pl.when(j == 0)
    def _():
        ct_ref[...] = jnp.zeros(ct_ref.shape, F32)
        m_ref[...] = jnp.full(m_ref.shape, M_INIT, F32)

    def make_emit(dst_ref, other_ref, gate_ref):
        def emit(h, hv):
            sl = slice(h * ML_DV, (h + 1) * ML_DV)
            if not merge:
                dst_ref[:, sl] = hv
                return
            hs = hv + other_ref[:, sl]
            mu = jnp.mean(hs, axis=-1, keepdims=True)
            hc = hs - mu
            var = jnp.mean(hc * hc, axis=-1, keepdims=True)
            hn = hc * lax.rsqrt(var + LN_EPS) * ng_ref[:, sl]
            dst_ref[:, sl] = (hn * gate_ref[:, sl].astype(F32)).astype(dst_ref.dtype)
        return emit

    is_meta = (j == nb) if reverse else (j == 0)

    @pl.when(is_meta)
    def _():
        emit = make_emit(outm_ref, hbm_ref if merge else None, om_ref if merge else None)
        _mlstm_chunk(N_META, reverse, qm_ref, ktm_ref, vm_ref, gcm_ref, gtm_ref, ct_ref, m_ref, emit)

    @pl.when(jnp.logical_not(is_meta))
    def _():
        emit = make_emit(out_ref, hb_ref if merge else None, o_ref if merge else None)
        _mlstm_chunk(CHUNK, reverse, q_ref, kt_ref, v_ref, gc_ref, gt_ref, ct_ref, m_ref, emit)


def _mlstm(pm, kt, gc, gt, pm_m, kt_m, gc_m, gt_m, batch, nb, *, reverse, merged=None):
    t = pm.shape[0]
    if reverse:
        rblk = lambda b, j: b * nb + jnp.maximum(nb - 1 - j, 0)
    else:
        rblk = lambda b, j: b * nb + jnp.maximum(j - 1, 0)
    const = lambda b, j: (0, 0)
    in_specs = [
        pl.BlockSpec((CHUNK, ML_HEADS * ML_DQK), lambda b, j: (rblk(b, j), PM_MQ // (ML_HEADS * ML_DQK))),
        pl.BlockSpec((ML_HEADS * ML_DQK, CHUNK), lambda b, j: (0, rblk(b, j))),
        pl.BlockSpec((CHUNK, D_MODEL), lambda b, j: (rblk(b, j), PM_MV // D_MODEL)),
        pl.BlockSpec((CHUNK, LANES), lambda b, j: (rblk(b, j), 0)),
        pl.BlockSpec((LANES, CHUNK), lambda b, j: (0, rblk(b, j))),
        pl.BlockSpec((N_META, ML_HEADS * ML_DQK), lambda b, j: (0, PM_MQ // (ML_HEADS * ML_DQK))),
        pl.BlockSpec((ML_HEADS * ML_DQK, N_META), const),
        pl.BlockSpec((N_META, D_MODEL), lambda b, j: (0, PM_MV // D_MODEL)),
        pl.BlockSpec((N_META, LANES), const),
        pl.BlockSpec((LANES, N_META), const),
    ]
    args = [pm, kt, pm, gc, gt, pm_m, kt_m, pm_m, gc_m, gt_m]
    merge = merged is not None
    if merge:
        hb, hb_m, ng = merged
        in_specs += [
            pl.BlockSpec((CHUNK, D_MODEL), lambda b, j: (rblk(b, j), PM_MO // D_MODEL)),
            pl.BlockSpec((N_META, D_MODEL), lambda b, j: (0, PM_MO // D_MODEL)),
            pl.BlockSpec((CHUNK, D_MODEL), lambda b, j: (rblk(b, j), 0)),
            pl.BlockSpec((N_META, D_MODEL), lambda b, j: (b, 0)),
            pl.BlockSpec((1, D_MODEL), const),
        ]
        args += [pm, pm_m, hb, hb_m, ng]
    odt = BF16 if merge else F32
    return pl.pallas_call(
        functools.partial(_mlstm_kernel, reverse=reverse, nb=nb, merge=merge),
        grid=(batch, nb + 1),
        in_specs=in_specs,
        out_specs=[
            pl.BlockSpec((CHUNK, D_MODEL), lambda b, j: (rblk(b, j), 0)),
            pl.BlockSpec((N_META, D_MODEL), lambda b, j: (b, 0)),
        ],
        out_shape=[
            jax.ShapeDtypeStruct((t, D_MODEL), odt),
            jax.ShapeDtypeStruct((batch * N_META, D_MODEL), odt),
        ],
        scratch_shapes=[pltpu.VMEM((ML_HEADS, ML_DQK, CT_W), F32), pltpu.VMEM((ML_HEADS, LANES), F32)],
        compiler_params=_cparams(("arbitrary", "arbitrary")),
        name="mlstm_bwd" if reverse else "mlstm_fwd",
    )(*args)


def _softmax_pv(sg, snk, vcat):
    m = jnp.maximum(jnp.max(sg, axis=1, keepdims=True), snk)
    p = jnp.exp(sg - m)
    l = jnp.sum(p, axis=1, keepdims=True) + jnp.exp(snk - m)
    return _dot(p.astype(BF16), vcat) * (1.0 / l)


LOG2E = math.log2(math.e)


def _attn_kernel(q_ref, kp_ref, kc_ref, kn_ref, vp_ref, vc_ref, vn_ref, km_ref, vm_ref,
                 bias_ref, sink_ref, o_ref):
    pad = jnp.zeros((CHUNK - N_META, AT_DH), BF16)
    ones = jnp.ones((4 * CHUNK, LANES), BF16)
    scale2 = AT_DH ** -0.5 * LOG2E
    for c in range(AT_KV_HEADS):
        ks = slice(c * AT_DH, (c + 1) * AT_DH)
        kcat = jnp.concatenate([km_ref[:, ks], pad, kp_ref[:, ks], kc_ref[:, ks], kn_ref[:, ks]], axis=0)
        vcat = jnp.concatenate([vm_ref[:, ks], pad, vp_ref[:, ks], vc_ref[:, ks], vn_ref[:, ks]], axis=0)
        vaug = jnp.concatenate([vcat, ones], axis=1)
        qc = jnp.concatenate(
            [q_ref[:, (c * AT_GROUP + g) * AT_DH:(c * AT_GROUP + g + 1) * AT_DH] for g in range(AT_GROUP)], axis=0)
        t = _dot_nt(qc, kcat) * scale2 + bias_ref[0, c]
        snk = sink_ref[c][:, 0:1]
        m = jnp.maximum(jnp.max(t, axis=1, keepdims=True), snk)
        p = jnp.exp2(t - m).astype(BF16)
        ov = _dot(p, vaug)
        l = ov[:, AT_DH:AT_DH + 1] + jnp.exp2(snk - m)
        o = ov[:, :AT_DH] * (1.0 / l)
        for g in range(AT_GROUP):
            h = c * AT_GROUP + g
            o_ref[:, h * AT_DH:(h + 1) * AT_DH] = o[g * CHUNK:(g + 1) * CHUNK].astype(o_ref.dtype)


def _attn_real(pm, pm_m, bias3, sink_col, batch, nb):
    assert nb >= 2
    t = pm.shape[0]
    kvw = AT_KV_HEADS * AT_DH
    cur = lambda b, j: b * nb + j
    prv = lambda b, j: b * nb + jnp.maximum(j - 1, 0)
    nxt = lambda b, j: b * nb + jnp.minimum(j + 1, nb - 1)
    variant = lambda b, j: (jnp.where(j == 0, 1, jnp.where(j == nb - 1, 2, 0)), 0, 0, 0)
    kcol, vcol = PM_AK // kvw, PM_AV // kvw
    return pl.pallas_call(
        _attn_kernel,
        grid=(batch, nb),
        in_specs=[
            pl.BlockSpec((CHUNK, D_MODEL), lambda b, j: (cur(b, j), PM_AQ // D_MODEL)),
            pl.BlockSpec((CHUNK, kvw), lambda b, j: (prv(b, j), kcol)),
            pl.BlockSpec((CHUNK, kvw), lambda b, j: (cur(b, j), kcol)),
            pl.BlockSpec((CHUNK, kvw), lambda b, j: (nxt(b, j), kcol)),
            pl.BlockSpec((CHUNK, kvw), lambda b, j: (prv(b, j), vcol)),
            pl.BlockSpec((CHUNK, kvw), lambda b, j: (cur(b, j), vcol)),
            pl.BlockSpec((CHUNK, kvw), lambda b, j: (nxt(b, j), vcol)),
            pl.BlockSpec((N_META, kvw), lambda b, j: (0, kcol)),
            pl.BlockSpec((N_META, kvw), lambda b, j: (0, vcol)),
            pl.BlockSpec((1, AT_KV_HEADS, AT_GROUP * CHUNK, 4 * CHUNK), variant),
            pl.BlockSpec((AT_KV_HEADS, AT_GROUP * CHUNK, LANES), lambda b, j: (0, 0, 0)),
        ],
        out_specs=pl.BlockSpec((CHUNK, D_MODEL), lambda b, j: (cur(b, j), 0)),
        out_shape=jax.ShapeDtypeStruct((t, D_MODEL), BF16),
        compiler_params=_cparams(("arbitrary", "arbitrary")),
        name="attn_real",
    )(pm, pm, pm, pm, pm, pm, pm, pm_m, pm_m, bias3, sink_col)


def _attn_meta_kernel(q_ref, kr_ref, vr_ref, km_ref, vm_ref, bias_ref, sink_ref, o_ref):
    pad = jnp.zeros((CHUNK - N_META, AT_DH), BF16)
    scale = AT_DH ** -0.5
    for c in range(AT_KV_HEADS):
        ks = slice(c * AT_DH, (c + 1) * AT_DH)
        kcat = jnp.concatenate([km_ref[:, ks], pad, kr_ref[:, ks]], axis=0)
        vcat = jnp.concatenate([vm_ref[:, ks], pad, vr_ref[:, ks]], axis=0)
        qc = jnp.concatenate(
            [q_ref[:, (c * AT_GROUP + g) * AT_DH:(c * AT_GROUP + g + 1) * AT_DH] for g in range(AT_GROUP)], axis=0)
        s = _dot_nt(qc, kcat) * scale
        for g in range(AT_GROUP):
            h = c * AT_GROUP + g
            sg = s[g * N_META:(g + 1) * N_META] + bias_ref[h]
            o = _softmax_pv(sg, sink_ref[h:h + 1, 0:1], vcat)
            o_ref[:, h * AT_DH:(h + 1) * AT_DH] = o.astype(o_ref.dtype)


def _attn_meta(pm, pm_m, bias_mq, sink_b, batch, nb):
    kvw = AT_KV_HEADS * AT_DH
    kcol, vcol = PM_AK // kvw, PM_AV // kvw
    return pl.pallas_call(
        _attn_meta_kernel,
        grid=(batch,),
        in_specs=[
            pl.BlockSpec((N_META, D_MODEL), lambda b: (0, PM_AQ // D_MODEL)),
            pl.BlockSpec((CHUNK, kvw), lambda b: (b * nb, kcol)),
            pl.BlockSpec((CHUNK, kvw), lambda b: (b * nb, vcol)),
            pl.BlockSpec((N_META, kvw), lambda b: (0, kcol)),
            pl.BlockSpec((N_META, kvw), lambda b: (0, vcol)),
            pl.BlockSpec((AT_HEADS, N_META, 2 * CHUNK), lambda b: (0, 0, 0)),
            pl.BlockSpec((AT_HEADS, LANES), lambda b: (0, 0)),
        ],
        out_specs=pl.BlockSpec((N_META, D_MODEL), lambda b: (b, 0)),
        out_shape=jax.ShapeDtypeStruct((batch * N_META, D_MODEL), BF16),
        compiler_params=_cparams(("arbitrary",)),
        name="attn_meta",
    )(pm_m, pm, pm, pm_m, pm_m, bias_mq, sink_b)


def _mix_kernel(ha_ref, hb_ref, ga_ref, gb_ref, wa_ref, wb_ref, o_ref):
    a = _dot(ha_ref[...], wa_ref[...])
    b = _dot(hb_ref[...], wb_ref[...])
    o_ref[...] = (ga_ref[...].astype(F32) * a + gb_ref[...].astype(F32) * b).astype(o_ref.dtype)


def _mix(ha, hb, gates_src, wa, wb, tm, ga_col, gb_col):
    t = ha.shape[0]
    nt = D_MODEL // MIX_TN
    return pl.pallas_call(
        _mix_kernel,
        grid=(t // tm, nt),
        in_specs=[
            pl.BlockSpec((tm, D_MODEL), lambda i, n: (i, 0)),
            pl.BlockSpec((tm, D_MODEL), lambda i, n: (i, 0)),
            pl.BlockSpec((tm, MIX_TN), lambda i, n: (i, ga_col + n)),
            pl.BlockSpec((tm, MIX_TN), lambda i, n: (i, gb_col + n)),
            pl.BlockSpec((D_MODEL, MIX_TN), lambda i, n: (0, n)),
            pl.BlockSpec((D_MODEL, MIX_TN), lambda i, n: (0, n)),
        ],
        out_specs=pl.BlockSpec((tm, MIX_TN), lambda i, n: (i, n)),
        out_shape=jax.ShapeDtypeStruct((t, D_MODEL), BF16),
        compiler_params=_cparams(("arbitrary", "arbitrary")),
        name="branch_mix",
    )(ha, hb, gates_src, gates_src, wa, wb)


HALF_D = D_MODEL // 2


def _pack_bf16_pairs(x):
    lo = lax.bitcast_convert_type(x[:, :HALF_D].astype(BF16).astype(F32), I32)
    hi = lax.bitcast_convert_type(x[:, HALF_D:].astype(BF16).astype(F32), I32)
    return lax.shift_right_logical(lo, 16) | (hi & jnp.int32(-65536))


def _unpack_bf16_pairs(u):
    lo = lax.bitcast_convert_type(u << 16, F32)
    hi = lax.bitcast_convert_type(u & jnp.int32(-65536), F32)
    return lo.astype(BF16), hi.astype(BF16)


PACK_SUB = HALF_D // LANES


def _out_kernel(*refs, real):
    mix_ref, x_ref, g0_ref, b0_ref, g1_ref, b1_ref, wo_ref, wrt_ref = refs[:8]
    if real:
        x1_ref, x1p_ref, aff_ref = refs[8:]
    else:
        x1p_ref, aff_ref = refs[9:]

    def body():
        y = _dot(mix_ref[...], wo_ref[...])
        h0 = _ln(x_ref[...], g0_ref[...], b0_ref[...])
        x1 = _ln(ALPHA * h0 + y, g1_ref[...], b1_ref[...])
        packed = _pack_bf16_pairs(x1)
        for c in range(PACK_SUB):
            x1p_ref[:, c, :] = packed[:, c * LANES:(c + 1) * LANES]
        lt = _dot_nt(wrt_ref[...], x1.astype(BF16))
        e = jnp.exp(lt - jnp.max(lt, axis=0, keepdims=True))
        aff = e / jnp.sum(e, axis=0, keepdims=True)
        if real:
            x1_ref[...] = x1
            for c in range(aff.shape[1] // CHUNK):
                aff_ref[c] = aff[:, c * CHUNK:(c + 1) * CHUNK]
        else:
            aff_ref[...] = aff

    if not real:
        body()
        return
    last = pl.program_id(0) == pl.num_programs(0) - 1
    pl.when(jnp.logical_not(last))(body)

    @pl.when(last)
    def _():
        x1p_ref[...] = jnp.zeros(x1p_ref.shape, I32)


def _out_ln_router(mix, x2, g0, b0, g1, b1, wo, wrt, tm, n_meta_rows, packed=None):
    t = mix.shape[0]
    real = packed is None
    n = t // tm
    const = lambda i: (0, 0)
    row = lambda i: (jnp.minimum(i, n - 1), 0)
    in_specs = [
        pl.BlockSpec((tm, D_MODEL), row),
        pl.BlockSpec((tm, D_MODEL), row),
        pl.BlockSpec((1, D_MODEL), const), pl.BlockSpec((1, D_MODEL), const),
        pl.BlockSpec((1, D_MODEL), const), pl.BlockSpec((1, D_MODEL), const),
        pl.BlockSpec((D_MODEL, D_MODEL), const),
        pl.BlockSpec((N_EXPERTS, D_MODEL), const),
    ]
    args = [mix, x2, g0, b0, g1, b1, wo, wrt]
    if real:
        assert n_meta_rows <= tm
        out_specs = [
            pl.BlockSpec((tm, D_MODEL), row),
            pl.BlockSpec((tm, PACK_SUB, LANES), lambda i: (i, 0, 0)),
            pl.BlockSpec((tm // CHUNK, N_EXPERTS, CHUNK), lambda i: (jnp.minimum(i, n - 1), 0, 0)),
        ]
        out_shape = [
            jax.ShapeDtypeStruct((t, D_MODEL), F32),
            jax.ShapeDtypeStruct((t + n_meta_rows, PACK_SUB, LANES), I32),
            jax.ShapeDtypeStruct((t // CHUNK, N_EXPERTS, CHUNK), F32),
        ]
        aliases = {}
    else:
        n_real = packed.shape[0] - t
        assert tm == t and n_real % t == 0
        in_specs.append(pl.BlockSpec(memory_space=pl.ANY))
        args.append(packed)
        out_specs = [
            pl.BlockSpec((t, PACK_SUB, LANES), lambda i: (n_real // t, 0, 0)),
            pl.BlockSpec((N_EXPERTS, t), const),
        ]
        out_shape = [jax.ShapeDtypeStruct(packed.shape, I32), jax.ShapeDtypeStruct((N_EXPERTS, t), F32)]
        aliases = {8: 0}
    return pl.pallas_call(
        functools.partial(_out_kernel, real=real),
        grid=(n + 1 if real else n,),
        in_specs=in_specs,
        out_specs=out_specs,
        out_shape=out_shape,
        input_output_aliases=aliases,
        compiler_params=_cparams(("arbitrary",)),
        name="out_ln1_router" if real else "out_ln1_router_meta",
    )(*args)


def _route_select_kernel(aff_ref, rank_ref, base_ref, p_s, b_s, *, cap, nc):
    aff = aff_ref[...]
    bits = lax.bitcast_convert_type(aff, I32)

    def count(maskf):
        return jnp.sum(jnp.sum(maskf, axis=0), axis=1, keepdims=True)

    def search(it, lo):
        cand = lo | (jnp.int32(1) << (30 - it))
        cnt = count(jnp.where(bits >= cand[None], 1.0, 0.0))
        return jnp.where(cnt >= cap, cand, lo)

    thr = lax.fori_loop(0, 31, search, jnp.zeros((N_EXPERTS, 1), I32))
    gt = bits > thr[None]
    eq = bits == thr[None]
    need = cap - count(jnp.where(gt, 1.0, 0.0))
    r_i = lax.broadcasted_iota(I32, (CHUNK, CHUNK), 0)
    c_i = lax.broadcasted_iota(I32, (CHUNK, CHUNK), 1)
    upper = jnp.where(r_i <= c_i, 1.0, 0.0).astype(BF16)

    def prefix(maskb):
        p = _dot(jnp.where(maskb, 1.0, 0.0).astype(BF16).reshape(nc * N_EXPERTS, CHUNK), upper)
        p_s[...] = p.reshape(nc, N_EXPERTS, CHUNK)

        def step(c, carry):
            b_s[c] = jnp.broadcast_to(carry, (N_EXPERTS, CHUNK))
            return carry + p_s[c][:, CHUNK - 1:CHUNK]

        lax.fori_loop(0, nc, step, jnp.zeros((N_EXPERTS, 1), F32))

    prefix(eq)
    eq_rank = b_s[...] + p_s[...] - 1.0
    sel = gt | (eq & (eq_rank < need[None]))
    prefix(sel)
    rank_ref[...] = jnp.where(sel, p_s[...] - 1.0, -1.0).astype(I32)
    base_ref[...] = b_s[...].astype(I32)


def _route_select(aff, cap):
    nc = aff.shape[0]
    shp = (nc, N_EXPERTS, CHUNK)
    full = pl.BlockSpec(shp, lambda i: (0, 0, 0))
    return pl.pallas_call(
        functools.partial(_route_select_kernel, cap=cap, nc=nc),
        grid=(1,),
        in_specs=[full],
        out_specs=[full, full],
        out_shape=[jax.ShapeDtypeStruct(shp, I32), jax.ShapeDtypeStruct(shp, I32)],
        scratch_shapes=[pltpu.VMEM(shp, F32), pltpu.VMEM(shp, F32)],
        compiler_params=_cparams(("arbitrary",)),
        name="route_select",
    )(aff)


FFN_MAX_ROWS = 2304
FFN_TF = 256
LIST_ROWS = 16


def _route_compact_kernel(base_ref, rank_ref, aff_ref, out_ref):
    c = pl.program_id(0)

    @pl.when(c == 0)
    def _():
        out_ref[...] = jnp.zeros(out_ref.shape, F32)

    lane = lax.broadcasted_iota(I32, (1, CHUNK), 1)
    tok = c * CHUNK + lane
    t_hi = (tok >> 8).astype(F32)
    t_lo = (tok & 255).astype(F32)
    r_io = lax.broadcasted_iota(I32, (CHUNK, CHUNK), 0)
    zeros = jnp.zeros((LIST_ROWS - 5, CHUNK), F32)
    for e in range(N_EXPERTS):
        rk = rank_ref[0, e:e + 1, :]
        onehot = jnp.where(r_io == rk, 1.0, 0.0).astype(BF16)
        a_hi, a_mid, a_lo = _split3(aff_ref[0, e:e + 1, :])
        vals = jnp.concatenate([t_hi, t_lo, a_hi.astype(F32), a_mid.astype(F32), a_lo.astype(F32), zeros],
                               axis=0).astype(BF16)
        comp = _dot_nt(vals, onehot)
        base = base_ref[c, e]
        blk = base // CHUNK
        off = base - blk * CHUNK
        rolled = pltpu.roll(comp, off, axis=1)
        keep_lo = lane >= off
        out_ref[e, blk] = jnp.where(keep_lo, rolled, out_ref[e, blk])
        out_ref[e, blk + 1] = jnp.where(keep_lo, out_ref[e, blk + 1], rolled)


def _route_compact(base_s, rank, aff, n_blk):
    nc = rank.shape[0]
    blk = pl.BlockSpec((1, N_EXPERTS, CHUNK), lambda c, b: (c, 0, 0))
    return pl.pallas_call(
        _route_compact_kernel,
        grid_spec=pltpu.PrefetchScalarGridSpec(
            num_scalar_prefetch=1, grid=(nc,),
            in_specs=[blk, blk],
            out_specs=pl.BlockSpec((N_EXPERTS, n_blk, LIST_ROWS, CHUNK), lambda c, b: (0, 0, 0, 0))),
        out_shape=jax.ShapeDtypeStruct((N_EXPERTS, n_blk, LIST_ROWS, CHUNK), F32),
        compiler_params=_cparams(("arbitrary",)),
        name="route_compact",
    )(base_s, rank, aff)


def _ffn_kernel(idx_ref, idxn_ref, gate_ref, xp_hbm, wg_ref, wu_ref, wd_ref, ye_ref,
                rows_ref, xp_ref, hid_ref, sem, *, tm, rs, nf, tf, n_tiles):
    e, s, j = pl.program_id(0), pl.program_id(1), pl.program_id(2)
    tile = e * pl.num_programs(1) + s
    n_sub = tm // rs
    per_slot = tm // (nf * n_sub)

    def relayout(c):
        return pltpu.make_async_copy(rows_ref.at[:, c, :], xp_ref.at[:, pl.ds(c * LANES, LANES)], sem.at[1])

    def start_relayout():
        for c in range(PACK_SUB):
            relayout(c).start()

    def wait_relayout():
        for c in range(PACK_SUB):
            relayout(c).wait()

    def row_copy(iref, r):
        return pltpu.make_async_copy(xp_hbm.at[pl.ds(iref[0, 0, r], 1)], rows_ref.at[pl.ds(r, 1)], sem.at[0])

    def wait_rows():
        pltpu.make_async_copy(xp_hbm.at[pl.ds(0, tm)], rows_ref, sem.at[0]).wait()

    def prefetch_next(r):
        first = (j * n_sub + r) * per_slot
        for k in range(per_slot):
            row_copy(idxn_ref, first + k).start()

    def rows(r):
        return pl.ds(pl.multiple_of(r * rs, BF16_ROWS), rs)

    @pl.when((tile == 0) & (j == 0))
    def _():
        def body(r, carry):
            row_copy(idx_ref, r).start()
            return carry

        lax.fori_loop(0, tm, body, 0)
        wait_rows()
        start_relayout()

    @pl.when(j == 0)
    def _():
        wait_relayout()

    @pl.when(j < nf)
    def _():
        wg = wg_ref[0].astype(BF16)
        wu = wu_ref[0].astype(BF16)

        def body(r, carry):
            prefetch_next(r)
            lo, hi = _unpack_bf16_pairs(xp_ref[rows(r), :])
            x = jnp.concatenate([lo, hi], axis=1)
            g = _dot(x, wg)
            u = _dot(x, wu)
            hid_ref[j, rows(r), :] = (g * jax.nn.sigmoid(g) * u).astype(BF16)
            return carry

        lax.fori_loop(0, n_sub, body, 0)

    @pl.when(j == nf)
    def _():
        wait_rows()
        start_relayout()

    @pl.when(j >= nf)
    def _():
        wd = wd_ref[0].astype(BF16)

        def body(r, carry):
            acc = _dot(hid_ref[0, rows(r), :], wd[0:tf])
            for f in range(1, nf):
                acc = acc + _dot(hid_ref[f, rows(r), :], wd[f * tf:(f + 1) * tf])
            ye_ref[0, rows(r), :] = (acc * gate_ref[0, rows(r), :]).astype(ye_ref.dtype)
            return carry

        lax.fori_loop(0, n_sub, body, 0)

    @pl.when((tile == n_tiles - 1) & (j == 2 * nf - 1))
    def _():
        wait_relayout()


def _expert_ffn(idx3, gate3, xp, wg, wu, wd, cap_p, tm, tf):
    ns = cap_p // tm
    nf = EXPERT_FF // tf
    n_tiles = N_EXPERTS * ns
    rs = tm // 4
    assert rs % BF16_ROWS == 0 and D_MODEL // tf == nf and tm % (nf * (tm // rs)) == 0
    return pl.pallas_call(
        functools.partial(_ffn_kernel, tm=tm, rs=rs, nf=nf, tf=tf, n_tiles=n_tiles),
        grid=(N_EXPERTS, ns, 2 * nf),
        in_specs=[
            pl.BlockSpec((1, 1, tm), lambda e, s, j: (e * ns + s, 0, 0), memory_space=pltpu.SMEM),
            pl.BlockSpec((1, 1, tm), lambda e, s, j: (jnp.minimum(e * ns + s + 1, n_tiles - 1), 0, 0),
                         memory_space=pltpu.SMEM),
            pl.BlockSpec((1, tm, 1), lambda e, s, j: (e, s, 0)),
            pl.BlockSpec(memory_space=pl.ANY),
            pl.BlockSpec((1, D_MODEL, tf), lambda e, s, j: (e, 0, jnp.minimum(j, nf - 1))),
            pl.BlockSpec((1, D_MODEL, tf), lambda e, s, j: (e, 0, jnp.minimum(j, nf - 1))),
            pl.BlockSpec((1, EXPERT_FF, tf), lambda e, s, j: (e, 0, jnp.maximum(j - nf, 0))),
        ],
        out_specs=pl.BlockSpec((1, tm, tf), lambda e, s, j: (e, s, jnp.maximum(j - nf, 0))),
        out_shape=jax.ShapeDtypeStruct((N_EXPERTS, cap_p, D_MODEL), BF16),
        scratch_shapes=[
            pltpu.VMEM((tm, PACK_SUB, LANES), I32),
            pltpu.VMEM((tm, HALF_D), I32),
            pltpu.VMEM((nf, tm, tf), BF16),
            pltpu.SemaphoreType.DMA((2,)),
        ],
        compiler_params=_cparams(("arbitrary", "arbitrary", "arbitrary")),
        name="expert_ffn",
    )(idx3, idx3, gate3, xp, wg, wu, wd)


COMB_ROWS = CHUNK + BF16_ROWS


COMB_WIN = 64


def _combine_kernel(base_ref, x1_ref, rkt_ref, ye_hbm, g2_ref, b2_ref, y_ref, buf, big, acc_ref, sem, bsem,
                    *, n_tiles, cap_p):
    i = pl.program_id(0)

    def window(c, e):
        b = base_ref[c, e]
        st = jnp.minimum((b // BF16_ROWS) * BF16_ROWS, cap_p - COMB_WIN)
        fits = base_ref[c + 1, e] - st <= COMB_WIN
        return b, st, fits

    def copy(c, e, slot):
        st = pl.multiple_of(window(c, e)[1], BF16_ROWS)
        return pltpu.make_async_copy(ye_hbm.at[e, pl.ds(st, COMB_WIN), :],
                                     buf.at[slot, pl.ds(e * COMB_WIN, COMB_WIN), :], sem.at[slot])

    @pl.when(i == 0)
    def _():
        for e in range(N_EXPERTS):
            copy(0, e, 0).start()

    @pl.when(i + 1 < n_tiles)
    def _():
        for e in range(N_EXPERTS):
            copy(i + 1, e, (i + 1) % 2).start()

    slot = i % 2
    rkt = rkt_ref[0]
    lane = lax.broadcasted_iota(I32, (1, LANES), 1)
    per_tile = LANES // COMB_WIN
    pieces = []
    for p in range(N_EXPERTS // per_tile):
        target = jnp.full((CHUNK, 1), -1, I32)
        for q in range(per_tile):
            e = p * per_tile + q
            b, st, fits = window(i, e)
            rk = rkt[:, e:e + 1]
            pos = jnp.where((rk >= 0) & fits, rk + (b - st + q * COMB_WIN), -1)
            target = jnp.where((lane >= q * COMB_WIN) & (lane < (q + 1) * COMB_WIN), pos, target) if q else pos
        pieces.append(jnp.where(target == lane, 1.0, 0.0).astype(BF16))
    onehot = jnp.concatenate(pieces, axis=1)
    pltpu.make_async_copy(buf.at[slot], buf.at[slot], sem.at[slot]).wait()
    acc_ref[...] = ALPHA * x1_ref[...] + _dot(onehot, buf[slot])

    r_io = lax.broadcasted_iota(I32, (1, COMB_ROWS), 1)
    for e in range(N_EXPERTS):
        b, _, fits = window(i, e)

        @pl.when(jnp.logical_not(fits))
        def _():
            st = pl.multiple_of(jnp.minimum((b // BF16_ROWS) * BF16_ROWS, cap_p - COMB_ROWS), BF16_ROWS)
            cp = pltpu.make_async_copy(ye_hbm.at[e, pl.ds(st, COMB_ROWS), :], big, bsem.at[0])
            cp.start()
            cp.wait()
            rk = rkt[:, e:e + 1]
            srel = jnp.where(rk >= 0, rk + (b - st), -1)
            acc_ref[...] += _dot(jnp.where(srel == r_io, 1.0, 0.0).astype(BF16), big[...])

    y_ref[...] = _ln(acc_ref[...], g2_ref[...], b2_ref[...])


def _combine(base_s, x1, rank_t, ye, g2, b2, cap_p):
    assert base_s.shape[0] > x1.shape[0] // CHUNK
    t = x1.shape[0]
    n_tiles = t // CHUNK
    const = lambda i, b: (0, 0)
    return pl.pallas_call(
        functools.partial(_combine_kernel, n_tiles=n_tiles, cap_p=cap_p),
        grid_spec=pltpu.PrefetchScalarGridSpec(
            num_scalar_prefetch=1, grid=(n_tiles,),
            in_specs=[
                pl.BlockSpec((CHUNK, D_MODEL), lambda i, b: (i, 0)),
                pl.BlockSpec((1, CHUNK, N_EXPERTS), lambda i, b: (i, 0, 0)),
                pl.BlockSpec(memory_space=pl.ANY),
                pl.BlockSpec((1, D_MODEL), const), pl.BlockSpec((1, D_MODEL), const),
            ],
            out_specs=pl.BlockSpec((CHUNK, D_MODEL), lambda i, b: (i, 0)),
            scratch_shapes=[
                pltpu.VMEM((2, N_EXPERTS * COMB_WIN, D_MODEL), BF16),
                pltpu.VMEM((COMB_ROWS, D_MODEL), BF16),
                pltpu.VMEM((CHUNK, D_MODEL), F32),
                pltpu.SemaphoreType.DMA((2,)),
                pltpu.SemaphoreType.DMA((1,)),
            ]),
        out_shape=jax.ShapeDtypeStruct((t, D_MODEL), F32),
        compiler_params=_cparams(("arbitrary",)),
        name="moe_combine_ln2",
    )(base_s, x1, rank_t, ye, g2, b2)


def _row(v):
    return v.reshape(1, -1).astype(F32)


def _attn_bias_variants(bias_main, bias_meta0):
    dead = jnp.full((AT_HEADS, CHUNK, CHUNK), NEG, F32)
    first = jnp.concatenate([bias_meta0, dead, bias_main[:, :, 2 * CHUNK:]], axis=2)
    last = jnp.concatenate([bias_main[:, :, :3 * CHUNK], dead], axis=2)
    tabs = jnp.stack([bias_main, first, last]) * LOG2E
    return tabs.reshape(3, AT_KV_HEADS, AT_GROUP * CHUNK, 4 * CHUNK)


def _pick_tile(n, pref):
    tm = pref
    while n % tm:
        tm //= 2
    return tm


def _trunk(x, p, meta):
    batch, seq, _ = x.shape
    nb = seq // CHUNK
    t = batch * seq
    bm = batch * N_META
    x2 = x.reshape(t, D_MODEL)
    pm_m, kt_m, gc_m, gt_m = meta
    pm, kt, gc, gt = _ln_proj(x2, p["g0"], p["b0"], p["w_main"], p["w_kt"], p["w_g"], p["w_gt"], p["bg"], p["bgt"],
                              _pick_tile(t, 512))
    hb, hb_m = _mlstm(pm, kt, gc, gt, pm_m, kt_m, gc_m, gt_m, batch, nb, reverse=True)
    ha, ha_m = _mlstm(pm, kt, gc, gt, pm_m, kt_m, gc_m, gt_m, batch, nb, reverse=False,
                      merged=(hb, hb_m, p["ng"]))
    at = _attn_real(pm, pm_m, p["bias3"], p["sink_col"], batch, nb)
    at_m = _attn_meta(pm, pm_m, p["bias_mq"], p["sink_b"], batch, nb)

    mix = _mix(ha, at, pm, p["wa"], p["wb"], _pick_tile(t, 512), PM_GA // MIX_TN, PM_GB // MIX_TN)
    gates_m = jnp.tile(pm_m[:, PM_GA:PM_MQ], (batch, 1))
    mix_m = _mix(ha_m, at_m, gates_m, p["wa"], p["wb"], bm, 0, D_MODEL // MIX_TN)

    ln_args = (p["g0"], p["b0"], p["g1"], p["b1"], p["wo"], p["wrt"])
    x1, xp, aff = _out_ln_router(mix, x2, *ln_args, _pick_tile(t, 256), bm)
    xm = jnp.tile(p["meta_tokens"], (batch, 1))
    xp, aff_m = _out_ln_router(mix_m, xm, *ln_args, bm, bm, packed=xp)

    n_tok = t + bm
    nc = -(-n_tok // CHUNK)
    aff_m = jnp.pad(aff_m, ((0, 0), (0, nc * CHUNK - n_tok)), constant_values=-1.0)
    aff_m = aff_m.reshape(N_EXPERTS, -1, CHUNK).transpose(1, 0, 2)
    aff_all = jnp.concatenate([aff, aff_m], axis=0)
    cap = CAPACITY_FACTOR * n_tok // N_EXPERTS
    cap_p = -(-cap // CHUNK) * CHUNK
    rank, base = _route_select(aff_all, cap)
    base_s = base[:, :, 0]
    n_blk = cap_p // CHUNK + 2
    lists = _route_compact(base_s, rank, aff_all, n_blk)
    lists = lists.transpose(0, 2, 1, 3).reshape(N_EXPERTS, LIST_ROWS, n_blk * CHUNK)[:, :, :cap_p]
    idx = (lists[:, 0] * 256.0 + lists[:, 1]).astype(I32)
    gate = lists[:, 2] + lists[:, 3] + lists[:, 4]
    ns = -(-cap_p // FFN_MAX_ROWS)
    tm = cap_p // ns
    ye = _expert_ffn(idx.reshape(N_EXPERTS * ns, 1, tm), gate.reshape(N_EXPERTS, cap_p, 1), xp,
                     p["wgate"], p["wup"], p["wdown"], cap_p, tm, FFN_TF)
    rank_t = rank[:t // CHUNK].transpose(0, 2, 1)
    y = _combine(base_s, x1, rank_t, ye, p["g2"], p["b2"], cap_p)
    return y.reshape(batch, seq, D_MODEL)


def kernel(x_prompt, x_sample, meta_tokens, ln0_g, ln0_b, rel_bias, attn_sink, w_in, b_gate, ml_norm_g,
           w_branch_a, w_branch_b, w_out, ln1_g, ln1_b, w_router, w_gate, w_up, w_down, ln2_g, ln2_b):
    assert w_in.shape[0] == DEPTH
    w = w_in[0]
    sizes = (ML_HEADS * ML_DQK, ML_HEADS * ML_DQK, ML_HEADS * ML_DV, ML_HEADS * ML_DV, 4 * ML_HEADS,
             AT_HEADS * AT_DH, AT_KV_HEADS * AT_DH, AT_KV_HEADS * AT_DH, D_MODEL, D_MODEL)
    offs = np.concatenate([[0], np.cumsum(sizes)])
    mq, mk, mv, mo, mg, aq, ak, av, ga, gb = [w[:, offs[i]:offs[i + 1]] for i in range(10)]
    w_g = jnp.pad(mg, ((0, 0), (0, LANES - 4 * ML_HEADS))).astype(BF16)
    bg = jnp.pad(b_gate[0].astype(F32), (0, LANES - 4 * ML_HEADS))
    nb_max = max(x_prompt.shape[1], x_sample.shape[1]) // CHUNK
    bk_main, bk_meta0, bk_mq = _bucket_tables(nb_max)
    rb = rel_bias.astype(F32)
    p = {
        "g0": _row(ln0_g), "b0": _row(ln0_b), "g1": _row(ln1_g[0]), "b1": _row(ln1_b[0]),
        "g2": _row(ln2_g[0]), "b2": _row(ln2_b[0]), "ng": _row(ml_norm_g[0]),
        "w_main": jnp.concatenate([mv, mo, aq, ga, gb, mq, ak, av], axis=1).astype(BF16),
        "w_kt": mk.T.astype(BF16), "w_g": w_g, "w_gt": w_g.T,
        "bg": bg.reshape(1, LANES), "bgt": bg.reshape(LANES, 1),
        "wa": w_branch_a[0].astype(BF16), "wb": w_branch_b[0].astype(BF16), "wo": w_out[0].astype(BF16),
        "wrt": w_router[0].T.astype(BF16),
        "wgate": w_gate[0], "wup": w_up[0], "wdown": w_down[0],
        "bias3": _attn_bias_variants(_bias_table(rb, bk_main), _bias_table(rb, bk_meta0)),
        "sink_col": jnp.broadcast_to(
            jnp.repeat(attn_sink[0].astype(F32).reshape(AT_KV_HEADS, AT_GROUP), CHUNK, axis=1)[:, :, None] * LOG2E,
            (AT_KV_HEADS, AT_GROUP * CHUNK, LANES)),
        "bias_mq": _bias_table(rb, bk_mq),
        "sink_b": jnp.broadcast_to(attn_sink[0].astype(F32)[:, None], (AT_HEADS, LANES)),
        "meta_tokens": meta_tokens.astype(F32),
    }
    meta = _ln_proj(p["meta_tokens"], p["g0"], p["b0"], p["w_main"], p["w_kt"], p["w_g"], p["w_gt"],
                    p["bg"], p["bgt"], N_META)
    return (_trunk(x_prompt, p, meta), _trunk(x_sample, p, meta))
```

```python
import functools
import math

import numpy as np
import jax
import jax.numpy as jnp
from jax import lax
from jax.experimental import pallas as pl
from jax.experimental.pallas import tpu as pltpu

F32 = jnp.float32
BF16 = jnp.bfloat16
I32 = jnp.int32

D_MODEL = 2048
N_META = 16
CHUNK = 128
ML_HEADS = 8
ML_DV = D_MODEL // ML_HEADS
ML_DQK = ML_DV // 2
AT_DH = 128
AT_HEADS = D_MODEL // AT_DH
AT_KV_HEADS = AT_HEADS // 4
AT_GROUP = AT_HEADS // AT_KV_HEADS
WINDOW = 128
REL_BUCKETS = 32
REL_MAX_DIST = 128
N_EXPERTS = 16
EXPERT_FF = D_MODEL
CAPACITY_FACTOR = 2
DEPTH = 1
ALPHA = (2.0 * DEPTH) ** 0.25
LN_EPS = 1e-5
M_INIT = -1e30
NEG = -1e30

LANES = 128
BF16_ROWS = 16
VMEM_LIMIT = 56 * 1024 * 1024

PM_MV, PM_MO, PM_AQ, PM_GA, PM_GB, PM_MQ, PM_AK, PM_AV = 0, 2048, 4096, 6144, 8192, 10240, 11264, 11776
PM_WIDTH = 12288
PM_TN = 2048
MIX_TN = 1024
CT_W = ML_DV + LANES

NT_DIMS = (((1,), (1,)), ((), ()))


def _cparams(sem, vmem=VMEM_LIMIT):
    return pltpu.CompilerParams(dimension_semantics=sem, vmem_limit_bytes=vmem)


def _dot(a, b):
    return jnp.dot(a, b, preferred_element_type=F32)


def _dot_nt(a, b):
    return lax.dot_general(a, b, NT_DIMS, preferred_element_type=F32)


def _ln(x, g, b):
    mu = jnp.mean(x, axis=-1, keepdims=True)
    xc = x - mu
    var = jnp.mean(xc * xc, axis=-1, keepdims=True)
    return xc * lax.rsqrt(var + LN_EPS) * g + b


def _split3(x):
    hi = x.astype(BF16)
    r1 = x - hi.astype(F32)
    mid = r1.astype(BF16)
    lo = (r1 - mid.astype(F32)).astype(BF16)
    return hi, mid, lo


def _bias_kernel(rb_ref, bk_ref, o_ref):
    h = pl.program_id(0)
    bk = bk_ref[...]
    acc = jnp.full(bk.shape, NEG, F32)
    for b in range(REL_BUCKETS):
        acc = jnp.where(bk == b, rb_ref[b, h], acc)
    o_ref[0] = acc


def _bias_table(rel_bias, bucket):
    r, c = bucket.shape
    return pl.pallas_call(
        _bias_kernel,
        grid_spec=pltpu.PrefetchScalarGridSpec(
            num_scalar_prefetch=1, grid=(AT_HEADS,),
            in_specs=[pl.BlockSpec((r, c), lambda h, rb: (0, 0))],
            out_specs=pl.BlockSpec((1, r, c), lambda h, rb: (h, 0, 0))),
        out_shape=jax.ShapeDtypeStruct((AT_HEADS, r, c), F32),
        compiler_params=_cparams(("arbitrary",)),
        name="bias_table",
    )(rel_bias, bucket)


def _t5_bucket(rel):
    half = REL_BUCKETS // 2
    max_exact = half // 2
    n = jnp.abs(rel)
    nf = jnp.maximum(n, 1).astype(jnp.float32)
    large = max_exact + (jnp.log(nf / max_exact) / math.log(REL_MAX_DIST / max_exact)
                         * (half - max_exact)).astype(jnp.int32)
    large = jnp.minimum(large, half - 1)
    return jnp.where(rel > 0, half, 0) + jnp.where(n < max_exact, n, large)


def _t5_bucket_np(rel):
    half = REL_BUCKETS // 2
    max_exact = half // 2
    n = np.abs(rel)
    nf = np.maximum(n, 1).astype(np.float64)
    large = max_exact + (np.log(nf / max_exact) / math.log(REL_MAX_DIST / max_exact) * (half - max_exact)).astype(np.int64)
    large = np.minimum(large, half - 1)
    return np.where(rel > 0, half, 0) + np.where(n < max_exact, n, large)


def _bucket_tables(nb_max):
    i = np.arange(CHUNK)[:, None]
    m = np.arange(N_META)[None, :]
    ref_tab = _t5_bucket_np(m - (N_META + CHUNK + i))
    for j in range(1, nb_max):
        assert np.array_equal(_t5_bucket_np(m - (N_META + j * CHUNK + i)), ref_tab)
    qi = jnp.arange(CHUNK, dtype=I32)[:, None]
    c = jnp.arange(4 * CHUNK, dtype=I32)[None, :]
    rel_meta = c - (N_META + CHUNK + qi)
    rel_nb = (c - CHUNK) - CHUNK - qi
    vis_nb = (c >= CHUNK) & (jnp.abs(rel_nb) <= WINDOW)
    main = jnp.where(c < N_META, _t5_bucket(rel_meta), jnp.where(vis_nb, _t5_bucket(rel_nb), -1))
    c1 = jnp.arange(CHUNK, dtype=I32)[None, :]
    meta0 = jnp.where(c1 < N_META, _t5_bucket(c1 - (N_META + qi)), -1)
    mi = jnp.arange(N_META, dtype=I32)[:, None]
    c2 = jnp.arange(2 * CHUNK, dtype=I32)[None, :]
    rel_r = N_META + (c2 - CHUNK) - mi
    vis_r = (c2 >= CHUNK) & (jnp.abs(rel_r) <= WINDOW)
    mq = jnp.where(c2 < N_META, _t5_bucket(c2 - mi), jnp.where(vis_r, _t5_bucket(rel_r), -1))
    return main.astype(I32), meta0.astype(I32), mq.astype(I32)


def _proj_kernel(x_ref, g0_ref, b0_ref, w_ref, wkt_ref, wg_ref, wgt_ref, bg_ref, bgt_ref,
                 pm_ref, kt_ref, gc_ref, gt_ref, xs_ref):
    n = pl.program_id(1)

    @pl.when(n == 0)
    def _():
        xn = _ln(x_ref[...], g0_ref[...], b0_ref[...]).astype(BF16)
        xs_ref[...] = xn
        kt_ref[...] = _dot_nt(wkt_ref[...], xn).astype(BF16)
        gc_ref[...] = _dot(xn, wg_ref[...]) + bg_ref[...]
        gt_ref[...] = _dot_nt(wgt_ref[...], xn) + bgt_ref[...]

    y = _dot(xs_ref[...], w_ref[...])
    lo1, hi1 = PM_MO // PM_TN, PM_AQ // PM_TN
    lo2, hi2 = PM_GA // PM_TN, PM_MQ // PM_TN
    is_sig = ((n >= lo1) & (n < hi1)) | ((n >= lo2) & (n < hi2))

    @pl.when(is_sig)
    def _():
        pm_ref[...] = jax.nn.sigmoid(y).astype(BF16)

    @pl.when(jnp.logical_not(is_sig))
    def _():
        pm_ref[...] = y.astype(BF16)


def _ln_proj(x2, g0, b0, w_main, w_kt, w_g, w_gt, bg, bgt, tm):
    t = x2.shape[0]
    grid = (t // tm, PM_WIDTH // PM_TN)
    const = lambda i, n: (0, 0)
    return pl.pallas_call(
        _proj_kernel,
        grid=grid,
        in_specs=[
            pl.BlockSpec((tm, D_MODEL), lambda i, n: (i, 0)),
            pl.BlockSpec((1, D_MODEL), const),
            pl.BlockSpec((1, D_MODEL), const),
            pl.BlockSpec((D_MODEL, PM_TN), lambda i, n: (0, n)),
            pl.BlockSpec((ML_HEADS * ML_DQK, D_MODEL), const),
            pl.BlockSpec((D_MODEL, LANES), const),
            pl.BlockSpec((LANES, D_MODEL), const),
            pl.BlockSpec((1, LANES), const),
            pl.BlockSpec((LANES, 1), const),
        ],
        out_specs=[
            pl.BlockSpec((tm, PM_TN), lambda i, n: (i, n)),
            pl.BlockSpec((ML_HEADS * ML_DQK, tm), lambda i, n: (0, i)),
            pl.BlockSpec((tm, LANES), lambda i, n: (i, 0)),
            pl.BlockSpec((LANES, tm), lambda i, n: (0, i)),
        ],
        out_shape=[
            jax.ShapeDtypeStruct((t, PM_WIDTH), BF16),
            jax.ShapeDtypeStruct((ML_HEADS * ML_DQK, t), BF16),
            jax.ShapeDtypeStruct((t, LANES), F32),
            jax.ShapeDtypeStruct((LANES, t), F32),
        ],
        scratch_shapes=[pltpu.VMEM((tm, D_MODEL), BF16)],
        compiler_params=_cparams(("arbitrary", "arbitrary")),
        name="ln_proj",
    )(x2, g0, b0, w_main, w_kt, w_g, w_gt, bg, bgt)


def _mlstm_chunk(L, reverse, q_ref, kt_ref, v_ref, gc_ref, gt_ref, ct_ref, m_ref, finish):
    H = ML_HEADS
    row = lax.broadcasted_iota(I32, (L, L), 0)
    col = lax.broadcasted_iota(I32, (L, L), 1)
    if reverse:
        tri = row <= col
        tri_t = row >= col
    else:
        tri = row >= col
        tri_t = row <= col
    tri_b = jnp.where(tri, 1.0, 0.0).astype(BF16)
    tri_tb = jnp.where(tri_t, 1.0, 0.0).astype(BF16)
    gc = gc_ref[...]
    gt = gt_ref[...]
    lfc = jax.nn.log_sigmoid(gc)
    lft = jax.nn.log_sigmoid(gt)
    bc_all = sum(_dot(tri_b, p) for p in _split3(lfc))
    br_all = sum(_dot(p, tri_tb) for p in _split3(lft))
    end = 0 if reverse else L - 1
    ci0, cf0 = (2 * H, 3 * H) if reverse else (0, H)
    scale = ML_DQK ** -0.5

    def stack(parts):
        return jnp.concatenate(parts, axis=0)

    def cols(x_all, c0):
        return stack([jnp.broadcast_to(x_all[:, c0 + h:c0 + h + 1], (L, L)) for h in range(H)])

    def rows_b(x_t, c0):
        return stack([jnp.broadcast_to(x_t[c0 + h:c0 + h + 1, :], (L, L)) for h in range(H)])

    def per_head(vals):
        return stack([jnp.broadcast_to(x, (L, L)) for x in vals])

    def rep(x):
        return jnp.broadcast_to(x, (H * L, L))

    def wide(x, n):
        if L == LANES and n % LANES == 0:
            return jnp.concatenate([x] * (n // LANES), axis=1)
        return jnp.broadcast_to(x[:, 0:1], (H * L, n))

    def wide_row(x, n):
        if L == LANES and n % LANES == 0:
            return jnp.concatenate([x] * (n // LANES), axis=1)
        return jnp.broadcast_to(x[:, 0:1], (1, n))

    def head(x, h):
        return x[h * L:(h + 1) * L]

    q = [q_ref[:, h * ML_DQK:(h + 1) * ML_DQK] for h in range(H)]
    kt = [kt_ref[h * ML_DQK:(h + 1) * ML_DQK, :] for h in range(H)]
    v = [v_ref[:, h * ML_DV:(h + 1) * ML_DV] for h in range(H)]
    ct = [ct_ref[h] for h in range(H)]
    m_prev_h = [m_ref[h:h + 1, 0:L] for h in range(H)]
    btot_h = [jnp.broadcast_to(bc_all[end:end + 1, cf0 + h:cf0 + h + 1], (1, L)) for h in range(H)]

    bc = cols(bc_all, cf0)
    igc = cols(gc, ci0)
    m_prev = per_head(m_prev_h)
    btot = per_head(btot_h)
    r_st = lax.broadcasted_iota(I32, (H * L, L), 0) & (L - 1)
    c_st = lax.broadcasted_iota(I32, (H * L, L), 1)
    mask = (r_st <= c_st) if reverse else (r_st >= c_st)
    d = jnp.where(mask, bc + (rows_b(gt, ci0) - rows_b(br_all, cf0)), NEG)
    inter = bc + m_prev
    m_comb = jnp.maximum(inter, rep(jnp.max(d, axis=1, keepdims=True)))
    w = jnp.exp(d - m_comb)
    w_inter = jnp.exp(inter - m_comb)
    s = stack([_dot(q[h], kt[h]) for h in range(H)]) * w * scale
    qc = stack([_dot(q[h], ct[h].astype(BF16)) for h in range(H)]) * scale
    sb = s.astype(BF16)
    num = stack([_dot(head(sb, h), v[h]) for h in range(H)]) + wide(w_inter, ML_DV) * qc[:, :ML_DV]
    den = rep(jnp.sum(s, axis=1, keepdims=True)) + w_inter * wide(rep(qc[:, ML_DV:ML_DV + 1]), L)
    finish(num * wide(1.0 / jnp.maximum(jnp.abs(den), jnp.exp(-m_comb)), ML_DV))

    dec = btot - bc + igc
    inter_end_h = [btot_h[h] + m_prev_h[h] for h in range(H)]
    m_new_h = [jnp.maximum(inter_end_h[h], jnp.max(head(dec, h), axis=0, keepdims=True)) for h in range(H)]
    w_end = jnp.exp(dec - per_head(m_new_h))
    vf = stack([x.astype(F32) for x in v])
    wv = jnp.concatenate([(wide(w_end, ML_DV) * vf).astype(BF16), wide(w_end, LANES).astype(BF16)], axis=1)
    for h in range(H):
        w_prev = jnp.exp(inter_end_h[h] - m_new_h[h])
        ct_ref[h] = wide_row(w_prev, CT_W) * ct[h] + _dot(kt[h], head(wv, h))
        m_ref[h:h + 1, :] = wide_row(m_new_h[h], LANES)


def _mlstm_kernel(*refs, reverse, nb, merge):
    (q_ref, kt_ref, v_ref, gc_ref, gt_ref, qm_ref, ktm_ref, vm_ref, gcm_ref, gtm_ref) = refs[:10]
    if merge:
        o_ref, om_ref, hb_ref, hbm_ref, ng_ref, out_ref, outm_ref, ct_ref, m_ref = refs[10:]
    else:
        out_ref, outm_ref, ct_ref, m_ref = refs[10:]
    j = pl.program_id(1)

    @pl.when(j == 0)
    def _():
        ct_ref[...] = jnp.zeros(ct_ref.shape, F32)
        m_ref[...] = jnp.full(m_ref.shape, M_INIT, F32)

    def make_emit(dst_ref, other_ref, gate_ref):
        def emit(hv):
            L = hv.shape[0] // ML_HEADS
            sls = [slice(h * ML_DV, (h + 1) * ML_DV) for h in range(ML_HEADS)]
            if merge:
                hs = hv + jnp.concatenate([other_ref[:, sl] for sl in sls], axis=0)
                mu = jnp.broadcast_to(jnp.mean(hs, axis=-1, keepdims=True), hs.shape)
                hc = hs - mu
                var = jnp.broadcast_to(jnp.mean(hc * hc, axis=-1, keepdims=True), hs.shape)
                ng = jnp.concatenate([jnp.broadcast_to(ng_ref[:, sl], (L, ML_DV)) for sl in sls], axis=0)
                og = jnp.concatenate([gate_ref[:, sl].astype(F32) for sl in sls], axis=0)
                hv = hc * lax.rsqrt(var + LN_EPS) * (ng * og)
            for h, sl in enumerate(sls):
                dst_ref[:, sl] = hv[h * L:(h + 1) * L].astype(dst_ref.dtype)
        return emit

    is_meta = (j == nb) if reverse else (j == 0)

    @pl.when(is_meta)
    def _():
        emit = make_emit(outm_ref, hbm_ref if merge else None, om_ref if merge else None)
        _mlstm_chunk(N_META, reverse, qm_ref, ktm_ref, vm_ref, gcm_ref, gtm_ref, ct_ref, m_ref, emit)

    @pl.when(jnp.logical_not(is_meta))
    def _():
        emit = make_emit(out_ref, hb_ref if merge else None, o_ref if merge else None)
        _mlstm_chunk(CHUNK, reverse, q_ref, kt_ref, v_ref, gc_ref, gt_ref, ct_ref, m_ref, emit)


def _mlstm(pm, kt, gc, gt, pm_m, kt_m, gc_m, gt_m, batch, nb, *, reverse, merged=None):
    t = pm.shape[0]
    if reverse:
        rblk = lambda b, j: b * nb + jnp.maximum(nb - 1 - j, 0)
    else:
        rblk = lambda b, j: b * nb + jnp.maximum(j - 1, 0)
    const = lambda b, j: (0, 0)
    in_specs = [
        pl.BlockSpec((CHUNK, ML_HEADS * ML_DQK), lambda b, j: (rblk(b, j), PM_MQ // (ML_HEADS * ML_DQK))),
        pl.BlockSpec((ML_HEADS * ML_DQK, CHUNK), lambda b, j: (0, rblk(b, j))),
        pl.BlockSpec((CHUNK, D_MODEL), lambda b, j: (rblk(b, j), PM_MV // D_MODEL)),
        pl.BlockSpec((CHUNK, LANES), lambda b, j: (rblk(b, j), 0)),
        pl.BlockSpec((LANES, CHUNK), lambda b, j: (0, rblk(b, j))),
        pl.BlockSpec((N_META, ML_HEADS * ML_DQK), lambda b, j: (0, PM_MQ // (ML_HEADS * ML_DQK))),
        pl.BlockSpec((ML_HEADS * ML_DQK, N_META), const),
        pl.BlockSpec((N_META, D_MODEL), lambda b, j: (0, PM_MV // D_MODEL)),
        pl.BlockSpec((N_META, LANES), const),
        pl.BlockSpec((LANES, N_META), const),
    ]
    args = [pm, kt, pm, gc, gt, pm_m, kt_m, pm_m, gc_m, gt_m]
    merge = merged is not None
    if merge:
        hb, hb_m, ng = merged
        in_specs += [
            pl.BlockSpec((CHUNK, D_MODEL), lambda b, j: (rblk(b, j), PM_MO // D_MODEL)),
            pl.BlockSpec((N_META, D_MODEL), lambda b, j: (0, PM_MO // D_MODEL)),
            pl.BlockSpec((CHUNK, D_MODEL), lambda b, j: (rblk(b, j), 0)),
            pl.BlockSpec((N_META, D_MODEL), lambda b, j: (b, 0)),
            pl.BlockSpec((1, D_MODEL), const),
        ]
        args += [pm, pm_m, hb, hb_m, ng]
    odt = BF16 if merge else F32
    return pl.pallas_call(
        functools.partial(_mlstm_kernel, reverse=reverse, nb=nb, merge=merge),
        grid=(batch, nb + 1),
        in_specs=in_specs,
        out_specs=[
            pl.BlockSpec((CHUNK, D_MODEL), lambda b, j: (rblk(b, j), 0)),
            pl.BlockSpec((N_META, D_MODEL), lambda b, j: (b, 0)),
        ],
        out_shape=[
            jax.ShapeDtypeStruct((t, D_MODEL), odt),
            jax.ShapeDtypeStruct((batch * N_META, D_MODEL), odt),
        ],
        scratch_shapes=[pltpu.VMEM((ML_HEADS, ML_DQK, CT_W), F32), pltpu.VMEM((ML_HEADS, LANES), F32)],
        compiler_params=_cparams(("arbitrary", "arbitrary")),
        name="mlstm_bwd" if reverse else "mlstm_fwd",
    )(*args)


def _softmax_pv(sg, snk, vcat):
    m = jnp.maximum(jnp.max(sg, axis=1, keepdims=True), snk)
    p = jnp.exp(sg - m)
    l = jnp.sum(p, axis=1, keepdims=True) + jnp.exp(snk - m)
    return _dot(p.astype(BF16), vcat) * (1.0 / l)


LOG2E = math.log2(math.e)


def _attn_kernel(q_ref, kp_ref, kc_ref, kn_ref, vp_ref, vc_ref, vn_ref, km_ref, vm_ref,
                 bias_ref, sink_ref, o_ref):
    pad = jnp.zeros((CHUNK - N_META, AT_DH), BF16)
    ones = jnp.ones((4 * CHUNK, LANES), BF16)
    scale2 = AT_DH ** -0.5 * LOG2E
    for c in range(AT_KV_HEADS):
        ks = slice(c * AT_DH, (c + 1) * AT_DH)
        kcat = jnp.concatenate([km_ref[:, ks], pad, kp_ref[:, ks], kc_ref[:, ks], kn_ref[:, ks]], axis=0)
        vcat = jnp.concatenate([vm_ref[:, ks], pad, vp_ref[:, ks], vc_ref[:, ks], vn_ref[:, ks]], axis=0)
        vaug = jnp.concatenate([vcat, ones], axis=1)
        qc = jnp.concatenate(
            [q_ref[:, (c * AT_GROUP + g) * AT_DH:(c * AT_GROUP + g + 1) * AT_DH] for g in range(AT_GROUP)], axis=0)
        t = _dot_nt(qc, kcat) * scale2 + bias_ref[0, c]
        snk = sink_ref[c][:, 0:1]
        m = jnp.maximum(jnp.max(t, axis=1, keepdims=True), snk)
        p = jnp.exp2(t - m).astype(BF16)
        ov = _dot(p, vaug)
        l = ov[:, AT_DH:AT_DH + 1] + jnp.exp2(snk - m)
        o = ov[:, :AT_DH] * (1.0 / l)
        for g in range(AT_GROUP):
            h = c * AT_GROUP + g
            o_ref[:, h * AT_DH:(h + 1) * AT_DH] = o[g * CHUNK:(g + 1) * CHUNK].astype(o_ref.dtype)


def _attn_real(pm, pm_m, bias3, sink_col, batch, nb):
    assert nb >= 2
    t = pm.shape[0]
    kvw = AT_KV_HEADS * AT_DH
    cur = lambda b, j: b * nb + j
    prv = lambda b, j: b * nb + jnp.maximum(j - 1, 0)
    nxt = lambda b, j: b * nb + jnp.minimum(j + 1, nb - 1)
    variant = lambda b, j: (jnp.where(j == 0, 1, jnp.where(j == nb - 1, 2, 0)), 0, 0, 0)
    kcol, vcol = PM_AK // kvw, PM_AV // kvw
    return pl.pallas_call(
        _attn_kernel,
        grid=(batch, nb),
        in_specs=[
            pl.BlockSpec((CHUNK, D_MODEL), lambda b, j: (cur(b, j), PM_AQ // D_MODEL)),
            pl.BlockSpec((CHUNK, kvw), lambda b, j: (prv(b, j), kcol)),
            pl.BlockSpec((CHUNK, kvw), lambda b, j: (cur(b, j), kcol)),
            pl.BlockSpec((CHUNK, kvw), lambda b, j: (nxt(b, j), kcol)),
            pl.BlockSpec((CHUNK, kvw), lambda b, j: (prv(b, j), vcol)),
            pl.BlockSpec((CHUNK, kvw), lambda b, j: (cur(b, j), vcol)),
            pl.BlockSpec((CHUNK, kvw), lambda b, j: (nxt(b, j), vcol)),
            pl.BlockSpec((N_META, kvw), lambda b, j: (0, kcol)),
            pl.BlockSpec((N_META, kvw), lambda b, j: (0, vcol)),
            pl.BlockSpec((1, AT_KV_HEADS, AT_GROUP * CHUNK, 4 * CHUNK), variant),
            pl.BlockSpec((AT_KV_HEADS, AT_GROUP * CHUNK, LANES), lambda b, j: (0, 0, 0)),
        ],
        out_specs=pl.BlockSpec((CHUNK, D_MODEL), lambda b, j: (cur(b, j), 0)),
        out_shape=jax.ShapeDtypeStruct((t, D_MODEL), BF16),
        compiler_params=_cparams(("arbitrary", "arbitrary")),
        name="attn_real",
    )(pm, pm, pm, pm, pm, pm, pm, pm_m, pm_m, bias3, sink_col)


def _attn_meta_kernel(q_ref, kr_ref, vr_ref, km_ref, vm_ref, bias_ref, sink_ref, o_ref):
    pad = jnp.zeros((CHUNK - N_META, AT_DH), BF16)
    scale = AT_DH ** -0.5
    for c in range(AT_KV_HEADS):
        ks = slice(c * AT_DH, (c + 1) * AT_DH)
        kcat = jnp.concatenate([km_ref[:, ks], pad, kr_ref[:, ks]], axis=0)
        vcat = jnp.concatenate([vm_ref[:, ks], pad, vr_ref[:, ks]], axis=0)
        qc = jnp.concatenate(
            [q_ref[:, (c * AT_GROUP + g) * AT_DH:(c * AT_GROUP + g + 1) * AT_DH] for g in range(AT_GROUP)], axis=0)
        s = _dot_nt(qc, kcat) * scale
        for g in range(AT_GROUP):
            h = c * AT_GROUP + g
            sg = s[g * N_META:(g + 1) * N_META] + bias_ref[h]
            o = _softmax_pv(sg, sink_ref[h:h + 1, 0:1], vcat)
            o_ref[:, h * AT_DH:(h + 1) * AT_DH] = o.astype(o_ref.dtype)


def _attn_meta(pm, pm_m, bias_mq, sink_b, batch, nb):
    kvw = AT_KV_HEADS * AT_DH
    kcol, vcol = PM_AK // kvw, PM_AV // kvw
    return pl.pallas_call(
        _attn_meta_kernel,
        grid=(batch,),
        in_specs=[
            pl.BlockSpec((N_META, D_MODEL), lambda b: (0, PM_AQ // D_MODEL)),
            pl.BlockSpec((CHUNK, kvw), lambda b: (b * nb, kcol)),
            pl.BlockSpec((CHUNK, kvw), lambda b: (b * nb, vcol)),
            pl.BlockSpec((N_META, kvw), lambda b: (0, kcol)),
            pl.BlockSpec((N_META, kvw), lambda b: (0, vcol)),
            pl.BlockSpec((AT_HEADS, N_META, 2 * CHUNK), lambda b: (0, 0, 0)),
            pl.BlockSpec((AT_HEADS, LANES), lambda b: (0, 0)),
        ],
        out_specs=pl.BlockSpec((N_META, D_MODEL), lambda b: (b, 0)),
        out_shape=jax.ShapeDtypeStruct((batch * N_META, D_MODEL), BF16),
        compiler_params=_cparams(("arbitrary",)),
        name="attn_meta",
    )(pm_m, pm, pm, pm_m, pm_m, bias_mq, sink_b)


def _mix_kernel(ha_ref, hb_ref, ga_ref, gb_ref, wa_ref, wb_ref, o_ref):
    a = _dot(ha_ref[...], wa_ref[...])
    b = _dot(hb_ref[...], wb_ref[...])
    o_ref[...] = (ga_ref[...].astype(F32) * a + gb_ref[...].astype(F32) * b).astype(o_ref.dtype)


def _mix(ha, hb, gates_src, wa, wb, tm, ga_col, gb_col):
    t = ha.shape[0]
    nt = D_MODEL // MIX_TN
    return pl.pallas_call(
        _mix_kernel,
        grid=(t // tm, nt),
        in_specs=[
            pl.BlockSpec((tm, D_MODEL), lambda i, n: (i, 0)),
            pl.BlockSpec((tm, D_MODEL), lambda i, n: (i, 0)),
            pl.BlockSpec((tm, MIX_TN), lambda i, n: (i, ga_col + n)),
            pl.BlockSpec((tm, MIX_TN), lambda i, n: (i, gb_col + n)),
            pl.BlockSpec((D_MODEL, MIX_TN), lambda i, n: (0, n)),
            pl.BlockSpec((D_MODEL, MIX_TN), lambda i, n: (0, n)),
        ],
        out_specs=pl.BlockSpec((tm, MIX_TN), lambda i, n: (i, n)),
        out_shape=jax.ShapeDtypeStruct((t, D_MODEL), BF16),
        compiler_params=_cparams(("arbitrary", "arbitrary")),
        name="branch_mix",
    )(ha, hb, gates_src, gates_src, wa, wb)


HALF_D = D_MODEL // 2


def _pack_bf16_pairs(x):
    lo = lax.bitcast_convert_type(x[:, :HALF_D].astype(BF16).astype(F32), I32)
    hi = lax.bitcast_convert_type(x[:, HALF_D:].astype(BF16).astype(F32), I32)
    return lax.shift_right_logical(lo, 16) | (hi & jnp.int32(-65536))


def _unpack_bf16_pairs(u):
    lo = lax.bitcast_convert_type(u << 16, F32)
    hi = lax.bitcast_convert_type(u & jnp.int32(-65536), F32)
    return lo.astype(BF16), hi.astype(BF16)


PACK_SUB = HALF_D // LANES


def _out_kernel(*refs, real):
    mix_ref, x_ref, g0_ref, b0_ref, g1_ref, b1_ref, wo_ref, wrt_ref = refs[:8]
    if real:
        x1_ref, x1p_ref, aff_ref = refs[8:]
    else:
        x1p_ref, aff_ref = refs[9:]

    def body():
        y = _dot(mix_ref[...], wo_ref[...])
        h0 = _ln(x_ref[...], g0_ref[...], b0_ref[...])
        x1 = _ln(ALPHA * h0 + y, g1_ref[...], b1_ref[...])
        packed = _pack_bf16_pairs(x1)
        for c in range(PACK_SUB):
            x1p_ref[:, c, :] = packed[:, c * LANES:(c + 1) * LANES]
        lt = _dot_nt(wrt_ref[...], x1.astype(BF16))
        e = jnp.exp(lt - jnp.max(lt, axis=0, keepdims=True))
        aff = e / jnp.sum(e, axis=0, keepdims=True)
        if real:
            x1_ref[...] = x1
            for c in range(aff.shape[1] // CHUNK):
                aff_ref[c] = aff[:, c * CHUNK:(c + 1) * CHUNK]
        else:
            aff_ref[...] = aff

    if not real:
        body()
        return
    last = pl.program_id(0) == pl.num_programs(0) - 1
    pl.when(jnp.logical_not(last))(body)

    @pl.when(last)
    def _():
        x1p_ref[...] = jnp.zeros(x1p_ref.shape, I32)


def _out_ln_router(mix, x2, g0, b0, g1, b1, wo, wrt, tm, n_meta_rows, packed=None):
    t = mix.shape[0]
    real = packed is None
    n = t // tm
    const = lambda i: (0, 0)
    row = lambda i: (jnp.minimum(i, n - 1), 0)
    in_specs = [
        pl.BlockSpec((tm, D_MODEL), row),
        pl.BlockSpec((tm, D_MODEL), row),
        pl.BlockSpec((1, D_MODEL), const), pl.BlockSpec((1, D_MODEL), const),
        pl.BlockSpec((1, D_MODEL), const), pl.BlockSpec((1, D_MODEL), const),
        pl.BlockSpec((D_MODEL, D_MODEL), const),
        pl.BlockSpec((N_EXPERTS, D_MODEL), const),
    ]
    args = [mix, x2, g0, b0, g1, b1, wo, wrt]
    if real:
        assert n_meta_rows <= tm
        out_specs = [
            pl.BlockSpec((tm, D_MODEL), row),
            pl.BlockSpec((tm, PACK_SUB, LANES), lambda i: (i, 0, 0)),
            pl.BlockSpec((tm // CHUNK, N_EXPERTS, CHUNK), lambda i: (jnp.minimum(i, n - 1), 0, 0)),
        ]
        out_shape = [
            jax.ShapeDtypeStruct((t, D_MODEL), F32),
            jax.ShapeDtypeStruct((t + n_meta_rows, PACK_SUB, LANES), I32),
            jax.ShapeDtypeStruct((t // CHUNK, N_EXPERTS, CHUNK), F32),
        ]
        aliases = {}
    else:
        n_real = packed.shape[0] - t
        assert tm == t and n_real % t == 0
        in_specs.append(pl.BlockSpec(memory_space=pl.ANY))
        args.append(packed)
        out_specs = [
            pl.BlockSpec((t, PACK_SUB, LANES), lambda i: (n_real // t, 0, 0)),
            pl.BlockSpec((N_EXPERTS, t), const),
        ]
        out_shape = [jax.ShapeDtypeStruct(packed.shape, I32), jax.ShapeDtypeStruct((N_EXPERTS, t), F32)]
        aliases = {8: 0}
    return pl.pallas_call(
        functools.partial(_out_kernel, real=real),
        grid=(n + 1 if real else n,),
        in_specs=in_specs,
        out_specs=out_specs,
        out_shape=out_shape,
        input_output_aliases=aliases,
        compiler_params=_cparams(("arbitrary",)),
        name="out_ln1_router" if real else "out_ln1_router_meta",
    )(*args)


def _route_select_kernel(aff_ref, rank_ref, base_ref, p_s, b_s, *, cap, nc):
    aff = aff_ref[...]
    bits = lax.bitcast_convert_type(aff, I32)

    def count(maskf):
        return jnp.sum(jnp.sum(maskf, axis=0), axis=1, keepdims=True)

    def search(it, lo):
        cand = lo | (jnp.int32(1) << (30 - it))
        cnt = count(jnp.where(bits >= cand[None], 1.0, 0.0))
        return jnp.where(cnt >= cap, cand, lo)

    thr = lax.fori_loop(0, 31, search, jnp.zeros((N_EXPERTS, 1), I32))
    gt = bits > thr[None]
    eq = bits == thr[None]
    need = cap - count(jnp.where(gt, 1.0, 0.0))
    r_i = lax.broadcasted_iota(I32, (CHUNK, CHUNK), 0)
    c_i = lax.broadcasted_iota(I32, (CHUNK, CHUNK), 1)
    upper = jnp.where(r_i <= c_i, 1.0, 0.0).astype(BF16)

    def prefix(maskb):
        p = _dot(jnp.where(maskb, 1.0, 0.0).astype(BF16).reshape(nc * N_EXPERTS, CHUNK), upper)
        p_s[...] = p.reshape(nc, N_EXPERTS, CHUNK)

        def step(c, carry):
            b_s[c] = jnp.broadcast_to(carry, (N_EXPERTS, CHUNK))
            return carry + p_s[c][:, CHUNK - 1:CHUNK]

        lax.fori_loop(0, nc, step, jnp.zeros((N_EXPERTS, 1), F32))

    prefix(eq)
    eq_rank = b_s[...] + p_s[...] - 1.0
    sel = gt | (eq & (eq_rank < need[None]))
    prefix(sel)
    rank_ref[...] = jnp.where(sel, p_s[...] - 1.0, -1.0).astype(I32)
    base_ref[...] = b_s[...].astype(I32)


def _route_select(aff, cap):
    nc = aff.shape[0]
    shp = (nc, N_EXPERTS, CHUNK)
    full = pl.BlockSpec(shp, lambda i: (0, 0, 0))
    return pl.pallas_call(
        functools.partial(_route_select_kernel, cap=cap, nc=nc),
        grid=(1,),
        in_specs=[full],
        out_specs=[full, full],
        out_shape=[jax.ShapeDtypeStruct(shp, I32), jax.ShapeDtypeStruct(shp, I32)],
        scratch_shapes=[pltpu.VMEM(shp, F32), pltpu.VMEM(shp, F32)],
        compiler_params=_cparams(("arbitrary",)),
        name="route_select",
    )(aff)


FFN_MAX_ROWS = 2304
FFN_TF = 256
FFN_ROW_SPLIT = 2
RELAYOUT_BLOCKS = 8
LIST_ROWS = 16


def _route_compact_kernel(base_ref, rank_ref, aff_ref, out_ref):
    c = pl.program_id(0)

    @pl.when(c == 0)
    def _():
        out_ref[...] = jnp.zeros(out_ref.shape, F32)

    lane = lax.broadcasted_iota(I32, (1, CHUNK), 1)
    tok = c * CHUNK + lane
    t_hi = (tok >> 8).astype(F32)
    t_lo = (tok & 255).astype(F32)
    r_io = lax.broadcasted_iota(I32, (CHUNK, CHUNK), 0)
    zeros = jnp.zeros((LIST_ROWS - 5, CHUNK), F32)
    for e in range(N_EXPERTS):
        rk = rank_ref[0, e:e + 1, :]
        onehot = jnp.where(r_io == rk, 1.0, 0.0).astype(BF16)
        a_hi, a_mid, a_lo = _split3(aff_ref[0, e:e + 1, :])
        vals = jnp.concatenate([t_hi, t_lo, a_hi.astype(F32), a_mid.astype(F32), a_lo.astype(F32), zeros],
                               axis=0).astype(BF16)
        comp = _dot_nt(vals, onehot)
        base = base_ref[c, e]
        blk = base // CHUNK
        off = base - blk * CHUNK
        rolled = pltpu.roll(comp, off, axis=1)
        keep_lo = lane >= off
        out_ref[e, blk] = jnp.where(keep_lo, rolled, out_ref[e, blk])
        out_ref[e, blk + 1] = jnp.where(keep_lo, out_ref[e, blk + 1], rolled)


def _route_compact(base_s, rank, aff, n_blk):
    nc = rank.shape[0]
    blk = pl.BlockSpec((1, N_EXPERTS, CHUNK), lambda c, b: (c, 0, 0))
    return pl.pallas_call(
        _route_compact_kernel,
        grid_spec=pltpu.PrefetchScalarGridSpec(
            num_scalar_prefetch=1, grid=(nc,),
            in_specs=[blk, blk],
            out_specs=pl.BlockSpec((N_EXPERTS, n_blk, LIST_ROWS, CHUNK), lambda c, b: (0, 0, 0, 0))),
        out_shape=jax.ShapeDtypeStruct((N_EXPERTS, n_blk, LIST_ROWS, CHUNK), F32),
        compiler_params=_cparams(("arbitrary",)),
        name="route_compact",
    )(base_s, rank, aff)


def _ffn_kernel(idx_ref, idxn_ref, gate_ref, xp_hbm, wg_ref, wu_ref, wd_ref, ye_ref,
                rows_ref, xp_ref, hid_ref, sem, *, tm, rs, nf, tf):
    e, s, j = pl.program_id(0), pl.program_id(1), pl.program_id(2)
    tile = e * pl.num_programs(1) + s
    n_sub = tm // rs
    per_slot = tm // (nf * n_sub)

    def relayout():
        rb = tm // RELAYOUT_BLOCKS

        def body(r, carry):
            sl = pl.ds(pl.multiple_of(r * rb, 8), rb)
            y = pltpu.einshape("tcl->ctl", rows_ref[sl])
            for c in range(PACK_SUB):
                xp_ref[sl, c * LANES:(c + 1) * LANES] = y[c]
            return carry

        lax.fori_loop(0, RELAYOUT_BLOCKS, body, 0)

    def row_copy(iref, r):
        return pltpu.make_async_copy(xp_hbm.at[pl.ds(iref[0, 0, r], 1)], rows_ref.at[pl.ds(r, 1)], sem.at[0])

    def wait_rows():
        pltpu.make_async_copy(xp_hbm.at[pl.ds(0, tm)], rows_ref, sem.at[0]).wait()

    def prefetch_next(r):
        first = (j * n_sub + r) * per_slot
        for k in range(per_slot):
            row_copy(idxn_ref, first + k).start()

    def rows(r):
        return pl.ds(pl.multiple_of(r * rs, BF16_ROWS), rs)

    @pl.when((tile == 0) & (j == 0))
    def _():
        def body(r, carry):
            row_copy(idx_ref, r).start()
            return carry

        lax.fori_loop(0, tm, body, 0)
        wait_rows()
        relayout()

    @pl.when(j < nf)
    def _():
        wg = wg_ref[0].astype(BF16)
        wu = wu_ref[0].astype(BF16)

        def body(r, carry):
            prefetch_next(r)
            lo, hi = _unpack_bf16_pairs(xp_ref[rows(r), :])
            x = jnp.concatenate([lo, hi], axis=1)
            g = _dot(x, wg)
            u = _dot(x, wu)
            hid_ref[j, rows(r), :] = (g * jax.nn.sigmoid(g) * u).astype(BF16)
            return carry

        lax.fori_loop(0, n_sub, body, 0)

    @pl.when(j == nf)
    def _():
        wait_rows()
        relayout()

    @pl.when(j >= nf)
    def _():
        wd = wd_ref[0].astype(BF16)

        def body(r, carry):
            acc = _dot(hid_ref[0, rows(r), :], wd[0:tf])
            for f in range(1, nf):
                acc = acc + _dot(hid_ref[f, rows(r), :], wd[f * tf:(f + 1) * tf])
            ye_ref[0, rows(r), :] = (acc * gate_ref[0, rows(r), :]).astype(ye_ref.dtype)
            return carry

        lax.fori_loop(0, n_sub, body, 0)


def _expert_ffn(idx3, gate3, xp, wg, wu, wd, cap_p, tm, tf):
    ns = cap_p // tm
    nf = EXPERT_FF // tf
    n_tiles = N_EXPERTS * ns
    rs = tm // FFN_ROW_SPLIT
    assert rs % BF16_ROWS == 0 and D_MODEL // tf == nf and tm % (nf * (tm // rs)) == 0
    assert tm % (8 * RELAYOUT_BLOCKS) == 0
    return pl.pallas_call(
        functools.partial(_ffn_kernel, tm=tm, rs=rs, nf=nf, tf=tf),
        grid=(N_EXPERTS, ns, 2 * nf),
        in_specs=[
            pl.BlockSpec((1, 1, tm), lambda e, s, j: (e * ns + s, 0, 0), memory_space=pltpu.SMEM),
            pl.BlockSpec((1, 1, tm), lambda e, s, j: (jnp.minimum(e * ns + s + 1, n_tiles - 1), 0, 0),
                         memory_space=pltpu.SMEM),
            pl.BlockSpec((1, tm, 1), lambda e, s, j: (e, s, 0)),
            pl.BlockSpec(memory_space=pl.ANY),
            pl.BlockSpec((1, D_MODEL, tf), lambda e, s, j: (e, 0, jnp.minimum(j, nf - 1))),
            pl.BlockSpec((1, D_MODEL, tf), lambda e, s, j: (e, 0, jnp.minimum(j, nf - 1))),
            pl.BlockSpec((1, EXPERT_FF, tf), lambda e, s, j: (e, 0, jnp.maximum(j - nf, 0))),
        ],
        out_specs=pl.BlockSpec((1, tm, tf), lambda e, s, j: (e, s, jnp.maximum(j - nf, 0))),
        out_shape=jax.ShapeDtypeStruct((N_EXPERTS, cap_p, D_MODEL), BF16),
        scratch_shapes=[
            pltpu.VMEM((tm, PACK_SUB, LANES), I32),
            pltpu.VMEM((tm, HALF_D), I32),
            pltpu.VMEM((nf, tm, tf), BF16),
            pltpu.SemaphoreType.DMA((1,)),
        ],
        compiler_params=_cparams(("arbitrary", "arbitrary", "arbitrary")),
        name="expert_ffn",
    )(idx3, idx3, gate3, xp, wg, wu, wd)


COMB_ROWS = CHUNK + BF16_ROWS


COMB_WIN = 64


def _combine_kernel(base_ref, x1_ref, rkt_ref, ye_hbm, g2_ref, b2_ref, y_ref, buf, big, acc_ref, sem, bsem,
                    *, n_tiles, cap_p):
    i = pl.program_id(0)

    def window(c, e):
        b = base_ref[c, e]
        st = jnp.minimum((b // BF16_ROWS) * BF16_ROWS, cap_p - COMB_WIN)
        fits = base_ref[c + 1, e] - st <= COMB_WIN
        return b, st, fits

    def copy(c, e, slot):
        st = pl.multiple_of(window(c, e)[1], BF16_ROWS)
        return pltpu.make_async_copy(ye_hbm.at[e, pl.ds(st, COMB_WIN), :],
                                     buf.at[slot, pl.ds(e * COMB_WIN, COMB_WIN), :], sem.at[slot])

    @pl.when(i == 0)
    def _():
        for e in range(N_EXPERTS):
            copy(0, e, 0).start()

    @pl.when(i + 1 < n_tiles)
    def _():
        for e in range(N_EXPERTS):
            copy(i + 1, e, (i + 1) % 2).start()

    slot = i % 2
    rkt = rkt_ref[0]
    lane = lax.broadcasted_iota(I32, (1, LANES), 1)
    per_tile = LANES // COMB_WIN
    pieces = []
    for p in range(N_EXPERTS // per_tile):
        target = jnp.full((CHUNK, 1), -1, I32)
        for q in range(per_tile):
            e = p * per_tile + q
            b, st, fits = window(i, e)
            rk = rkt[:, e:e + 1]
            pos = jnp.where((rk >= 0) & fits, rk + (b - st + q * COMB_WIN), -1)
            target = jnp.where((lane >= q * COMB_WIN) & (lane < (q + 1) * COMB_WIN), pos, target) if q else pos
        pieces.append(jnp.where(target == lane, 1.0, 0.0).astype(BF16))
    onehot = jnp.concatenate(pieces, axis=1)
    pltpu.make_async_copy(buf.at[slot], buf.at[slot], sem.at[slot]).wait()
    acc_ref[...] = ALPHA * x1_ref[...] + _dot(onehot, buf[slot])

    r_io = lax.broadcasted_iota(I32, (1, COMB_ROWS), 1)
    for e in range(N_EXPERTS):
        b, _, fits = window(i, e)

        @pl.when(jnp.logical_not(fits))
        def _():
            st = pl.multiple_of(jnp.minimum((b // BF16_ROWS) * BF16_ROWS, cap_p - COMB_ROWS), BF16_ROWS)
            cp = pltpu.make_async_copy(ye_hbm.at[e, pl.ds(st, COMB_ROWS), :], big, bsem.at[0])
            cp.start()
            cp.wait()
            rk = rkt[:, e:e + 1]
            srel = jnp.where(rk >= 0, rk + (b - st), -1)
            acc_ref[...] += _dot(jnp.where(srel == r_io, 1.0, 0.0).astype(BF16), big[...])

    y_ref[...] = _ln(acc_ref[...], g2_ref[...], b2_ref[...])


def _combine(base_s, x1, rank_t, ye, g2, b2, cap_p):
    assert base_s.shape[0] > x1.shape[0] // CHUNK
    t = x1.shape[0]
    n_tiles = t // CHUNK
    const = lambda i, b: (0, 0)
    return pl.pallas_call(
        functools.partial(_combine_kernel, n_tiles=n_tiles, cap_p=cap_p),
        grid_spec=pltpu.PrefetchScalarGridSpec(
            num_scalar_prefetch=1, grid=(n_tiles,),
            in_specs=[
                pl.BlockSpec((CHUNK, D_MODEL), lambda i, b: (i, 0)),
                pl.BlockSpec((1, CHUNK, N_EXPERTS), lambda i, b: (i, 0, 0)),
                pl.BlockSpec(memory_space=pl.ANY),
                pl.BlockSpec((1, D_MODEL), const), pl.BlockSpec((1, D_MODEL), const),
            ],
            out_specs=pl.BlockSpec((CHUNK, D_MODEL), lambda i, b: (i, 0)),
            scratch_shapes=[
                pltpu.VMEM((2, N_EXPERTS * COMB_WIN, D_MODEL), BF16),
                pltpu.VMEM((COMB_ROWS, D_MODEL), BF16),
                pltpu.VMEM((CHUNK, D_MODEL), F32),
                pltpu.SemaphoreType.DMA((2,)),
                pltpu.SemaphoreType.DMA((1,)),
            ]),
        out_shape=jax.ShapeDtypeStruct((t, D_MODEL), F32),
        compiler_params=_cparams(("arbitrary",)),
        name="moe_combine_ln2",
    )(base_s, x1, rank_t, ye, g2, b2)


def _row(v):
    return v.reshape(1, -1).astype(F32)


def _attn_bias_variants(bias_main, bias_meta0):
    dead = jnp.full((AT_HEADS, CHUNK, CHUNK), NEG, F32)
    first = jnp.concatenate([bias_meta0, dead, bias_main[:, :, 2 * CHUNK:]], axis=2)
    last = jnp.concatenate([bias_main[:, :, :3 * CHUNK], dead], axis=2)
    tabs = jnp.stack([bias_main, first, last]) * LOG2E
    return tabs.reshape(3, AT_KV_HEADS, AT_GROUP * CHUNK, 4 * CHUNK)


def _pick_tile(n, pref):
    tm = pref
    while n % tm:
        tm //= 2
    return tm


def _trunk(x, p, meta):
    batch, seq, _ = x.shape
    nb = seq // CHUNK
    t = batch * seq
    bm = batch * N_META
    x2 = x.reshape(t, D_MODEL)
    pm_m, kt_m, gc_m, gt_m = meta
    pm, kt, gc, gt = _ln_proj(x2, p["g0"], p["b0"], p["w_main"], p["w_kt"], p["w_g"], p["w_gt"], p["bg"], p["bgt"],
                              _pick_tile(t, 512))
    hb, hb_m = _mlstm(pm, kt, gc, gt, pm_m, kt_m, gc_m, gt_m, batch, nb, reverse=True)
    ha, ha_m = _mlstm(pm, kt, gc, gt, pm_m, kt_m, gc_m, gt_m, batch, nb, reverse=False,
                      merged=(hb, hb_m, p["ng"]))
    at = _attn_real(pm, pm_m, p["bias3"], p["sink_col"], batch, nb)
    at_m = _attn_meta(pm, pm_m, p["bias_mq"], p["sink_b"], batch, nb)

    mix = _mix(ha, at, pm, p["wa"], p["wb"], _pick_tile(t, 512), PM_GA // MIX_TN, PM_GB // MIX_TN)
    gates_m = jnp.tile(pm_m[:, PM_GA:PM_MQ], (batch, 1))
    mix_m = _mix(ha_m, at_m, gates_m, p["wa"], p["wb"], bm, 0, D_MODEL // MIX_TN)

    ln_args = (p["g0"], p["b0"], p["g1"], p["b1"], p["wo"], p["wrt"])
    x1, xp, aff = _out_ln_router(mix, x2, *ln_args, _pick_tile(t, 256), bm)
    xm = jnp.tile(p["meta_tokens"], (batch, 1))
    xp, aff_m = _out_ln_router(mix_m, xm, *ln_args, bm, bm, packed=xp)

    n_tok = t + bm
    nc = -(-n_tok // CHUNK)
    aff_m = jnp.pad(aff_m, ((0, 0), (0, nc * CHUNK - n_tok)), constant_values=-1.0)
    aff_m = aff_m.reshape(N_EXPERTS, -1, CHUNK).transpose(1, 0, 2)
    aff_all = jnp.concatenate([aff, aff_m], axis=0)
    cap = CAPACITY_FACTOR * n_tok // N_EXPERTS
    cap_p = -(-cap // CHUNK) * CHUNK
    rank, base = _route_select(aff_all, cap)
    base_s = base[:, :, 0]
    n_blk = cap_p // CHUNK + 2
    lists = _route_compact(base_s, rank, aff_all, n_blk)
    lists = lists.transpose(0, 2, 1, 3).reshape(N_EXPERTS, LIST_ROWS, n_blk * CHUNK)[:, :, :cap_p]
    idx = (lists[:, 0] * 256.0 + lists[:, 1]).astype(I32)
    gate = lists[:, 2] + lists[:, 3] + lists[:, 4]
    ns = -(-cap_p // FFN_MAX_ROWS)
    tm = cap_p // ns
    ye = _expert_ffn(idx.reshape(N_EXPERTS * ns, 1, tm), gate.reshape(N_EXPERTS, cap_p, 1), xp,
                     p["wgate"], p["wup"], p["wdown"], cap_p, tm, FFN_TF)
    rank_t = rank[:t // CHUNK].transpose(0, 2, 1)
    y = _combine(base_s, x1, rank_t, ye, p["g2"], p["b2"], cap_p)
    return y.reshape(batch, seq, D_MODEL)


def kernel(x_prompt, x_sample, meta_tokens, ln0_g, ln0_b, rel_bias, attn_sink, w_in, b_gate, ml_norm_g,
           w_branch_a, w_branch_b, w_out, ln1_g, ln1_b, w_router, w_gate, w_up, w_down, ln2_g, ln2_b):
    assert w_in.shape[0] == DEPTH
    w = w_in[0]
    sizes = (ML_HEADS * ML_DQK, ML_HEADS * ML_DQK, ML_HEADS * ML_DV, ML_HEADS * ML_DV, 4 * ML_HEADS,
             AT_HEADS * AT_DH, AT_KV_HEADS * AT_DH, AT_KV_HEADS * AT_DH, D_MODEL, D_MODEL)
    offs = np.concatenate([[0], np.cumsum(sizes)])
    mq, mk, mv, mo, mg, aq, ak, av, ga, gb = [w[:, offs[i]:offs[i + 1]] for i in range(10)]
    w_g = jnp.pad(mg, ((0, 0), (0, LANES - 4 * ML_HEADS))).astype(BF16)
    bg = jnp.pad(b_gate[0].astype(F32), (0, LANES - 4 * ML_HEADS))
    nb_max = max(x_prompt.shape[1], x_sample.shape[1]) // CHUNK
    bk_main, bk_meta0, bk_mq = _bucket_tables(nb_max)
    rb = rel_bias.astype(F32)
    p = {
        "g0": _row(ln0_g), "b0": _row(ln0_b), "g1": _row(ln1_g[0]), "b1": _row(ln1_b[0]),
        "g2": _row(ln2_g[0]), "b2": _row(ln2_b[0]), "ng": _row(ml_norm_g[0]),
        "w_main": jnp.concatenate([mv, mo, aq, ga, gb, mq, ak, av], axis=1).astype(BF16),
        "w_kt": mk.T.astype(BF16), "w_g": w_g, "w_gt": w_g.T,
        "bg": bg.reshape(1, LANES), "bgt": bg.reshape(LANES, 1),
        "wa": w_branch_a[0].astype(BF16), "wb": w_branch_b[0].astype(BF16), "wo": w_out[0].astype(BF16),
        "wrt": w_router[0].T.astype(BF16),
        "wgate": w_gate[0], "wup": w_up[0], "wdown": w_down[0],
        "bias3": _attn_bias_variants(_bias_table(rb, bk_main), _bias_table(rb, bk_meta0)),
        "sink_col": jnp.broadcast_to(
            jnp.repeat(attn_sink[0].astype(F32).reshape(AT_KV_HEADS, AT_GROUP), CHUNK, axis=1)[:, :, None] * LOG2E,
            (AT_KV_HEADS, AT_GROUP * CHUNK, LANES)),
        "bias_mq": _bias_table(rb, bk_mq),
        "sink_b": jnp.broadcast_to(attn_sink[0].astype(F32)[:, None], (AT_HEADS, LANES)),
        "meta_tokens": meta_tokens.astype(F32),
    }
    meta = _ln_proj(p["meta_tokens"], p["g0"], p["b0"], p["w_main"], p["w_kt"], p["w_g"], p["w_gt"],
                    p["bg"], p["bgt"], N_META)
    return (_trunk(x_prompt, p, meta), _trunk(x_sample, p, meta))
```

```python
import functools
import math

import numpy as np
import jax
import jax.numpy as jnp
from jax import lax
from jax.experimental import pallas as pl
from jax.experimental.pallas import tpu as pltpu

F32 = jnp.float32
BF16 = jnp.bfloat16
I32 = jnp.int32

D_MODEL = 2048
N_META = 16
CHUNK = 128
ML_HEADS = 8
ML_DV = D_MODEL // ML_HEADS
ML_DQK = ML_DV // 2
AT_DH = 128
AT_HEADS = D_MODEL // AT_DH
AT_KV_HEADS = AT_HEADS // 4
AT_GROUP = AT_HEADS // AT_KV_HEADS
WINDOW = 128
REL_BUCKETS = 32
REL_MAX_DIST = 128
N_EXPERTS = 16
EXPERT_FF = D_MODEL
CAPACITY_FACTOR = 2
DEPTH = 1
ALPHA = (2.0 * DEPTH) ** 0.25
LN_EPS = 1e-5
M_INIT = -1e30
NEG = -1e30

LANES = 128
BF16_ROWS = 16
VMEM_LIMIT = 56 * 1024 * 1024

PM_MV, PM_MO, PM_AQ, PM_GA, PM_GB, PM_MQ, PM_AK, PM_AV = 0, 2048, 4096, 6144, 8192, 10240, 11264, 11776
PM_WIDTH = 12288
PM_TN = 2048
PM_TM = 512
LN_ROWS = 256
MIX_TN = 1024
OUT_TM = 512
CT_W = ML_DV + LANES

NT_DIMS = (((1,), (1,)), ((), ()))


def _cparams(sem, vmem=VMEM_LIMIT):
    return pltpu.CompilerParams(dimension_semantics=sem, vmem_limit_bytes=vmem)


def _dot(a, b):
    return jnp.dot(a, b, preferred_element_type=F32)


def _dot_nt(a, b):
    return lax.dot_general(a, b, NT_DIMS, preferred_element_type=F32)


def _ln(x, g, b):
    mu = jnp.mean(x, axis=-1, keepdims=True)
    xc = x - mu
    var = jnp.mean(xc * xc, axis=-1, keepdims=True)
    return xc * lax.rsqrt(var + LN_EPS) * g + b


def _split3(x):
    hi = x.astype(BF16)
    r1 = x - hi.astype(F32)
    mid = r1.astype(BF16)
    lo = (r1 - mid.astype(F32)).astype(BF16)
    return hi, mid, lo


def _bias_kernel(rb_ref, bk_ref, o_ref):
    h = pl.program_id(0)
    bk = bk_ref[...]
    acc = jnp.full(bk.shape, NEG, F32)
    for b in range(REL_BUCKETS):
        acc = jnp.where(bk == b, rb_ref[b, h], acc)
    o_ref[0] = acc


def _bias_table(rel_bias, bucket):
    r, c = bucket.shape
    return pl.pallas_call(
        _bias_kernel,
        grid_spec=pltpu.PrefetchScalarGridSpec(
            num_scalar_prefetch=1, grid=(AT_HEADS,),
            in_specs=[pl.BlockSpec((r, c), lambda h, rb: (0, 0))],
            out_specs=pl.BlockSpec((1, r, c), lambda h, rb: (h, 0, 0))),
        out_shape=jax.ShapeDtypeStruct((AT_HEADS, r, c), F32),
        compiler_params=_cparams(("arbitrary",)),
        name="bias_table",
    )(rel_bias, bucket)


def _t5_bucket(rel):
    half = REL_BUCKETS // 2
    max_exact = half // 2
    n = jnp.abs(rel)
    nf = jnp.maximum(n, 1).astype(jnp.float32)
    large = max_exact + (jnp.log(nf / max_exact) / math.log(REL_MAX_DIST / max_exact)
                         * (half - max_exact)).astype(jnp.int32)
    large = jnp.minimum(large, half - 1)
    return jnp.where(rel > 0, half, 0) + jnp.where(n < max_exact, n, large)


def _t5_bucket_np(rel):
    half = REL_BUCKETS // 2
    max_exact = half // 2
    n = np.abs(rel)
    nf = np.maximum(n, 1).astype(np.float64)
    large = max_exact + (np.log(nf / max_exact) / math.log(REL_MAX_DIST / max_exact) * (half - max_exact)).astype(np.int64)
    large = np.minimum(large, half - 1)
    return np.where(rel > 0, half, 0) + np.where(n < max_exact, n, large)


def _bucket_tables(nb_max):
    i = np.arange(CHUNK)[:, None]
    m = np.arange(N_META)[None, :]
    ref_tab = _t5_bucket_np(m - (N_META + CHUNK + i))
    for j in range(1, nb_max):
        assert np.array_equal(_t5_bucket_np(m - (N_META + j * CHUNK + i)), ref_tab)
    qi = jnp.arange(CHUNK, dtype=I32)[:, None]
    c = jnp.arange(4 * CHUNK, dtype=I32)[None, :]
    rel_meta = c - (N_META + CHUNK + qi)
    rel_nb = (c - CHUNK) - CHUNK - qi
    vis_nb = (c >= CHUNK) & (jnp.abs(rel_nb) <= WINDOW)
    main = jnp.where(c < N_META, _t5_bucket(rel_meta), jnp.where(vis_nb, _t5_bucket(rel_nb), -1))
    c1 = jnp.arange(CHUNK, dtype=I32)[None, :]
    meta0 = jnp.where(c1 < N_META, _t5_bucket(c1 - (N_META + qi)), -1)
    mi = jnp.arange(N_META, dtype=I32)[:, None]
    c2 = jnp.arange(2 * CHUNK, dtype=I32)[None, :]
    rel_r = N_META + (c2 - CHUNK) - mi
    vis_r = (c2 >= CHUNK) & (jnp.abs(rel_r) <= WINDOW)
    mq = jnp.where(c2 < N_META, _t5_bucket(c2 - mi), jnp.where(vis_r, _t5_bucket(rel_r), -1))
    return main.astype(I32), meta0.astype(I32), mq.astype(I32)


def _proj_kernel(x_ref, g0_ref, b0_ref, w_ref, wkt_ref, wg_ref, wgt_ref, bg_ref, bgt_ref,
                 pm_ref, kt_ref, gc_ref, gt_ref, xs_ref):
    n = pl.program_id(1)

    @pl.when(n == 0)
    def _():
        tm = x_ref.shape[0]
        rb = min(tm, LN_ROWS)

        def ln_block(r, carry):
            sl = pl.ds(pl.multiple_of(r * rb, BF16_ROWS), rb)
            xs_ref[sl, :] = _ln(x_ref[sl, :], g0_ref[...], b0_ref[...]).astype(BF16)
            return carry

        lax.fori_loop(0, tm // rb, ln_block, 0)
        xn = xs_ref[...]
        kt_ref[...] = _dot_nt(wkt_ref[...], xn).astype(BF16)
        gc = _dot(xn, wg_ref[...]) + bg_ref[...]
        gc_ref[...] = gc
        if tm % LANES == 0:
            gt_ref[...] = gc.T
        else:
            gt_ref[...] = _dot_nt(wgt_ref[...], xn) + bgt_ref[...]

    y = _dot(xs_ref[...], w_ref[...])
    lo1, hi1 = PM_MO // PM_TN, PM_AQ // PM_TN
    lo2, hi2 = PM_GA // PM_TN, PM_MQ // PM_TN
    is_sig = ((n >= lo1) & (n < hi1)) | ((n >= lo2) & (n < hi2))
    pm_ref[...] = jnp.where(is_sig, jax.nn.sigmoid(y), y).astype(BF16)


def _ln_proj(x2, g0, b0, w_main, w_kt, w_g, w_gt, bg, bgt, tm):
    t = x2.shape[0]
    grid = (t // tm, PM_WIDTH // PM_TN)
    const = lambda i, n: (0, 0)
    return pl.pallas_call(
        _proj_kernel,
        grid=grid,
        in_specs=[
            pl.BlockSpec((tm, D_MODEL), lambda i, n: (i, 0)),
            pl.BlockSpec((1, D_MODEL), const),
            pl.BlockSpec((1, D_MODEL), const),
            pl.BlockSpec((D_MODEL, PM_TN), lambda i, n: (0, n)),
            pl.BlockSpec((ML_HEADS * ML_DQK, D_MODEL), const, pipeline_mode=pl.Buffered(1)),
            pl.BlockSpec((D_MODEL, LANES), const),
            pl.BlockSpec((LANES, D_MODEL), const),
            pl.BlockSpec((1, LANES), const),
            pl.BlockSpec((LANES, 1), const),
        ],
        out_specs=[
            pl.BlockSpec((tm, PM_TN), lambda i, n: (i, n)),
            pl.BlockSpec((ML_HEADS * ML_DQK, tm), lambda i, n: (0, i)),
            pl.BlockSpec((tm, LANES), lambda i, n: (i, 0)),
            pl.BlockSpec((LANES, tm), lambda i, n: (0, i)),
        ],
        out_shape=[
            jax.ShapeDtypeStruct((t, PM_WIDTH), BF16),
            jax.ShapeDtypeStruct((ML_HEADS * ML_DQK, t), BF16),
            jax.ShapeDtypeStruct((t, LANES), F32),
            jax.ShapeDtypeStruct((LANES, t), F32),
        ],
        scratch_shapes=[pltpu.VMEM((tm, D_MODEL), BF16)],
        compiler_params=_cparams(("arbitrary", "arbitrary")),
        name="ln_proj",
    )(x2, g0, b0, w_main, w_kt, w_g, w_gt, bg, bgt)


def _mlstm_chunk(L, reverse, q_ref, kt_ref, v_ref, gc_ref, gt_ref, ct_ref, m_ref, finish):
    H = ML_HEADS
    row = lax.broadcasted_iota(I32, (L, L), 0)
    col = lax.broadcasted_iota(I32, (L, L), 1)
    if reverse:
        tri = row <= col
        tri_t = row >= col
    else:
        tri = row >= col
        tri_t = row <= col
    tri_b = jnp.where(tri, 1.0, 0.0).astype(BF16)
    tri_tb = jnp.where(tri_t, 1.0, 0.0).astype(BF16)
    gc = gc_ref[...]
    gt = gt_ref[...]
    lfc = jax.nn.log_sigmoid(gc)
    lft = jax.nn.log_sigmoid(gt)
    bc_all = sum(_dot(tri_b, p) for p in _split3(lfc))
    br_all = sum(_dot(p, tri_tb) for p in _split3(lft))
    end = 0 if reverse else L - 1
    ci0, cf0 = (2 * H, 3 * H) if reverse else (0, H)
    scale = ML_DQK ** -0.5

    def stack(parts):
        return jnp.concatenate(parts, axis=0)

    def cols(x_all, c0):
        return stack([jnp.broadcast_to(x_all[:, c0 + h:c0 + h + 1], (L, L)) for h in range(H)])

    def rows_b(x_t, c0):
        return stack([jnp.broadcast_to(x_t[c0 + h:c0 + h + 1, :], (L, L)) for h in range(H)])

    def per_head(vals):
        return stack([jnp.broadcast_to(x, (L, L)) for x in vals])

    def rep(x):
        return jnp.broadcast_to(x, (H * L, L))

    def wide(x, n):
        if L == LANES and n % LANES == 0:
            return jnp.concatenate([x] * (n // LANES), axis=1)
        return jnp.broadcast_to(x[:, 0:1], (H * L, n))

    def wide_row(x, n):
        if L == LANES and n % LANES == 0:
            return jnp.concatenate([x] * (n // LANES), axis=1)
        return jnp.broadcast_to(x[:, 0:1], (1, n))

    def head(x, h):
        return x[h * L:(h + 1) * L]

    q = [q_ref[:, h * ML_DQK:(h + 1) * ML_DQK] for h in range(H)]
    kt = [kt_ref[h * ML_DQK:(h + 1) * ML_DQK, :] for h in range(H)]
    v = [v_ref[:, h * ML_DV:(h + 1) * ML_DV] for h in range(H)]
    ct = [ct_ref[h] for h in range(H)]
    m_prev_h = [m_ref[h:h + 1, 0:L] for h in range(H)]
    btot_h = [jnp.broadcast_to(bc_all[end:end + 1, cf0 + h:cf0 + h + 1], (1, L)) for h in range(H)]

    bc = cols(bc_all, cf0)
    igc = cols(gc, ci0)
    m_prev = per_head(m_prev_h)
    btot = per_head(btot_h)
    r_st = lax.broadcasted_iota(I32, (H * L, L), 0) & (L - 1)
    c_st = lax.broadcasted_iota(I32, (H * L, L), 1)
    mask = (r_st <= c_st) if reverse else (r_st >= c_st)
    d = jnp.where(mask, bc + (rows_b(gt, ci0) - rows_b(br_all, cf0)), NEG)
    inter = bc + m_prev
    m_comb = jnp.maximum(inter, rep(jnp.max(d, axis=1, keepdims=True)))
    w = jnp.exp(d - m_comb)
    w_inter = jnp.exp(inter - m_comb)
    s = stack([_dot(q[h], kt[h]) for h in range(H)]) * w * scale
    qc = stack([_dot(q[h], ct[h].astype(BF16)) for h in range(H)]) * scale
    sb = s.astype(BF16)
    num = stack([_dot(head(sb, h), v[h]) for h in range(H)]) + wide(w_inter, ML_DV) * qc[:, :ML_DV]
    den = rep(jnp.sum(s, axis=1, keepdims=True)) + w_inter * wide(rep(qc[:, ML_DV:ML_DV + 1]), L)
    finish(num * wide(1.0 / jnp.maximum(jnp.abs(den), jnp.exp(-m_comb)), ML_DV))

    dec = btot - bc + igc
    inter_end_h = [btot_h[h] + m_prev_h[h] for h in range(H)]
    m_new_h = [jnp.maximum(inter_end_h[h], jnp.max(head(dec, h), axis=0, keepdims=True)) for h in range(H)]
    w_end = jnp.exp(dec - per_head(m_new_h))
    vf = stack([x.astype(F32) for x in v])
    wv = jnp.concatenate([(wide(w_end, ML_DV) * vf).astype(BF16), wide(w_end, LANES).astype(BF16)], axis=1)
    for h in range(H):
        w_prev = jnp.exp(inter_end_h[h] - m_new_h[h])
        ct_ref[h] = wide_row(w_prev, CT_W) * ct[h] + _dot(kt[h], head(wv, h))
        m_ref[h:h + 1, :] = wide_row(m_new_h[h], LANES)


def _mlstm_kernel(*refs, reverse, nb, merge):
    (q_ref, kt_ref, v_ref, gc_ref, gt_ref, qm_ref, ktm_ref, vm_ref, gcm_ref, gtm_ref) = refs[:10]
    if merge:
        o_ref, om_ref, hb_ref, hbm_ref, ng_ref, out_ref, outm_ref, ct_ref, m_ref = refs[10:]
    else:
        out_ref, outm_ref, ct_ref, m_ref = refs[10:]
    j = pl.program_id(1)

    @pl.when(j == 0)
    def _():
        ct_ref[...] = jnp.zeros(ct_ref.shape, F32)
        m_ref[...] = jnp.full(m_ref.shape, M_INIT, F32)

    def make_emit(dst_ref, other_ref, gate_ref):
        def emit(hv):
            L = hv.shape[0] // ML_HEADS
            sls = [slice(h * ML_DV, (h + 1) * ML_DV) for h in range(ML_HEADS)]
            if merge:
                hs = hv + jnp.concatenate([other_ref[:, sl] for sl in sls], axis=0)
                mu = jnp.broadcast_to(jnp.mean(hs, axis=-1, keepdims=True), hs.shape)
                hc = hs - mu
                var = jnp.broadcast_to(jnp.mean(hc * hc, axis=-1, keepdims=True), hs.shape)
                ng = jnp.concatenate([jnp.broadcast_to(ng_ref[:, sl], (L, ML_DV)) for sl in sls], axis=0)
                og = jnp.concatenate([gate_ref[:, sl].astype(F32) for sl in sls], axis=0)
                hv = hc * lax.rsqrt(var + LN_EPS) * (ng * og)
            for h, sl in enumerate(sls):
                dst_ref[:, sl] = hv[h * L:(h + 1) * L].astype(dst_ref.dtype)
        return emit

    is_meta = (j == nb) if reverse else (j == 0)

    @pl.when(is_meta)
    def _():
        emit = make_emit(outm_ref, hbm_ref if merge else None, om_ref if merge else None)
        _mlstm_chunk(N_META, reverse, qm_ref, ktm_ref, vm_ref, gcm_ref, gtm_ref, ct_ref, m_ref, emit)

    @pl.when(jnp.logical_not(is_meta))
    def _():
        emit = make_emit(out_ref, hb_ref if merge else None, o_ref if merge else None)
        _mlstm_chunk(CHUNK, reverse, q_ref, kt_ref, v_ref, gc_ref, gt_ref, ct_ref, m_ref, emit)


def _mlstm(pm, kt, gc, gt, pm_m, kt_m, gc_m, gt_m, batch, nb, *, reverse, merged=None):
    t = pm.shape[0]
    if reverse:
        rblk = lambda b, j: b * nb + jnp.maximum(nb - 1 - j, 0)
    else:
        rblk = lambda b, j: b * nb + jnp.maximum(j - 1, 0)
    const = lambda b, j: (0, 0)
    in_specs = [
        pl.BlockSpec((CHUNK, ML_HEADS * ML_DQK), lambda b, j: (rblk(b, j), PM_MQ // (ML_HEADS * ML_DQK))),
        pl.BlockSpec((ML_HEADS * ML_DQK, CHUNK), lambda b, j: (0, rblk(b, j))),
        pl.BlockSpec((CHUNK, D_MODEL), lambda b, j: (rblk(b, j), PM_MV // D_MODEL)),
        pl.BlockSpec((CHUNK, LANES), lambda b, j: (rblk(b, j), 0)),
        pl.BlockSpec((LANES, CHUNK), lambda b, j: (0, rblk(b, j))),
        pl.BlockSpec((N_META, ML_HEADS * ML_DQK), lambda b, j: (0, PM_MQ // (ML_HEADS * ML_DQK))),
        pl.BlockSpec((ML_HEADS * ML_DQK, N_META), const),
        pl.BlockSpec((N_META, D_MODEL), lambda b, j: (0, PM_MV // D_MODEL)),
        pl.BlockSpec((N_META, LANES), const),
        pl.BlockSpec((LANES, N_META), const),
    ]
    args = [pm, kt, pm, gc, gt, pm_m, kt_m, pm_m, gc_m, gt_m]
    merge = merged is not None
    if merge:
        hb, hb_m, ng = merged
        in_specs += [
            pl.BlockSpec((CHUNK, D_MODEL), lambda b, j: (rblk(b, j), PM_MO // D_MODEL)),
            pl.BlockSpec((N_META, D_MODEL), lambda b, j: (0, PM_MO // D_MODEL)),
            pl.BlockSpec((CHUNK, D_MODEL), lambda b, j: (rblk(b, j), 0)),
            pl.BlockSpec((N_META, D_MODEL), lambda b, j: (b, 0)),
            pl.BlockSpec((1, D_MODEL), const),
        ]
        args += [pm, pm_m, hb, hb_m, ng]
    odt = BF16 if merge else F32
    return pl.pallas_call(
        functools.partial(_mlstm_kernel, reverse=reverse, nb=nb, merge=merge),
        grid=(batch, nb + 1),
        in_specs=in_specs,
        out_specs=[
            pl.BlockSpec((CHUNK, D_MODEL), lambda b, j: (rblk(b, j), 0)),
            pl.BlockSpec((N_META, D_MODEL), lambda b, j: (b, 0)),
        ],
        out_shape=[
            jax.ShapeDtypeStruct((t, D_MODEL), odt),
            jax.ShapeDtypeStruct((batch * N_META, D_MODEL), odt),
        ],
        scratch_shapes=[pltpu.VMEM((ML_HEADS, ML_DQK, CT_W), F32), pltpu.VMEM((ML_HEADS, LANES), F32)],
        compiler_params=_cparams(("arbitrary", "arbitrary")),
        name="mlstm_bwd" if reverse else "mlstm_fwd",
    )(*args)


def _softmax_pv(sg, snk, vcat):
    m = jnp.maximum(jnp.max(sg, axis=1, keepdims=True), snk)
    p = jnp.exp(sg - m)
    l = jnp.sum(p, axis=1, keepdims=True) + jnp.exp(snk - m)
    return _dot(p.astype(BF16), vcat) * (1.0 / l)


LOG2E = math.log2(math.e)


def _attn_kernel(q_ref, kp_ref, kc_ref, kn_ref, vp_ref, vc_ref, vn_ref, km_ref, vm_ref,
                 bias_ref, sink_ref, o_ref):
    pad = jnp.zeros((CHUNK - N_META, AT_DH), BF16)
    ones = jnp.ones((4 * CHUNK, LANES), BF16)
    scale2 = AT_DH ** -0.5 * LOG2E
    for c in range(AT_KV_HEADS):
        ks = slice(c * AT_DH, (c + 1) * AT_DH)
        kcat = jnp.concatenate([km_ref[:, ks], pad, kp_ref[:, ks], kc_ref[:, ks], kn_ref[:, ks]], axis=0)
        vcat = jnp.concatenate([vm_ref[:, ks], pad, vp_ref[:, ks], vc_ref[:, ks], vn_ref[:, ks]], axis=0)
        vaug = jnp.concatenate([vcat, ones], axis=1)
        qc = jnp.concatenate(
            [q_ref[:, (c * AT_GROUP + g) * AT_DH:(c * AT_GROUP + g + 1) * AT_DH] for g in range(AT_GROUP)], axis=0)
        t = _dot_nt(qc, kcat) * scale2 + bias_ref[0, c]
        snk = sink_ref[c][:, 0:1]
        m = jnp.maximum(jnp.max(t, axis=1, keepdims=True), snk)
        p = jnp.exp2(t - m).astype(BF16)
        ov = _dot(p, vaug)
        l = ov[:, AT_DH:AT_DH + 1] + jnp.exp2(snk - m)
        o = ov[:, :AT_DH] * (1.0 / l)
        for g in range(AT_GROUP):
            h = c * AT_GROUP + g
            o_ref[:, h * AT_DH:(h + 1) * AT_DH] = o[g * CHUNK:(g + 1) * CHUNK].astype(o_ref.dtype)


def _attn_real(pm, pm_m, bias3, sink_col, batch, nb):
    assert nb >= 2
    t = pm.shape[0]
    kvw = AT_KV_HEADS * AT_DH
    cur = lambda b, j: b * nb + j
    prv = lambda b, j: b * nb + jnp.maximum(j - 1, 0)
    nxt = lambda b, j: b * nb + jnp.minimum(j + 1, nb - 1)
    variant = lambda b, j: (jnp.where(j == 0, 1, jnp.where(j == nb - 1, 2, 0)), 0, 0, 0)
    kcol, vcol = PM_AK // kvw, PM_AV // kvw
    return pl.pallas_call(
        _attn_kernel,
        grid=(batch, nb),
        in_specs=[
            pl.BlockSpec((CHUNK, D_MODEL), lambda b, j: (cur(b, j), PM_AQ // D_MODEL)),
            pl.BlockSpec((CHUNK, kvw), lambda b, j: (prv(b, j), kcol)),
            pl.BlockSpec((CHUNK, kvw), lambda b, j: (cur(b, j), kcol)),
            pl.BlockSpec((CHUNK, kvw), lambda b, j: (nxt(b, j), kcol)),
            pl.BlockSpec((CHUNK, kvw), lambda b, j: (prv(b, j), vcol)),
            pl.BlockSpec((CHUNK, kvw), lambda b, j: (cur(b, j), vcol)),
            pl.BlockSpec((CHUNK, kvw), lambda b, j: (nxt(b, j), vcol)),
            pl.BlockSpec((N_META, kvw), lambda b, j: (0, kcol)),
            pl.BlockSpec((N_META, kvw), lambda b, j: (0, vcol)),
            pl.BlockSpec((1, AT_KV_HEADS, AT_GROUP * CHUNK, 4 * CHUNK), variant),
            pl.BlockSpec((AT_KV_HEADS, AT_GROUP * CHUNK, LANES), lambda b, j: (0, 0, 0)),
        ],
        out_specs=pl.BlockSpec((CHUNK, D_MODEL), lambda b, j: (cur(b, j), 0)),
        out_shape=jax.ShapeDtypeStruct((t, D_MODEL), BF16),
        compiler_params=_cparams(("arbitrary", "arbitrary")),
        name="attn_real",
    )(pm, pm, pm, pm, pm, pm, pm, pm_m, pm_m, bias3, sink_col)


def _attn_meta_kernel(q_ref, kr_ref, vr_ref, km_ref, vm_ref, bias_ref, sink_ref, o_ref):
    pad = jnp.zeros((CHUNK - N_META, AT_DH), BF16)
    scale = AT_DH ** -0.5
    for c in range(AT_KV_HEADS):
        ks = slice(c * AT_DH, (c + 1) * AT_DH)
        kcat = jnp.concatenate([km_ref[:, ks], pad, kr_ref[:, ks]], axis=0)
        vcat = jnp.concatenate([vm_ref[:, ks], pad, vr_ref[:, ks]], axis=0)
        qc = jnp.concatenate(
            [q_ref[:, (c * AT_GROUP + g) * AT_DH:(c * AT_GROUP + g + 1) * AT_DH] for g in range(AT_GROUP)], axis=0)
        s = _dot_nt(qc, kcat) * scale
        for g in range(AT_GROUP):
            h = c * AT_GROUP + g
            sg = s[g * N_META:(g + 1) * N_META] + bias_ref[h]
            o = _softmax_pv(sg, sink_ref[h:h + 1, 0:1], vcat)
            o_ref[:, h * AT_DH:(h + 1) * AT_DH] = o.astype(o_ref.dtype)


def _attn_meta(pm, pm_m, bias_mq, sink_b, batch, nb):
    kvw = AT_KV_HEADS * AT_DH
    kcol, vcol = PM_AK // kvw, PM_AV // kvw
    return pl.pallas_call(
        _attn_meta_kernel,
        grid=(batch,),
        in_specs=[
            pl.BlockSpec((N_META, D_MODEL), lambda b: (0, PM_AQ // D_MODEL)),
            pl.BlockSpec((CHUNK, kvw), lambda b: (b * nb, kcol)),
            pl.BlockSpec((CHUNK, kvw), lambda b: (b * nb, vcol)),
            pl.BlockSpec((N_META, kvw), lambda b: (0, kcol)),
            pl.BlockSpec((N_META, kvw), lambda b: (0, vcol)),
            pl.BlockSpec((AT_HEADS, N_META, 2 * CHUNK), lambda b: (0, 0, 0)),
            pl.BlockSpec((AT_HEADS, LANES), lambda b: (0, 0)),
        ],
        out_specs=pl.BlockSpec((N_META, D_MODEL), lambda b: (b, 0)),
        out_shape=jax.ShapeDtypeStruct((batch * N_META, D_MODEL), BF16),
        compiler_params=_cparams(("arbitrary",)),
        name="attn_meta",
    )(pm_m, pm, pm, pm_m, pm_m, bias_mq, sink_b)


def _mix_kernel(ha_ref, hb_ref, ga_ref, gb_ref, wa_ref, wb_ref, o_ref):
    a = _dot(ha_ref[...], wa_ref[...])
    b = _dot(hb_ref[...], wb_ref[...])
    o_ref[...] = (ga_ref[...].astype(F32) * a + gb_ref[...].astype(F32) * b).astype(o_ref.dtype)


def _mix(ha, hb, gates_src, wa, wb, tm, ga_col, gb_col):
    t = ha.shape[0]
    nt = D_MODEL // MIX_TN
    return pl.pallas_call(
        _mix_kernel,
        grid=(t // tm, nt),
        in_specs=[
            pl.BlockSpec((tm, D_MODEL), lambda i, n: (i, 0)),
            pl.BlockSpec((tm, D_MODEL), lambda i, n: (i, 0)),
            pl.BlockSpec((tm, MIX_TN), lambda i, n: (i, ga_col + n)),
            pl.BlockSpec((tm, MIX_TN), lambda i, n: (i, gb_col + n)),
            pl.BlockSpec((D_MODEL, MIX_TN), lambda i, n: (0, n)),
            pl.BlockSpec((D_MODEL, MIX_TN), lambda i, n: (0, n)),
        ],
        out_specs=pl.BlockSpec((tm, MIX_TN), lambda i, n: (i, n)),
        out_shape=jax.ShapeDtypeStruct((t, D_MODEL), BF16),
        compiler_params=_cparams(("arbitrary", "arbitrary")),
        name="branch_mix",
    )(ha, hb, gates_src, gates_src, wa, wb)


HALF_D = D_MODEL // 2


def _pack_bf16_pairs(x):
    lo = lax.bitcast_convert_type(x[:, :HALF_D].astype(BF16).astype(F32), I32)
    hi = lax.bitcast_convert_type(x[:, HALF_D:].astype(BF16).astype(F32), I32)
    return lax.shift_right_logical(lo, 16) | (hi & jnp.int32(-65536))


def _unpack_bf16_pairs(u):
    lo = lax.bitcast_convert_type(u << 16, F32)
    hi = lax.bitcast_convert_type(u & jnp.int32(-65536), F32)
    return lo.astype(BF16), hi.astype(BF16)


PACK_SUB = HALF_D // LANES


def _out_kernel(*refs, real):
    mix_ref, x_ref, g0_ref, b0_ref, g1_ref, b1_ref, wo_ref, wrt_ref = refs[:8]
    if real:
        x1_ref, x1p_ref, aff_ref = refs[8:]
    else:
        x1p_ref, aff_ref = refs[9:]

    def body():
        y = _dot(mix_ref[...], wo_ref[...])
        h0 = _ln(x_ref[...], g0_ref[...], b0_ref[...])
        x1 = _ln(ALPHA * h0 + y, g1_ref[...], b1_ref[...])
        packed = _pack_bf16_pairs(x1)
        for c in range(PACK_SUB):
            x1p_ref[:, c, :] = packed[:, c * LANES:(c + 1) * LANES]
        lt = _dot_nt(wrt_ref[...], x1.astype(BF16))
        e = jnp.exp(lt - jnp.max(lt, axis=0, keepdims=True))
        aff = e / jnp.sum(e, axis=0, keepdims=True)
        if real:
            x1_ref[...] = x1
            for c in range(aff.shape[1] // CHUNK):
                aff_ref[c] = aff[:, c * CHUNK:(c + 1) * CHUNK]
        else:
            aff_ref[...] = aff

    if not real:
        body()
        return
    last = pl.program_id(0) == pl.num_programs(0) - 1
    pl.when(jnp.logical_not(last))(body)

    @pl.when(last)
    def _():
        x1p_ref[...] = jnp.zeros(x1p_ref.shape, I32)


def _out_ln_router(mix, x2, g0, b0, g1, b1, wo, wrt, tm, n_meta_rows, packed=None):
    t = mix.shape[0]
    real = packed is None
    n = t // tm
    const = lambda i: (0, 0)
    row = lambda i: (jnp.minimum(i, n - 1), 0)
    in_specs = [
        pl.BlockSpec((tm, D_MODEL), row),
        pl.BlockSpec((tm, D_MODEL), row),
        pl.BlockSpec((1, D_MODEL), const), pl.BlockSpec((1, D_MODEL), const),
        pl.BlockSpec((1, D_MODEL), const), pl.BlockSpec((1, D_MODEL), const),
        pl.BlockSpec((D_MODEL, D_MODEL), const, pipeline_mode=pl.Buffered(1)),
        pl.BlockSpec((N_EXPERTS, D_MODEL), const),
    ]
    args = [mix, x2, g0, b0, g1, b1, wo, wrt]
    if real:
        assert n_meta_rows <= tm
        out_specs = [
            pl.BlockSpec((tm, D_MODEL), row),
            pl.BlockSpec((tm, PACK_SUB, LANES), lambda i: (i, 0, 0)),
            pl.BlockSpec((tm // CHUNK, N_EXPERTS, CHUNK), lambda i: (jnp.minimum(i, n - 1), 0, 0)),
        ]
        out_shape = [
            jax.ShapeDtypeStruct((t, D_MODEL), F32),
            jax.ShapeDtypeStruct((t + n_meta_rows, PACK_SUB, LANES), I32),
            jax.ShapeDtypeStruct((t // CHUNK, N_EXPERTS, CHUNK), F32),
        ]
        aliases = {}
    else:
        n_real = packed.shape[0] - t
        assert tm == t and n_real % t == 0
        in_specs.append(pl.BlockSpec(memory_space=pl.ANY))
        args.append(packed)
        out_specs = [
            pl.BlockSpec((t, PACK_SUB, LANES), lambda i: (n_real // t, 0, 0)),
            pl.BlockSpec((N_EXPERTS, t), const),
        ]
        out_shape = [jax.ShapeDtypeStruct(packed.shape, I32), jax.ShapeDtypeStruct((N_EXPERTS, t), F32)]
        aliases = {8: 0}
    return pl.pallas_call(
        functools.partial(_out_kernel, real=real),
        grid=(n + 1 if real else n,),
        in_specs=in_specs,
        out_specs=out_specs,
        out_shape=out_shape,
        input_output_aliases=aliases,
        compiler_params=_cparams(("arbitrary",)),
        name="out_ln1_router" if real else "out_ln1_router_meta",
    )(*args)


def _route_select_kernel(aff_ref, rank_ref, base_ref, p_s, b_s, *, cap, nc):
    aff = aff_ref[...]
    bits = lax.bitcast_convert_type(aff, I32)

    def count(maskf):
        return jnp.sum(jnp.sum(maskf, axis=0), axis=1, keepdims=True)

    def search(it, lo):
        cand = lo | (jnp.int32(1) << (30 - it))
        cnt = count(jnp.where(bits >= cand[None], 1.0, 0.0))
        return jnp.where(cnt >= cap, cand, lo)

    thr = lax.fori_loop(0, 31, search, jnp.zeros((N_EXPERTS, 1), I32))
    gt = bits > thr[None]
    eq = bits == thr[None]
    need = cap - count(jnp.where(gt, 1.0, 0.0))
    r_i = lax.broadcasted_iota(I32, (CHUNK, CHUNK), 0)
    c_i = lax.broadcasted_iota(I32, (CHUNK, CHUNK), 1)
    upper = jnp.where(r_i <= c_i, 1.0, 0.0).astype(BF16)

    def prefix(maskb):
        p = _dot(jnp.where(maskb, 1.0, 0.0).astype(BF16).reshape(nc * N_EXPERTS, CHUNK), upper)
        p_s[...] = p.reshape(nc, N_EXPERTS, CHUNK)

        def step(c, carry):
            b_s[c] = jnp.broadcast_to(carry, (N_EXPERTS, CHUNK))
            return carry + p_s[c][:, CHUNK - 1:CHUNK]

        lax.fori_loop(0, nc, step, jnp.zeros((N_EXPERTS, 1), F32))

    prefix(eq)
    eq_rank = b_s[...] + p_s[...] - 1.0
    sel = gt | (eq & (eq_rank < need[None]))
    prefix(sel)
    rank_ref[...] = jnp.where(sel, p_s[...] - 1.0, -1.0).astype(I32)
    base_ref[...] = b_s[...].astype(I32)


def _route_select(aff, cap):
    nc = aff.shape[0]
    shp = (nc, N_EXPERTS, CHUNK)
    full = pl.BlockSpec(shp, lambda i: (0, 0, 0))
    return pl.pallas_call(
        functools.partial(_route_select_kernel, cap=cap, nc=nc),
        grid=(1,),
        in_specs=[full],
        out_specs=[full, full],
        out_shape=[jax.ShapeDtypeStruct(shp, I32), jax.ShapeDtypeStruct(shp, I32)],
        scratch_shapes=[pltpu.VMEM(shp, F32), pltpu.VMEM(shp, F32)],
        compiler_params=_cparams(("arbitrary",)),
        name="route_select",
    )(aff)


FFN_MAX_ROWS = 2304
FFN_TF = 256
FFN_ROW_SPLIT = 2
RELAYOUT_BLOCKS = 8
LIST_ROWS = 16


def _route_compact_kernel(base_ref, rank_ref, aff_ref, out_ref):
    c = pl.program_id(0)

    @pl.when(c == 0)
    def _():
        out_ref[...] = jnp.zeros(out_ref.shape, F32)

    lane = lax.broadcasted_iota(I32, (1, CHUNK), 1)
    tok = c * CHUNK + lane
    t_hi = (tok >> 8).astype(F32)
    t_lo = (tok & 255).astype(F32)
    r_io = lax.broadcasted_iota(I32, (CHUNK, CHUNK), 0)
    zeros = jnp.zeros((LIST_ROWS - 5, CHUNK), F32)
    for e in range(N_EXPERTS):
        rk = rank_ref[0, e:e + 1, :]
        onehot = jnp.where(r_io == rk, 1.0, 0.0).astype(BF16)
        a_hi, a_mid, a_lo = _split3(aff_ref[0, e:e + 1, :])
        vals = jnp.concatenate([t_hi, t_lo, a_hi.astype(F32), a_mid.astype(F32), a_lo.astype(F32), zeros],
                               axis=0).astype(BF16)
        comp = _dot_nt(vals, onehot)
        base = base_ref[c, e]
        blk = base // CHUNK
        off = base - blk * CHUNK
        rolled = pltpu.roll(comp, off, axis=1)
        keep_lo = lane >= off
        out_ref[e, blk] = jnp.where(keep_lo, rolled, out_ref[e, blk])
        out_ref[e, blk + 1] = jnp.where(keep_lo, out_ref[e, blk + 1], rolled)


def _route_compact(base_s, rank, aff, n_blk):
    nc = rank.shape[0]
    blk = pl.BlockSpec((1, N_EXPERTS, CHUNK), lambda c, b: (c, 0, 0))
    return pl.pallas_call(
        _route_compact_kernel,
        grid_spec=pltpu.PrefetchScalarGridSpec(
            num_scalar_prefetch=1, grid=(nc,),
            in_specs=[blk, blk],
            out_specs=pl.BlockSpec((N_EXPERTS, n_blk, LIST_ROWS, CHUNK), lambda c, b: (0, 0, 0, 0))),
        out_shape=jax.ShapeDtypeStruct((N_EXPERTS, n_blk, LIST_ROWS, CHUNK), F32),
        compiler_params=_cparams(("arbitrary",)),
        name="route_compact",
    )(base_s, rank, aff)


def _ffn_kernel(idx_ref, idxn_ref, gate_ref, xp_hbm, wg_ref, wu_ref, wd_ref, ye_ref,
                rows_ref, xp_ref, hid_ref, sem, *, tm, rs, nf, tf):
    e, s, j = pl.program_id(0), pl.program_id(1), pl.program_id(2)
    tile = e * pl.num_programs(1) + s
    n_sub = tm // rs
    per_slot = tm // (nf * n_sub)

    def relayout():
        rb = tm // RELAYOUT_BLOCKS

        def body(r, carry):
            sl = pl.ds(pl.multiple_of(r * rb, 8), rb)
            y = pltpu.einshape("tcl->ctl", rows_ref[sl])
            for c in range(PACK_SUB):
                xp_ref[sl, c * LANES:(c + 1) * LANES] = y[c]
            return carry

        lax.fori_loop(0, RELAYOUT_BLOCKS, body, 0)

    def row_copy(iref, r):
        return pltpu.make_async_copy(xp_hbm.at[pl.ds(iref[0, 0, r], 1)], rows_ref.at[pl.ds(r, 1)], sem.at[0])

    def wait_rows():
        pltpu.make_async_copy(xp_hbm.at[pl.ds(0, tm)], rows_ref, sem.at[0]).wait()

    def prefetch_next(r):
        first = (j * n_sub + r) * per_slot
        for k in range(per_slot):
            row_copy(idxn_ref, first + k).start()

    def rows(r):
        return pl.ds(pl.multiple_of(r * rs, BF16_ROWS), rs)

    @pl.when((tile == 0) & (j == 0))
    def _():
        def body(r, carry):
            row_copy(idx_ref, r).start()
            return carry

        lax.fori_loop(0, tm, body, 0)
        wait_rows()
        relayout()

    @pl.when(j < nf)
    def _():
        wg = wg_ref[0].astype(BF16)
        wu = wu_ref[0].astype(BF16)

        def body(r, carry):
            prefetch_next(r)
            lo, hi = _unpack_bf16_pairs(xp_ref[rows(r), :])
            x = jnp.concatenate([lo, hi], axis=1)
            g = _dot(x, wg)
            u = _dot(x, wu)
            hid_ref[j, rows(r), :] = (g * jax.nn.sigmoid(g) * u).astype(BF16)
            return carry

        lax.fori_loop(0, n_sub, body, 0)

    @pl.when(j == nf)
    def _():
        wait_rows()
        relayout()

    @pl.when(j >= nf)
    def _():
        wd = wd_ref[0].astype(BF16)

        def body(r, carry):
            acc = _dot(hid_ref[0, rows(r), :], wd[0:tf])
            for f in range(1, nf):
                acc = acc + _dot(hid_ref[f, rows(r), :], wd[f * tf:(f + 1) * tf])
            ye_ref[0, rows(r), :] = (acc * gate_ref[0, rows(r), :]).astype(ye_ref.dtype)
            return carry

        lax.fori_loop(0, n_sub, body, 0)


def _expert_ffn(idx3, gate3, xp, wg, wu, wd, cap_p, tm, tf):
    ns = cap_p // tm
    nf = EXPERT_FF // tf
    n_tiles = N_EXPERTS * ns
    rs = tm // FFN_ROW_SPLIT
    assert rs % BF16_ROWS == 0 and D_MODEL // tf == nf and tm % (nf * (tm // rs)) == 0
    assert tm % (8 * RELAYOUT_BLOCKS) == 0
    return pl.pallas_call(
        functools.partial(_ffn_kernel, tm=tm, rs=rs, nf=nf, tf=tf),
        grid=(N_EXPERTS, ns, 2 * nf),
        in_specs=[
            pl.BlockSpec((1, 1, tm), lambda e, s, j: (e * ns + s, 0, 0), memory_space=pltpu.SMEM),
            pl.BlockSpec((1, 1, tm), lambda e, s, j: (jnp.minimum(e * ns + s + 1, n_tiles - 1), 0, 0),
                         memory_space=pltpu.SMEM),
            pl.BlockSpec((1, tm, 1), lambda e, s, j: (e, s, 0)),
            pl.BlockSpec(memory_space=pl.ANY),
            pl.BlockSpec((1, D_MODEL, tf), lambda e, s, j: (e, 0, jnp.minimum(j, nf - 1))),
            pl.BlockSpec((1, D_MODEL, tf), lambda e, s, j: (e, 0, jnp.minimum(j, nf - 1))),
            pl.BlockSpec((1, EXPERT_FF, tf), lambda e, s, j: (e, 0, jnp.maximum(j - nf, 0))),
        ],
        out_specs=pl.BlockSpec((1, tm, tf), lambda e, s, j: (e, s, jnp.maximum(j - nf, 0))),
        out_shape=jax.ShapeDtypeStruct((N_EXPERTS, cap_p, D_MODEL), BF16),
        scratch_shapes=[
            pltpu.VMEM((tm, PACK_SUB, LANES), I32),
            pltpu.VMEM((tm, HALF_D), I32),
            pltpu.VMEM((nf, tm, tf), BF16),
            pltpu.SemaphoreType.DMA((1,)),
        ],
        compiler_params=_cparams(("arbitrary", "arbitrary", "arbitrary")),
        name="expert_ffn",
    )(idx3, idx3, gate3, xp, wg, wu, wd)


COMB_ROWS = CHUNK + BF16_ROWS


COMB_WIN = 64


def _combine_kernel(base_ref, x1_ref, rkt_ref, ye_hbm, g2_ref, b2_ref, y_ref, buf, big, acc_ref, sem, bsem,
                    *, n_tiles, cap_p):
    i = pl.program_id(0)

    def window(c, e):
        b = base_ref[c, e]
        st = jnp.minimum((b // BF16_ROWS) * BF16_ROWS, cap_p - COMB_WIN)
        fits = base_ref[c + 1, e] - st <= COMB_WIN
        return b, st, fits

    def copy(c, e, slot):
        st = pl.multiple_of(window(c, e)[1], BF16_ROWS)
        return pltpu.make_async_copy(ye_hbm.at[e, pl.ds(st, COMB_WIN), :],
                                     buf.at[slot, pl.ds(e * COMB_WIN, COMB_WIN), :], sem.at[slot])

    @pl.when(i == 0)
    def _():
        for e in range(N_EXPERTS):
            copy(0, e, 0).start()

    @pl.when(i + 1 < n_tiles)
    def _():
        for e in range(N_EXPERTS):
            copy(i + 1, e, (i + 1) % 2).start()

    slot = i % 2
    rkt = rkt_ref[0]
    lane = lax.broadcasted_iota(I32, (1, LANES), 1)
    per_tile = LANES // COMB_WIN
    pieces = []
    for p in range(N_EXPERTS // per_tile):
        target = jnp.full((CHUNK, 1), -1, I32)
        for q in range(per_tile):
            e = p * per_tile + q
            b, st, fits = window(i, e)
            rk = rkt[:, e:e + 1]
            pos = jnp.where((rk >= 0) & fits, rk + (b - st + q * COMB_WIN), -1)
            target = jnp.where((lane >= q * COMB_WIN) & (lane < (q + 1) * COMB_WIN), pos, target) if q else pos
        pieces.append(jnp.where(target == lane, 1.0, 0.0).astype(BF16))
    onehot = jnp.concatenate(pieces, axis=1)
    pltpu.make_async_copy(buf.at[slot], buf.at[slot], sem.at[slot]).wait()
    acc_ref[...] = ALPHA * x1_ref[...] + _dot(onehot, buf[slot])

    r_io = lax.broadcasted_iota(I32, (1, COMB_ROWS), 1)
    for e in range(N_EXPERTS):
        b, _, fits = window(i, e)

        @pl.when(jnp.logical_not(fits))
        def _():
            st = pl.multiple_of(jnp.minimum((b // BF16_ROWS) * BF16_ROWS, cap_p - COMB_ROWS), BF16_ROWS)
            cp = pltpu.make_async_copy(ye_hbm.at[e, pl.ds(st, COMB_ROWS), :], big, bsem.at[0])
            cp.start()
            cp.wait()
            rk = rkt[:, e:e + 1]
            srel = jnp.where(rk >= 0, rk + (b - st), -1)
            acc_ref[...] += _dot(jnp.where(srel == r_io, 1.0, 0.0).astype(BF16), big[...])

    y_ref[...] = _ln(acc_ref[...], g2_ref[...], b2_ref[...])


def _combine(base_s, x1, rank_t, ye, g2, b2, cap_p):
    assert base_s.shape[0] > x1.shape[0] // CHUNK
    t = x1.shape[0]
    n_tiles = t // CHUNK
    const = lambda i, b: (0, 0)
    return pl.pallas_call(
        functools.partial(_combine_kernel, n_tiles=n_tiles, cap_p=cap_p),
        grid_spec=pltpu.PrefetchScalarGridSpec(
            num_scalar_prefetch=1, grid=(n_tiles,),
            in_specs=[
                pl.BlockSpec((CHUNK, D_MODEL), lambda i, b: (i, 0)),
                pl.BlockSpec((1, CHUNK, N_EXPERTS), lambda i, b: (i, 0, 0)),
                pl.BlockSpec(memory_space=pl.ANY),
                pl.BlockSpec((1, D_MODEL), const), pl.BlockSpec((1, D_MODEL), const),
            ],
            out_specs=pl.BlockSpec((CHUNK, D_MODEL), lambda i, b: (i, 0)),
            scratch_shapes=[
                pltpu.VMEM((2, N_EXPERTS * COMB_WIN, D_MODEL), BF16),
                pltpu.VMEM((COMB_ROWS, D_MODEL), BF16),
                pltpu.VMEM((CHUNK, D_MODEL), F32),
                pltpu.SemaphoreType.DMA((2,)),
                pltpu.SemaphoreType.DMA((1,)),
            ]),
        out_shape=jax.ShapeDtypeStruct((t, D_MODEL), F32),
        compiler_params=_cparams(("arbitrary",)),
        name="moe_combine_ln2",
    )(base_s, x1, rank_t, ye, g2, b2)


def _row(v):
    return v.reshape(1, -1).astype(F32)


def _attn_bias_variants(bias_main, bias_meta0):
    dead = jnp.full((AT_HEADS, CHUNK, CHUNK), NEG, F32)
    first = jnp.concatenate([bias_meta0, dead, bias_main[:, :, 2 * CHUNK:]], axis=2)
    last = jnp.concatenate([bias_main[:, :, :3 * CHUNK], dead], axis=2)
    tabs = jnp.stack([bias_main, first, last]) * LOG2E
    return tabs.reshape(3, AT_KV_HEADS, AT_GROUP * CHUNK, 4 * CHUNK)


def _pick_tile(n, pref):
    tm = pref
    while n % tm:
        tm //= 2
    return tm


def _trunk(x, p, meta):
    batch, seq, _ = x.shape
    nb = seq // CHUNK
    t = batch * seq
    bm = batch * N_META
    x2 = x.reshape(t, D_MODEL)
    pm_m, kt_m, gc_m, gt_m = meta
    pm, kt, gc, gt = _ln_proj(x2, p["g0"], p["b0"], p["w_main"], p["w_kt"], p["w_g"], p["w_gt"], p["bg"], p["bgt"],
                              _pick_tile(t, PM_TM))
    hb, hb_m = _mlstm(pm, kt, gc, gt, pm_m, kt_m, gc_m, gt_m, batch, nb, reverse=True)
    ha, ha_m = _mlstm(pm, kt, gc, gt, pm_m, kt_m, gc_m, gt_m, batch, nb, reverse=False,
                      merged=(hb, hb_m, p["ng"]))
    at = _attn_real(pm, pm_m, p["bias3"], p["sink_col"], batch, nb)
    at_m = _attn_meta(pm, pm_m, p["bias_mq"], p["sink_b"], batch, nb)

    mix = _mix(ha, at, pm, p["wa"], p["wb"], _pick_tile(t, 512), PM_GA // MIX_TN, PM_GB // MIX_TN)
    gates_m = jnp.tile(pm_m[:, PM_GA:PM_MQ], (batch, 1))
    mix_m = _mix(ha_m, at_m, gates_m, p["wa"], p["wb"], bm, 0, D_MODEL // MIX_TN)

    ln_args = (p["g0"], p["b0"], p["g1"], p["b1"], p["wo"], p["wrt"])
    x1, xp, aff = _out_ln_router(mix, x2, *ln_args, _pick_tile(t, OUT_TM), bm)
    xm = jnp.tile(p["meta_tokens"], (batch, 1))
    xp, aff_m = _out_ln_router(mix_m, xm, *ln_args, bm, bm, packed=xp)

    n_tok = t + bm
    nc = -(-n_tok // CHUNK)
    aff_m = jnp.pad(aff_m, ((0, 0), (0, nc * CHUNK - n_tok)), constant_values=-1.0)
    aff_m = aff_m.reshape(N_EXPERTS, -1, CHUNK).transpose(1, 0, 2)
    aff_all = jnp.concatenate([aff, aff_m], axis=0)
    cap = CAPACITY_FACTOR * n_tok // N_EXPERTS
    cap_p = -(-cap // CHUNK) * CHUNK
    rank, base = _route_select(aff_all, cap)
    base_s = base[:, :, 0]
    n_blk = cap_p // CHUNK + 2
    lists = _route_compact(base_s, rank, aff_all, n_blk)
    lists = lists.transpose(0, 2, 1, 3).reshape(N_EXPERTS, LIST_ROWS, n_blk * CHUNK)[:, :, :cap_p]
    idx = (lists[:, 0] * 256.0 + lists[:, 1]).astype(I32)
    gate = lists[:, 2] + lists[:, 3] + lists[:, 4]
    ns = -(-cap_p // FFN_MAX_ROWS)
    tm = cap_p // ns
    ye = _expert_ffn(idx.reshape(N_EXPERTS * ns, 1, tm), gate.reshape(N_EXPERTS, cap_p, 1), xp,
                     p["wgate"], p["wup"], p["wdown"], cap_p, tm, FFN_TF)
    rank_t = rank[:t // CHUNK].transpose(0, 2, 1)
    y = _combine(base_s, x1, rank_t, ye, p["g2"], p["b2"], cap_p)
    return y.reshape(batch, seq, D_MODEL)


def kernel(x_prompt, x_sample, meta_tokens, ln0_g, ln0_b, rel_bias, attn_sink, w_in, b_gate, ml_norm_g,
           w_branch_a, w_branch_b, w_out, ln1_g, ln1_b, w_router, w_gate, w_up, w_down, ln2_g, ln2_b):
    assert w_in.shape[0] == DEPTH
    w = w_in[0]
    sizes = (ML_HEADS * ML_DQK, ML_HEADS * ML_DQK, ML_HEADS * ML_DV, ML_HEADS * ML_DV, 4 * ML_HEADS,
             AT_HEADS * AT_DH, AT_KV_HEADS * AT_DH, AT_KV_HEADS * AT_DH, D_MODEL, D_MODEL)
    offs = np.concatenate([[0], np.cumsum(sizes)])
    mq, mk, mv, mo, mg, aq, ak, av, ga, gb = [w[:, offs[i]:offs[i + 1]] for i in range(10)]
    w_g = jnp.pad(mg, ((0, 0), (0, LANES - 4 * ML_HEADS))).astype(BF16)
    bg = jnp.pad(b_gate[0].astype(F32), (0, LANES - 4 * ML_HEADS))
    nb_max = max(x_prompt.shape[1], x_sample.shape[1]) // CHUNK
    bk_main, bk_meta0, bk_mq = _bucket_tables(nb_max)
    rb = rel_bias.astype(F32)
    p = {
        "g0": _row(ln0_g), "b0": _row(ln0_b), "g1": _row(ln1_g[0]), "b1": _row(ln1_b[0]),
        "g2": _row(ln2_g[0]), "b2": _row(ln2_b[0]), "ng": _row(ml_norm_g[0]),
        "w_main": jnp.concatenate([mv, mo, aq, ga, gb, mq, ak, av], axis=1).astype(BF16),
        "w_kt": mk.T.astype(BF16), "w_g": w_g, "w_gt": w_g.T,
        "bg": bg.reshape(1, LANES), "bgt": bg.reshape(LANES, 1),
        "wa": w_branch_a[0].astype(BF16), "wb": w_branch_b[0].astype(BF16), "wo": w_out[0].astype(BF16),
        "wrt": w_router[0].T.astype(BF16),
        "wgate": w_gate[0], "wup": w_up[0], "wdown": w_down[0],
        "bias3": _attn_bias_variants(_bias_table(rb, bk_main), _bias_table(rb, bk_meta0)),
        "sink_col": jnp.broadcast_to(
            jnp.repeat(attn_sink[0].astype(F32).reshape(AT_KV_HEADS, AT_GROUP), CHUNK, axis=1)[:, :, None] * LOG2E,
            (AT_KV_HEADS, AT_GROUP * CHUNK, LANES)),
        "bias_mq": _bias_table(rb, bk_mq),
        "sink_b": jnp.broadcast_to(attn_sink[0].astype(F32)[:, None], (AT_HEADS, LANES)),
        "meta_tokens": meta_tokens.astype(F32),
    }
    meta = _ln_proj(p["meta_tokens"], p["g0"], p["b0"], p["w_main"], p["w_kt"], p["w_g"], p["w_gt"],
                    p["bg"], p["bgt"], N_META)
    return (_trunk(x_prompt, p, meta), _trunk(x_sample, p, meta))
```

```python
import functools
import math

import numpy as np
import jax
import jax.numpy as jnp
from jax import lax
from jax.experimental import pallas as pl
from jax.experimental.pallas import tpu as pltpu

F32 = jnp.float32
BF16 = jnp.bfloat16
I32 = jnp.int32

D_MODEL = 2048
N_META = 16
CHUNK = 128
ML_HEADS = 8
ML_DV = D_MODEL // ML_HEADS
ML_DQK = ML_DV // 2
AT_DH = 128
AT_HEADS = D_MODEL // AT_DH
AT_KV_HEADS = AT_HEADS // 4
AT_GROUP = AT_HEADS // AT_KV_HEADS
WINDOW = 128
REL_BUCKETS = 32
REL_MAX_DIST = 128
N_EXPERTS = 16
EXPERT_FF = D_MODEL
CAPACITY_FACTOR = 2
DEPTH = 1
ALPHA = (2.0 * DEPTH) ** 0.25
LN_EPS = 1e-5
M_INIT = -1e30
NEG = -1e30

LANES = 128
BF16_ROWS = 16
VMEM_LIMIT = 56 * 1024 * 1024

PM_MV, PM_MO, PM_AQ, PM_GA, PM_GB, PM_MQ, PM_AK, PM_AV = 0, 2048, 4096, 6144, 8192, 10240, 11264, 11776
PM_WIDTH = 12288
PM_TN = 2048
PM_TM = 512
LN_ROWS = 256
MIX_TN = 1024
OUT_TM = 512
CT_W = ML_DV + LANES

NT_DIMS = (((1,), (1,)), ((), ()))


def _cparams(sem, vmem=VMEM_LIMIT):
    return pltpu.CompilerParams(dimension_semantics=sem, vmem_limit_bytes=vmem)


def _dot(a, b):
    return jnp.dot(a, b, preferred_element_type=F32)


def _dot_nt(a, b):
    return lax.dot_general(a, b, NT_DIMS, preferred_element_type=F32)


def _ln(x, g, b):
    mu = jnp.mean(x, axis=-1, keepdims=True)
    xc = x - mu
    var = jnp.mean(xc * xc, axis=-1, keepdims=True)
    return xc * lax.rsqrt(var + LN_EPS) * g + b


def _split3(x):
    hi = x.astype(BF16)
    r1 = x - hi.astype(F32)
    mid = r1.astype(BF16)
    lo = (r1 - mid.astype(F32)).astype(BF16)
    return hi, mid, lo


def _bias_kernel(rb_ref, bk_ref, o_ref):
    h = pl.program_id(0)
    bk = bk_ref[...]
    acc = jnp.full(bk.shape, NEG, F32)
    for b in range(REL_BUCKETS):
        acc = jnp.where(bk == b, rb_ref[b, h], acc)
    o_ref[0] = acc


def _bias_table(rel_bias, bucket):
    r, c = bucket.shape
    return pl.pallas_call(
        _bias_kernel,
        grid_spec=pltpu.PrefetchScalarGridSpec(
            num_scalar_prefetch=1, grid=(AT_HEADS,),
            in_specs=[pl.BlockSpec((r, c), lambda h, rb: (0, 0))],
            out_specs=pl.BlockSpec((1, r, c), lambda h, rb: (h, 0, 0))),
        out_shape=jax.ShapeDtypeStruct((AT_HEADS, r, c), F32),
        compiler_params=_cparams(("arbitrary",)),
        name="bias_table",
    )(rel_bias, bucket)


def _t5_bucket(rel):
    half = REL_BUCKETS // 2
    max_exact = half // 2
    n = jnp.abs(rel)
    nf = jnp.maximum(n, 1).astype(jnp.float32)
    large = max_exact + (jnp.log(nf / max_exact) / math.log(REL_MAX_DIST / max_exact)
                         * (half - max_exact)).astype(jnp.int32)
    large = jnp.minimum(large, half - 1)
    return jnp.where(rel > 0, half, 0) + jnp.where(n < max_exact, n, large)


def _t5_bucket_np(rel):
    half = REL_BUCKETS // 2
    max_exact = half // 2
    n = np.abs(rel)
    nf = np.maximum(n, 1).astype(np.float64)
    large = max_exact + (np.log(nf / max_exact) / math.log(REL_MAX_DIST / max_exact) * (half - max_exact)).astype(np.int64)
    large = np.minimum(large, half - 1)
    return np.where(rel > 0, half, 0) + np.where(n < max_exact, n, large)


def _bucket_tables(nb_max):
    i = np.arange(CHUNK)[:, None]
    m = np.arange(N_META)[None, :]
    ref_tab = _t5_bucket_np(m - (N_META + CHUNK + i))
    for j in range(1, nb_max):
        assert np.array_equal(_t5_bucket_np(m - (N_META + j * CHUNK + i)), ref_tab)
    qi = jnp.arange(CHUNK, dtype=I32)[:, None]
    c = jnp.arange(4 * CHUNK, dtype=I32)[None, :]
    rel_meta = c - (N_META + CHUNK + qi)
    rel_nb = (c - CHUNK) - CHUNK - qi
    vis_nb = (c >= CHUNK) & (jnp.abs(rel_nb) <= WINDOW)
    main = jnp.where(c < N_META, _t5_bucket(rel_meta), jnp.where(vis_nb, _t5_bucket(rel_nb), -1))
    c1 = jnp.arange(CHUNK, dtype=I32)[None, :]
    meta0 = jnp.where(c1 < N_META, _t5_bucket(c1 - (N_META + qi)), -1)
    mi = jnp.arange(N_META, dtype=I32)[:, None]
    c2 = jnp.arange(2 * CHUNK, dtype=I32)[None, :]
    rel_r = N_META + (c2 - CHUNK) - mi
    vis_r = (c2 >= CHUNK) & (jnp.abs(rel_r) <= WINDOW)
    mq = jnp.where(c2 < N_META, _t5_bucket(c2 - mi), jnp.where(vis_r, _t5_bucket(rel_r), -1))
    return main.astype(I32), meta0.astype(I32), mq.astype(I32)


def _proj_kernel(x_ref, g0_ref, b0_ref, w_ref, wkt_ref, wg_ref, wgt_ref, bg_ref, bgt_ref,
                 pm_ref, kt_ref, gc_ref, gt_ref, xs_ref):
    n = pl.program_id(1)

    @pl.when(n == 0)
    def _():
        tm = x_ref.shape[0]
        rb = min(tm, LN_ROWS)

        def ln_block(r, carry):
            sl = pl.ds(pl.multiple_of(r * rb, BF16_ROWS), rb)
            xs_ref[sl, :] = _ln(x_ref[sl, :], g0_ref[...], b0_ref[...]).astype(BF16)
            return carry

        lax.fori_loop(0, tm // rb, ln_block, 0)
        xn = xs_ref[...]
        kt_ref[...] = _dot_nt(wkt_ref[...], xn).astype(BF16)
        gc = _dot(xn, wg_ref[...]) + bg_ref[...]
        gc_ref[...] = gc
        if tm % LANES == 0:
            gt_ref[...] = gc.T
        else:
            gt_ref[...] = _dot_nt(wgt_ref[...], xn) + bgt_ref[...]

    y = _dot(xs_ref[...], w_ref[...])
    lo1, hi1 = PM_MO // PM_TN, PM_AQ // PM_TN
    lo2, hi2 = PM_GA // PM_TN, PM_MQ // PM_TN
    is_sig = ((n >= lo1) & (n < hi1)) | ((n >= lo2) & (n < hi2))
    pm_ref[...] = jnp.where(is_sig, jax.nn.sigmoid(y), y).astype(BF16)


def _ln_proj(x2, g0, b0, w_main, w_kt, w_g, w_gt, bg, bgt, tm):
    t = x2.shape[0]
    grid = (t // tm, PM_WIDTH // PM_TN)
    const = lambda i, n: (0, 0)
    return pl.pallas_call(
        _proj_kernel,
        grid=grid,
        in_specs=[
            pl.BlockSpec((tm, D_MODEL), lambda i, n: (i, 0)),
            pl.BlockSpec((1, D_MODEL), const),
            pl.BlockSpec((1, D_MODEL), const),
            pl.BlockSpec((D_MODEL, PM_TN), lambda i, n: (0, n)),
            pl.BlockSpec((ML_HEADS * ML_DQK, D_MODEL), const, pipeline_mode=pl.Buffered(1)),
            pl.BlockSpec((D_MODEL, LANES), const),
            pl.BlockSpec((LANES, D_MODEL), const),
            pl.BlockSpec((1, LANES), const),
            pl.BlockSpec((LANES, 1), const),
        ],
        out_specs=[
            pl.BlockSpec((tm, PM_TN), lambda i, n: (i, n)),
            pl.BlockSpec((ML_HEADS * ML_DQK, tm), lambda i, n: (0, i)),
            pl.BlockSpec((tm, LANES), lambda i, n: (i, 0)),
            pl.BlockSpec((LANES, tm), lambda i, n: (0, i)),
        ],
        out_shape=[
            jax.ShapeDtypeStruct((t, PM_WIDTH), BF16),
            jax.ShapeDtypeStruct((ML_HEADS * ML_DQK, t), BF16),
            jax.ShapeDtypeStruct((t, LANES), F32),
            jax.ShapeDtypeStruct((LANES, t), F32),
        ],
        scratch_shapes=[pltpu.VMEM((tm, D_MODEL), BF16)],
        compiler_params=_cparams(("arbitrary", "arbitrary")),
        name="ln_proj",
    )(x2, g0, b0, w_main, w_kt, w_g, w_gt, bg, bgt)


def _mlstm_chunk(L, reverse, q_ref, kt_ref, v_ref, gc_ref, gt_ref, ct_ref, m_ref, finish):
    H = ML_HEADS
    row = lax.broadcasted_iota(I32, (L, L), 0)
    col = lax.broadcasted_iota(I32, (L, L), 1)
    if reverse:
        tri = row <= col
        tri_t = row >= col
    else:
        tri = row >= col
        tri_t = row <= col
    tri_b = jnp.where(tri, 1.0, 0.0).astype(BF16)
    tri_tb = jnp.where(tri_t, 1.0, 0.0).astype(BF16)
    gc = gc_ref[...] * LOG2E
    gt = gt_ref[...] * LOG2E
    lfc = jax.nn.log_sigmoid(gc_ref[...]) * LOG2E
    lft = jax.nn.log_sigmoid(gt_ref[...]) * LOG2E
    bc_all = sum(_dot(tri_b, p) for p in _split3(lfc))
    br_all = sum(_dot(p, tri_tb) for p in _split3(lft))
    end = 0 if reverse else L - 1
    ci0, cf0 = (2 * H, 3 * H) if reverse else (0, H)
    scale = ML_DQK ** -0.5

    def stack(parts):
        return jnp.concatenate(parts, axis=0)

    def cols(x_all, c0):
        return stack([jnp.broadcast_to(x_all[:, c0 + h:c0 + h + 1], (L, L)) for h in range(H)])

    def rows_b(x_t, c0):
        return stack([jnp.broadcast_to(x_t[c0 + h:c0 + h + 1, :], (L, L)) for h in range(H)])

    def per_head(vals):
        return stack([jnp.broadcast_to(x, (L, L)) for x in vals])

    def rep(x):
        return jnp.broadcast_to(x, (H * L, L))

    def wide(x, n):
        if L == LANES and n % LANES == 0:
            return jnp.concatenate([x] * (n // LANES), axis=1)
        return jnp.broadcast_to(x[:, 0:1], (H * L, n))

    def wide_row(x, n):
        if L == LANES and n % LANES == 0:
            return jnp.concatenate([x] * (n // LANES), axis=1)
        return jnp.broadcast_to(x[:, 0:1], (1, n))

    def head(x, h):
        return x[h * L:(h + 1) * L]

    q = [q_ref[:, h * ML_DQK:(h + 1) * ML_DQK] for h in range(H)]
    kt = [kt_ref[h * ML_DQK:(h + 1) * ML_DQK, :] for h in range(H)]
    v = [v_ref[:, h * ML_DV:(h + 1) * ML_DV] for h in range(H)]
    ct = [ct_ref[h] for h in range(H)]
    m_prev_h = [m_ref[h:h + 1, 0:L] for h in range(H)]
    btot_h = [jnp.broadcast_to(bc_all[end:end + 1, cf0 + h:cf0 + h + 1], (1, L)) for h in range(H)]

    bc = cols(bc_all, cf0)
    igc = cols(gc, ci0)
    m_prev = per_head(m_prev_h)
    btot = per_head(btot_h)
    mask = stack([tri] * H)
    a_rows = gt[ci0:ci0 + H, :] - br_all[cf0:cf0 + H, :]
    d = jnp.where(mask, bc + rows_b(a_rows, 0), NEG)
    inter = bc + m_prev
    m_comb = jnp.maximum(inter, rep(jnp.max(d, axis=1, keepdims=True)))
    w = jnp.exp2(d - m_comb)
    w_inter = jnp.exp2(inter - m_comb) * scale
    s = stack([_dot(q[h], kt[h]) for h in range(H)]) * w * scale
    qc = stack([_dot(q[h], ct[h].astype(BF16)) for h in range(H)])
    sb = s.astype(BF16)
    num = stack([_dot(head(sb, h), v[h]) for h in range(H)]) + wide(w_inter, ML_DV) * qc[:, :ML_DV]
    den = rep(jnp.sum(s, axis=1, keepdims=True)) + w_inter * wide(rep(qc[:, ML_DV:ML_DV + 1]), L)
    finish(num * wide(1.0 / jnp.maximum(jnp.abs(den), jnp.exp2(-m_comb)), ML_DV))

    dec = btot - bc + igc
    inter_end_h = [btot_h[h] + m_prev_h[h] for h in range(H)]
    m_new_h = [jnp.maximum(inter_end_h[h], jnp.max(head(dec, h), axis=0, keepdims=True)) for h in range(H)]
    w_end = jnp.exp2(dec - per_head(m_new_h))
    vf = stack([x.astype(F32) for x in v])
    wv = jnp.concatenate([(wide(w_end, ML_DV) * vf).astype(BF16), wide(w_end, LANES).astype(BF16)], axis=1)
    for h in range(H):
        w_prev = jnp.exp2(inter_end_h[h] - m_new_h[h])
        ct_ref[h] = wide_row(w_prev, CT_W) * ct[h] + _dot(kt[h], head(wv, h))
        m_ref[h:h + 1, :] = wide_row(m_new_h[h], LANES)


def _mlstm_kernel(*refs, reverse, nb, merge):
    (q_ref, kt_ref, v_ref, gc_ref, gt_ref, qm_ref, ktm_ref, vm_ref, gcm_ref, gtm_ref) = refs[:10]
    if merge:
        o_ref, om_ref, hb_ref, hbm_ref, ng_ref, out_ref, outm_ref, ct_ref, m_ref = refs[10:]
    else:
        out_ref, outm_ref, ct_ref, m_ref = refs[10:]
    j = pl.program_id(1)

    @pl.when(j == 0)
    def _():
        ct_ref[...] = jnp.zeros(ct_ref.shape, F32)
        m_ref[...] = jnp.full(m_ref.shape, M_INIT * LOG2E, F32)

    def make_emit(dst_ref, other_ref, gate_ref):
        def emit(hv):
            L = hv.shape[0] // ML_HEADS
            sls = [slice(h * ML_DV, (h + 1) * ML_DV) for h in range(ML_HEADS)]
            if merge:
                hs = hv + jnp.concatenate([other_ref[:, sl] for sl in sls], axis=0)
                mu = jnp.broadcast_to(jnp.mean(hs, axis=-1, keepdims=True), hs.shape)
                hc = hs - mu
                var = jnp.broadcast_to(jnp.mean(hc * hc, axis=-1, keepdims=True), hs.shape)
                ng = jnp.concatenate([jnp.broadcast_to(ng_ref[:, sl], (L, ML_DV)) for sl in sls], axis=0)
                og = jnp.concatenate([gate_ref[:, sl].astype(F32) for sl in sls], axis=0)
                hv = hc * lax.rsqrt(var + LN_EPS) * (ng * og)
            for h, sl in enumerate(sls):
                dst_ref[:, sl] = hv[h * L:(h + 1) * L].astype(dst_ref.dtype)
        return emit

    is_meta = (j == nb) if reverse else (j == 0)

    @pl.when(is_meta)
    def _():
        emit = make_emit(outm_ref, hbm_ref if merge else None, om_ref if merge else None)
        _mlstm_chunk(N_META, reverse, qm_ref, ktm_ref, vm_ref, gcm_ref, gtm_ref, ct_ref, m_ref, emit)

    @pl.when(jnp.logical_not(is_meta))
    def _():
        emit = make_emit(out_ref, hb_ref if merge else None, o_ref if merge else None)
        _mlstm_chunk(CHUNK, reverse, q_ref, kt_ref, v_ref, gc_ref, gt_ref, ct_ref, m_ref, emit)


def _mlstm(pm, kt, gc, gt, pm_m, kt_m, gc_m, gt_m, batch, nb, *, reverse, merged=None):
    t = pm.shape[0]
    if reverse:
        rblk = lambda b, j: b * nb + jnp.maximum(nb - 1 - j, 0)
    else:
        rblk = lambda b, j: b * nb + jnp.maximum(j - 1, 0)
    const = lambda b, j: (0, 0)
    in_specs = [
        pl.BlockSpec((CHUNK, ML_HEADS * ML_DQK), lambda b, j: (rblk(b, j), PM_MQ // (ML_HEADS * ML_DQK))),
        pl.BlockSpec((ML_HEADS * ML_DQK, CHUNK), lambda b, j: (0, rblk(b, j))),
        pl.BlockSpec((CHUNK, D_MODEL), lambda b, j: (rblk(b, j), PM_MV // D_MODEL)),
        pl.BlockSpec((CHUNK, LANES), lambda b, j: (rblk(b, j), 0)),
        pl.BlockSpec((LANES, CHUNK), lambda b, j: (0, rblk(b, j))),
        pl.BlockSpec((N_META, ML_HEADS * ML_DQK), lambda b, j: (0, PM_MQ // (ML_HEADS * ML_DQK))),
        pl.BlockSpec((ML_HEADS * ML_DQK, N_META), const),
        pl.BlockSpec((N_META, D_MODEL), lambda b, j: (0, PM_MV // D_MODEL)),
        pl.BlockSpec((N_META, LANES), const),
        pl.BlockSpec((LANES, N_META), const),
    ]
    args = [pm, kt, pm, gc, gt, pm_m, kt_m, pm_m, gc_m, gt_m]
    merge = merged is not None
    if merge:
        hb, hb_m, ng = merged
        in_specs += [
            pl.BlockSpec((CHUNK, D_MODEL), lambda b, j: (rblk(b, j), PM_MO // D_MODEL)),
            pl.BlockSpec((N_META, D_MODEL), lambda b, j: (0, PM_MO // D_MODEL)),
            pl.BlockSpec((CHUNK, D_MODEL), lambda b, j: (rblk(b, j), 0)),
            pl.BlockSpec((N_META, D_MODEL), lambda b, j: (b, 0)),
            pl.BlockSpec((1, D_MODEL), const),
        ]
        args += [pm, pm_m, hb, hb_m, ng]
    odt = BF16 if merge else F32
    return pl.pallas_call(
        functools.partial(_mlstm_kernel, reverse=reverse, nb=nb, merge=merge),
        grid=(batch, nb + 1),
        in_specs=in_specs,
        out_specs=[
            pl.BlockSpec((CHUNK, D_MODEL), lambda b, j: (rblk(b, j), 0)),
            pl.BlockSpec((N_META, D_MODEL), lambda b, j: (b, 0)),
        ],
        out_shape=[
            jax.ShapeDtypeStruct((t, D_MODEL), odt),
            jax.ShapeDtypeStruct((batch * N_META, D_MODEL), odt),
        ],
        scratch_shapes=[pltpu.VMEM((ML_HEADS, ML_DQK, CT_W), F32), pltpu.VMEM((ML_HEADS, LANES), F32)],
        compiler_params=_cparams(("arbitrary", "arbitrary")),
        name="mlstm_bwd" if reverse else "mlstm_fwd",
    )(*args)


def _softmax_pv(sg, snk, vcat):
    m = jnp.maximum(jnp.max(sg, axis=1, keepdims=True), snk)
    p = jnp.exp(sg - m)
    l = jnp.sum(p, axis=1, keepdims=True) + jnp.exp(snk - m)
    return _dot(p.astype(BF16), vcat) * (1.0 / l)


LOG2E = math.log2(math.e)


def _attn_kernel(q_ref, kp_ref, kc_ref, kn_ref, vp_ref, vc_ref, vn_ref, km_ref, vm_ref,
                 bias_ref, sink_ref, o_ref):
    pad = jnp.zeros((CHUNK - N_META, AT_DH), BF16)
    ones = jnp.ones((4 * CHUNK, LANES), BF16)
    scale2 = AT_DH ** -0.5 * LOG2E
    for c in range(AT_KV_HEADS):
        ks = slice(c * AT_DH, (c + 1) * AT_DH)
        kcat = jnp.concatenate([km_ref[:, ks], pad, kp_ref[:, ks], kc_ref[:, ks], kn_ref[:, ks]], axis=0)
        vcat = jnp.concatenate([vm_ref[:, ks], pad, vp_ref[:, ks], vc_ref[:, ks], vn_ref[:, ks]], axis=0)
        vaug = jnp.concatenate([vcat, ones], axis=1)
        qc = jnp.concatenate(
            [q_ref[:, (c * AT_GROUP + g) * AT_DH:(c * AT_GROUP + g + 1) * AT_DH] for g in range(AT_GROUP)], axis=0)
        t = _dot_nt(qc, kcat) * scale2 + bias_ref[0, c]
        snk = sink_ref[c][:, 0:1]
        m = jnp.maximum(jnp.max(t, axis=1, keepdims=True), snk)
        p = jnp.exp2(t - m).astype(BF16)
        ov = _dot(p, vaug)
        l = ov[:, AT_DH:AT_DH + 1] + jnp.exp2(snk - m)
        o = ov[:, :AT_DH] * (1.0 / l)
        for g in range(AT_GROUP):
            h = c * AT_GROUP + g
            o_ref[:, h * AT_DH:(h + 1) * AT_DH] = o[g * CHUNK:(g + 1) * CHUNK].astype(o_ref.dtype)


def _attn_real(pm, pm_m, bias3, sink_col, batch, nb):
    assert nb >= 2
    t = pm.shape[0]
    kvw = AT_KV_HEADS * AT_DH
    cur = lambda b, j: b * nb + j
    prv = lambda b, j: b * nb + jnp.maximum(j - 1, 0)
    nxt = lambda b, j: b * nb + jnp.minimum(j + 1, nb - 1)
    variant = lambda b, j: (jnp.where(j == 0, 1, jnp.where(j == nb - 1, 2, 0)), 0, 0, 0)
    kcol, vcol = PM_AK // kvw, PM_AV // kvw
    return pl.pallas_call(
        _attn_kernel,
        grid=(batch, nb),
        in_specs=[
            pl.BlockSpec((CHUNK, D_MODEL), lambda b, j: (cur(b, j), PM_AQ // D_MODEL)),
            pl.BlockSpec((CHUNK, kvw), lambda b, j: (prv(b, j), kcol)),
            pl.BlockSpec((CHUNK, kvw), lambda b, j: (cur(b, j), kcol)),
            pl.BlockSpec((CHUNK, kvw), lambda b, j: (nxt(b, j), kcol)),
            pl.BlockSpec((CHUNK, kvw), lambda b, j: (prv(b, j), vcol)),
            pl.BlockSpec((CHUNK, kvw), lambda b, j: (cur(b, j), vcol)),
            pl.BlockSpec((CHUNK, kvw), lambda b, j: (nxt(b, j), vcol)),
            pl.BlockSpec((N_META, kvw), lambda b, j: (0, kcol)),
            pl.BlockSpec((N_META, kvw), lambda b, j: (0, vcol)),
            pl.BlockSpec((1, AT_KV_HEADS, AT_GROUP * CHUNK, 4 * CHUNK), variant),
            pl.BlockSpec((AT_KV_HEADS, AT_GROUP * CHUNK, LANES), lambda b, j: (0, 0, 0)),
        ],
        out_specs=pl.BlockSpec((CHUNK, D_MODEL), lambda b, j: (cur(b, j), 0)),
        out_shape=jax.ShapeDtypeStruct((t, D_MODEL), BF16),
        compiler_params=_cparams(("arbitrary", "arbitrary")),
        name="attn_real",
    )(pm, pm, pm, pm, pm, pm, pm, pm_m, pm_m, bias3, sink_col)


def _attn_meta_kernel(q_ref, kr_ref, vr_ref, km_ref, vm_ref, bias_ref, sink_ref, o_ref):
    pad = jnp.zeros((CHUNK - N_META, AT_DH), BF16)
    scale = AT_DH ** -0.5
    for c in range(AT_KV_HEADS):
        ks = slice(c * AT_DH, (c + 1) * AT_DH)
        kcat = jnp.concatenate([km_ref[:, ks], pad, kr_ref[:, ks]], axis=0)
        vcat = jnp.concatenate([vm_ref[:, ks], pad, vr_ref[:, ks]], axis=0)
        qc = jnp.concatenate(
            [q_ref[:, (c * AT_GROUP + g) * AT_DH:(c * AT_GROUP + g + 1) * AT_DH] for g in range(AT_GROUP)], axis=0)
        s = _dot_nt(qc, kcat) * scale
        for g in range(AT_GROUP):
            h = c * AT_GROUP + g
            sg = s[g * N_META:(g + 1) * N_META] + bias_ref[h]
            o = _softmax_pv(sg, sink_ref[h:h + 1, 0:1], vcat)
            o_ref[:, h * AT_DH:(h + 1) * AT_DH] = o.astype(o_ref.dtype)


def _attn_meta(pm, pm_m, bias_mq, sink_b, batch, nb):
    kvw = AT_KV_HEADS * AT_DH
    kcol, vcol = PM_AK // kvw, PM_AV // kvw
    return pl.pallas_call(
        _attn_meta_kernel,
        grid=(batch,),
        in_specs=[
            pl.BlockSpec((N_META, D_MODEL), lambda b: (0, PM_AQ // D_MODEL)),
            pl.BlockSpec((CHUNK, kvw), lambda b: (b * nb, kcol)),
            pl.BlockSpec((CHUNK, kvw), lambda b: (b * nb, vcol)),
            pl.BlockSpec((N_META, kvw), lambda b: (0, kcol)),
            pl.BlockSpec((N_META, kvw), lambda b: (0, vcol)),
            pl.BlockSpec((AT_HEADS, N_META, 2 * CHUNK), lambda b: (0, 0, 0)),
            pl.BlockSpec((AT_HEADS, LANES), lambda b: (0, 0)),
        ],
        out_specs=pl.BlockSpec((N_META, D_MODEL), lambda b: (b, 0)),
        out_shape=jax.ShapeDtypeStruct((batch * N_META, D_MODEL), BF16),
        compiler_params=_cparams(("arbitrary",)),
        name="attn_meta",
    )(pm_m, pm, pm, pm_m, pm_m, bias_mq, sink_b)


def _mix_kernel(ha_ref, hb_ref, ga_ref, gb_ref, wa_ref, wb_ref, o_ref):
    a = _dot(ha_ref[...], wa_ref[...])
    b = _dot(hb_ref[...], wb_ref[...])
    o_ref[...] = (ga_ref[...].astype(F32) * a + gb_ref[...].astype(F32) * b).astype(o_ref.dtype)


def _mix(ha, hb, gates_src, wa, wb, tm, ga_col, gb_col):
    t = ha.shape[0]
    nt = D_MODEL // MIX_TN
    return pl.pallas_call(
        _mix_kernel,
        grid=(t // tm, nt),
        in_specs=[
            pl.BlockSpec((tm, D_MODEL), lambda i, n: (i, 0)),
            pl.BlockSpec((tm, D_MODEL), lambda i, n: (i, 0)),
            pl.BlockSpec((tm, MIX_TN), lambda i, n: (i, ga_col + n)),
            pl.BlockSpec((tm, MIX_TN), lambda i, n: (i, gb_col + n)),
            pl.BlockSpec((D_MODEL, MIX_TN), lambda i, n: (0, n)),
            pl.BlockSpec((D_MODEL, MIX_TN), lambda i, n: (0, n)),
        ],
        out_specs=pl.BlockSpec((tm, MIX_TN), lambda i, n: (i, n)),
        out_shape=jax.ShapeDtypeStruct((t, D_MODEL), BF16),
        compiler_params=_cparams(("arbitrary", "arbitrary")),
        name="branch_mix",
    )(ha, hb, gates_src, gates_src, wa, wb)


HALF_D = D_MODEL // 2


def _pack_bf16_pairs(x):
    lo = lax.bitcast_convert_type(x[:, :HALF_D].astype(BF16).astype(F32), I32)
    hi = lax.bitcast_convert_type(x[:, HALF_D:].astype(BF16).astype(F32), I32)
    return lax.shift_right_logical(lo, 16) | (hi & jnp.int32(-65536))


def _unpack_bf16_pairs(u):
    lo = lax.bitcast_convert_type(u << 16, F32)
    hi = lax.bitcast_convert_type(u & jnp.int32(-65536), F32)
    return lo.astype(BF16), hi.astype(BF16)


PACK_SUB = HALF_D // LANES


def _out_kernel(*refs, real):
    mix_ref, x_ref, g0_ref, b0_ref, g1_ref, b1_ref, wo_ref, wrt_ref = refs[:8]
    if real:
        x1_ref, x1p_ref, aff_ref = refs[8:]
    else:
        x1p_ref, aff_ref = refs[9:]

    def body():
        y = _dot(mix_ref[...], wo_ref[...])
        h0 = _ln(x_ref[...], g0_ref[...], b0_ref[...])
        x1 = _ln(ALPHA * h0 + y, g1_ref[...], b1_ref[...])
        packed = _pack_bf16_pairs(x1)
        for c in range(PACK_SUB):
            x1p_ref[:, c, :] = packed[:, c * LANES:(c + 1) * LANES]
        lt = _dot_nt(wrt_ref[...], x1.astype(BF16))
        e = jnp.exp(lt - jnp.max(lt, axis=0, keepdims=True))
        aff = e / jnp.sum(e, axis=0, keepdims=True)
        if real:
            x1_ref[...] = x1
            for c in range(aff.shape[1] // CHUNK):
                aff_ref[c] = aff[:, c * CHUNK:(c + 1) * CHUNK]
        else:
            aff_ref[...] = aff

    if not real:
        body()
        return
    last = pl.program_id(0) == pl.num_programs(0) - 1
    pl.when(jnp.logical_not(last))(body)

    @pl.when(last)
    def _():
        x1p_ref[...] = jnp.zeros(x1p_ref.shape, I32)


def _out_ln_router(mix, x2, g0, b0, g1, b1, wo, wrt, tm, n_meta_rows, packed=None):
    t = mix.shape[0]
    real = packed is None
    n = t // tm
    const = lambda i: (0, 0)
    row = lambda i: (jnp.minimum(i, n - 1), 0)
    in_specs = [
        pl.BlockSpec((tm, D_MODEL), row),
        pl.BlockSpec((tm, D_MODEL), row),
        pl.BlockSpec((1, D_MODEL), const), pl.BlockSpec((1, D_MODEL), const),
        pl.BlockSpec((1, D_MODEL), const), pl.BlockSpec((1, D_MODEL), const),
        pl.BlockSpec((D_MODEL, D_MODEL), const, pipeline_mode=pl.Buffered(1)),
        pl.BlockSpec((N_EXPERTS, D_MODEL), const),
    ]
    args = [mix, x2, g0, b0, g1, b1, wo, wrt]
    if real:
        assert n_meta_rows <= tm
        out_specs = [
            pl.BlockSpec((tm, D_MODEL), row),
            pl.BlockSpec((tm, PACK_SUB, LANES), lambda i: (i, 0, 0)),
            pl.BlockSpec((tm // CHUNK, N_EXPERTS, CHUNK), lambda i: (jnp.minimum(i, n - 1), 0, 0)),
        ]
        out_shape = [
            jax.ShapeDtypeStruct((t, D_MODEL), F32),
            jax.ShapeDtypeStruct((t + n_meta_rows, PACK_SUB, LANES), I32),
            jax.ShapeDtypeStruct((t // CHUNK, N_EXPERTS, CHUNK), F32),
        ]
        aliases = {}
    else:
        n_real = packed.shape[0] - t
        assert tm == t and n_real % t == 0
        in_specs.append(pl.BlockSpec(memory_space=pl.ANY))
        args.append(packed)
        out_specs = [
            pl.BlockSpec((t, PACK_SUB, LANES), lambda i: (n_real // t, 0, 0)),
            pl.BlockSpec((N_EXPERTS, t), const),
        ]
        out_shape = [jax.ShapeDtypeStruct(packed.shape, I32), jax.ShapeDtypeStruct((N_EXPERTS, t), F32)]
        aliases = {8: 0}
    return pl.pallas_call(
        functools.partial(_out_kernel, real=real),
        grid=(n + 1 if real else n,),
        in_specs=in_specs,
        out_specs=out_specs,
        out_shape=out_shape,
        input_output_aliases=aliases,
        compiler_params=_cparams(("arbitrary",)),
        name="out_ln1_router" if real else "out_ln1_router_meta",
    )(*args)


def _route_select_kernel(aff_ref, rank_ref, base_ref, p_s, b_s, *, cap, nc):
    aff = aff_ref[...]
    bits = lax.bitcast_convert_type(aff, I32)

    def count(maskf):
        return jnp.sum(jnp.sum(maskf, axis=0), axis=1, keepdims=True)

    def search(it, lo):
        cand = lo | (jnp.int32(1) << (30 - it))
        cnt = count(jnp.where(bits >= cand[None], 1.0, 0.0))
        return jnp.where(cnt >= cap, cand, lo)

    thr = lax.fori_loop(0, 31, search, jnp.zeros((N_EXPERTS, 1), I32))
    gt = bits > thr[None]
    eq = bits == thr[None]
    need = cap - count(jnp.where(gt, 1.0, 0.0))
    r_i = lax.broadcasted_iota(I32, (CHUNK, CHUNK), 0)
    c_i = lax.broadcasted_iota(I32, (CHUNK, CHUNK), 1)
    upper = jnp.where(r_i <= c_i, 1.0, 0.0).astype(BF16)

    def prefix(maskb):
        p = _dot(jnp.where(maskb, 1.0, 0.0).astype(BF16).reshape(nc * N_EXPERTS, CHUNK), upper)
        p_s[...] = p.reshape(nc, N_EXPERTS, CHUNK)

        def step(c, carry):
            b_s[c] = jnp.broadcast_to(carry, (N_EXPERTS, CHUNK))
            return carry + p_s[c][:, CHUNK - 1:CHUNK]

        lax.fori_loop(0, nc, step, jnp.zeros((N_EXPERTS, 1), F32))

    prefix(eq)
    eq_rank = b_s[...] + p_s[...] - 1.0
    sel = gt | (eq & (eq_rank < need[None]))
    prefix(sel)
    rank_ref[...] = jnp.where(sel, p_s[...] - 1.0, -1.0).astype(I32)
    base_ref[...] = b_s[...].astype(I32)


def _route_select(aff, cap):
    nc = aff.shape[0]
    shp = (nc, N_EXPERTS, CHUNK)
    full = pl.BlockSpec(shp, lambda i: (0, 0, 0))
    return pl.pallas_call(
        functools.partial(_route_select_kernel, cap=cap, nc=nc),
        grid=(1,),
        in_specs=[full],
        out_specs=[full, full],
        out_shape=[jax.ShapeDtypeStruct(shp, I32), jax.ShapeDtypeStruct(shp, I32)],
        scratch_shapes=[pltpu.VMEM(shp, F32), pltpu.VMEM(shp, F32)],
        compiler_params=_cparams(("arbitrary",)),
        name="route_select",
    )(aff)


FFN_MAX_ROWS = 2304
FFN_TF = 256
FFN_ROW_SPLIT = 2
RELAYOUT_BLOCKS = 8
LIST_ROWS = 16


def _route_compact_kernel(base_ref, rank_ref, aff_ref, out_ref):
    c = pl.program_id(0)

    @pl.when(c == 0)
    def _():
        out_ref[...] = jnp.zeros(out_ref.shape, F32)

    lane = lax.broadcasted_iota(I32, (1, CHUNK), 1)
    tok = c * CHUNK + lane
    t_hi = (tok >> 8).astype(F32)
    t_lo = (tok & 255).astype(F32)
    r_io = lax.broadcasted_iota(I32, (CHUNK, CHUNK), 0)
    zeros = jnp.zeros((LIST_ROWS - 5, CHUNK), F32)
    for e in range(N_EXPERTS):
        rk = rank_ref[0, e:e + 1, :]
        onehot = jnp.where(r_io == rk, 1.0, 0.0).astype(BF16)
        a_hi, a_mid, a_lo = _split3(aff_ref[0, e:e + 1, :])
        vals = jnp.concatenate([t_hi, t_lo, a_hi.astype(F32), a_mid.astype(F32), a_lo.astype(F32), zeros],
                               axis=0).astype(BF16)
        comp = _dot_nt(vals, onehot)
        base = base_ref[c, e]
        blk = base // CHUNK
        off = base - blk * CHUNK
        rolled = pltpu.roll(comp, off, axis=1)
        keep_lo = lane >= off
        out_ref[e, blk] = jnp.where(keep_lo, rolled, out_ref[e, blk])
        out_ref[e, blk + 1] = jnp.where(keep_lo, out_ref[e, blk + 1], rolled)


def _route_compact(base_s, rank, aff, n_blk):
    nc = rank.shape[0]
    blk = pl.BlockSpec((1, N_EXPERTS, CHUNK), lambda c, b: (c, 0, 0))
    return pl.pallas_call(
        _route_compact_kernel,
        grid_spec=pltpu.PrefetchScalarGridSpec(
            num_scalar_prefetch=1, grid=(nc,),
            in_specs=[blk, blk],
            out_specs=pl.BlockSpec((N_EXPERTS, n_blk, LIST_ROWS, CHUNK), lambda c, b: (0, 0, 0, 0))),
        out_shape=jax.ShapeDtypeStruct((N_EXPERTS, n_blk, LIST_ROWS, CHUNK), F32),
        compiler_params=_cparams(("arbitrary",)),
        name="route_compact",
    )(base_s, rank, aff)


def _ffn_kernel(idx_ref, idxn_ref, gate_ref, xp_hbm, wg_ref, wu_ref, wd_ref, ye_ref,
                rows_ref, xp_ref, hid_ref, sem, *, tm, rs, nf, tf):
    e, s, j = pl.program_id(0), pl.program_id(1), pl.program_id(2)
    tile = e * pl.num_programs(1) + s
    n_sub = tm // rs
    per_slot = tm // (nf * n_sub)

    def relayout():
        rb = tm // RELAYOUT_BLOCKS

        def body(r, carry):
            sl = pl.ds(pl.multiple_of(r * rb, 8), rb)
            y = pltpu.einshape("tcl->ctl", rows_ref[sl])
            for c in range(PACK_SUB):
                xp_ref[sl, c * LANES:(c + 1) * LANES] = y[c]
            return carry

        lax.fori_loop(0, RELAYOUT_BLOCKS, body, 0)

    def row_copy(iref, r):
        return pltpu.make_async_copy(xp_hbm.at[pl.ds(iref[0, 0, r], 1)], rows_ref.at[pl.ds(r, 1)], sem.at[0])

    def wait_rows():
        pltpu.make_async_copy(xp_hbm.at[pl.ds(0, tm)], rows_ref, sem.at[0]).wait()

    def prefetch_next(r):
        first = (j * n_sub + r) * per_slot
        for k in range(per_slot):
            row_copy(idxn_ref, first + k).start()

    def rows(r):
        return pl.ds(pl.multiple_of(r * rs, BF16_ROWS), rs)

    @pl.when((tile == 0) & (j == 0))
    def _():
        def body(r, carry):
            row_copy(idx_ref, r).start()
            return carry

        lax.fori_loop(0, tm, body, 0)
        wait_rows()
        relayout()

    @pl.when(j < nf)
    def _():
        wg = wg_ref[0].astype(BF16)
        wu = wu_ref[0].astype(BF16)

        def body(r, carry):
            prefetch_next(r)
            lo, hi = _unpack_bf16_pairs(xp_ref[rows(r), :])
            x = jnp.concatenate([lo, hi], axis=1)
            g = _dot(x, wg)
            u = _dot(x, wu)
            hid_ref[j, rows(r), :] = (g * jax.nn.sigmoid(g) * u).astype(BF16)
            return carry

        lax.fori_loop(0, n_sub, body, 0)

    @pl.when(j == nf)
    def _():
        wait_rows()
        relayout()

    @pl.when(j >= nf)
    def _():
        wd = wd_ref[0].astype(BF16)

        def body(r, carry):
            acc = _dot(hid_ref[0, rows(r), :], wd[0:tf])
            for f in range(1, nf):
                acc = acc + _dot(hid_ref[f, rows(r), :], wd[f * tf:(f + 1) * tf])
            ye_ref[0, rows(r), :] = (acc * gate_ref[0, rows(r), :]).astype(ye_ref.dtype)
            return carry

        lax.fori_loop(0, n_sub, body, 0)


def _expert_ffn(idx3, gate3, xp, wg, wu, wd, cap_p, tm, tf):
    ns = cap_p // tm
    nf = EXPERT_FF // tf
    n_tiles = N_EXPERTS * ns
    rs = tm // FFN_ROW_SPLIT
    assert rs % BF16_ROWS == 0 and D_MODEL // tf == nf and tm % (nf * (tm // rs)) == 0
    assert tm % (8 * RELAYOUT_BLOCKS) == 0
    return pl.pallas_call(
        functools.partial(_ffn_kernel, tm=tm, rs=rs, nf=nf, tf=tf),
        grid=(N_EXPERTS, ns, 2 * nf),
        in_specs=[
            pl.BlockSpec((1, 1, tm), lambda e, s, j: (e * ns + s, 0, 0), memory_space=pltpu.SMEM),
            pl.BlockSpec((1, 1, tm), lambda e, s, j: (jnp.minimum(e * ns + s + 1, n_tiles - 1), 0, 0),
                         memory_space=pltpu.SMEM),
            pl.BlockSpec((1, tm, 1), lambda e, s, j: (e, s, 0)),
            pl.BlockSpec(memory_space=pl.ANY),
            pl.BlockSpec((1, D_MODEL, tf), lambda e, s, j: (e, 0, jnp.minimum(j, nf - 1))),
            pl.BlockSpec((1, D_MODEL, tf), lambda e, s, j: (e, 0, jnp.minimum(j, nf - 1))),
            pl.BlockSpec((1, EXPERT_FF, tf), lambda e, s, j: (e, 0, jnp.maximum(j - nf, 0))),
        ],
        out_specs=pl.BlockSpec((1, tm, tf), lambda e, s, j: (e, s, jnp.maximum(j - nf, 0))),
        out_shape=jax.ShapeDtypeStruct((N_EXPERTS, cap_p, D_MODEL), BF16),
        scratch_shapes=[
            pltpu.VMEM((tm, PACK_SUB, LANES), I32),
            pltpu.VMEM((tm, HALF_D), I32),
            pltpu.VMEM((nf, tm, tf), BF16),
            pltpu.SemaphoreType.DMA((1,)),
        ],
        compiler_params=_cparams(("arbitrary", "arbitrary", "arbitrary")),
        name="expert_ffn",
    )(idx3, idx3, gate3, xp, wg, wu, wd)


COMB_ROWS = CHUNK + BF16_ROWS


COMB_WIN = 48


def _combine_kernel(base_ref, x1_ref, rkt_ref, ye_hbm, g2_ref, b2_ref, y_ref, buf, big, acc_ref, sem, bsem,
                    *, n_tiles, cap_p):
    i = pl.program_id(0)

    def window(c, e):
        b = base_ref[c, e]
        st = jnp.minimum((b // BF16_ROWS) * BF16_ROWS, cap_p - COMB_WIN)
        fits = base_ref[c + 1, e] - st <= COMB_WIN
        return b, st, fits

    def copy(c, e, slot):
        st = pl.multiple_of(window(c, e)[1], BF16_ROWS)
        return pltpu.make_async_copy(ye_hbm.at[e, pl.ds(st, COMB_WIN), :],
                                     buf.at[slot, pl.ds(e * COMB_WIN, COMB_WIN), :], sem.at[slot])

    @pl.when(i == 0)
    def _():
        for e in range(N_EXPERTS):
            copy(0, e, 0).start()

    @pl.when(i + 1 < n_tiles)
    def _():
        for e in range(N_EXPERTS):
            copy(i + 1, e, (i + 1) % 2).start()

    slot = i % 2
    rkt = rkt_ref[0]
    lane = lax.broadcasted_iota(I32, (1, LANES), 1)
    pos = []
    for e in range(N_EXPERTS):
        b, st, fits = window(i, e)
        rk = rkt[:, e:e + 1]
        pos.append(jnp.where((rk >= 0) & fits, rk + (b - st + e * COMB_WIN), -1))
    pieces = []
    for p in range(N_EXPERTS * COMB_WIN // LANES):
        lo, hi = p * LANES, (p + 1) * LANES
        target = None
        for e in range(N_EXPERTS):
            e_lo, e_hi = max(e * COMB_WIN, lo), min((e + 1) * COMB_WIN, hi)
            if e_lo >= e_hi:
                continue
            cand = pos[e] - lo
            if target is None:
                target = cand
            else:
                target = jnp.where((lane >= e_lo - lo) & (lane < e_hi - lo), cand, target)
        pieces.append(jnp.where(target == lane, 1.0, 0.0).astype(BF16))
    onehot = jnp.concatenate(pieces, axis=1)
    pltpu.make_async_copy(buf.at[slot], buf.at[slot], sem.at[slot]).wait()
    acc_ref[...] = ALPHA * x1_ref[...] + _dot(onehot, buf[slot])

    r_io = lax.broadcasted_iota(I32, (1, COMB_ROWS), 1)
    for e in range(N_EXPERTS):
        b, _, fits = window(i, e)

        @pl.when(jnp.logical_not(fits))
        def _():
            st = pl.multiple_of(jnp.minimum((b // BF16_ROWS) * BF16_ROWS, cap_p - COMB_ROWS), BF16_ROWS)
            cp = pltpu.make_async_copy(ye_hbm.at[e, pl.ds(st, COMB_ROWS), :], big, bsem.at[0])
            cp.start()
            cp.wait()
            rk = rkt[:, e:e + 1]
            srel = jnp.where(rk >= 0, rk + (b - st), -1)
            acc_ref[...] += _dot(jnp.where(srel == r_io, 1.0, 0.0).astype(BF16), big[...])

    y_ref[...] = _ln(acc_ref[...], g2_ref[...], b2_ref[...])


def _combine(base_s, x1, rank_t, ye, g2, b2, cap_p):
    assert base_s.shape[0] > x1.shape[0] // CHUNK
    t = x1.shape[0]
    n_tiles = t // CHUNK
    const = lambda i, b: (0, 0)
    return pl.pallas_call(
        functools.partial(_combine_kernel, n_tiles=n_tiles, cap_p=cap_p),
        grid_spec=pltpu.PrefetchScalarGridSpec(
            num_scalar_prefetch=1, grid=(n_tiles,),
            in_specs=[
                pl.BlockSpec((CHUNK, D_MODEL), lambda i, b: (i, 0)),
                pl.BlockSpec((1, CHUNK, N_EXPERTS), lambda i, b: (i, 0, 0)),
                pl.BlockSpec(memory_space=pl.ANY),
                pl.BlockSpec((1, D_MODEL), const), pl.BlockSpec((1, D_MODEL), const),
            ],
            out_specs=pl.BlockSpec((CHUNK, D_MODEL), lambda i, b: (i, 0)),
            scratch_shapes=[
                pltpu.VMEM((2, N_EXPERTS * COMB_WIN, D_MODEL), BF16),
                pltpu.VMEM((COMB_ROWS, D_MODEL), BF16),
                pltpu.VMEM((CHUNK, D_MODEL), F32),
                pltpu.SemaphoreType.DMA((2,)),
                pltpu.SemaphoreType.DMA((1,)),
            ]),
        out_shape=jax.ShapeDtypeStruct((t, D_MODEL), F32),
        compiler_params=_cparams(("arbitrary",)),
        name="moe_combine_ln2",
    )(base_s, x1, rank_t, ye, g2, b2)


def _row(v):
    return v.reshape(1, -1).astype(F32)


def _attn_bias_variants(bias_main, bias_meta0):
    dead = jnp.full((AT_HEADS, CHUNK, CHUNK), NEG, F32)
    first = jnp.concatenate([bias_meta0, dead, bias_main[:, :, 2 * CHUNK:]], axis=2)
    last = jnp.concatenate([bias_main[:, :, :3 * CHUNK], dead], axis=2)
    tabs = jnp.stack([bias_main, first, last]) * LOG2E
    return tabs.reshape(3, AT_KV_HEADS, AT_GROUP * CHUNK, 4 * CHUNK)


def _pick_tile(n, pref):
    tm = pref
    while n % tm:
        tm //= 2
    return tm


def _trunk(x, p, meta):
    batch, seq, _ = x.shape
    nb = seq // CHUNK
    t = batch * seq
    bm = batch * N_META
    x2 = x.reshape(t, D_MODEL)
    pm_m, kt_m, gc_m, gt_m = meta
    pm, kt, gc, gt = _ln_proj(x2, p["g0"], p["b0"], p["w_main"], p["w_kt"], p["w_g"], p["w_gt"], p["bg"], p["bgt"],
                              _pick_tile(t, PM_TM))
    hb, hb_m = _mlstm(pm, kt, gc, gt, pm_m, kt_m, gc_m, gt_m, batch, nb, reverse=True)
    ha, ha_m = _mlstm(pm, kt, gc, gt, pm_m, kt_m, gc_m, gt_m, batch, nb, reverse=False,
                      merged=(hb, hb_m, p["ng"]))
    at = _attn_real(pm, pm_m, p["bias3"], p["sink_col"], batch, nb)
    at_m = _attn_meta(pm, pm_m, p["bias_mq"], p["sink_b"], batch, nb)

    mix = _mix(ha, at, pm, p["wa"], p["wb"], _pick_tile(t, 512), PM_GA // MIX_TN, PM_GB // MIX_TN)
    gates_m = jnp.tile(pm_m[:, PM_GA:PM_MQ], (batch, 1))
    mix_m = _mix(ha_m, at_m, gates_m, p["wa"], p["wb"], bm, 0, D_MODEL // MIX_TN)

    ln_args = (p["g0"], p["b0"], p["g1"], p["b1"], p["wo"], p["wrt"])
    x1, xp, aff = _out_ln_router(mix, x2, *ln_args, _pick_tile(t, OUT_TM), bm)
    xm = jnp.tile(p["meta_tokens"], (batch, 1))
    xp, aff_m = _out_ln_router(mix_m, xm, *ln_args, bm, bm, packed=xp)

    n_tok = t + bm
    nc = -(-n_tok // CHUNK)
    aff_m = jnp.pad(aff_m, ((0, 0), (0, nc * CHUNK - n_tok)), constant_values=-1.0)
    aff_m = aff_m.reshape(N_EXPERTS, -1, CHUNK).transpose(1, 0, 2)
    aff_all = jnp.concatenate([aff, aff_m], axis=0)
    cap = CAPACITY_FACTOR * n_tok // N_EXPERTS
    cap_p = -(-cap // CHUNK) * CHUNK
    rank, base = _route_select(aff_all, cap)
    base_s = base[:, :, 0]
    n_blk = cap_p // CHUNK + 2
    lists = _route_compact(base_s, rank, aff_all, n_blk)
    lists = lists.transpose(0, 2, 1, 3).reshape(N_EXPERTS, LIST_ROWS, n_blk * CHUNK)[:, :, :cap_p]
    idx = (lists[:, 0] * 256.0 + lists[:, 1]).astype(I32)
    gate = lists[:, 2] + lists[:, 3] + lists[:, 4]
    ns = -(-cap_p // FFN_MAX_ROWS)
    tm = cap_p // ns
    ye = _expert_ffn(idx.reshape(N_EXPERTS * ns, 1, tm), gate.reshape(N_EXPERTS, cap_p, 1), xp,
                     p["wgate"], p["wup"], p["wdown"], cap_p, tm, FFN_TF)
    rank_t = rank[:t // CHUNK].transpose(0, 2, 1)
    y = _combine(base_s, x1, rank_t, ye, p["g2"], p["b2"], cap_p)
    return y.reshape(batch, seq, D_MODEL)


def kernel(x_prompt, x_sample, meta_tokens, ln0_g, ln0_b, rel_bias, attn_sink, w_in, b_gate, ml_norm_g,
           w_branch_a, w_branch_b, w_out, ln1_g, ln1_b, w_router, w_gate, w_up, w_down, ln2_g, ln2_b):
    assert w_in.shape[0] == DEPTH
    w = w_in[0]
    sizes = (ML_HEADS * ML_DQK, ML_HEADS * ML_DQK, ML_HEADS * ML_DV, ML_HEADS * ML_DV, 4 * ML_HEADS,
             AT_HEADS * AT_DH, AT_KV_HEADS * AT_DH, AT_KV_HEADS * AT_DH, D_MODEL, D_MODEL)
    offs = np.concatenate([[0], np.cumsum(sizes)])
    mq, mk, mv, mo, mg, aq, ak, av, ga, gb = [w[:, offs[i]:offs[i + 1]] for i in range(10)]
    w_g = jnp.pad(mg, ((0, 0), (0, LANES - 4 * ML_HEADS))).astype(BF16)
    bg = jnp.pad(b_gate[0].astype(F32), (0, LANES - 4 * ML_HEADS))
    nb_max = max(x_prompt.shape[1], x_sample.shape[1]) // CHUNK
    bk_main, bk_meta0, bk_mq = _bucket_tables(nb_max)
    rb = rel_bias.astype(F32)
    p = {
        "g0": _row(ln0_g), "b0": _row(ln0_b), "g1": _row(ln1_g[0]), "b1": _row(ln1_b[0]),
        "g2": _row(ln2_g[0]), "b2": _row(ln2_b[0]), "ng": _row(ml_norm_g[0]),
        "w_main": jnp.concatenate([mv, mo, aq, ga, gb, mq, ak, av], axis=1).astype(BF16),
        "w_kt": mk.T.astype(BF16), "w_g": w_g, "w_gt": w_g.T,
        "bg": bg.reshape(1, LANES), "bgt": bg.reshape(LANES, 1),
        "wa": w_branch_a[0].astype(BF16), "wb": w_branch_b[0].astype(BF16), "wo": w_out[0].astype(BF16),
        "wrt": w_router[0].T.astype(BF16),
        "wgate": w_gate[0], "wup": w_up[0], "wdown": w_down[0],
        "bias3": _attn_bias_variants(_bias_table(rb, bk_main), _bias_table(rb, bk_meta0)),
        "sink_col": jnp.broadcast_to(
            jnp.repeat(attn_sink[0].astype(F32).reshape(AT_KV_HEADS, AT_GROUP), CHUNK, axis=1)[:, :, None] * LOG2E,
            (AT_KV_HEADS, AT_GROUP * CHUNK, LANES)),
        "bias_mq": _bias_table(rb, bk_mq),
        "sink_b": jnp.broadcast_to(attn_sink[0].astype(F32)[:, None], (AT_HEADS, LANES)),
        "meta_tokens": meta_tokens.astype(F32),
    }
    meta = _ln_proj(p["meta_tokens"], p["g0"], p["b0"], p["w_main"], p["w_kt"], p["w_g"], p["w_gt"],
                    p["bg"], p["bgt"], N_META)
    return (_trunk(x_prompt, p, meta), _trunk(x_sample, p, meta))
```

```python
import functools
import math

import numpy as np
import jax
import jax.numpy as jnp
from jax import lax
from jax.experimental import pallas as pl
from jax.experimental.pallas import tpu as pltpu

F32 = jnp.float32
BF16 = jnp.bfloat16
I32 = jnp.int32

D_MODEL = 2048
N_META = 16
CHUNK = 128
ML_HEADS = 8
ML_DV = D_MODEL // ML_HEADS
ML_DQK = ML_DV // 2
AT_DH = 128
AT_HEADS = D_MODEL // AT_DH
AT_KV_HEADS = AT_HEADS // 4
AT_GROUP = AT_HEADS // AT_KV_HEADS
WINDOW = 128
REL_BUCKETS = 32
REL_MAX_DIST = 128
N_EXPERTS = 16
EXPERT_FF = D_MODEL
CAPACITY_FACTOR = 2
DEPTH = 1
ALPHA = (2.0 * DEPTH) ** 0.25
LN_EPS = 1e-5
M_INIT = -1e30
NEG = -1e30

LANES = 128
BF16_ROWS = 16
VMEM_LIMIT = 56 * 1024 * 1024

PM_MV, PM_MO, PM_AQ, PM_GA, PM_GB, PM_MQ, PM_AK, PM_AV = 0, 2048, 4096, 6144, 8192, 10240, 11264, 11776
PM_WIDTH = 12288
PM_TN = 2048
PM_TM = 512
LN_ROWS = 256
MIX_TN = 1024
OUT_TM = 512
CT_W = ML_DV + LANES

NT_DIMS = (((1,), (1,)), ((), ()))


def _cparams(sem, vmem=VMEM_LIMIT):
    return pltpu.CompilerParams(dimension_semantics=sem, vmem_limit_bytes=vmem)


def _dot(a, b):
    return jnp.dot(a, b, preferred_element_type=F32)


def _dot_nt(a, b):
    return lax.dot_general(a, b, NT_DIMS, preferred_element_type=F32)


def _ln(x, g, b):
    mu = jnp.mean(x, axis=-1, keepdims=True)
    xc = x - mu
    var = jnp.mean(xc * xc, axis=-1, keepdims=True)
    return xc * lax.rsqrt(var + LN_EPS) * g + b


def _split3(x):
    hi = x.astype(BF16)
    r1 = x - hi.astype(F32)
    mid = r1.astype(BF16)
    lo = (r1 - mid.astype(F32)).astype(BF16)
    return hi, mid, lo


def _bias_kernel(rb_ref, bk_ref, o_ref):
    h = pl.program_id(0)
    bk = bk_ref[...]
    acc = jnp.full(bk.shape, NEG, F32)
    for b in range(REL_BUCKETS):
        acc = jnp.where(bk == b, rb_ref[b, h], acc)
    o_ref[0] = acc


def _bias_table(rel_bias, bucket):
    r, c = bucket.shape
    return pl.pallas_call(
        _bias_kernel,
        grid_spec=pltpu.PrefetchScalarGridSpec(
            num_scalar_prefetch=1, grid=(AT_HEADS,),
            in_specs=[pl.BlockSpec((r, c), lambda h, rb: (0, 0))],
            out_specs=pl.BlockSpec((1, r, c), lambda h, rb: (h, 0, 0))),
        out_shape=jax.ShapeDtypeStruct((AT_HEADS, r, c), F32),
        compiler_params=_cparams(("arbitrary",)),
        name="bias_table",
    )(rel_bias, bucket)


def _t5_bucket(rel):
    half = REL_BUCKETS // 2
    max_exact = half // 2
    n = jnp.abs(rel)
    nf = jnp.maximum(n, 1).astype(jnp.float32)
    large = max_exact + (jnp.log(nf / max_exact) / math.log(REL_MAX_DIST / max_exact)
                         * (half - max_exact)).astype(jnp.int32)
    large = jnp.minimum(large, half - 1)
    return jnp.where(rel > 0, half, 0) + jnp.where(n < max_exact, n, large)


def _t5_bucket_np(rel):
    half = REL_BUCKETS // 2
    max_exact = half // 2
    n = np.abs(rel)
    nf = np.maximum(n, 1).astype(np.float64)
    large = max_exact + (np.log(nf / max_exact) / math.log(REL_MAX_DIST / max_exact) * (half - max_exact)).astype(np.int64)
    large = np.minimum(large, half - 1)
    return np.where(rel > 0, half, 0) + np.where(n < max_exact, n, large)


def _bucket_tables(nb_max):
    i = np.arange(CHUNK)[:, None]
    m = np.arange(N_META)[None, :]
    ref_tab = _t5_bucket_np(m - (N_META + CHUNK + i))
    for j in range(1, nb_max):
        assert np.array_equal(_t5_bucket_np(m - (N_META + j * CHUNK + i)), ref_tab)
    qi = jnp.arange(CHUNK, dtype=I32)[:, None]
    c = jnp.arange(4 * CHUNK, dtype=I32)[None, :]
    rel_meta = c - (N_META + CHUNK + qi)
    rel_nb = (c - CHUNK) - CHUNK - qi
    vis_nb = (c >= CHUNK) & (jnp.abs(rel_nb) <= WINDOW)
    main = jnp.where(c < N_META, _t5_bucket(rel_meta), jnp.where(vis_nb, _t5_bucket(rel_nb), -1))
    c1 = jnp.arange(CHUNK, dtype=I32)[None, :]
    meta0 = jnp.where(c1 < N_META, _t5_bucket(c1 - (N_META + qi)), -1)
    mi = jnp.arange(N_META, dtype=I32)[:, None]
    c2 = jnp.arange(2 * CHUNK, dtype=I32)[None, :]
    rel_r = N_META + (c2 - CHUNK) - mi
    vis_r = (c2 >= CHUNK) & (jnp.abs(rel_r) <= WINDOW)
    mq = jnp.where(c2 < N_META, _t5_bucket(c2 - mi), jnp.where(vis_r, _t5_bucket(rel_r), -1))
    return main.astype(I32), meta0.astype(I32), mq.astype(I32)


def _proj_kernel(x_ref, g0_ref, b0_ref, w_ref, wkt_ref, wg_ref, wgt_ref, bg_ref, bgt_ref,
                 pm_ref, kt_ref, gc_ref, gt_ref, xs_ref):
    n = pl.program_id(1)

    @pl.when(n == 0)
    def _():
        tm = x_ref.shape[0]
        rb = min(tm, LN_ROWS)

        def ln_block(r, carry):
            sl = pl.ds(pl.multiple_of(r * rb, BF16_ROWS), rb)
            xs_ref[sl, :] = _ln(x_ref[sl, :], g0_ref[...], b0_ref[...]).astype(BF16)
            return carry

        lax.fori_loop(0, tm // rb, ln_block, 0)
        xn = xs_ref[...]
        kt_ref[...] = _dot_nt(wkt_ref[...], xn).astype(BF16)
        gc = _dot(xn, wg_ref[...]) + bg_ref[...]
        gc_ref[...] = gc
        if tm % LANES == 0:
            gt_ref[...] = gc.T
        else:
            gt_ref[...] = _dot_nt(wgt_ref[...], xn) + bgt_ref[...]

    y = _dot(xs_ref[...], w_ref[...])
    lo1, hi1 = PM_MO // PM_TN, PM_AQ // PM_TN
    lo2, hi2 = PM_GA // PM_TN, PM_MQ // PM_TN
    is_sig = ((n >= lo1) & (n < hi1)) | ((n >= lo2) & (n < hi2))
    pm_ref[...] = jnp.where(is_sig, jax.nn.sigmoid(y), y).astype(BF16)


def _ln_proj(x2, g0, b0, w_main, w_kt, w_g, w_gt, bg, bgt, tm):
    t = x2.shape[0]
    grid = (t // tm, PM_WIDTH // PM_TN)
    const = lambda i, n: (0, 0)
    return pl.pallas_call(
        _proj_kernel,
        grid=grid,
        in_specs=[
            pl.BlockSpec((tm, D_MODEL), lambda i, n: (i, 0)),
            pl.BlockSpec((1, D_MODEL), const),
            pl.BlockSpec((1, D_MODEL), const),
            pl.BlockSpec((D_MODEL, PM_TN), lambda i, n: (0, n)),
            pl.BlockSpec((ML_HEADS * ML_DQK, D_MODEL), const, pipeline_mode=pl.Buffered(1)),
            pl.BlockSpec((D_MODEL, LANES), const),
            pl.BlockSpec((LANES, D_MODEL), const),
            pl.BlockSpec((1, LANES), const),
            pl.BlockSpec((LANES, 1), const),
        ],
        out_specs=[
            pl.BlockSpec((tm, PM_TN), lambda i, n: (i, n)),
            pl.BlockSpec((ML_HEADS * ML_DQK, tm), lambda i, n: (0, i)),
            pl.BlockSpec((tm, LANES), lambda i, n: (i, 0)),
            pl.BlockSpec((LANES, tm), lambda i, n: (0, i)),
        ],
        out_shape=[
            jax.ShapeDtypeStruct((t, PM_WIDTH), BF16),
            jax.ShapeDtypeStruct((ML_HEADS * ML_DQK, t), BF16),
            jax.ShapeDtypeStruct((t, LANES), F32),
            jax.ShapeDtypeStruct((LANES, t), F32),
        ],
        scratch_shapes=[pltpu.VMEM((tm, D_MODEL), BF16)],
        compiler_params=_cparams(("arbitrary", "arbitrary")),
        name="ln_proj",
    )(x2, g0, b0, w_main, w_kt, w_g, w_gt, bg, bgt)


def _mlstm_chunk(L, reverse, q_ref, kt_ref, v_ref, gc_ref, gt_ref, ct_ref, m_ref, finish):
    H = ML_HEADS
    row = lax.broadcasted_iota(I32, (L, L), 0)
    col = lax.broadcasted_iota(I32, (L, L), 1)
    if reverse:
        tri = row <= col
        tri_t = row >= col
    else:
        tri = row >= col
        tri_t = row <= col
    tri_b = jnp.where(tri, 1.0, 0.0).astype(BF16)
    tri_tb = jnp.where(tri_t, 1.0, 0.0).astype(BF16)
    gc = gc_ref[...] * LOG2E
    gt = gt_ref[...] * LOG2E
    lfc = jax.nn.log_sigmoid(gc_ref[...]) * LOG2E
    lft = jax.nn.log_sigmoid(gt_ref[...]) * LOG2E
    bc_all = sum(_dot(tri_b, p) for p in _split3(lfc))
    br_all = sum(_dot(p, tri_tb) for p in _split3(lft))
    end = 0 if reverse else L - 1
    ci0, cf0 = (2 * H, 3 * H) if reverse else (0, H)
    scale = ML_DQK ** -0.5

    def stack(parts):
        return jnp.concatenate(parts, axis=0)

    def cols(x_all, c0):
        return stack([jnp.broadcast_to(x_all[:, c0 + h:c0 + h + 1], (L, L)) for h in range(H)])

    def rows_b(x_t, c0):
        return stack([jnp.broadcast_to(x_t[c0 + h:c0 + h + 1, :], (L, L)) for h in range(H)])

    def per_head(vals):
        return stack([jnp.broadcast_to(x, (L, L)) for x in vals])

    def rep(x):
        return jnp.broadcast_to(x, (H * L, L))

    def wide(x, n):
        if L == LANES and n % LANES == 0:
            return jnp.concatenate([x] * (n // LANES), axis=1)
        return jnp.broadcast_to(x[:, 0:1], (H * L, n))

    def wide_row(x, n):
        if L == LANES and n % LANES == 0:
            return jnp.concatenate([x] * (n // LANES), axis=1)
        return jnp.broadcast_to(x[:, 0:1], (1, n))

    def head(x, h):
        return x[h * L:(h + 1) * L]

    q = [q_ref[:, h * ML_DQK:(h + 1) * ML_DQK] for h in range(H)]
    kt = [kt_ref[h * ML_DQK:(h + 1) * ML_DQK, :] for h in range(H)]
    v = [v_ref[:, h * ML_DV:(h + 1) * ML_DV] for h in range(H)]
    ct = [ct_ref[h] for h in range(H)]
    m_prev_h = [m_ref[h:h + 1, 0:L] for h in range(H)]
    btot_h = [jnp.broadcast_to(bc_all[end:end + 1, cf0 + h:cf0 + h + 1], (1, L)) for h in range(H)]

    bc = cols(bc_all, cf0)
    igc = cols(gc, ci0)
    m_prev = per_head(m_prev_h)
    btot = per_head(btot_h)
    mask = stack([tri] * H)
    a_rows = gt[ci0:ci0 + H, :] - br_all[cf0:cf0 + H, :]
    d = jnp.where(mask, bc + rows_b(a_rows, 0), NEG)
    inter = bc + m_prev
    m_comb = jnp.maximum(inter, rep(jnp.max(d, axis=1, keepdims=True)))
    w = jnp.exp2(d - m_comb)
    w_inter = jnp.exp2(inter - m_comb) * scale
    s = stack([_dot(q[h], kt[h]) for h in range(H)]) * w * scale
    qc = stack([_dot(q[h], ct[h].astype(BF16)) for h in range(H)])
    sb = s.astype(BF16)
    num = stack([_dot(head(sb, h), v[h]) for h in range(H)]) + wide(w_inter, ML_DV) * qc[:, :ML_DV]
    den = rep(jnp.sum(s, axis=1, keepdims=True)) + w_inter * qc[:, ML_DV:ML_DV + L]
    finish(num * wide(1.0 / jnp.maximum(jnp.abs(den), jnp.exp2(-m_comb)), ML_DV))

    dec = btot - bc + igc
    inter_end_h = [btot_h[h] + m_prev_h[h] for h in range(H)]
    m_new_h = [jnp.maximum(inter_end_h[h], jnp.max(head(dec, h), axis=0, keepdims=True)) for h in range(H)]
    w_end = jnp.exp2(dec - per_head(m_new_h))
    vf = stack([x.astype(F32) for x in v])
    wv = jnp.concatenate([(wide(w_end, ML_DV) * vf).astype(BF16), wide(w_end, LANES).astype(BF16)], axis=1)
    for h in range(H):
        w_prev = jnp.exp2(inter_end_h[h] - m_new_h[h])
        ct_ref[h] = wide_row(w_prev, CT_W) * ct[h] + _dot(kt[h], head(wv, h))
        m_ref[h:h + 1, :] = wide_row(m_new_h[h], LANES)


def _mlstm_kernel(*refs, reverse, nb, merge):
    (q_ref, kt_ref, v_ref, gc_ref, gt_ref, qm_ref, ktm_ref, vm_ref, gcm_ref, gtm_ref) = refs[:10]
    if merge:
        o_ref, om_ref, hb_ref, hbm_ref, ng_ref, out_ref, outm_ref, ct_ref, m_ref = refs[10:]
    else:
        out_ref, outm_ref, ct_ref, m_ref = refs[10:]
    j = pl.program_id(1)

    @pl.when(j == 0)
    def _():
        ct_ref[...] = jnp.zeros(ct_ref.shape, F32)
        m_ref[...] = jnp.full(m_ref.shape, M_INIT * LOG2E, F32)

    def make_emit(dst_ref, other_ref, gate_ref):
        def emit(hv):
            L = hv.shape[0] // ML_HEADS
            sls = [slice(h * ML_DV, (h + 1) * ML_DV) for h in range(ML_HEADS)]
            if merge:
                hs = hv + jnp.concatenate([other_ref[:, sl] for sl in sls], axis=0)
                mu = jnp.broadcast_to(jnp.mean(hs, axis=-1, keepdims=True), hs.shape)
                hc = hs - mu
                var = jnp.broadcast_to(jnp.mean(hc * hc, axis=-1, keepdims=True), hs.shape)
                ng = jnp.concatenate([jnp.broadcast_to(ng_ref[:, sl], (L, ML_DV)) for sl in sls], axis=0)
                og = jnp.concatenate([gate_ref[:, sl].astype(F32) for sl in sls], axis=0)
                hv = hc * lax.rsqrt(var + LN_EPS) * (ng * og)
            for h, sl in enumerate(sls):
                dst_ref[:, sl] = hv[h * L:(h + 1) * L].astype(dst_ref.dtype)
        return emit

    is_meta = (j == nb) if reverse else (j == 0)

    @pl.when(is_meta)
    def _():
        emit = make_emit(outm_ref, hbm_ref if merge else None, om_ref if merge else None)
        _mlstm_chunk(N_META, reverse, qm_ref, ktm_ref, vm_ref, gcm_ref, gtm_ref, ct_ref, m_ref, emit)

    @pl.when(jnp.logical_not(is_meta))
    def _():
        emit = make_emit(out_ref, hb_ref if merge else None, o_ref if merge else None)
        _mlstm_chunk(CHUNK, reverse, q_ref, kt_ref, v_ref, gc_ref, gt_ref, ct_ref, m_ref, emit)


def _mlstm(pm, kt, gc, gt, pm_m, kt_m, gc_m, gt_m, batch, nb, *, reverse, merged=None):
    t = pm.shape[0]
    if reverse:
        rblk = lambda b, j: b * nb + jnp.maximum(nb - 1 - j, 0)
    else:
        rblk = lambda b, j: b * nb + jnp.maximum(j - 1, 0)
    const = lambda b, j: (0, 0)
    in_specs = [
        pl.BlockSpec((CHUNK, ML_HEADS * ML_DQK), lambda b, j: (rblk(b, j), PM_MQ // (ML_HEADS * ML_DQK))),
        pl.BlockSpec((ML_HEADS * ML_DQK, CHUNK), lambda b, j: (0, rblk(b, j))),
        pl.BlockSpec((CHUNK, D_MODEL), lambda b, j: (rblk(b, j), PM_MV // D_MODEL)),
        pl.BlockSpec((CHUNK, LANES), lambda b, j: (rblk(b, j), 0)),
        pl.BlockSpec((LANES, CHUNK), lambda b, j: (0, rblk(b, j))),
        pl.BlockSpec((N_META, ML_HEADS * ML_DQK), lambda b, j: (0, PM_MQ // (ML_HEADS * ML_DQK))),
        pl.BlockSpec((ML_HEADS * ML_DQK, N_META), const),
        pl.BlockSpec((N_META, D_MODEL), lambda b, j: (0, PM_MV // D_MODEL)),
        pl.BlockSpec((N_META, LANES), const),
        pl.BlockSpec((LANES, N_META), const),
    ]
    args = [pm, kt, pm, gc, gt, pm_m, kt_m, pm_m, gc_m, gt_m]
    merge = merged is not None
    if merge:
        hb, hb_m, ng = merged
        in_specs += [
            pl.BlockSpec((CHUNK, D_MODEL), lambda b, j: (rblk(b, j), PM_MO // D_MODEL)),
            pl.BlockSpec((N_META, D_MODEL), lambda b, j: (0, PM_MO // D_MODEL)),
            pl.BlockSpec((CHUNK, D_MODEL), lambda b, j: (rblk(b, j), 0)),
            pl.BlockSpec((N_META, D_MODEL), lambda b, j: (b, 0)),
            pl.BlockSpec((1, D_MODEL), const),
        ]
        args += [pm, pm_m, hb, hb_m, ng]
    odt = BF16 if merge else F32
    return pl.pallas_call(
        functools.partial(_mlstm_kernel, reverse=reverse, nb=nb, merge=merge),
        grid=(batch, nb + 1),
        in_specs=in_specs,
        out_specs=[
            pl.BlockSpec((CHUNK, D_MODEL), lambda b, j: (rblk(b, j), 0)),
            pl.BlockSpec((N_META, D_MODEL), lambda b, j: (b, 0)),
        ],
        out_shape=[
            jax.ShapeDtypeStruct((t, D_MODEL), odt),
            jax.ShapeDtypeStruct((batch * N_META, D_MODEL), odt),
        ],
        scratch_shapes=[pltpu.VMEM((ML_HEADS, ML_DQK, CT_W), F32), pltpu.VMEM((ML_HEADS, LANES), F32)],
        compiler_params=_cparams(("arbitrary", "arbitrary")),
        name="mlstm_bwd" if reverse else "mlstm_fwd",
    )(*args)


def _softmax_pv(sg, snk, vcat):
    m = jnp.maximum(jnp.max(sg, axis=1, keepdims=True), snk)
    p = jnp.exp(sg - m)
    l = jnp.sum(p, axis=1, keepdims=True) + jnp.exp(snk - m)
    return _dot(p.astype(BF16), vcat) * (1.0 / l)


LOG2E = math.log2(math.e)


def _attn_kernel(q_ref, kp_ref, kc_ref, kn_ref, vp_ref, vc_ref, vn_ref, km_ref, vm_ref,
                 bias_ref, sink_ref, o_ref):
    half = CHUNK // 2
    pad = jnp.zeros((half - N_META, AT_DH), BF16)
    ones = jnp.ones((3 * CHUNK, LANES), BF16)
    scale2 = AT_DH ** -0.5 * LOG2E
    rows_u = AT_GROUP * half
    for c in range(AT_KV_HEADS):
        ks = slice(c * AT_DH, (c + 1) * AT_DH)

        def keys(u, m_ref, p_ref, c_ref, n_ref):
            if u == 0:
                return jnp.concatenate([p_ref[:, ks], c_ref[:, ks], n_ref[:half, ks], m_ref[:, ks], pad], axis=0)
            return jnp.concatenate([p_ref[half:, ks], m_ref[:, ks], pad, c_ref[:, ks], n_ref[:, ks]], axis=0)

        def queries(u):
            return jnp.concatenate(
                [q_ref[u * half:(u + 1) * half, (c * AT_GROUP + g) * AT_DH:(c * AT_GROUP + g + 1) * AT_DH]
                 for g in range(AT_GROUP)], axis=0)

        t = jnp.concatenate([_dot_nt(queries(u), keys(u, km_ref, kp_ref, kc_ref, kn_ref)) for u in range(2)],
                            axis=0) * scale2 + bias_ref[0, c]
        snk = sink_ref[c]
        m = jnp.maximum(jnp.broadcast_to(jnp.max(t, axis=1, keepdims=True), snk.shape), snk)
        p = jnp.exp2(t - jnp.concatenate([m] * 3, axis=1)).astype(BF16)
        ov = jnp.concatenate(
            [_dot(p[u * rows_u:(u + 1) * rows_u],
                  jnp.concatenate([keys(u, vm_ref, vp_ref, vc_ref, vn_ref), ones], axis=1)) for u in range(2)], axis=0)
        l = ov[:, AT_DH:] + jnp.exp2(snk - m)
        o = ov[:, :AT_DH] * (1.0 / l)
        for u in range(2):
            for g in range(AT_GROUP):
                h = c * AT_GROUP + g
                r0 = u * rows_u + g * half
                o_ref[u * half:(u + 1) * half, h * AT_DH:(h + 1) * AT_DH] = o[r0:r0 + half].astype(o_ref.dtype)


def _attn_real(pm, pm_m, bias3, sink_col, batch, nb):
    assert nb >= 2
    t = pm.shape[0]
    kvw = AT_KV_HEADS * AT_DH
    cur = lambda b, j: b * nb + j
    prv = lambda b, j: b * nb + jnp.maximum(j - 1, 0)
    nxt = lambda b, j: b * nb + jnp.minimum(j + 1, nb - 1)
    variant = lambda b, j: (jnp.where(j == 0, 1, jnp.where(j == nb - 1, 2, 0)), 0, 0, 0)
    kcol, vcol = PM_AK // kvw, PM_AV // kvw
    return pl.pallas_call(
        _attn_kernel,
        grid=(batch, nb),
        in_specs=[
            pl.BlockSpec((CHUNK, D_MODEL), lambda b, j: (cur(b, j), PM_AQ // D_MODEL)),
            pl.BlockSpec((CHUNK, kvw), lambda b, j: (prv(b, j), kcol)),
            pl.BlockSpec((CHUNK, kvw), lambda b, j: (cur(b, j), kcol)),
            pl.BlockSpec((CHUNK, kvw), lambda b, j: (nxt(b, j), kcol)),
            pl.BlockSpec((CHUNK, kvw), lambda b, j: (prv(b, j), vcol)),
            pl.BlockSpec((CHUNK, kvw), lambda b, j: (cur(b, j), vcol)),
            pl.BlockSpec((CHUNK, kvw), lambda b, j: (nxt(b, j), vcol)),
            pl.BlockSpec((N_META, kvw), lambda b, j: (0, kcol)),
            pl.BlockSpec((N_META, kvw), lambda b, j: (0, vcol)),
            pl.BlockSpec((1, AT_KV_HEADS, AT_GROUP * CHUNK, 3 * CHUNK), variant),
            pl.BlockSpec((AT_KV_HEADS, AT_GROUP * CHUNK, LANES), lambda b, j: (0, 0, 0)),
        ],
        out_specs=pl.BlockSpec((CHUNK, D_MODEL), lambda b, j: (cur(b, j), 0)),
        out_shape=jax.ShapeDtypeStruct((t, D_MODEL), BF16),
        compiler_params=_cparams(("arbitrary", "arbitrary")),
        name="attn_real",
    )(pm, pm, pm, pm, pm, pm, pm, pm_m, pm_m, bias3, sink_col)


def _attn_meta_kernel(q_ref, kr_ref, vr_ref, km_ref, vm_ref, bias_ref, sink_ref, o_ref):
    pad = jnp.zeros((CHUNK - N_META, AT_DH), BF16)
    scale = AT_DH ** -0.5
    for c in range(AT_KV_HEADS):
        ks = slice(c * AT_DH, (c + 1) * AT_DH)
        kcat = jnp.concatenate([km_ref[:, ks], pad, kr_ref[:, ks]], axis=0)
        vcat = jnp.concatenate([vm_ref[:, ks], pad, vr_ref[:, ks]], axis=0)
        qc = jnp.concatenate(
            [q_ref[:, (c * AT_GROUP + g) * AT_DH:(c * AT_GROUP + g + 1) * AT_DH] for g in range(AT_GROUP)], axis=0)
        s = _dot_nt(qc, kcat) * scale
        for g in range(AT_GROUP):
            h = c * AT_GROUP + g
            sg = s[g * N_META:(g + 1) * N_META] + bias_ref[h]
            o = _softmax_pv(sg, sink_ref[h:h + 1, 0:1], vcat)
            o_ref[:, h * AT_DH:(h + 1) * AT_DH] = o.astype(o_ref.dtype)


def _attn_meta(pm, pm_m, bias_mq, sink_b, batch, nb):
    kvw = AT_KV_HEADS * AT_DH
    kcol, vcol = PM_AK // kvw, PM_AV // kvw
    return pl.pallas_call(
        _attn_meta_kernel,
        grid=(batch,),
        in_specs=[
            pl.BlockSpec((N_META, D_MODEL), lambda b: (0, PM_AQ // D_MODEL)),
            pl.BlockSpec((CHUNK, kvw), lambda b: (b * nb, kcol)),
            pl.BlockSpec((CHUNK, kvw), lambda b: (b * nb, vcol)),
            pl.BlockSpec((N_META, kvw), lambda b: (0, kcol)),
            pl.BlockSpec((N_META, kvw), lambda b: (0, vcol)),
            pl.BlockSpec((AT_HEADS, N_META, 2 * CHUNK), lambda b: (0, 0, 0)),
            pl.BlockSpec((AT_HEADS, LANES), lambda b: (0, 0)),
        ],
        out_specs=pl.BlockSpec((N_META, D_MODEL), lambda b: (b, 0)),
        out_shape=jax.ShapeDtypeStruct((batch * N_META, D_MODEL), BF16),
        compiler_params=_cparams(("arbitrary",)),
        name="attn_meta",
    )(pm_m, pm, pm, pm_m, pm_m, bias_mq, sink_b)


def _mix_kernel(ha_ref, hb_ref, ga_ref, gb_ref, wa_ref, wb_ref, o_ref):
    a = _dot(ha_ref[...], wa_ref[...])
    b = _dot(hb_ref[...], wb_ref[...])
    o_ref[...] = (ga_ref[...].astype(F32) * a + gb_ref[...].astype(F32) * b).astype(o_ref.dtype)


def _mix(ha, hb, gates_src, wa, wb, tm, ga_col, gb_col):
    t = ha.shape[0]
    nt = D_MODEL // MIX_TN
    return pl.pallas_call(
        _mix_kernel,
        grid=(t // tm, nt),
        in_specs=[
            pl.BlockSpec((tm, D_MODEL), lambda i, n: (i, 0)),
            pl.BlockSpec((tm, D_MODEL), lambda i, n: (i, 0)),
            pl.BlockSpec((tm, MIX_TN), lambda i, n: (i, ga_col + n)),
            pl.BlockSpec((tm, MIX_TN), lambda i, n: (i, gb_col + n)),
            pl.BlockSpec((D_MODEL, MIX_TN), lambda i, n: (0, n)),
            pl.BlockSpec((D_MODEL, MIX_TN), lambda i, n: (0, n)),
        ],
        out_specs=pl.BlockSpec((tm, MIX_TN), lambda i, n: (i, n)),
        out_shape=jax.ShapeDtypeStruct((t, D_MODEL), BF16),
        compiler_params=_cparams(("arbitrary", "arbitrary")),
        name="branch_mix",
    )(ha, hb, gates_src, gates_src, wa, wb)


HALF_D = D_MODEL // 2


def _pack_bf16_pairs(x):
    lo = lax.bitcast_convert_type(x[:, :HALF_D].astype(BF16).astype(F32), I32)
    hi = lax.bitcast_convert_type(x[:, HALF_D:].astype(BF16).astype(F32), I32)
    return lax.shift_right_logical(lo, 16) | (hi & jnp.int32(-65536))


def _unpack_bf16_pairs(u):
    lo = lax.bitcast_convert_type(u << 16, F32)
    hi = lax.bitcast_convert_type(u & jnp.int32(-65536), F32)
    return lo.astype(BF16), hi.astype(BF16)


PACK_SUB = HALF_D // LANES


def _out_kernel(*refs, real):
    mix_ref, x_ref, g0_ref, b0_ref, g1_ref, b1_ref, wo_ref, wrt_ref = refs[:8]
    if real:
        x1_ref, x1p_ref, aff_ref = refs[8:]
    else:
        x1p_ref, aff_ref = refs[9:]

    def body():
        y = _dot(mix_ref[...], wo_ref[...])
        h0 = _ln(x_ref[...], g0_ref[...], b0_ref[...])
        x1 = _ln(ALPHA * h0 + y, g1_ref[...], b1_ref[...])
        packed = _pack_bf16_pairs(x1)
        for c in range(PACK_SUB):
            x1p_ref[:, c, :] = packed[:, c * LANES:(c + 1) * LANES]
        lt = _dot_nt(wrt_ref[...], x1.astype(BF16))
        e = jnp.exp(lt - jnp.max(lt, axis=0, keepdims=True))
        aff = e / jnp.sum(e, axis=0, keepdims=True)
        if real:
            x1_ref[...] = x1
            for c in range(aff.shape[1] // CHUNK):
                aff_ref[c] = aff[:, c * CHUNK:(c + 1) * CHUNK]
        else:
            aff_ref[...] = aff

    if not real:
        body()
        return
    last = pl.program_id(0) == pl.num_programs(0) - 1
    pl.when(jnp.logical_not(last))(body)

    @pl.when(last)
    def _():
        x1p_ref[...] = jnp.zeros(x1p_ref.shape, I32)


def _out_ln_router(mix, x2, g0, b0, g1, b1, wo, wrt, tm, n_meta_rows, packed=None):
    t = mix.shape[0]
    real = packed is None
    n = t // tm
    const = lambda i: (0, 0)
    row = lambda i: (jnp.minimum(i, n - 1), 0)
    in_specs = [
        pl.BlockSpec((tm, D_MODEL), row),
        pl.BlockSpec((tm, D_MODEL), row),
        pl.BlockSpec((1, D_MODEL), const), pl.BlockSpec((1, D_MODEL), const),
        pl.BlockSpec((1, D_MODEL), const), pl.BlockSpec((1, D_MODEL), const),
        pl.BlockSpec((D_MODEL, D_MODEL), const, pipeline_mode=pl.Buffered(1)),
        pl.BlockSpec((N_EXPERTS, D_MODEL), const),
    ]
    args = [mix, x2, g0, b0, g1, b1, wo, wrt]
    if real:
        assert n_meta_rows <= tm
        out_specs = [
            pl.BlockSpec((tm, D_MODEL), row),
            pl.BlockSpec((tm, PACK_SUB, LANES), lambda i: (i, 0, 0)),
            pl.BlockSpec((tm // CHUNK, N_EXPERTS, CHUNK), lambda i: (jnp.minimum(i, n - 1), 0, 0)),
        ]
        out_shape = [
            jax.ShapeDtypeStruct((t, D_MODEL), F32),
            jax.ShapeDtypeStruct((t + n_meta_rows, PACK_SUB, LANES), I32),
            jax.ShapeDtypeStruct((t // CHUNK, N_EXPERTS, CHUNK), F32),
        ]
        aliases = {}
    else:
        n_real = packed.shape[0] - t
        assert tm == t and n_real % t == 0
        in_specs.append(pl.BlockSpec(memory_space=pl.ANY))
        args.append(packed)
        out_specs = [
            pl.BlockSpec((t, PACK_SUB, LANES), lambda i: (n_real // t, 0, 0)),
            pl.BlockSpec((N_EXPERTS, t), const),
        ]
        out_shape = [jax.ShapeDtypeStruct(packed.shape, I32), jax.ShapeDtypeStruct((N_EXPERTS, t), F32)]
        aliases = {8: 0}
    return pl.pallas_call(
        functools.partial(_out_kernel, real=real),
        grid=(n + 1 if real else n,),
        in_specs=in_specs,
        out_specs=out_specs,
        out_shape=out_shape,
        input_output_aliases=aliases,
        compiler_params=_cparams(("arbitrary",)),
        name="out_ln1_router" if real else "out_ln1_router_meta",
    )(*args)


def _route_select_kernel(aff_ref, rank_ref, base_ref, p_s, b_s, *, cap, nc):
    aff = aff_ref[...]
    bits = lax.bitcast_convert_type(aff, I32)

    def count(maskf):
        return jnp.sum(jnp.sum(maskf, axis=0), axis=1, keepdims=True)

    def search(it, lo):
        cand = lo | (jnp.int32(1) << (30 - it))
        cnt = count(jnp.where(bits >= cand[None], 1.0, 0.0))
        return jnp.where(cnt >= cap, cand, lo)

    thr = lax.fori_loop(0, 31, search, jnp.zeros((N_EXPERTS, 1), I32))
    gt = bits > thr[None]
    eq = bits == thr[None]
    need = cap - count(jnp.where(gt, 1.0, 0.0))
    r_i = lax.broadcasted_iota(I32, (CHUNK, CHUNK), 0)
    c_i = lax.broadcasted_iota(I32, (CHUNK, CHUNK), 1)
    upper = jnp.where(r_i <= c_i, 1.0, 0.0).astype(BF16)

    def prefix(maskb):
        p = _dot(jnp.where(maskb, 1.0, 0.0).astype(BF16).reshape(nc * N_EXPERTS, CHUNK), upper)
        p_s[...] = p.reshape(nc, N_EXPERTS, CHUNK)

        def step(c, carry):
            b_s[c] = jnp.broadcast_to(carry, (N_EXPERTS, CHUNK))
            return carry + p_s[c][:, CHUNK - 1:CHUNK]

        lax.fori_loop(0, nc, step, jnp.zeros((N_EXPERTS, 1), F32))

    prefix(eq)
    eq_rank = b_s[...] + p_s[...] - 1.0
    sel = gt | (eq & (eq_rank < need[None]))
    prefix(sel)
    rank_ref[...] = jnp.where(sel, p_s[...] - 1.0, -1.0).astype(I32)
    base_ref[...] = b_s[...].astype(I32)


def _route_select(aff, cap):
    nc = aff.shape[0]
    shp = (nc, N_EXPERTS, CHUNK)
    full = pl.BlockSpec(shp, lambda i: (0, 0, 0))
    return pl.pallas_call(
        functools.partial(_route_select_kernel, cap=cap, nc=nc),
        grid=(1,),
        in_specs=[full],
        out_specs=[full, full],
        out_shape=[jax.ShapeDtypeStruct(shp, I32), jax.ShapeDtypeStruct(shp, I32)],
        scratch_shapes=[pltpu.VMEM(shp, F32), pltpu.VMEM(shp, F32)],
        compiler_params=_cparams(("arbitrary",)),
        name="route_select",
    )(aff)


FFN_MAX_ROWS = 2304
FFN_TF = 256
FFN_ROW_SPLIT = 2
RELAYOUT_BLOCKS = 8
LIST_ROWS = 16


def _route_compact_kernel(base_ref, rank_ref, aff_ref, out_ref, *, group):
    step = pl.program_id(0)

    @pl.when(step == 0)
    def _():
        out_ref[...] = jnp.zeros(out_ref.shape, F32)

    lane = lax.broadcasted_iota(I32, (1, CHUNK), 1)
    r_io = lax.broadcasted_iota(I32, (CHUNK, CHUNK), 0)
    zeros = jnp.zeros((LIST_ROWS - 5, CHUNK), F32)
    for k in range(group):
        c = step * group + k
        tok = c * CHUNK + lane
        t_hi = (tok >> 8).astype(F32)
        t_lo = (tok & 255).astype(F32)
        for e in range(N_EXPERTS):
            rk = rank_ref[k, e:e + 1, :]
            onehot = jnp.where(r_io == rk, 1.0, 0.0).astype(BF16)
            a_hi, a_mid, a_lo = _split3(aff_ref[k, e:e + 1, :])
            vals = jnp.concatenate([t_hi, t_lo, a_hi.astype(F32), a_mid.astype(F32), a_lo.astype(F32), zeros],
                                   axis=0).astype(BF16)
            comp = _dot_nt(vals, onehot)
            base = base_ref[c, e]
            blk = base // CHUNK
            off = base - blk * CHUNK
            rolled = pltpu.roll(comp, off, axis=1)
            keep_lo = lane >= off
            out_ref[e, blk] = jnp.where(keep_lo, rolled, out_ref[e, blk])
            out_ref[e, blk + 1] = jnp.where(keep_lo, out_ref[e, blk + 1], rolled)


def _route_compact(base_s, rank, aff, n_blk):
    nc = rank.shape[0]
    group = next(g for g in (4, 3, 2, 1) if nc % g == 0)
    blk = pl.BlockSpec((group, N_EXPERTS, CHUNK), lambda c, b: (c, 0, 0))
    return pl.pallas_call(
        functools.partial(_route_compact_kernel, group=group),
        grid_spec=pltpu.PrefetchScalarGridSpec(
            num_scalar_prefetch=1, grid=(nc // group,),
            in_specs=[blk, blk],
            out_specs=pl.BlockSpec((N_EXPERTS, n_blk, LIST_ROWS, CHUNK), lambda c, b: (0, 0, 0, 0))),
        out_shape=jax.ShapeDtypeStruct((N_EXPERTS, n_blk, LIST_ROWS, CHUNK), F32),
        compiler_params=_cparams(("arbitrary",)),
        name="route_compact",
    )(base_s, rank, aff)


def _ffn_kernel(idx_ref, idxn_ref, gate_ref, xp_hbm, wg_ref, wu_ref, wd_ref, ye_ref,
                rows_ref, xp_ref, hid_ref, sem, *, tm, rs, nf, tf):
    e, s, j = pl.program_id(0), pl.program_id(1), pl.program_id(2)
    tile = e * pl.num_programs(1) + s
    n_sub = tm // rs
    per_slot = tm // (nf * n_sub)

    def relayout():
        rb = tm // RELAYOUT_BLOCKS

        def body(r, carry):
            sl = pl.ds(pl.multiple_of(r * rb, 8), rb)
            y = pltpu.einshape("tcl->ctl", rows_ref[sl])
            for c in range(PACK_SUB):
                xp_ref[sl, c * LANES:(c + 1) * LANES] = y[c]
            return carry

        lax.fori_loop(0, RELAYOUT_BLOCKS, body, 0)

    def row_copy(iref, r):
        return pltpu.make_async_copy(xp_hbm.at[pl.ds(iref[0, 0, r], 1)], rows_ref.at[pl.ds(r, 1)], sem.at[0])

    def wait_rows():
        pltpu.make_async_copy(xp_hbm.at[pl.ds(0, tm)], rows_ref, sem.at[0]).wait()

    def prefetch_next(r):
        first = (j * n_sub + r) * per_slot
        for k in range(per_slot):
            row_copy(idxn_ref, first + k).start()

    def rows(r):
        return pl.ds(pl.multiple_of(r * rs, BF16_ROWS), rs)

    @pl.when((tile == 0) & (j == 0))
    def _():
        def body(r, carry):
            row_copy(idx_ref, r).start()
            return carry

        lax.fori_loop(0, tm, body, 0)
        wait_rows()
        relayout()

    @pl.when(j < nf)
    def _():
        wg = wg_ref[0].astype(BF16)
        wu = wu_ref[0].astype(BF16)

        def body(r, carry):
            prefetch_next(r)
            lo, hi = _unpack_bf16_pairs(xp_ref[rows(r), :])
            x = jnp.concatenate([lo, hi], axis=1)
            g = _dot(x, wg)
            u = _dot(x, wu)
            hid_ref[j, rows(r), :] = (g * jax.nn.sigmoid(g) * u).astype(BF16)
            return carry

        lax.fori_loop(0, n_sub, body, 0)

    @pl.when(j == nf)
    def _():
        wait_rows()
        relayout()

    @pl.when(j >= nf)
    def _():
        wd = wd_ref[0].astype(BF16)

        def body(r, carry):
            acc = _dot(hid_ref[0, rows(r), :], wd[0:tf])
            for f in range(1, nf):
                acc = acc + _dot(hid_ref[f, rows(r), :], wd[f * tf:(f + 1) * tf])
            ye_ref[0, rows(r), :] = (acc * gate_ref[0, rows(r), :]).astype(ye_ref.dtype)
            return carry

        lax.fori_loop(0, n_sub, body, 0)


def _expert_ffn(idx3, gate3, xp, wg, wu, wd, cap_p, tm, tf):
    ns = cap_p // tm
    nf = EXPERT_FF // tf
    n_tiles = N_EXPERTS * ns
    rs = tm // FFN_ROW_SPLIT
    assert rs % BF16_ROWS == 0 and D_MODEL // tf == nf and tm % (nf * (tm // rs)) == 0
    assert tm % (8 * RELAYOUT_BLOCKS) == 0
    return pl.pallas_call(
        functools.partial(_ffn_kernel, tm=tm, rs=rs, nf=nf, tf=tf),
        grid=(N_EXPERTS, ns, 2 * nf),
        in_specs=[
            pl.BlockSpec((1, 1, tm), lambda e, s, j: (e * ns + s, 0, 0), memory_space=pltpu.SMEM),
            pl.BlockSpec((1, 1, tm), lambda e, s, j: (jnp.minimum(e * ns + s + 1, n_tiles - 1), 0, 0),
                         memory_space=pltpu.SMEM),
            pl.BlockSpec((1, tm, 1), lambda e, s, j: (e, s, 0)),
            pl.BlockSpec(memory_space=pl.ANY),
            pl.BlockSpec((1, D_MODEL, tf), lambda e, s, j: (e, 0, jnp.minimum(j, nf - 1))),
            pl.BlockSpec((1, D_MODEL, tf), lambda e, s, j: (e, 0, jnp.minimum(j, nf - 1))),
            pl.BlockSpec((1, EXPERT_FF, tf), lambda e, s, j: (e, 0, jnp.maximum(j - nf, 0))),
        ],
        out_specs=pl.BlockSpec((1, tm, tf), lambda e, s, j: (e, s, jnp.maximum(j - nf, 0))),
        out_shape=jax.ShapeDtypeStruct((N_EXPERTS, cap_p, D_MODEL), BF16),
        scratch_shapes=[
            pltpu.VMEM((tm, PACK_SUB, LANES), I32),
            pltpu.VMEM((tm, HALF_D), I32),
            pltpu.VMEM((nf, tm, tf), BF16),
            pltpu.SemaphoreType.DMA((1,)),
        ],
        compiler_params=_cparams(("arbitrary", "arbitrary", "arbitrary")),
        name="expert_ffn",
    )(idx3, idx3, gate3, xp, wg, wu, wd)


COMB_ROWS = CHUNK + BF16_ROWS


COMB_WIN = 48


def _combine_kernel(base_ref, x1_ref, rkt_ref, ye_hbm, g2_ref, b2_ref, y_ref, buf, big, acc_ref, sem, bsem,
                    *, n_tiles, cap_p):
    i = pl.program_id(0)

    def window(c, e):
        b = base_ref[c, e]
        st = jnp.minimum((b // BF16_ROWS) * BF16_ROWS, cap_p - COMB_WIN)
        fits = base_ref[c + 1, e] - st <= COMB_WIN
        return b, st, fits

    def copy(c, e, slot):
        st = pl.multiple_of(window(c, e)[1], BF16_ROWS)
        return pltpu.make_async_copy(ye_hbm.at[e, pl.ds(st, COMB_WIN), :],
                                     buf.at[slot, pl.ds(e * COMB_WIN, COMB_WIN), :], sem.at[slot])

    @pl.when(i == 0)
    def _():
        for e in range(N_EXPERTS):
            copy(0, e, 0).start()

    @pl.when(i + 1 < n_tiles)
    def _():
        for e in range(N_EXPERTS):
            copy(i + 1, e, (i + 1) % 2).start()

    slot = i % 2
    rkt = rkt_ref[0]
    lane = lax.broadcasted_iota(I32, (1, LANES), 1)
    pos = []
    for e in range(N_EXPERTS):
        b, st, fits = window(i, e)
        rk = rkt[:, e:e + 1]
        pos.append(jnp.where((rk >= 0) & fits, rk + (b - st + e * COMB_WIN), -1))
    pieces = []
    for p in range(N_EXPERTS * COMB_WIN // LANES):
        lo, hi = p * LANES, (p + 1) * LANES
        target = None
        for e in range(N_EXPERTS):
            e_lo, e_hi = max(e * COMB_WIN, lo), min((e + 1) * COMB_WIN, hi)
            if e_lo >= e_hi:
                continue
            cand = pos[e] - lo
            if target is None:
                target = cand
            else:
                target = jnp.where((lane >= e_lo - lo) & (lane < e_hi - lo), cand, target)
        pieces.append(jnp.where(target == lane, 1.0, 0.0).astype(BF16))
    onehot = jnp.concatenate(pieces, axis=1)
    pltpu.make_async_copy(buf.at[slot], buf.at[slot], sem.at[slot]).wait()
    acc_ref[...] = ALPHA * x1_ref[...] + _dot(onehot, buf[slot])

    r_io = lax.broadcasted_iota(I32, (1, COMB_ROWS), 1)
    for e in range(N_EXPERTS):
        b, _, fits = window(i, e)

        @pl.when(jnp.logical_not(fits))
        def _():
            st = pl.multiple_of(jnp.minimum((b // BF16_ROWS) * BF16_ROWS, cap_p - COMB_ROWS), BF16_ROWS)
            cp = pltpu.make_async_copy(ye_hbm.at[e, pl.ds(st, COMB_ROWS), :], big, bsem.at[0])
            cp.start()
            cp.wait()
            rk = rkt[:, e:e + 1]
            srel = jnp.where(rk >= 0, rk + (b - st), -1)
            acc_ref[...] += _dot(jnp.where(srel == r_io, 1.0, 0.0).astype(BF16), big[...])

    y_ref[...] = _ln(acc_ref[...], g2_ref[...], b2_ref[...])


def _combine(base_s, x1, rank_t, ye, g2, b2, cap_p):
    assert base_s.shape[0] > x1.shape[0] // CHUNK
    t = x1.shape[0]
    n_tiles = t // CHUNK
    const = lambda i, b: (0, 0)
    return pl.pallas_call(
        functools.partial(_combine_kernel, n_tiles=n_tiles, cap_p=cap_p),
        grid_spec=pltpu.PrefetchScalarGridSpec(
            num_scalar_prefetch=1, grid=(n_tiles,),
            in_specs=[
                pl.BlockSpec((CHUNK, D_MODEL), lambda i, b: (i, 0)),
                pl.BlockSpec((1, CHUNK, N_EXPERTS), lambda i, b: (i, 0, 0)),
                pl.BlockSpec(memory_space=pl.ANY),
                pl.BlockSpec((1, D_MODEL), const), pl.BlockSpec((1, D_MODEL), const),
            ],
            out_specs=pl.BlockSpec((CHUNK, D_MODEL), lambda i, b: (i, 0)),
            scratch_shapes=[
                pltpu.VMEM((2, N_EXPERTS * COMB_WIN, D_MODEL), BF16),
                pltpu.VMEM((COMB_ROWS, D_MODEL), BF16),
                pltpu.VMEM((CHUNK, D_MODEL), F32),
                pltpu.SemaphoreType.DMA((2,)),
                pltpu.SemaphoreType.DMA((1,)),
            ]),
        out_shape=jax.ShapeDtypeStruct((t, D_MODEL), F32),
        compiler_params=_cparams(("arbitrary",)),
        name="moe_combine_ln2",
    )(base_s, x1, rank_t, ye, g2, b2)


def _row(v):
    return v.reshape(1, -1).astype(F32)


def _attn_bias_variants(bias_main, bias_meta0):
    half = CHUNK // 2
    dead = jnp.full((AT_HEADS, CHUNK, CHUNK), NEG, F32)
    dead_pad = jnp.full((AT_HEADS, half, half - N_META), NEG, F32)

    def halves(meta, prev, cur, nxt):
        top = jnp.concatenate([prev[:, :half], cur[:, :half], nxt[:, :half, :half], meta[:, :half], dead_pad], axis=2)
        bot = jnp.concatenate([prev[:, half:, half:], meta[:, half:], dead_pad, cur[:, half:], nxt[:, half:]], axis=2)
        return jnp.stack([top, bot])

    meta, prev = bias_main[:, :, :N_META], bias_main[:, :, CHUNK:2 * CHUNK]
    cur, nxt = bias_main[:, :, 2 * CHUNK:3 * CHUNK], bias_main[:, :, 3 * CHUNK:]
    tabs = jnp.stack([halves(meta, prev, cur, nxt),
                      halves(bias_meta0[:, :, :N_META], dead, cur, nxt),
                      halves(meta, prev, cur, dead)]) * LOG2E
    tabs = tabs.reshape(3, 2, AT_KV_HEADS, AT_GROUP * half, 3 * CHUNK).transpose(0, 2, 1, 3, 4)
    return tabs.reshape(3, AT_KV_HEADS, 2 * AT_GROUP * half, 3 * CHUNK)


def _pick_tile(n, pref):
    tm = pref
    while n % tm:
        tm //= 2
    return tm


def _trunk(x, p, meta):
    batch, seq, _ = x.shape
    nb = seq // CHUNK
    t = batch * seq
    bm = batch * N_META
    x2 = x.reshape(t, D_MODEL)
    pm_m, kt_m, gc_m, gt_m = meta
    pm, kt, gc, gt = _ln_proj(x2, p["g0"], p["b0"], p["w_main"], p["w_kt"], p["w_g"], p["w_gt"], p["bg"], p["bgt"],
                              _pick_tile(t, PM_TM))
    hb, hb_m = _mlstm(pm, kt, gc, gt, pm_m, kt_m, gc_m, gt_m, batch, nb, reverse=True)
    ha, ha_m = _mlstm(pm, kt, gc, gt, pm_m, kt_m, gc_m, gt_m, batch, nb, reverse=False,
                      merged=(hb, hb_m, p["ng"]))
    at = _attn_real(pm, pm_m, p["bias3"], p["sink_col"], batch, nb)
    at_m = _attn_meta(pm, pm_m, p["bias_mq"], p["sink_b"], batch, nb)

    mix = _mix(ha, at, pm, p["wa"], p["wb"], _pick_tile(t, 512), PM_GA // MIX_TN, PM_GB // MIX_TN)
    gates_m = jnp.tile(pm_m[:, PM_GA:PM_MQ], (batch, 1))
    mix_m = _mix(ha_m, at_m, gates_m, p["wa"], p["wb"], bm, 0, D_MODEL // MIX_TN)

    ln_args = (p["g0"], p["b0"], p["g1"], p["b1"], p["wo"], p["wrt"])
    x1, xp, aff = _out_ln_router(mix, x2, *ln_args, _pick_tile(t, OUT_TM), bm)
    xm = jnp.tile(p["meta_tokens"], (batch, 1))
    xp, aff_m = _out_ln_router(mix_m, xm, *ln_args, bm, bm, packed=xp)

    n_tok = t + bm
    nc = -(-n_tok // CHUNK)
    aff_m = jnp.pad(aff_m, ((0, 0), (0, nc * CHUNK - n_tok)), constant_values=-1.0)
    aff_m = aff_m.reshape(N_EXPERTS, -1, CHUNK).transpose(1, 0, 2)
    aff_all = jnp.concatenate([aff, aff_m], axis=0)
    cap = CAPACITY_FACTOR * n_tok // N_EXPERTS
    cap_p = -(-cap // CHUNK) * CHUNK
    rank, base = _route_select(aff_all, cap)
    base_s = base[:, :, 0]
    n_blk = cap_p // CHUNK + 2
    lists = _route_compact(base_s, rank, aff_all, n_blk)
    lists = lists.transpose(0, 2, 1, 3).reshape(N_EXPERTS, LIST_ROWS, n_blk * CHUNK)[:, :, :cap_p]
    idx = (lists[:, 0] * 256.0 + lists[:, 1]).astype(I32)
    gate = lists[:, 2] + lists[:, 3] + lists[:, 4]
    ns = -(-cap_p // FFN_MAX_ROWS)
    tm = cap_p // ns
    ye = _expert_ffn(idx.reshape(N_EXPERTS * ns, 1, tm), gate.reshape(N_EXPERTS, cap_p, 1), xp,
                     p["wgate"], p["wup"], p["wdown"], cap_p, tm, FFN_TF)
    rank_t = rank[:t // CHUNK].transpose(0, 2, 1)
    y = _combine(base_s, x1, rank_t, ye, p["g2"], p["b2"], cap_p)
    return y.reshape(batch, seq, D_MODEL)


def kernel(x_prompt, x_sample, meta_tokens, ln0_g, ln0_b, rel_bias, attn_sink, w_in, b_gate, ml_norm_g,
           w_branch_a, w_branch_b, w_out, ln1_g, ln1_b, w_router, w_gate, w_up, w_down, ln2_g, ln2_b):
    assert w_in.shape[0] == DEPTH
    w = w_in[0]
    sizes = (ML_HEADS * ML_DQK, ML_HEADS * ML_DQK, ML_HEADS * ML_DV, ML_HEADS * ML_DV, 4 * ML_HEADS,
             AT_HEADS * AT_DH, AT_KV_HEADS * AT_DH, AT_KV_HEADS * AT_DH, D_MODEL, D_MODEL)
    offs = np.concatenate([[0], np.cumsum(sizes)])
    mq, mk, mv, mo, mg, aq, ak, av, ga, gb = [w[:, offs[i]:offs[i + 1]] for i in range(10)]
    w_g = jnp.pad(mg, ((0, 0), (0, LANES - 4 * ML_HEADS))).astype(BF16)
    bg = jnp.pad(b_gate[0].astype(F32), (0, LANES - 4 * ML_HEADS))
    nb_max = max(x_prompt.shape[1], x_sample.shape[1]) // CHUNK
    bk_main, bk_meta0, bk_mq = _bucket_tables(nb_max)
    rb = rel_bias.astype(F32)
    p = {
        "g0": _row(ln0_g), "b0": _row(ln0_b), "g1": _row(ln1_g[0]), "b1": _row(ln1_b[0]),
        "g2": _row(ln2_g[0]), "b2": _row(ln2_b[0]), "ng": _row(ml_norm_g[0]),
        "w_main": jnp.concatenate([mv, mo, aq, ga, gb, mq, ak, av], axis=1).astype(BF16),
        "w_kt": mk.T.astype(BF16), "w_g": w_g, "w_gt": w_g.T,
        "bg": bg.reshape(1, LANES), "bgt": bg.reshape(LANES, 1),
        "wa": w_branch_a[0].astype(BF16), "wb": w_branch_b[0].astype(BF16), "wo": w_out[0].astype(BF16),
        "wrt": w_router[0].T.astype(BF16),
        "wgate": w_gate[0], "wup": w_up[0], "wdown": w_down[0],
        "bias3": _attn_bias_variants(_bias_table(rb, bk_main), _bias_table(rb, bk_meta0)),
        "sink_col": jnp.broadcast_to(
            jnp.tile(jnp.repeat(attn_sink[0].astype(F32).reshape(AT_KV_HEADS, AT_GROUP), CHUNK // 2, axis=1),
                     (1, 2))[:, :, None] * LOG2E, (AT_KV_HEADS, AT_GROUP * CHUNK, LANES)),
        "bias_mq": _bias_table(rb, bk_mq),
        "sink_b": jnp.broadcast_to(attn_sink[0].astype(F32)[:, None], (AT_HEADS, LANES)),
        "meta_tokens": meta_tokens.astype(F32),
    }
    meta = _ln_proj(p["meta_tokens"], p["g0"], p["b0"], p["w_main"], p["w_kt"], p["w_g"], p["w_gt"],
                    p["bg"], p["bgt"], N_META)
    return (_trunk(x_prompt, p, meta), _trunk(x_sample, p, meta))
```

```python
import functools
import math

import numpy as np
import jax
import jax.numpy as jnp
from jax import lax
from jax.experimental import pallas as pl
from jax.experimental.pallas import tpu as pltpu

F32 = jnp.float32
BF16 = jnp.bfloat16
I32 = jnp.int32

D_MODEL = 2048
N_META = 16
CHUNK = 128
ML_HEADS = 8
ML_DV = D_MODEL // ML_HEADS
ML_DQK = ML_DV // 2
AT_DH = 128
AT_HEADS = D_MODEL // AT_DH
AT_KV_HEADS = AT_HEADS // 4
AT_GROUP = AT_HEADS // AT_KV_HEADS
WINDOW = 128
REL_BUCKETS = 32
REL_MAX_DIST = 128
N_EXPERTS = 16
EXPERT_FF = D_MODEL
CAPACITY_FACTOR = 2
DEPTH = 1
ALPHA = (2.0 * DEPTH) ** 0.25
LN_EPS = 1e-5
M_INIT = -1e30
NEG = -1e30

LANES = 128
BF16_ROWS = 16
VMEM_LIMIT = 56 * 1024 * 1024

PM_MV, PM_MO, PM_AQ, PM_GA, PM_GB, PM_MQ, PM_AK, PM_AV = 0, 2048, 4096, 6144, 8192, 10240, 11264, 11776
PM_WIDTH = 12288
PM_TN = 2048
PM_TM = 512
LN_ROWS = 256
MIX_TN = 1024
OUT_TM = 512
CT_W = ML_DV + LANES

NT_DIMS = (((1,), (1,)), ((), ()))


def _cparams(sem, vmem=VMEM_LIMIT):
    return pltpu.CompilerParams(dimension_semantics=sem, vmem_limit_bytes=vmem)


def _dot(a, b):
    return jnp.dot(a, b, preferred_element_type=F32)


def _dot_nt(a, b):
    return lax.dot_general(a, b, NT_DIMS, preferred_element_type=F32)


def _ln(x, g, b):
    mu = jnp.mean(x, axis=-1, keepdims=True)
    xc = x - mu
    var = jnp.mean(xc * xc, axis=-1, keepdims=True)
    return xc * lax.rsqrt(var + LN_EPS) * g + b


def _split3(x):
    hi = x.astype(BF16)
    r1 = x - hi.astype(F32)
    mid = r1.astype(BF16)
    lo = (r1 - mid.astype(F32)).astype(BF16)
    return hi, mid, lo


def _bias_kernel(rb_ref, bk_ref, o_ref):
    h = pl.program_id(0)
    bk = bk_ref[...]
    acc = jnp.full(bk.shape, NEG, F32)
    for b in range(REL_BUCKETS):
        acc = jnp.where(bk == b, rb_ref[b, h], acc)
    o_ref[0] = acc


def _bias_table(rel_bias, bucket):
    r, c = bucket.shape
    return pl.pallas_call(
        _bias_kernel,
        grid_spec=pltpu.PrefetchScalarGridSpec(
            num_scalar_prefetch=1, grid=(AT_HEADS,),
            in_specs=[pl.BlockSpec((r, c), lambda h, rb: (0, 0))],
            out_specs=pl.BlockSpec((1, r, c), lambda h, rb: (h, 0, 0))),
        out_shape=jax.ShapeDtypeStruct((AT_HEADS, r, c), F32),
        compiler_params=_cparams(("arbitrary",)),
        name="bias_table",
    )(rel_bias, bucket)


def _t5_bucket(rel):
    half = REL_BUCKETS // 2
    max_exact = half // 2
    n = jnp.abs(rel)
    nf = jnp.maximum(n, 1).astype(jnp.float32)
    large = max_exact + (jnp.log(nf / max_exact) / math.log(REL_MAX_DIST / max_exact)
                         * (half - max_exact)).astype(jnp.int32)
    large = jnp.minimum(large, half - 1)
    return jnp.where(rel > 0, half, 0) + jnp.where(n < max_exact, n, large)


def _t5_bucket_np(rel):
    half = REL_BUCKETS // 2
    max_exact = half // 2
    n = np.abs(rel)
    nf = np.maximum(n, 1).astype(np.float64)
    large = max_exact + (np.log(nf / max_exact) / math.log(REL_MAX_DIST / max_exact) * (half - max_exact)).astype(np.int64)
    large = np.minimum(large, half - 1)
    return np.where(rel > 0, half, 0) + np.where(n < max_exact, n, large)


def _bucket_tables(nb_max):
    i = np.arange(CHUNK)[:, None]
    m = np.arange(N_META)[None, :]
    ref_tab = _t5_bucket_np(m - (N_META + CHUNK + i))
    for j in range(1, nb_max):
        assert np.array_equal(_t5_bucket_np(m - (N_META + j * CHUNK + i)), ref_tab)
    qi = jnp.arange(CHUNK, dtype=I32)[:, None]
    c = jnp.arange(4 * CHUNK, dtype=I32)[None, :]
    rel_meta = c - (N_META + CHUNK + qi)
    rel_nb = (c - CHUNK) - CHUNK - qi
    vis_nb = (c >= CHUNK) & (jnp.abs(rel_nb) <= WINDOW)
    main = jnp.where(c < N_META, _t5_bucket(rel_meta), jnp.where(vis_nb, _t5_bucket(rel_nb), -1))
    c1 = jnp.arange(CHUNK, dtype=I32)[None, :]
    meta0 = jnp.where(c1 < N_META, _t5_bucket(c1 - (N_META + qi)), -1)
    mi = jnp.arange(N_META, dtype=I32)[:, None]
    c2 = jnp.arange(2 * CHUNK, dtype=I32)[None, :]
    rel_r = N_META + (c2 - CHUNK) - mi
    vis_r = (c2 >= CHUNK) & (jnp.abs(rel_r) <= WINDOW)
    mq = jnp.where(c2 < N_META, _t5_bucket(c2 - mi), jnp.where(vis_r, _t5_bucket(rel_r), -1))
    return main.astype(I32), meta0.astype(I32), mq.astype(I32)


def _proj_kernel(x_ref, g0_ref, b0_ref, w_ref, wkt_ref, wg_ref, wgt_ref, bg_ref, bgt_ref,
                 pm_ref, kt_ref, gc_ref, gt_ref, xs_ref):
    n = pl.program_id(1)

    @pl.when(n == 0)
    def _():
        tm = x_ref.shape[0]
        rb = min(tm, LN_ROWS)

        def ln_block(r, carry):
            sl = pl.ds(pl.multiple_of(r * rb, BF16_ROWS), rb)
            xs_ref[sl, :] = _ln(x_ref[sl, :], g0_ref[...], b0_ref[...]).astype(BF16)
            return carry

        lax.fori_loop(0, tm // rb, ln_block, 0)
        xn = xs_ref[...]
        kt_ref[...] = _dot_nt(wkt_ref[...], xn).astype(BF16)
        gc = _dot(xn, wg_ref[...]) + bg_ref[...]
        gc_ref[...] = gc
        if tm % LANES == 0:
            gt_ref[...] = gc.T
        else:
            gt_ref[...] = _dot_nt(wgt_ref[...], xn) + bgt_ref[...]

    y = _dot(xs_ref[...], w_ref[...])
    lo1, hi1 = PM_MO // PM_TN, PM_AQ // PM_TN
    lo2, hi2 = PM_GA // PM_TN, PM_MQ // PM_TN
    is_sig = ((n >= lo1) & (n < hi1)) | ((n >= lo2) & (n < hi2))
    pm_ref[...] = jnp.where(is_sig, jax.nn.sigmoid(y), y).astype(BF16)


def _ln_proj(x2, g0, b0, w_main, w_kt, w_g, w_gt, bg, bgt, tm):
    t = x2.shape[0]
    grid = (t // tm, PM_WIDTH // PM_TN)
    const = lambda i, n: (0, 0)
    return pl.pallas_call(
        _proj_kernel,
        grid=grid,
        in_specs=[
            pl.BlockSpec((tm, D_MODEL), lambda i, n: (i, 0)),
            pl.BlockSpec((1, D_MODEL), const),
            pl.BlockSpec((1, D_MODEL), const),
            pl.BlockSpec((D_MODEL, PM_TN), lambda i, n: (0, n)),
            pl.BlockSpec((ML_HEADS * ML_DQK, D_MODEL), const, pipeline_mode=pl.Buffered(1)),
            pl.BlockSpec((D_MODEL, LANES), const),
            pl.BlockSpec((LANES, D_MODEL), const),
            pl.BlockSpec((1, LANES), const),
            pl.BlockSpec((LANES, 1), const),
        ],
        out_specs=[
            pl.BlockSpec((tm, PM_TN), lambda i, n: (i, n)),
            pl.BlockSpec((ML_HEADS * ML_DQK, tm), lambda i, n: (0, i)),
            pl.BlockSpec((tm, LANES), lambda i, n: (i, 0)),
            pl.BlockSpec((LANES, tm), lambda i, n: (0, i)),
        ],
        out_shape=[
            jax.ShapeDtypeStruct((t, PM_WIDTH), BF16),
            jax.ShapeDtypeStruct((ML_HEADS * ML_DQK, t), BF16),
            jax.ShapeDtypeStruct((t, LANES), F32),
            jax.ShapeDtypeStruct((LANES, t), F32),
        ],
        scratch_shapes=[pltpu.VMEM((tm, D_MODEL), BF16)],
        compiler_params=_cparams(("arbitrary", "arbitrary")),
        name="ln_proj",
    )(x2, g0, b0, w_main, w_kt, w_g, w_gt, bg, bgt)


def _mlstm_chunk(L, reverse, q_ref, kt_ref, v_ref, gc_ref, gt_ref, ct_ref, m_ref, finish):
    H = ML_HEADS
    row = lax.broadcasted_iota(I32, (L, L), 0)
    col = lax.broadcasted_iota(I32, (L, L), 1)
    if reverse:
        tri = row <= col
        tri_t = row >= col
    else:
        tri = row >= col
        tri_t = row <= col
    tri_b = jnp.where(tri, 1.0, 0.0).astype(BF16)
    tri_tb = jnp.where(tri_t, 1.0, 0.0).astype(BF16)
    gc = gc_ref[...] * LOG2E
    gt = gt_ref[...] * LOG2E
    lfc = jax.nn.log_sigmoid(gc_ref[...]) * LOG2E
    lft = jax.nn.log_sigmoid(gt_ref[...]) * LOG2E
    bc_all = sum(_dot(tri_b, p) for p in _split3(lfc))
    br_all = sum(_dot(p, tri_tb) for p in _split3(lft))
    end = 0 if reverse else L - 1
    ci0, cf0 = (2 * H, 3 * H) if reverse else (0, H)
    scale = ML_DQK ** -0.5

    def stack(parts):
        return jnp.concatenate(parts, axis=0)

    def cols(x_all, c0):
        return stack([jnp.broadcast_to(x_all[:, c0 + h:c0 + h + 1], (L, L)) for h in range(H)])

    def rows_b(x_t, c0):
        return stack([jnp.broadcast_to(x_t[c0 + h:c0 + h + 1, :], (L, L)) for h in range(H)])

    def per_head(vals):
        return stack([jnp.broadcast_to(x, (L, L)) for x in vals])

    def rep(x):
        return jnp.broadcast_to(x, (H * L, L))

    def wide(x, n):
        if L == LANES and n % LANES == 0:
            return jnp.concatenate([x] * (n // LANES), axis=1)
        return jnp.broadcast_to(x[:, 0:1], (H * L, n))

    def wide_row(x, n):
        if L == LANES and n % LANES == 0:
            return jnp.concatenate([x] * (n // LANES), axis=1)
        return jnp.broadcast_to(x[:, 0:1], (1, n))

    def head(x, h):
        return x[h * L:(h + 1) * L]

    q = [q_ref[:, h * ML_DQK:(h + 1) * ML_DQK] for h in range(H)]
    kt = [kt_ref[h * ML_DQK:(h + 1) * ML_DQK, :] for h in range(H)]
    v = [v_ref[:, h * ML_DV:(h + 1) * ML_DV] for h in range(H)]
    ct = [ct_ref[h] for h in range(H)]
    m_prev_h = [m_ref[h:h + 1, 0:L] for h in range(H)]
    btot_h = [jnp.broadcast_to(bc_all[end:end + 1, cf0 + h:cf0 + h + 1], (1, L)) for h in range(H)]

    bc = cols(bc_all, cf0)
    igc = cols(gc, ci0)
    m_prev = per_head(m_prev_h)
    btot = per_head(btot_h)
    mask = stack([tri] * H)
    a_rows = gt[ci0:ci0 + H, :] - br_all[cf0:cf0 + H, :]
    d = jnp.where(mask, bc + rows_b(a_rows, 0), NEG)
    inter = bc + m_prev
    m_comb = jnp.maximum(inter, rep(jnp.max(d, axis=1, keepdims=True)))
    w = jnp.exp2(d - m_comb)
    w_inter = jnp.exp2(inter - m_comb) * scale
    s = stack([_dot(q[h], kt[h]) for h in range(H)]) * w * scale
    qc = stack([_dot(q[h], ct[h].astype(BF16)) for h in range(H)])
    sb = s.astype(BF16)
    num = stack([_dot(head(sb, h), v[h]) for h in range(H)]) + wide(w_inter, ML_DV) * qc[:, :ML_DV]
    den = rep(jnp.sum(s, axis=1, keepdims=True)) + w_inter * qc[:, ML_DV:ML_DV + L]
    finish(num * wide(1.0 / jnp.maximum(jnp.abs(den), jnp.exp2(-m_comb)), ML_DV))

    dec = btot - bc + igc
    inter_end_h = [btot_h[h] + m_prev_h[h] for h in range(H)]
    m_new_h = [jnp.maximum(inter_end_h[h], jnp.max(head(dec, h), axis=0, keepdims=True)) for h in range(H)]
    w_end = jnp.exp2(dec - per_head(m_new_h))
    vf = stack([x.astype(F32) for x in v])
    wv = jnp.concatenate([(wide(w_end, ML_DV) * vf).astype(BF16), wide(w_end, LANES).astype(BF16)], axis=1)
    for h in range(H):
        w_prev = jnp.exp2(inter_end_h[h] - m_new_h[h])
        ct_ref[h] = wide_row(w_prev, CT_W) * ct[h] + _dot(kt[h], head(wv, h))
        m_ref[h:h + 1, :] = wide_row(m_new_h[h], LANES)


def _mlstm_kernel(*refs, reverse, nb, merge):
    (q_ref, kt_ref, v_ref, gc_ref, gt_ref, qm_ref, ktm_ref, vm_ref, gcm_ref, gtm_ref) = refs[:10]
    if merge:
        o_ref, om_ref, hb_ref, hbm_ref, ng_ref, out_ref, outm_ref, ct_ref, m_ref = refs[10:]
    else:
        out_ref, outm_ref, ct_ref, m_ref = refs[10:]
    j = pl.program_id(1)

    @pl.when(j == 0)
    def _():
        ct_ref[...] = jnp.zeros(ct_ref.shape, F32)
        m_ref[...] = jnp.full(m_ref.shape, M_INIT * LOG2E, F32)

    def make_emit(dst_ref, other_ref, gate_ref):
        def emit(hv):
            L = hv.shape[0] // ML_HEADS
            sls = [slice(h * ML_DV, (h + 1) * ML_DV) for h in range(ML_HEADS)]
            if merge:
                hs = hv + jnp.concatenate([other_ref[:, sl] for sl in sls], axis=0)
                mu = jnp.broadcast_to(jnp.mean(hs, axis=-1, keepdims=True), hs.shape)
                hc = hs - mu
                var = jnp.broadcast_to(jnp.mean(hc * hc, axis=-1, keepdims=True), hs.shape)
                ng = jnp.concatenate([jnp.broadcast_to(ng_ref[:, sl], (L, ML_DV)) for sl in sls], axis=0)
                og = jnp.concatenate([gate_ref[:, sl].astype(F32) for sl in sls], axis=0)
                hv = hc * lax.rsqrt(var + LN_EPS) * (ng * og)
            for h, sl in enumerate(sls):
                dst_ref[:, sl] = hv[h * L:(h + 1) * L].astype(dst_ref.dtype)
        return emit

    is_meta = (j == nb) if reverse else (j == 0)

    @pl.when(is_meta)
    def _():
        emit = make_emit(outm_ref, hbm_ref if merge else None, om_ref if merge else None)
        _mlstm_chunk(N_META, reverse, qm_ref, ktm_ref, vm_ref, gcm_ref, gtm_ref, ct_ref, m_ref, emit)

    @pl.when(jnp.logical_not(is_meta))
    def _():
        emit = make_emit(out_ref, hb_ref if merge else None, o_ref if merge else None)
        _mlstm_chunk(CHUNK, reverse, q_ref, kt_ref, v_ref, gc_ref, gt_ref, ct_ref, m_ref, emit)


def _mlstm(pm, kt, gc, gt, pm_m, kt_m, gc_m, gt_m, batch, nb, *, reverse, merged=None):
    t = pm.shape[0]
    if reverse:
        rblk = lambda b, j: b * nb + jnp.maximum(nb - 1 - j, 0)
    else:
        rblk = lambda b, j: b * nb + jnp.maximum(j - 1, 0)
    const = lambda b, j: (0, 0)
    in_specs = [
        pl.BlockSpec((CHUNK, ML_HEADS * ML_DQK), lambda b, j: (rblk(b, j), PM_MQ // (ML_HEADS * ML_DQK))),
        pl.BlockSpec((ML_HEADS * ML_DQK, CHUNK), lambda b, j: (0, rblk(b, j))),
        pl.BlockSpec((CHUNK, D_MODEL), lambda b, j: (rblk(b, j), PM_MV // D_MODEL)),
        pl.BlockSpec((CHUNK, LANES), lambda b, j: (rblk(b, j), 0)),
        pl.BlockSpec((LANES, CHUNK), lambda b, j: (0, rblk(b, j))),
        pl.BlockSpec((N_META, ML_HEADS * ML_DQK), lambda b, j: (0, PM_MQ // (ML_HEADS * ML_DQK))),
        pl.BlockSpec((ML_HEADS * ML_DQK, N_META), const),
        pl.BlockSpec((N_META, D_MODEL), lambda b, j: (0, PM_MV // D_MODEL)),
        pl.BlockSpec((N_META, LANES), const),
        pl.BlockSpec((LANES, N_META), const),
    ]
    args = [pm, kt, pm, gc, gt, pm_m, kt_m, pm_m, gc_m, gt_m]
    merge = merged is not None
    if merge:
        hb, hb_m, ng = merged
        in_specs += [
            pl.BlockSpec((CHUNK, D_MODEL), lambda b, j: (rblk(b, j), PM_MO // D_MODEL)),
            pl.BlockSpec((N_META, D_MODEL), lambda b, j: (0, PM_MO // D_MODEL)),
            pl.BlockSpec((CHUNK, D_MODEL), lambda b, j: (rblk(b, j), 0)),
            pl.BlockSpec((N_META, D_MODEL), lambda b, j: (b, 0)),
            pl.BlockSpec((1, D_MODEL), const),
        ]
        args += [pm, pm_m, hb, hb_m, ng]
    odt = BF16 if merge else F32
    return pl.pallas_call(
        functools.partial(_mlstm_kernel, reverse=reverse, nb=nb, merge=merge),
        grid=(batch, nb + 1),
        in_specs=in_specs,
        out_specs=[
            pl.BlockSpec((CHUNK, D_MODEL), lambda b, j: (rblk(b, j), 0)),
            pl.BlockSpec((N_META, D_MODEL), lambda b, j: (b, 0)),
        ],
        out_shape=[
            jax.ShapeDtypeStruct((t, D_MODEL), odt),
            jax.ShapeDtypeStruct((batch * N_META, D_MODEL), odt),
        ],
        scratch_shapes=[pltpu.VMEM((ML_HEADS, ML_DQK, CT_W), F32), pltpu.VMEM((ML_HEADS, LANES), F32)],
        compiler_params=_cparams(("arbitrary", "arbitrary")),
        name="mlstm_bwd" if reverse else "mlstm_fwd",
    )(*args)


def _softmax_pv(sg, snk, vcat):
    m = jnp.maximum(jnp.max(sg, axis=1, keepdims=True), snk)
    p = jnp.exp(sg - m)
    l = jnp.sum(p, axis=1, keepdims=True) + jnp.exp(snk - m)
    return _dot(p.astype(BF16), vcat) * (1.0 / l)


LOG2E = math.log2(math.e)


def _attn_kernel(q_ref, kp_ref, kc_ref, kn_ref, vp_ref, vc_ref, vn_ref, km_ref, vm_ref,
                 bias_ref, sink_ref, o_ref):
    half = CHUNK // 2
    pad = jnp.zeros((half - N_META, AT_DH), BF16)
    ones = jnp.ones((3 * CHUNK, LANES), BF16)
    scale2 = AT_DH ** -0.5 * LOG2E
    rows_u = AT_GROUP * half
    for c in range(AT_KV_HEADS):
        ks = slice(c * AT_DH, (c + 1) * AT_DH)

        def keys(u, m_ref, p_ref, c_ref, n_ref):
            if u == 0:
                return jnp.concatenate([p_ref[:, ks], c_ref[:, ks], n_ref[:half, ks], m_ref[:, ks], pad], axis=0)
            return jnp.concatenate([p_ref[half:, ks], m_ref[:, ks], pad, c_ref[:, ks], n_ref[:, ks]], axis=0)

        def queries(u):
            return jnp.concatenate(
                [q_ref[u * half:(u + 1) * half, (c * AT_GROUP + g) * AT_DH:(c * AT_GROUP + g + 1) * AT_DH]
                 for g in range(AT_GROUP)], axis=0)

        t = jnp.concatenate([_dot_nt(queries(u), keys(u, km_ref, kp_ref, kc_ref, kn_ref)) for u in range(2)],
                            axis=0) * scale2 + bias_ref[0, c]
        snk = sink_ref[c]
        m = jnp.maximum(jnp.broadcast_to(jnp.max(t, axis=1, keepdims=True), snk.shape), snk)
        p = jnp.exp2(t - jnp.concatenate([m] * 3, axis=1)).astype(BF16)
        ov = jnp.concatenate(
            [_dot(p[u * rows_u:(u + 1) * rows_u],
                  jnp.concatenate([keys(u, vm_ref, vp_ref, vc_ref, vn_ref), ones], axis=1)) for u in range(2)], axis=0)
        l = ov[:, AT_DH:] + jnp.exp2(snk - m)
        o = ov[:, :AT_DH] * (1.0 / l)
        for u in range(2):
            for g in range(AT_GROUP):
                h = c * AT_GROUP + g
                r0 = u * rows_u + g * half
                o_ref[u * half:(u + 1) * half, h * AT_DH:(h + 1) * AT_DH] = o[r0:r0 + half].astype(o_ref.dtype)


def _attn_real(pm, pm_m, bias3, sink_col, batch, nb):
    assert nb >= 2
    t = pm.shape[0]
    kvw = AT_KV_HEADS * AT_DH
    cur = lambda b, j: b * nb + j
    prv = lambda b, j: b * nb + jnp.maximum(j - 1, 0)
    nxt = lambda b, j: b * nb + jnp.minimum(j + 1, nb - 1)
    variant = lambda b, j: (jnp.where(j == 0, 1, jnp.where(j == nb - 1, 2, 0)), 0, 0, 0)
    kcol, vcol = PM_AK // kvw, PM_AV // kvw
    return pl.pallas_call(
        _attn_kernel,
        grid=(batch, nb),
        in_specs=[
            pl.BlockSpec((CHUNK, D_MODEL), lambda b, j: (cur(b, j), PM_AQ // D_MODEL)),
            pl.BlockSpec((CHUNK, kvw), lambda b, j: (prv(b, j), kcol)),
            pl.BlockSpec((CHUNK, kvw), lambda b, j: (cur(b, j), kcol)),
            pl.BlockSpec((CHUNK, kvw), lambda b, j: (nxt(b, j), kcol)),
            pl.BlockSpec((CHUNK, kvw), lambda b, j: (prv(b, j), vcol)),
            pl.BlockSpec((CHUNK, kvw), lambda b, j: (cur(b, j), vcol)),
            pl.BlockSpec((CHUNK, kvw), lambda b, j: (nxt(b, j), vcol)),
            pl.BlockSpec((N_META, kvw), lambda b, j: (0, kcol)),
            pl.BlockSpec((N_META, kvw), lambda b, j: (0, vcol)),
            pl.BlockSpec((1, AT_KV_HEADS, AT_GROUP * CHUNK, 3 * CHUNK), variant),
            pl.BlockSpec((AT_KV_HEADS, AT_GROUP * CHUNK, LANES), lambda b, j: (0, 0, 0)),
        ],
        out_specs=pl.BlockSpec((CHUNK, D_MODEL), lambda b, j: (cur(b, j), 0)),
        out_shape=jax.ShapeDtypeStruct((t, D_MODEL), BF16),
        compiler_params=_cparams(("arbitrary", "arbitrary")),
        name="attn_real",
    )(pm, pm, pm, pm, pm, pm, pm, pm_m, pm_m, bias3, sink_col)


def _attn_meta_kernel(q_ref, kr_ref, vr_ref, km_ref, vm_ref, bias_ref, sink_ref, o_ref):
    pad = jnp.zeros((CHUNK - N_META, AT_DH), BF16)
    scale = AT_DH ** -0.5
    for c in range(AT_KV_HEADS):
        ks = slice(c * AT_DH, (c + 1) * AT_DH)
        kcat = jnp.concatenate([km_ref[:, ks], pad, kr_ref[:, ks]], axis=0)
        vcat = jnp.concatenate([vm_ref[:, ks], pad, vr_ref[:, ks]], axis=0)
        qc = jnp.concatenate(
            [q_ref[:, (c * AT_GROUP + g) * AT_DH:(c * AT_GROUP + g + 1) * AT_DH] for g in range(AT_GROUP)], axis=0)
        s = _dot_nt(qc, kcat) * scale
        for g in range(AT_GROUP):
            h = c * AT_GROUP + g
            sg = s[g * N_META:(g + 1) * N_META] + bias_ref[h]
            o = _softmax_pv(sg, sink_ref[h:h + 1, 0:1], vcat)
            o_ref[:, h * AT_DH:(h + 1) * AT_DH] = o.astype(o_ref.dtype)


def _attn_meta(pm, pm_m, bias_mq, sink_b, batch, nb):
    kvw = AT_KV_HEADS * AT_DH
    kcol, vcol = PM_AK // kvw, PM_AV // kvw
    return pl.pallas_call(
        _attn_meta_kernel,
        grid=(batch,),
        in_specs=[
            pl.BlockSpec((N_META, D_MODEL), lambda b: (0, PM_AQ // D_MODEL)),
            pl.BlockSpec((CHUNK, kvw), lambda b: (b * nb, kcol)),
            pl.BlockSpec((CHUNK, kvw), lambda b: (b * nb, vcol)),
            pl.BlockSpec((N_META, kvw), lambda b: (0, kcol)),
            pl.BlockSpec((N_META, kvw), lambda b: (0, vcol)),
            pl.BlockSpec((AT_HEADS, N_META, 2 * CHUNK), lambda b: (0, 0, 0)),
            pl.BlockSpec((AT_HEADS, LANES), lambda b: (0, 0)),
        ],
        out_specs=pl.BlockSpec((N_META, D_MODEL), lambda b: (b, 0)),
        out_shape=jax.ShapeDtypeStruct((batch * N_META, D_MODEL), BF16),
        compiler_params=_cparams(("arbitrary",)),
        name="attn_meta",
    )(pm_m, pm, pm, pm_m, pm_m, bias_mq, sink_b)


def _mix_kernel(ha_ref, hb_ref, ga_ref, gb_ref, wa_ref, wb_ref, o_ref):
    a = _dot(ha_ref[...], wa_ref[...])
    b = _dot(hb_ref[...], wb_ref[...])
    o_ref[...] = (ga_ref[...].astype(F32) * a + gb_ref[...].astype(F32) * b).astype(o_ref.dtype)


def _mix(ha, hb, gates_src, wa, wb, tm, ga_col, gb_col):
    t = ha.shape[0]
    nt = D_MODEL // MIX_TN
    return pl.pallas_call(
        _mix_kernel,
        grid=(t // tm, nt),
        in_specs=[
            pl.BlockSpec((tm, D_MODEL), lambda i, n: (i, 0)),
            pl.BlockSpec((tm, D_MODEL), lambda i, n: (i, 0)),
            pl.BlockSpec((tm, MIX_TN), lambda i, n: (i, ga_col + n)),
            pl.BlockSpec((tm, MIX_TN), lambda i, n: (i, gb_col + n)),
            pl.BlockSpec((D_MODEL, MIX_TN), lambda i, n: (0, n)),
            pl.BlockSpec((D_MODEL, MIX_TN), lambda i, n: (0, n)),
        ],
        out_specs=pl.BlockSpec((tm, MIX_TN), lambda i, n: (i, n)),
        out_shape=jax.ShapeDtypeStruct((t, D_MODEL), BF16),
        compiler_params=_cparams(("arbitrary", "arbitrary")),
        name="branch_mix",
    )(ha, hb, gates_src, gates_src, wa, wb)


HALF_D = D_MODEL // 2


def _pack_bf16_pairs(x):
    lo = lax.bitcast_convert_type(x[:, :HALF_D].astype(BF16).astype(F32), I32)
    hi = lax.bitcast_convert_type(x[:, HALF_D:].astype(BF16).astype(F32), I32)
    return lax.shift_right_logical(lo, 16) | (hi & jnp.int32(-65536))


def _unpack_bf16_pairs(u):
    lo = lax.bitcast_convert_type(u << 16, F32)
    hi = lax.bitcast_convert_type(u & jnp.int32(-65536), F32)
    return lo.astype(BF16), hi.astype(BF16)


PACK_SUB = HALF_D // LANES


def _out_kernel(*refs, real):
    mix_ref, x_ref, g0_ref, b0_ref, g1_ref, b1_ref, wo_ref, wrt_ref = refs[:8]
    if real:
        x1_ref, x1p_ref, aff_ref = refs[8:]
    else:
        x1p_ref, aff_ref = refs[9:]

    def body():
        y = _dot(mix_ref[...], wo_ref[...])
        h0a = _ln(x_ref[...], g0_ref[...], b0_ref[...])
        x1 = _ln(h0a + y, g1_ref[...], b1_ref[...])
        x1p_ref[...] = pltpu.einshape("t(cl)->tcl", _pack_bf16_pairs(x1), c=PACK_SUB)
        lt = _dot_nt(wrt_ref[...], x1.astype(BF16))
        e = jnp.exp(lt - jnp.max(lt, axis=0, keepdims=True))
        aff = e / jnp.sum(e, axis=0, keepdims=True)
        if real:
            x1_ref[...] = x1
            for c in range(aff.shape[1] // CHUNK):
                aff_ref[c] = aff[:, c * CHUNK:(c + 1) * CHUNK]
        else:
            aff_ref[...] = aff

    if not real:
        body()
        return
    last = pl.program_id(0) == pl.num_programs(0) - 1
    pl.when(jnp.logical_not(last))(body)

    @pl.when(last)
    def _():
        x1p_ref[...] = jnp.zeros(x1p_ref.shape, I32)


def _out_ln_router(mix, x2, g0, b0, g1, b1, wo, wrt, tm, n_meta_rows, packed=None):
    t = mix.shape[0]
    real = packed is None
    n = t // tm
    const = lambda i: (0, 0)
    row = lambda i: (jnp.minimum(i, n - 1), 0)
    in_specs = [
        pl.BlockSpec((tm, D_MODEL), row),
        pl.BlockSpec((tm, D_MODEL), row),
        pl.BlockSpec((1, D_MODEL), const), pl.BlockSpec((1, D_MODEL), const),
        pl.BlockSpec((1, D_MODEL), const), pl.BlockSpec((1, D_MODEL), const),
        pl.BlockSpec((D_MODEL, D_MODEL), const, pipeline_mode=pl.Buffered(1)),
        pl.BlockSpec((N_EXPERTS, D_MODEL), const),
    ]
    args = [mix, x2, g0, b0, g1, b1, wo, wrt]
    if real:
        assert n_meta_rows <= tm
        out_specs = [
            pl.BlockSpec((tm, D_MODEL), row),
            pl.BlockSpec((tm, PACK_SUB, LANES), lambda i: (i, 0, 0)),
            pl.BlockSpec((tm // CHUNK, N_EXPERTS, CHUNK), lambda i: (jnp.minimum(i, n - 1), 0, 0)),
        ]
        out_shape = [
            jax.ShapeDtypeStruct((t, D_MODEL), F32),
            jax.ShapeDtypeStruct((t + n_meta_rows, PACK_SUB, LANES), I32),
            jax.ShapeDtypeStruct((t // CHUNK, N_EXPERTS, CHUNK), F32),
        ]
        aliases = {}
    else:
        n_real = packed.shape[0] - t
        assert tm == t and n_real % t == 0
        in_specs.append(pl.BlockSpec(memory_space=pl.ANY))
        args.append(packed)
        out_specs = [
            pl.BlockSpec((t, PACK_SUB, LANES), lambda i: (n_real // t, 0, 0)),
            pl.BlockSpec((N_EXPERTS, t), const),
        ]
        out_shape = [jax.ShapeDtypeStruct(packed.shape, I32), jax.ShapeDtypeStruct((N_EXPERTS, t), F32)]
        aliases = {8: 0}
    return pl.pallas_call(
        functools.partial(_out_kernel, real=real),
        grid=(n + 1 if real else n,),
        in_specs=in_specs,
        out_specs=out_specs,
        out_shape=out_shape,
        input_output_aliases=aliases,
        compiler_params=_cparams(("arbitrary",)),
        name="out_ln1_router" if real else "out_ln1_router_meta",
    )(*args)


def _route_select_kernel(aff_ref, rank_ref, base_ref, p_s, b_s, *, cap, nc):
    aff = aff_ref[...]
    bits = lax.bitcast_convert_type(aff, I32)

    def count(maskf):
        return jnp.sum(jnp.sum(maskf, axis=0), axis=1, keepdims=True)

    def search(it, lo):
        cand = lo | (jnp.int32(1) << (30 - it))
        cnt = count(jnp.where(bits >= cand[None], 1.0, 0.0))
        return jnp.where(cnt >= cap, cand, lo)

    thr = lax.fori_loop(0, 31, search, jnp.zeros((N_EXPERTS, 1), I32))
    gt = bits > thr[None]
    eq = bits == thr[None]
    need = cap - count(jnp.where(gt, 1.0, 0.0))
    r_i = lax.broadcasted_iota(I32, (CHUNK, CHUNK), 0)
    c_i = lax.broadcasted_iota(I32, (CHUNK, CHUNK), 1)
    upper = jnp.where(r_i <= c_i, 1.0, 0.0).astype(BF16)

    def prefix(maskb):
        p = _dot(jnp.where(maskb, 1.0, 0.0).astype(BF16).reshape(nc * N_EXPERTS, CHUNK), upper)
        p_s[...] = p.reshape(nc, N_EXPERTS, CHUNK)

        def step(c, carry):
            b_s[c] = jnp.broadcast_to(carry, (N_EXPERTS, CHUNK))
            return carry + p_s[c][:, CHUNK - 1:CHUNK]

        lax.fori_loop(0, nc, step, jnp.zeros((N_EXPERTS, 1), F32))

    prefix(eq)
    eq_rank = b_s[...] + p_s[...] - 1.0
    sel = gt | (eq & (eq_rank < need[None]))
    prefix(sel)
    rank_ref[...] = jnp.where(sel, p_s[...] - 1.0, -1.0).astype(I32)
    base_ref[...] = b_s[...].astype(I32)


def _route_select(aff, cap):
    nc = aff.shape[0]
    shp = (nc, N_EXPERTS, CHUNK)
    full = pl.BlockSpec(shp, lambda i: (0, 0, 0))
    return pl.pallas_call(
        functools.partial(_route_select_kernel, cap=cap, nc=nc),
        grid=(1,),
        in_specs=[full],
        out_specs=[full, full],
        out_shape=[jax.ShapeDtypeStruct(shp, I32), jax.ShapeDtypeStruct(shp, I32)],
        scratch_shapes=[pltpu.VMEM(shp, F32), pltpu.VMEM(shp, F32)],
        compiler_params=_cparams(("arbitrary",)),
        name="route_select",
    )(aff)


FFN_MAX_ROWS = 2304
FFN_TF = 256
FFN_ROW_SPLIT = 2
RELAYOUT_BLOCKS = 8
LIST_ROWS = 16


def _route_compact_kernel(base_ref, rank_ref, aff_ref, out_ref, *, group):
    step = pl.program_id(0)

    @pl.when(step == 0)
    def _():
        out_ref[...] = jnp.zeros(out_ref.shape, F32)

    lane = lax.broadcasted_iota(I32, (1, CHUNK), 1)
    r_io = lax.broadcasted_iota(I32, (CHUNK, CHUNK), 0)
    zeros = jnp.zeros((LIST_ROWS - 5, CHUNK), F32)
    for k in range(group):
        c = step * group + k
        tok = c * CHUNK + lane
        t_hi = (tok >> 8).astype(F32)
        t_lo = (tok & 255).astype(F32)
        for e in range(N_EXPERTS):
            rk = rank_ref[k, e:e + 1, :]
            onehot = jnp.where(r_io == rk, 1.0, 0.0).astype(BF16)
            a_hi, a_mid, a_lo = _split3(aff_ref[k, e:e + 1, :])
            vals = jnp.concatenate([t_hi, t_lo, a_hi.astype(F32), a_mid.astype(F32), a_lo.astype(F32), zeros],
                                   axis=0).astype(BF16)
            comp = _dot_nt(vals, onehot)
            base = base_ref[c, e]
            blk = base // CHUNK
            off = base - blk * CHUNK
            rolled = pltpu.roll(comp, off, axis=1)
            keep_lo = lane >= off
            out_ref[e, blk] = jnp.where(keep_lo, rolled, out_ref[e, blk])
            out_ref[e, blk + 1] = jnp.where(keep_lo, out_ref[e, blk + 1], rolled)


def _route_compact(base_s, rank, aff, n_blk):
    nc = rank.shape[0]
    group = next(g for g in (4, 3, 2, 1) if nc % g == 0)
    blk = pl.BlockSpec((group, N_EXPERTS, CHUNK), lambda c, b: (c, 0, 0))
    return pl.pallas_call(
        functools.partial(_route_compact_kernel, group=group),
        grid_spec=pltpu.PrefetchScalarGridSpec(
            num_scalar_prefetch=1, grid=(nc // group,),
            in_specs=[blk, blk],
            out_specs=pl.BlockSpec((N_EXPERTS, n_blk, LIST_ROWS, CHUNK), lambda c, b: (0, 0, 0, 0))),
        out_shape=jax.ShapeDtypeStruct((N_EXPERTS, n_blk, LIST_ROWS, CHUNK), F32),
        compiler_params=_cparams(("arbitrary",)),
        name="route_compact",
    )(base_s, rank, aff)


def _ffn_kernel(idx_ref, idxn_ref, gate_ref, xp_hbm, wg_ref, wu_ref, wd_ref, ye_ref,
                rows_ref, xp_ref, hid_ref, sem, *, tm, rs, nf, tf):
    e, s, j = pl.program_id(0), pl.program_id(1), pl.program_id(2)
    tile = e * pl.num_programs(1) + s
    n_sub = tm // rs
    per_slot = tm // (nf * n_sub)

    def relayout():
        rb = tm // RELAYOUT_BLOCKS

        def body(r, carry):
            sl = pl.ds(pl.multiple_of(r * rb, 8), rb)
            y = pltpu.einshape("tcl->ctl", rows_ref[sl])
            for c in range(PACK_SUB):
                xp_ref[sl, c * LANES:(c + 1) * LANES] = y[c]
            return carry

        lax.fori_loop(0, RELAYOUT_BLOCKS, body, 0)

    def row_copy(iref, r):
        return pltpu.make_async_copy(xp_hbm.at[pl.ds(iref[0, 0, r], 1)], rows_ref.at[pl.ds(r, 1)], sem.at[0])

    def wait_rows():
        pltpu.make_async_copy(xp_hbm.at[pl.ds(0, tm)], rows_ref, sem.at[0]).wait()

    def prefetch_next(r):
        first = (j * n_sub + r) * per_slot
        for k in range(per_slot):
            row_copy(idxn_ref, first + k).start()

    def rows(r):
        return pl.ds(pl.multiple_of(r * rs, BF16_ROWS), rs)

    @pl.when((tile == 0) & (j == 0))
    def _():
        def body(r, carry):
            row_copy(idx_ref, r).start()
            return carry

        lax.fori_loop(0, tm, body, 0)
        wait_rows()
        relayout()

    @pl.when(j < nf)
    def _():
        wg = wg_ref[0].astype(BF16)
        wu = wu_ref[0].astype(BF16)

        def body(r, carry):
            prefetch_next(r)
            lo, hi = _unpack_bf16_pairs(xp_ref[rows(r), :])
            x = jnp.concatenate([lo, hi], axis=1)
            g = _dot(x, wg)
            u = _dot(x, wu)
            hid_ref[j, rows(r), :] = (g * jax.nn.sigmoid(g) * u).astype(BF16)
            return carry

        lax.fori_loop(0, n_sub, body, 0)

    @pl.when(j == nf)
    def _():
        wait_rows()
        relayout()

    @pl.when(j >= nf)
    def _():
        wd = wd_ref[0].astype(BF16)

        def body(r, carry):
            acc = _dot(hid_ref[0, rows(r), :], wd[0:tf])
            for f in range(1, nf):
                acc = acc + _dot(hid_ref[f, rows(r), :], wd[f * tf:(f + 1) * tf])
            ye_ref[0, rows(r), :] = (acc * gate_ref[0, rows(r), :]).astype(ye_ref.dtype)
            return carry

        lax.fori_loop(0, n_sub, body, 0)


def _expert_ffn(idx3, gate3, xp, wg, wu, wd, cap_p, tm, tf):
    ns = cap_p // tm
    nf = EXPERT_FF // tf
    n_tiles = N_EXPERTS * ns
    rs = tm // FFN_ROW_SPLIT
    assert rs % BF16_ROWS == 0 and D_MODEL // tf == nf and tm % (nf * (tm // rs)) == 0
    assert tm % (8 * RELAYOUT_BLOCKS) == 0
    return pl.pallas_call(
        functools.partial(_ffn_kernel, tm=tm, rs=rs, nf=nf, tf=tf),
        grid=(N_EXPERTS, ns, 2 * nf),
        in_specs=[
            pl.BlockSpec((1, 1, tm), lambda e, s, j: (e * ns + s, 0, 0), memory_space=pltpu.SMEM),
            pl.BlockSpec((1, 1, tm), lambda e, s, j: (jnp.minimum(e * ns + s + 1, n_tiles - 1), 0, 0),
                         memory_space=pltpu.SMEM),
            pl.BlockSpec((1, tm, 1), lambda e, s, j: (e, s, 0)),
            pl.BlockSpec(memory_space=pl.ANY),
            pl.BlockSpec((1, D_MODEL, tf), lambda e, s, j: (e, 0, jnp.minimum(j, nf - 1))),
            pl.BlockSpec((1, D_MODEL, tf), lambda e, s, j: (e, 0, jnp.minimum(j, nf - 1))),
            pl.BlockSpec((1, EXPERT_FF, tf), lambda e, s, j: (e, 0, jnp.maximum(j - nf, 0))),
        ],
        out_specs=pl.BlockSpec((1, tm, tf), lambda e, s, j: (e, s, jnp.maximum(j - nf, 0))),
        out_shape=jax.ShapeDtypeStruct((N_EXPERTS, cap_p, D_MODEL), BF16),
        scratch_shapes=[
            pltpu.VMEM((tm, PACK_SUB, LANES), I32),
            pltpu.VMEM((tm, HALF_D), I32),
            pltpu.VMEM((nf, tm, tf), BF16),
            pltpu.SemaphoreType.DMA((1,)),
        ],
        compiler_params=_cparams(("arbitrary", "arbitrary", "arbitrary")),
        name="expert_ffn",
    )(idx3, idx3, gate3, xp, wg, wu, wd)


COMB_ROWS = CHUNK + BF16_ROWS


COMB_WIN = 48


def _combine_kernel(base_ref, x1_ref, rkt_ref, ye_hbm, g2_ref, b2_ref, y_ref, buf, big, acc_ref, sem, bsem,
                    *, n_tiles, cap_p):
    i = pl.program_id(0)

    def window(c, e):
        b = base_ref[c, e]
        st = jnp.minimum((b // BF16_ROWS) * BF16_ROWS, cap_p - COMB_WIN)
        fits = base_ref[c + 1, e] - st <= COMB_WIN
        return b, st, fits

    def copy(c, e, slot):
        st = pl.multiple_of(window(c, e)[1], BF16_ROWS)
        return pltpu.make_async_copy(ye_hbm.at[e, pl.ds(st, COMB_WIN), :],
                                     buf.at[slot, pl.ds(e * COMB_WIN, COMB_WIN), :], sem.at[slot])

    @pl.when(i == 0)
    def _():
        for e in range(N_EXPERTS):
            copy(0, e, 0).start()

    @pl.when(i + 1 < n_tiles)
    def _():
        for e in range(N_EXPERTS):
            copy(i + 1, e, (i + 1) % 2).start()

    slot = i % 2
    rkt = rkt_ref[0]
    lane = lax.broadcasted_iota(I32, (1, LANES), 1)
    pos = []
    for e in range(N_EXPERTS):
        b, st, fits = window(i, e)
        rk = rkt[:, e:e + 1]
        pos.append(jnp.where((rk >= 0) & fits, rk + (b - st + e * COMB_WIN), -1))
    pieces = []
    for p in range(N_EXPERTS * COMB_WIN // LANES):
        lo, hi = p * LANES, (p + 1) * LANES
        target = None
        for e in range(N_EXPERTS):
            e_lo, e_hi = max(e * COMB_WIN, lo), min((e + 1) * COMB_WIN, hi)
            if e_lo >= e_hi:
                continue
            cand = pos[e] - lo
            if target is None:
                target = cand
            else:
                target = jnp.where((lane >= e_lo - lo) & (lane < e_hi - lo), cand, target)
        pieces.append(jnp.where(target == lane, 1.0, 0.0).astype(BF16))
    onehot = jnp.concatenate(pieces, axis=1)
    pltpu.make_async_copy(buf.at[slot], buf.at[slot], sem.at[slot]).wait()
    acc_ref[...] = ALPHA * x1_ref[...] + _dot(onehot, buf[slot])

    r_io = lax.broadcasted_iota(I32, (1, COMB_ROWS), 1)
    for e in range(N_EXPERTS):
        b, _, fits = window(i, e)

        @pl.when(jnp.logical_not(fits))
        def _():
            st = pl.multiple_of(jnp.minimum((b // BF16_ROWS) * BF16_ROWS, cap_p - COMB_ROWS), BF16_ROWS)
            cp = pltpu.make_async_copy(ye_hbm.at[e, pl.ds(st, COMB_ROWS), :], big, bsem.at[0])
            cp.start()
            cp.wait()
            rk = rkt[:, e:e + 1]
            srel = jnp.where(rk >= 0, rk + (b - st), -1)
            acc_ref[...] += _dot(jnp.where(srel == r_io, 1.0, 0.0).astype(BF16), big[...])

    y_ref[...] = _ln(acc_ref[...], g2_ref[...], b2_ref[...])


def _combine(base_s, x1, rank_t, ye, g2, b2, cap_p):
    assert base_s.shape[0] > x1.shape[0] // CHUNK
    t = x1.shape[0]
    n_tiles = t // CHUNK
    const = lambda i, b: (0, 0)
    return pl.pallas_call(
        functools.partial(_combine_kernel, n_tiles=n_tiles, cap_p=cap_p),
        grid_spec=pltpu.PrefetchScalarGridSpec(
            num_scalar_prefetch=1, grid=(n_tiles,),
            in_specs=[
                pl.BlockSpec((CHUNK, D_MODEL), lambda i, b: (i, 0)),
                pl.BlockSpec((1, CHUNK, N_EXPERTS), lambda i, b: (i, 0, 0)),
                pl.BlockSpec(memory_space=pl.ANY),
                pl.BlockSpec((1, D_MODEL), const), pl.BlockSpec((1, D_MODEL), const),
            ],
            out_specs=pl.BlockSpec((CHUNK, D_MODEL), lambda i, b: (i, 0)),
            scratch_shapes=[
                pltpu.VMEM((2, N_EXPERTS * COMB_WIN, D_MODEL), BF16),
                pltpu.VMEM((COMB_ROWS, D_MODEL), BF16),
                pltpu.VMEM((CHUNK, D_MODEL), F32),
                pltpu.SemaphoreType.DMA((2,)),
                pltpu.SemaphoreType.DMA((1,)),
            ]),
        out_shape=jax.ShapeDtypeStruct((t, D_MODEL), F32),
        compiler_params=_cparams(("arbitrary",)),
        name="moe_combine_ln2",
    )(base_s, x1, rank_t, ye, g2, b2)


def _row(v):
    return v.reshape(1, -1).astype(F32)


def _attn_bias_variants(bias_main, bias_meta0):
    half = CHUNK // 2
    dead = jnp.full((AT_HEADS, CHUNK, CHUNK), NEG, F32)
    dead_pad = jnp.full((AT_HEADS, half, half - N_META), NEG, F32)

    def halves(meta, prev, cur, nxt):
        top = jnp.concatenate([prev[:, :half], cur[:, :half], nxt[:, :half, :half], meta[:, :half], dead_pad], axis=2)
        bot = jnp.concatenate([prev[:, half:, half:], meta[:, half:], dead_pad, cur[:, half:], nxt[:, half:]], axis=2)
        return jnp.stack([top, bot])

    meta, prev = bias_main[:, :, :N_META], bias_main[:, :, CHUNK:2 * CHUNK]
    cur, nxt = bias_main[:, :, 2 * CHUNK:3 * CHUNK], bias_main[:, :, 3 * CHUNK:]
    tabs = jnp.stack([halves(meta, prev, cur, nxt),
                      halves(bias_meta0[:, :, :N_META], dead, cur, nxt),
                      halves(meta, prev, cur, dead)]) * LOG2E
    tabs = tabs.reshape(3, 2, AT_KV_HEADS, AT_GROUP * half, 3 * CHUNK).transpose(0, 2, 1, 3, 4)
    return tabs.reshape(3, AT_KV_HEADS, 2 * AT_GROUP * half, 3 * CHUNK)


def _pick_tile(n, pref):
    tm = pref
    while n % tm:
        tm //= 2
    return tm


def _trunk(x, p, meta):
    batch, seq, _ = x.shape
    nb = seq // CHUNK
    t = batch * seq
    bm = batch * N_META
    x2 = x.reshape(t, D_MODEL)
    pm_m, kt_m, gc_m, gt_m = meta
    pm, kt, gc, gt = _ln_proj(x2, p["g0"], p["b0"], p["w_main"], p["w_kt"], p["w_g"], p["w_gt"], p["bg"], p["bgt"],
                              _pick_tile(t, PM_TM))
    hb, hb_m = _mlstm(pm, kt, gc, gt, pm_m, kt_m, gc_m, gt_m, batch, nb, reverse=True)
    ha, ha_m = _mlstm(pm, kt, gc, gt, pm_m, kt_m, gc_m, gt_m, batch, nb, reverse=False,
                      merged=(hb, hb_m, p["ng"]))
    at = _attn_real(pm, pm_m, p["bias3"], p["sink_col"], batch, nb)
    at_m = _attn_meta(pm, pm_m, p["bias_mq"], p["sink_b"], batch, nb)

    mix = _mix(ha, at, pm, p["wa"], p["wb"], _pick_tile(t, 512), PM_GA // MIX_TN, PM_GB // MIX_TN)
    gates_m = jnp.tile(pm_m[:, PM_GA:PM_MQ], (batch, 1))
    mix_m = _mix(ha_m, at_m, gates_m, p["wa"], p["wb"], bm, 0, D_MODEL // MIX_TN)

    ln_args = (ALPHA * p["g0"], ALPHA * p["b0"], p["g1"], p["b1"], p["wo"], p["wrt"])
    x1, xp, aff = _out_ln_router(mix, x2, *ln_args, _pick_tile(t, OUT_TM), bm)
    xm = jnp.tile(p["meta_tokens"], (batch, 1))
    xp, aff_m = _out_ln_router(mix_m, xm, *ln_args, bm, bm, packed=xp)

    n_tok = t + bm
    nc = -(-n_tok // CHUNK)
    aff_m = jnp.pad(aff_m, ((0, 0), (0, nc * CHUNK - n_tok)), constant_values=-1.0)
    aff_m = aff_m.reshape(N_EXPERTS, -1, CHUNK).transpose(1, 0, 2)
    aff_all = jnp.concatenate([aff, aff_m], axis=0)
    cap = CAPACITY_FACTOR * n_tok // N_EXPERTS
    cap_p = -(-cap // CHUNK) * CHUNK
    rank, base = _route_select(aff_all, cap)
    base_s = base[:, :, 0]
    n_blk = cap_p // CHUNK + 2
    lists = _route_compact(base_s, rank, aff_all, n_blk)
    lists = lists.transpose(0, 2, 1, 3).reshape(N_EXPERTS, LIST_ROWS, n_blk * CHUNK)[:, :, :cap_p]
    idx = (lists[:, 0] * 256.0 + lists[:, 1]).astype(I32)
    gate = lists[:, 2] + lists[:, 3] + lists[:, 4]
    ns = -(-cap_p // FFN_MAX_ROWS)
    tm = cap_p // ns
    ye = _expert_ffn(idx.reshape(N_EXPERTS * ns, 1, tm), gate.reshape(N_EXPERTS, cap_p, 1), xp,
                     p["wgate"], p["wup"], p["wdown"], cap_p, tm, FFN_TF)
    rank_t = rank[:t // CHUNK].transpose(0, 2, 1)
    y = _combine(base_s, x1, rank_t, ye, p["g2"], p["b2"], cap_p)
    return y.reshape(batch, seq, D_MODEL)


def kernel(x_prompt, x_sample, meta_tokens, ln0_g, ln0_b, rel_bias, attn_sink, w_in, b_gate, ml_norm_g,
           w_branch_a, w_branch_b, w_out, ln1_g, ln1_b, w_router, w_gate, w_up, w_down, ln2_g, ln2_b):
    assert w_in.shape[0] == DEPTH
    w = w_in[0]
    sizes = (ML_HEADS * ML_DQK, ML_HEADS * ML_DQK, ML_HEADS * ML_DV, ML_HEADS * ML_DV, 4 * ML_HEADS,
             AT_HEADS * AT_DH, AT_KV_HEADS * AT_DH, AT_KV_HEADS * AT_DH, D_MODEL, D_MODEL)
    offs = np.concatenate([[0], np.cumsum(sizes)])
    mq, mk, mv, mo, mg, aq, ak, av, ga, gb = [w[:, offs[i]:offs[i + 1]] for i in range(10)]
    w_g = jnp.pad(mg, ((0, 0), (0, LANES - 4 * ML_HEADS))).astype(BF16)
    bg = jnp.pad(b_gate[0].astype(F32), (0, LANES - 4 * ML_HEADS))
    nb_max = max(x_prompt.shape[1], x_sample.shape[1]) // CHUNK
    bk_main, bk_meta0, bk_mq = _bucket_tables(nb_max)
    rb = rel_bias.astype(F32)
    p = {
        "g0": _row(ln0_g), "b0": _row(ln0_b), "g1": _row(ln1_g[0]), "b1": _row(ln1_b[0]),
        "g2": _row(ln2_g[0]), "b2": _row(ln2_b[0]), "ng": _row(ml_norm_g[0]),
        "w_main": jnp.concatenate([mv, mo, aq, ga, gb, mq, ak, av], axis=1).astype(BF16),
        "w_kt": mk.T.astype(BF16), "w_g": w_g, "w_gt": w_g.T,
        "bg": bg.reshape(1, LANES), "bgt": bg.reshape(LANES, 1),
        "wa": w_branch_a[0].astype(BF16), "wb": w_branch_b[0].astype(BF16), "wo": w_out[0].astype(BF16),
        "wrt": w_router[0].T.astype(BF16),
        "wgate": w_gate[0], "wup": w_up[0], "wdown": w_down[0],
        "bias3": _attn_bias_variants(_bias_table(rb, bk_main), _bias_table(rb, bk_meta0)),
        "sink_col": jnp.broadcast_to(
            jnp.tile(jnp.repeat(attn_sink[0].astype(F32).reshape(AT_KV_HEADS, AT_GROUP), CHUNK // 2, axis=1),
                     (1, 2))[:, :, None] * LOG2E, (AT_KV_HEADS, AT_GROUP * CHUNK, LANES)),
        "bias_mq": _bias_table(rb, bk_mq),
        "sink_b": jnp.broadcast_to(attn_sink[0].astype(F32)[:, None], (AT_HEADS, LANES)),
        "meta_tokens": meta_tokens.astype(F32),
    }
    meta = _ln_proj(p["meta_tokens"], p["g0"], p["b0"], p["w_main"], p["w_kt"], p["w_g"], p["w_gt"],
                    p["bg"], p["bgt"], N_META)
    return (_trunk(x_prompt, p, meta), _trunk(x_sample, p, meta))
```

```python
import functools
import math

import numpy as np
import jax
import jax.numpy as jnp
from jax import lax
from jax.experimental import pallas as pl
from jax.experimental.pallas import tpu as pltpu

F32 = jnp.float32
BF16 = jnp.bfloat16
I32 = jnp.int32

D_MODEL = 2048
N_META = 16
CHUNK = 128
ML_HEADS = 8
ML_DV = D_MODEL // ML_HEADS
ML_DQK = ML_DV // 2
AT_DH = 128
AT_HEADS = D_MODEL // AT_DH
AT_KV_HEADS = AT_HEADS // 4
AT_GROUP = AT_HEADS // AT_KV_HEADS
WINDOW = 128
REL_BUCKETS = 32
REL_MAX_DIST = 128
N_EXPERTS = 16
EXPERT_FF = D_MODEL
CAPACITY_FACTOR = 2
DEPTH = 1
ALPHA = (2.0 * DEPTH) ** 0.25
LN_EPS = 1e-5
M_INIT = -1e30
NEG = -1e30

LANES = 128
SUBLANES = 8
TOK_DIGIT_BITS = 8
BF16_ROWS = 16
VMEM_LIMIT = 56 * 1024 * 1024

PM_MV, PM_MO, PM_AQ, PM_GA, PM_GB, PM_MQ, PM_AK, PM_AV = 0, 2048, 4096, 6144, 8192, 10240, 11264, 11776
PM_WIDTH = 12288
PM_TN = 2048
PM_TM = 512
LN_ROWS = 256
MIX_TN = 1024
OUT_TM = 512
CT_W = ML_DV + LANES

NT_DIMS = (((1,), (1,)), ((), ()))


def _cparams(sem, vmem=VMEM_LIMIT):
    return pltpu.CompilerParams(dimension_semantics=sem, vmem_limit_bytes=vmem)


def _dot(a, b):
    return jnp.dot(a, b, preferred_element_type=F32)


def _dot_nt(a, b):
    return lax.dot_general(a, b, NT_DIMS, preferred_element_type=F32)


def _ln(x, g, b):
    mu = jnp.mean(x, axis=-1, keepdims=True)
    xc = x - mu
    var = jnp.mean(xc * xc, axis=-1, keepdims=True)
    return xc * lax.rsqrt(var + LN_EPS) * g + b


def _split3(x):
    hi = x.astype(BF16)
    r1 = x - hi.astype(F32)
    mid = r1.astype(BF16)
    lo = (r1 - mid.astype(F32)).astype(BF16)
    return hi, mid, lo


def _bias_kernel(rb_ref, bk_ref, o_ref):
    h = pl.program_id(0)
    bk = bk_ref[...]
    acc = jnp.full(bk.shape, NEG, F32)
    for b in range(REL_BUCKETS):
        acc = jnp.where(bk == b, rb_ref[b, h], acc)
    o_ref[0] = acc


def _bias_table(rel_bias, bucket):
    r, c = bucket.shape
    return pl.pallas_call(
        _bias_kernel,
        grid_spec=pltpu.PrefetchScalarGridSpec(
            num_scalar_prefetch=1, grid=(AT_HEADS,),
            in_specs=[pl.BlockSpec((r, c), lambda h, rb: (0, 0))],
            out_specs=pl.BlockSpec((1, r, c), lambda h, rb: (h, 0, 0))),
        out_shape=jax.ShapeDtypeStruct((AT_HEADS, r, c), F32),
        compiler_params=_cparams(("arbitrary",)),
        name="bias_table",
    )(rel_bias, bucket)


def _t5_bucket(rel):
    half = REL_BUCKETS // 2
    max_exact = half // 2
    n = jnp.abs(rel)
    nf = jnp.maximum(n, 1).astype(jnp.float32)
    large = max_exact + (jnp.log(nf / max_exact) / math.log(REL_MAX_DIST / max_exact)
                         * (half - max_exact)).astype(jnp.int32)
    large = jnp.minimum(large, half - 1)
    return jnp.where(rel > 0, half, 0) + jnp.where(n < max_exact, n, large)


def _t5_bucket_np(rel):
    half = REL_BUCKETS // 2
    max_exact = half // 2
    n = np.abs(rel)
    nf = np.maximum(n, 1).astype(np.float64)
    large = max_exact + (np.log(nf / max_exact) / math.log(REL_MAX_DIST / max_exact) * (half - max_exact)).astype(np.int64)
    large = np.minimum(large, half - 1)
    return np.where(rel > 0, half, 0) + np.where(n < max_exact, n, large)


def _bucket_tables(nb_max):
    i = np.arange(CHUNK)[:, None]
    m = np.arange(N_META)[None, :]
    ref_tab = _t5_bucket_np(m - (N_META + CHUNK + i))
    for j in range(1, nb_max):
        assert np.array_equal(_t5_bucket_np(m - (N_META + j * CHUNK + i)), ref_tab)
    qi = jnp.arange(CHUNK, dtype=I32)[:, None]
    c = jnp.arange(4 * CHUNK, dtype=I32)[None, :]
    rel_meta = c - (N_META + CHUNK + qi)
    rel_nb = (c - CHUNK) - CHUNK - qi
    vis_nb = (c >= CHUNK) & (jnp.abs(rel_nb) <= WINDOW)
    main = jnp.where(c < N_META, _t5_bucket(rel_meta), jnp.where(vis_nb, _t5_bucket(rel_nb), -1))
    c1 = jnp.arange(CHUNK, dtype=I32)[None, :]
    meta0 = jnp.where(c1 < N_META, _t5_bucket(c1 - (N_META + qi)), -1)
    mi = jnp.arange(N_META, dtype=I32)[:, None]
    c2 = jnp.arange(2 * CHUNK, dtype=I32)[None, :]
    rel_r = N_META + (c2 - CHUNK) - mi
    vis_r = (c2 >= CHUNK) & (jnp.abs(rel_r) <= WINDOW)
    mq = jnp.where(c2 < N_META, _t5_bucket(c2 - mi), jnp.where(vis_r, _t5_bucket(rel_r), -1))
    return main.astype(I32), meta0.astype(I32), mq.astype(I32)


def _proj_kernel(x_ref, g0_ref, b0_ref, w_ref, wktg_ref, wg_ref, bg_ref, bgt_ref,
                 pm_ref, kt_ref, gc_ref, gt_ref, xs_ref):
    n = pl.program_id(1)

    @pl.when(n == 0)
    def _():
        tm = x_ref.shape[0]
        rb = min(tm, LN_ROWS)

        def ln_block(r, carry):
            sl = pl.ds(pl.multiple_of(r * rb, BF16_ROWS), rb)
            xs_ref[sl, :] = _ln(x_ref[sl, :], g0_ref[...], b0_ref[...]).astype(BF16)
            return carry

        lax.fori_loop(0, tm // rb, ln_block, 0)
        xn = xs_ref[...]
        nk = ML_HEADS * ML_DQK
        ktg = _dot_nt(wktg_ref[...], xn)
        kt_ref[...] = ktg[:nk].astype(BF16)
        gt = ktg[nk:] + bgt_ref[...]
        gt_ref[...] = gt
        if tm % LANES == 0:
            gc_ref[...] = gt.T
        else:
            gc_ref[...] = _dot(xn, wg_ref[...]) + bg_ref[...]

    y = _dot(xs_ref[...], w_ref[...])
    lo1, hi1 = PM_MO // PM_TN, PM_AQ // PM_TN
    lo2, hi2 = PM_GA // PM_TN, PM_MQ // PM_TN
    is_sig = ((n >= lo1) & (n < hi1)) | ((n >= lo2) & (n < hi2))
    pm_ref[...] = jnp.where(is_sig, jax.nn.sigmoid(y), y).astype(BF16)


def _ln_proj(x2, g0, b0, w_main, w_ktg, w_g, bg, bgt, tm):
    t = x2.shape[0]
    grid = (t // tm, PM_WIDTH // PM_TN)
    const = lambda i, n: (0, 0)
    return pl.pallas_call(
        _proj_kernel,
        grid=grid,
        in_specs=[
            pl.BlockSpec((tm, D_MODEL), lambda i, n: (i, 0)),
            pl.BlockSpec((1, D_MODEL), const),
            pl.BlockSpec((1, D_MODEL), const),
            pl.BlockSpec((D_MODEL, PM_TN), lambda i, n: (0, n)),
            pl.BlockSpec((ML_HEADS * ML_DQK + LANES, D_MODEL), const, pipeline_mode=pl.Buffered(1)),
            pl.BlockSpec((D_MODEL, LANES), const),
            pl.BlockSpec((1, LANES), const),
            pl.BlockSpec((LANES, 1), const),
        ],
        out_specs=[
            pl.BlockSpec((tm, PM_TN), lambda i, n: (i, n)),
            pl.BlockSpec((ML_HEADS * ML_DQK, tm), lambda i, n: (0, i)),
            pl.BlockSpec((tm, LANES), lambda i, n: (i, 0)),
            pl.BlockSpec((LANES, tm), lambda i, n: (0, i)),
        ],
        out_shape=[
            jax.ShapeDtypeStruct((t, PM_WIDTH), BF16),
            jax.ShapeDtypeStruct((ML_HEADS * ML_DQK, t), BF16),
            jax.ShapeDtypeStruct((t, LANES), F32),
            jax.ShapeDtypeStruct((LANES, t), F32),
        ],
        scratch_shapes=[pltpu.VMEM((tm, D_MODEL), BF16)],
        compiler_params=_cparams(("arbitrary", "arbitrary")),
        name="ln_proj",
    )(x2, g0, b0, w_main, w_ktg, w_g, bg, bgt)


def _mlstm_chunk(L, reverse, q_ref, kt_ref, v_ref, gc_ref, gt_ref, ct_ref, m_ref, finish):
    H = ML_HEADS
    row = lax.broadcasted_iota(I32, (L, L), 0)
    col = lax.broadcasted_iota(I32, (L, L), 1)
    if reverse:
        tri = row <= col
        tri_t = row >= col
    else:
        tri = row >= col
        tri_t = row <= col
    tri_b = jnp.where(tri, 1.0, 0.0).astype(BF16)
    tri_tb = jnp.where(tri_t, 1.0, 0.0).astype(BF16)
    gc = gc_ref[...] * LOG2E
    gt = gt_ref[...] * LOG2E
    lfc = jax.nn.log_sigmoid(gc_ref[...]) * LOG2E
    lft = jax.nn.log_sigmoid(gt_ref[...]) * LOG2E
    bc_all = sum(_dot(tri_b, p) for p in _split3(lfc))
    br_all = sum(_dot(p, tri_tb) for p in _split3(lft))
    end = 0 if reverse else L - 1
    ci0, cf0 = (2 * H, 3 * H) if reverse else (0, H)
    scale = ML_DQK ** -0.5

    def stack(parts):
        return jnp.concatenate(parts, axis=0)

    def cols(x_all, c0):
        return stack([jnp.broadcast_to(x_all[:, c0 + h:c0 + h + 1], (L, L)) for h in range(H)])

    def rows_b(x_t, c0):
        return stack([jnp.broadcast_to(x_t[c0 + h:c0 + h + 1, :], (L, L)) for h in range(H)])

    def per_head(vals):
        return stack([jnp.broadcast_to(x, (L, L)) for x in vals])

    def rep(x):
        return jnp.broadcast_to(x, (H * L, L))

    def wide(x, n):
        if L == LANES and n % LANES == 0:
            return jnp.concatenate([x] * (n // LANES), axis=1)
        return jnp.broadcast_to(x[:, 0:1], (H * L, n))

    def wide_row(x, n):
        if L == LANES and n % LANES == 0:
            return jnp.concatenate([x] * (n // LANES), axis=1)
        return jnp.broadcast_to(x[:, 0:1], (1, n))

    def head(x, h):
        return x[h * L:(h + 1) * L]

    q = [q_ref[:, h * ML_DQK:(h + 1) * ML_DQK] for h in range(H)]
    kt = [kt_ref[h * ML_DQK:(h + 1) * ML_DQK, :] for h in range(H)]
    v = [v_ref[:, h * ML_DV:(h + 1) * ML_DV] for h in range(H)]
    ct = [ct_ref[h] for h in range(H)]
    m_prev_h = [m_ref[h:h + 1, 0:L] for h in range(H)]
    btot_h = [jnp.broadcast_to(bc_all[end:end + 1, cf0 + h:cf0 + h + 1], (1, L)) for h in range(H)]

    bc = cols(bc_all, cf0)
    igc = cols(gc, ci0)
    m_prev = per_head(m_prev_h)
    btot = per_head(btot_h)
    mask = stack([tri] * H)
    a_rows = gt[ci0:ci0 + H, :] - br_all[cf0:cf0 + H, :]
    d = jnp.where(mask, bc + rows_b(a_rows, 0), NEG)
    inter = bc + m_prev
    m_comb = jnp.maximum(inter, rep(jnp.max(d, axis=1, keepdims=True)))
    w = jnp.exp2(d - m_comb)
    w_inter = jnp.exp2(inter - m_comb) * scale
    s = stack([_dot(q[h], kt[h]) for h in range(H)]) * w * scale
    qc = stack([_dot(q[h], ct[h].astype(BF16)) for h in range(H)])
    sb = s.astype(BF16)
    num = stack([_dot(head(sb, h), v[h]) for h in range(H)]) + wide(w_inter, ML_DV) * qc[:, :ML_DV]
    den = rep(jnp.sum(s, axis=1, keepdims=True)) + w_inter * qc[:, ML_DV:ML_DV + L]
    finish(num * wide(1.0 / jnp.maximum(jnp.abs(den), jnp.exp2(-m_comb)), ML_DV))

    dec = btot - bc + igc
    inter_end_h = [btot_h[h] + m_prev_h[h] for h in range(H)]
    m_new_h = [jnp.maximum(inter_end_h[h], jnp.max(head(dec, h), axis=0, keepdims=True)) for h in range(H)]
    w_end = jnp.exp2(dec - per_head(m_new_h))
    vf = stack([x.astype(F32) for x in v])
    wv = jnp.concatenate([(wide(w_end, ML_DV) * vf).astype(BF16), wide(w_end, LANES).astype(BF16)], axis=1)
    for h in range(H):
        w_prev = jnp.exp2(inter_end_h[h] - m_new_h[h])
        ct_ref[h] = wide_row(w_prev, CT_W) * ct[h] + _dot(kt[h], head(wv, h))
        m_ref[h:h + 1, :] = wide_row(m_new_h[h], LANES)


def _mlstm_kernel(*refs, reverse, nb, merge):
    (q_ref, kt_ref, v_ref, gc_ref, gt_ref, qm_ref, ktm_ref, vm_ref, gcm_ref, gtm_ref) = refs[:10]
    if merge:
        o_ref, om_ref, hb_ref, hbm_ref, ng_ref, out_ref, outm_ref, ct_ref, m_ref = refs[10:]
    else:
        out_ref, outm_ref, ct_ref, m_ref = refs[10:]
    j = pl.program_id(1)

    @pl.when(j == 0)
    def _():
        ct_ref[...] = jnp.zeros(ct_ref.shape, F32)
        m_ref[...] = jnp.full(m_ref.shape, M_INIT * LOG2E, F32)

    def make_emit(dst_ref, other_ref, gate_ref):
        def emit(hv):
            L = hv.shape[0] // ML_HEADS
            sls = [slice(h * ML_DV, (h + 1) * ML_DV) for h in range(ML_HEADS)]
            if merge:
                hs = hv + jnp.concatenate([other_ref[:, sl] for sl in sls], axis=0)
                mu = jnp.broadcast_to(jnp.mean(hs, axis=-1, keepdims=True), hs.shape)
                hc = hs - mu
                var = jnp.broadcast_to(jnp.mean(hc * hc, axis=-1, keepdims=True), hs.shape)
                ng = jnp.concatenate([jnp.broadcast_to(ng_ref[:, sl], (L, ML_DV)) for sl in sls], axis=0)
                og = jnp.concatenate([gate_ref[:, sl].astype(F32) for sl in sls], axis=0)
                hv = hc * lax.rsqrt(var + LN_EPS) * (ng * og)
            for h, sl in enumerate(sls):
                dst_ref[:, sl] = hv[h * L:(h + 1) * L].astype(dst_ref.dtype)
        return emit

    is_meta = (j == nb) if reverse else (j == 0)

    @pl.when(is_meta)
    def _():
        emit = make_emit(outm_ref, hbm_ref if merge else None, om_ref if merge else None)
        _mlstm_chunk(N_META, reverse, qm_ref, ktm_ref, vm_ref, gcm_ref, gtm_ref, ct_ref, m_ref, emit)

    @pl.when(jnp.logical_not(is_meta))
    def _():
        emit = make_emit(out_ref, hb_ref if merge else None, o_ref if merge else None)
        _mlstm_chunk(CHUNK, reverse, q_ref, kt_ref, v_ref, gc_ref, gt_ref, ct_ref, m_ref, emit)


def _mlstm(pm, kt, gc, gt, pm_m, kt_m, gc_m, gt_m, batch, nb, *, reverse, merged=None):
    t = pm.shape[0]
    if reverse:
        rblk = lambda b, j: b * nb + jnp.maximum(nb - 1 - j, 0)
    else:
        rblk = lambda b, j: b * nb + jnp.maximum(j - 1, 0)
    const = lambda b, j: (0, 0)
    in_specs = [
        pl.BlockSpec((CHUNK, ML_HEADS * ML_DQK), lambda b, j: (rblk(b, j), PM_MQ // (ML_HEADS * ML_DQK))),
        pl.BlockSpec((ML_HEADS * ML_DQK, CHUNK), lambda b, j: (0, rblk(b, j))),
        pl.BlockSpec((CHUNK, D_MODEL), lambda b, j: (rblk(b, j), PM_MV // D_MODEL)),
        pl.BlockSpec((CHUNK, LANES), lambda b, j: (rblk(b, j), 0)),
        pl.BlockSpec((LANES, CHUNK), lambda b, j: (0, rblk(b, j))),
        pl.BlockSpec((N_META, ML_HEADS * ML_DQK), lambda b, j: (0, PM_MQ // (ML_HEADS * ML_DQK))),
        pl.BlockSpec((ML_HEADS * ML_DQK, N_META), const),
        pl.BlockSpec((N_META, D_MODEL), lambda b, j: (0, PM_MV // D_MODEL)),
        pl.BlockSpec((N_META, LANES), const),
        pl.BlockSpec((LANES, N_META), const),
    ]
    args = [pm, kt, pm, gc, gt, pm_m, kt_m, pm_m, gc_m, gt_m]
    merge = merged is not None
    if merge:
        hb, hb_m, ng = merged
        in_specs += [
            pl.BlockSpec((CHUNK, D_MODEL), lambda b, j: (rblk(b, j), PM_MO // D_MODEL)),
            pl.BlockSpec((N_META, D_MODEL), lambda b, j: (0, PM_MO // D_MODEL)),
            pl.BlockSpec((CHUNK, D_MODEL), lambda b, j: (rblk(b, j), 0)),
            pl.BlockSpec((N_META, D_MODEL), lambda b, j: (b, 0)),
            pl.BlockSpec((1, D_MODEL), const),
        ]
        args += [pm, pm_m, hb, hb_m, ng]
    odt = BF16 if merge else F32
    return pl.pallas_call(
        functools.partial(_mlstm_kernel, reverse=reverse, nb=nb, merge=merge),
        grid=(batch, nb + 1),
        in_specs=in_specs,
        out_specs=[
            pl.BlockSpec((CHUNK, D_MODEL), lambda b, j: (rblk(b, j), 0)),
            pl.BlockSpec((N_META, D_MODEL), lambda b, j: (b, 0)),
        ],
        out_shape=[
            jax.ShapeDtypeStruct((t, D_MODEL), odt),
            jax.ShapeDtypeStruct((batch * N_META, D_MODEL), odt),
        ],
        scratch_shapes=[pltpu.VMEM((ML_HEADS, ML_DQK, CT_W), F32), pltpu.VMEM((ML_HEADS, LANES), F32)],
        compiler_params=_cparams(("arbitrary", "arbitrary")),
        name="mlstm_bwd" if reverse else "mlstm_fwd",
    )(*args)


def _softmax_pv(sg, snk, vcat):
    m = jnp.maximum(jnp.max(sg, axis=1, keepdims=True), snk)
    p = jnp.exp(sg - m)
    l = jnp.sum(p, axis=1, keepdims=True) + jnp.exp(snk - m)
    return _dot(p.astype(BF16), vcat) * (1.0 / l)


LOG2E = math.log2(math.e)


def _attn_kernel(q_ref, kp_ref, kc_ref, kn_ref, vp_ref, vc_ref, vn_ref, km_ref, vm_ref,
                 bias_ref, sink_ref, o_ref):
    half = CHUNK // 2
    pad = jnp.zeros((half - N_META, AT_DH), BF16)
    ones = jnp.ones((3 * CHUNK, LANES), BF16)
    scale2 = AT_DH ** -0.5 * LOG2E
    rows_u = AT_GROUP * half
    for c in range(AT_KV_HEADS):
        ks = slice(c * AT_DH, (c + 1) * AT_DH)

        def keys(u, m_ref, p_ref, c_ref, n_ref):
            if u == 0:
                return jnp.concatenate([p_ref[:, ks], c_ref[:, ks], n_ref[:half, ks], m_ref[:, ks], pad], axis=0)
            return jnp.concatenate([p_ref[half:, ks], m_ref[:, ks], pad, c_ref[:, ks], n_ref[:, ks]], axis=0)

        def queries(u):
            return jnp.concatenate(
                [q_ref[u * half:(u + 1) * half, (c * AT_GROUP + g) * AT_DH:(c * AT_GROUP + g + 1) * AT_DH]
                 for g in range(AT_GROUP)], axis=0)

        t = jnp.concatenate([_dot_nt(queries(u), keys(u, km_ref, kp_ref, kc_ref, kn_ref)) for u in range(2)],
                            axis=0) * scale2 + bias_ref[0, c]
        snk = sink_ref[c]
        m = jnp.maximum(jnp.broadcast_to(jnp.max(t, axis=1, keepdims=True), snk.shape), snk)
        p = jnp.exp2(t - jnp.concatenate([m] * 3, axis=1)).astype(BF16)
        ov = jnp.concatenate(
            [_dot(p[u * rows_u:(u + 1) * rows_u],
                  jnp.concatenate([keys(u, vm_ref, vp_ref, vc_ref, vn_ref), ones], axis=1)) for u in range(2)], axis=0)
        l = ov[:, AT_DH:] + jnp.exp2(snk - m)
        o = ov[:, :AT_DH] * (1.0 / l)
        for u in range(2):
            for g in range(AT_GROUP):
                h = c * AT_GROUP + g
                r0 = u * rows_u + g * half
                o_ref[u * half:(u + 1) * half, h * AT_DH:(h + 1) * AT_DH] = o[r0:r0 + half].astype(o_ref.dtype)


def _attn_real(pm, pm_m, bias3, sink_col, batch, nb):
    assert nb >= 2
    t = pm.shape[0]
    kvw = AT_KV_HEADS * AT_DH
    cur = lambda b, j: b * nb + j
    prv = lambda b, j: b * nb + jnp.maximum(j - 1, 0)
    nxt = lambda b, j: b * nb + jnp.minimum(j + 1, nb - 1)
    variant = lambda b, j: (jnp.where(j == 0, 1, jnp.where(j == nb - 1, 2, 0)), 0, 0, 0)
    kcol, vcol = PM_AK // kvw, PM_AV // kvw
    return pl.pallas_call(
        _attn_kernel,
        grid=(batch, nb),
        in_specs=[
            pl.BlockSpec((CHUNK, D_MODEL), lambda b, j: (cur(b, j), PM_AQ // D_MODEL)),
            pl.BlockSpec((CHUNK, kvw), lambda b, j: (prv(b, j), kcol)),
            pl.BlockSpec((CHUNK, kvw), lambda b, j: (cur(b, j), kcol)),
            pl.BlockSpec((CHUNK, kvw), lambda b, j: (nxt(b, j), kcol)),
            pl.BlockSpec((CHUNK, kvw), lambda b, j: (prv(b, j), vcol)),
            pl.BlockSpec((CHUNK, kvw), lambda b, j: (cur(b, j), vcol)),
            pl.BlockSpec((CHUNK, kvw), lambda b, j: (nxt(b, j), vcol)),
            pl.BlockSpec((N_META, kvw), lambda b, j: (0, kcol)),
            pl.BlockSpec((N_META, kvw), lambda b, j: (0, vcol)),
            pl.BlockSpec((1, AT_KV_HEADS, AT_GROUP * CHUNK, 3 * CHUNK), variant),
            pl.BlockSpec((AT_KV_HEADS, AT_GROUP * CHUNK, LANES), lambda b, j: (0, 0, 0)),
        ],
        out_specs=pl.BlockSpec((CHUNK, D_MODEL), lambda b, j: (cur(b, j), 0)),
        out_shape=jax.ShapeDtypeStruct((t, D_MODEL), BF16),
        compiler_params=_cparams(("arbitrary", "arbitrary")),
        name="attn_real",
    )(pm, pm, pm, pm, pm, pm, pm, pm_m, pm_m, bias3, sink_col)


def _attn_meta_kernel(q_ref, kr_ref, vr_ref, km_ref, vm_ref, bias_ref, sink_ref, o_ref):
    pad = jnp.zeros((CHUNK - N_META, AT_DH), BF16)
    scale = AT_DH ** -0.5
    for c in range(AT_KV_HEADS):
        ks = slice(c * AT_DH, (c + 1) * AT_DH)
        kcat = jnp.concatenate([km_ref[:, ks], pad, kr_ref[:, ks]], axis=0)
        vcat = jnp.concatenate([vm_ref[:, ks], pad, vr_ref[:, ks]], axis=0)
        qc = jnp.concatenate(
            [q_ref[:, (c * AT_GROUP + g) * AT_DH:(c * AT_GROUP + g + 1) * AT_DH] for g in range(AT_GROUP)], axis=0)
        s = _dot_nt(qc, kcat) * scale
        for g in range(AT_GROUP):
            h = c * AT_GROUP + g
            sg = s[g * N_META:(g + 1) * N_META] + bias_ref[h]
            o = _softmax_pv(sg, sink_ref[h:h + 1, 0:1], vcat)
            o_ref[:, h * AT_DH:(h + 1) * AT_DH] = o.astype(o_ref.dtype)


def _attn_meta(pm, pm_m, bias_mq, sink_b, batch, nb):
    kvw = AT_KV_HEADS * AT_DH
    kcol, vcol = PM_AK // kvw, PM_AV // kvw
    return pl.pallas_call(
        _attn_meta_kernel,
        grid=(batch,),
        in_specs=[
            pl.BlockSpec((N_META, D_MODEL), lambda b: (0, PM_AQ // D_MODEL)),
            pl.BlockSpec((CHUNK, kvw), lambda b: (b * nb, kcol)),
            pl.BlockSpec((CHUNK, kvw), lambda b: (b * nb, vcol)),
            pl.BlockSpec((N_META, kvw), lambda b: (0, kcol)),
            pl.BlockSpec((N_META, kvw), lambda b: (0, vcol)),
            pl.BlockSpec((AT_HEADS, N_META, 2 * CHUNK), lambda b: (0, 0, 0)),
            pl.BlockSpec((AT_HEADS, LANES), lambda b: (0, 0)),
        ],
        out_specs=pl.BlockSpec((N_META, D_MODEL), lambda b: (b, 0)),
        out_shape=jax.ShapeDtypeStruct((batch * N_META, D_MODEL), BF16),
        compiler_params=_cparams(("arbitrary",)),
        name="attn_meta",
    )(pm_m, pm, pm, pm_m, pm_m, bias_mq, sink_b)


def _mix_kernel(ha_ref, hb_ref, ga_ref, gb_ref, wa_ref, wb_ref, o_ref):
    a = _dot(ha_ref[...], wa_ref[...])
    b = _dot(hb_ref[...], wb_ref[...])
    o_ref[...] = (ga_ref[...].astype(F32) * a + gb_ref[...].astype(F32) * b).astype(o_ref.dtype)


def _mix(ha, hb, gates_src, wa, wb, tm, ga_col, gb_col):
    t = ha.shape[0]
    nt = D_MODEL // MIX_TN
    return pl.pallas_call(
        _mix_kernel,
        grid=(t // tm, nt),
        in_specs=[
            pl.BlockSpec((tm, D_MODEL), lambda i, n: (i, 0)),
            pl.BlockSpec((tm, D_MODEL), lambda i, n: (i, 0)),
            pl.BlockSpec((tm, MIX_TN), lambda i, n: (i, ga_col + n)),
            pl.BlockSpec((tm, MIX_TN), lambda i, n: (i, gb_col + n)),
            pl.BlockSpec((D_MODEL, MIX_TN), lambda i, n: (0, n)),
            pl.BlockSpec((D_MODEL, MIX_TN), lambda i, n: (0, n)),
        ],
        out_specs=pl.BlockSpec((tm, MIX_TN), lambda i, n: (i, n)),
        out_shape=jax.ShapeDtypeStruct((t, D_MODEL), BF16),
        compiler_params=_cparams(("arbitrary", "arbitrary")),
        name="branch_mix",
    )(ha, hb, gates_src, gates_src, wa, wb)


HALF_D = D_MODEL // 2
BF16_BITS = 16
HIGH_HALF = -(1 << BF16_BITS)


def _pack_bf16_pairs(x):
    lo = lax.bitcast_convert_type(x[:, :HALF_D].astype(BF16).astype(F32), I32)
    hi = lax.bitcast_convert_type(x[:, HALF_D:].astype(BF16).astype(F32), I32)
    return lax.shift_right_logical(lo, BF16_BITS) | (hi & jnp.int32(HIGH_HALF))


def _unpack_bf16_pairs(u):
    lo = lax.bitcast_convert_type(u << BF16_BITS, F32)
    hi = lax.bitcast_convert_type(u & jnp.int32(HIGH_HALF), F32)
    return lo.astype(BF16), hi.astype(BF16)


PACK_SUB = HALF_D // LANES


def _out_kernel(*refs, real):
    mix_ref, x_ref, g0_ref, b0_ref, g1_ref, b1_ref, wo_ref, wrt_ref = refs[:8]
    if real:
        x1_ref, x1p_ref, aff_ref = refs[8:]
    else:
        x1p_ref, aff_ref = refs[9:]

    def body():
        y = _dot(mix_ref[...], wo_ref[...])
        h0a = _ln(x_ref[...], g0_ref[...], b0_ref[...])
        x1 = _ln(h0a + y, g1_ref[...], b1_ref[...])
        x1p_ref[...] = pltpu.einshape("t(cl)->tcl", _pack_bf16_pairs(x1), c=PACK_SUB)
        lt = _dot_nt(wrt_ref[...], x1.astype(BF16))
        e = jnp.exp(lt - jnp.max(lt, axis=0, keepdims=True))
        aff = e / jnp.sum(e, axis=0, keepdims=True)
        if real:
            x1_ref[...] = x1
            for c in range(aff.shape[1] // CHUNK):
                aff_ref[c] = aff[:, c * CHUNK:(c + 1) * CHUNK]
        else:
            aff_ref[...] = aff

    if not real:
        body()
        return
    last = pl.program_id(0) == pl.num_programs(0) - 1
    pl.when(jnp.logical_not(last))(body)

    @pl.when(last)
    def _():
        x1p_ref[...] = jnp.zeros(x1p_ref.shape, I32)


def _out_ln_router(mix, x2, g0, b0, g1, b1, wo, wrt, tm, n_meta_rows, packed=None):
    t = mix.shape[0]
    real = packed is None
    n = t // tm
    const = lambda i: (0, 0)
    row = lambda i: (jnp.minimum(i, n - 1), 0)
    in_specs = [
        pl.BlockSpec((tm, D_MODEL), row),
        pl.BlockSpec((tm, D_MODEL), row),
        pl.BlockSpec((1, D_MODEL), const), pl.BlockSpec((1, D_MODEL), const),
        pl.BlockSpec((1, D_MODEL), const), pl.BlockSpec((1, D_MODEL), const),
        pl.BlockSpec((D_MODEL, D_MODEL), const, pipeline_mode=pl.Buffered(1)),
        pl.BlockSpec((N_EXPERTS, D_MODEL), const),
    ]
    args = [mix, x2, g0, b0, g1, b1, wo, wrt]
    if real:
        assert n_meta_rows <= tm
        out_specs = [
            pl.BlockSpec((tm, D_MODEL), row),
            pl.BlockSpec((tm, PACK_SUB, LANES), lambda i: (i, 0, 0)),
            pl.BlockSpec((tm // CHUNK, N_EXPERTS, CHUNK), lambda i: (jnp.minimum(i, n - 1), 0, 0)),
        ]
        out_shape = [
            jax.ShapeDtypeStruct((t, D_MODEL), F32),
            jax.ShapeDtypeStruct((t + n_meta_rows, PACK_SUB, LANES), I32),
            jax.ShapeDtypeStruct((t // CHUNK, N_EXPERTS, CHUNK), F32),
        ]
        aliases = {}
    else:
        n_real = packed.shape[0] - t
        assert tm == t and n_real % t == 0
        in_specs.append(pl.BlockSpec(memory_space=pl.ANY))
        args.append(packed)
        out_specs = [
            pl.BlockSpec((t, PACK_SUB, LANES), lambda i: (n_real // t, 0, 0)),
            pl.BlockSpec((N_EXPERTS, t), const),
        ]
        out_shape = [jax.ShapeDtypeStruct(packed.shape, I32), jax.ShapeDtypeStruct((N_EXPERTS, t), F32)]
        aliases = {8: 0}
    return pl.pallas_call(
        functools.partial(_out_kernel, real=real),
        grid=(n + 1 if real else n,),
        in_specs=in_specs,
        out_specs=out_specs,
        out_shape=out_shape,
        input_output_aliases=aliases,
        compiler_params=_cparams(("arbitrary",)),
        name="out_ln1_router" if real else "out_ln1_router_meta",
    )(*args)


def _route_select_kernel(aff_ref, rank_ref, base_ref, p_s, b_s, *, cap, nc):
    aff = aff_ref[...]
    bits = lax.bitcast_convert_type(aff, I32)

    def count(maskf):
        return jnp.sum(jnp.sum(maskf, axis=0), axis=1, keepdims=True)

    def search(it, lo):
        cand = lo | (jnp.int32(1) << (30 - it))
        cnt = count(jnp.where(bits >= cand[None], 1.0, 0.0))
        return jnp.where(cnt >= cap, cand, lo)

    thr = lax.fori_loop(0, 31, search, jnp.zeros((N_EXPERTS, 1), I32))
    gt = bits > thr[None]
    eq = bits == thr[None]
    need = cap - count(jnp.where(gt, 1.0, 0.0))
    r_i = lax.broadcasted_iota(I32, (CHUNK, CHUNK), 0)
    c_i = lax.broadcasted_iota(I32, (CHUNK, CHUNK), 1)
    upper = jnp.where(r_i <= c_i, 1.0, 0.0).astype(BF16)

    def prefix(maskb):
        p = _dot(jnp.where(maskb, 1.0, 0.0).astype(BF16).reshape(nc * N_EXPERTS, CHUNK), upper)
        p_s[...] = p.reshape(nc, N_EXPERTS, CHUNK)

        def step(c, carry):
            b_s[c] = jnp.broadcast_to(carry, (N_EXPERTS, CHUNK))
            return carry + p_s[c][:, CHUNK - 1:CHUNK]

        lax.fori_loop(0, nc, step, jnp.zeros((N_EXPERTS, 1), F32))

    prefix(eq)
    eq_rank = b_s[...] + p_s[...] - 1.0
    sel = gt | (eq & (eq_rank < need[None]))
    prefix(sel)
    rank_ref[...] = jnp.where(sel, p_s[...] - 1.0, -1.0).astype(I32)
    base_ref[...] = b_s[...].astype(I32)


def _route_select(aff, cap):
    nc = aff.shape[0]
    shp = (nc, N_EXPERTS, CHUNK)
    full = pl.BlockSpec(shp, lambda i: (0, 0, 0))
    return pl.pallas_call(
        functools.partial(_route_select_kernel, cap=cap, nc=nc),
        grid=(1,),
        in_specs=[full],
        out_specs=[full, full],
        out_shape=[jax.ShapeDtypeStruct(shp, I32), jax.ShapeDtypeStruct(shp, I32)],
        scratch_shapes=[pltpu.VMEM(shp, F32), pltpu.VMEM(shp, F32)],
        compiler_params=_cparams(("arbitrary",)),
        name="route_select",
    )(aff)


FFN_MAX_ROWS = 2304
FFN_TF = 256
FFN_ROW_SPLIT = 2
RELAYOUT_BLOCKS = 8
LIST_ROWS = 16


def _route_compact_kernel(base_ref, rank_ref, aff_ref, out_ref, *, group):
    step = pl.program_id(0)

    @pl.when(step == 0)
    def _():
        out_ref[...] = jnp.zeros(out_ref.shape, F32)

    lane = lax.broadcasted_iota(I32, (1, CHUNK), 1)
    r_io = lax.broadcasted_iota(I32, (CHUNK, CHUNK), 0)
    zeros = jnp.zeros((LIST_ROWS - 5, CHUNK), F32)
    for k in range(group):
        c = step * group + k
        tok = c * CHUNK + lane
        t_hi = (tok >> TOK_DIGIT_BITS).astype(F32)
        t_lo = (tok & ((1 << TOK_DIGIT_BITS) - 1)).astype(F32)
        for e in range(N_EXPERTS):
            rk = rank_ref[k, e:e + 1, :]
            onehot = jnp.where(r_io == rk, 1.0, 0.0).astype(BF16)
            a_hi, a_mid, a_lo = _split3(aff_ref[k, e:e + 1, :])
            vals = jnp.concatenate([t_hi, t_lo, a_hi.astype(F32), a_mid.astype(F32), a_lo.astype(F32), zeros],
                                   axis=0).astype(BF16)
            comp = _dot_nt(vals, onehot)
            base = base_ref[c, e]
            blk = base // CHUNK
            off = base - blk * CHUNK
            rolled = pltpu.roll(comp, off, axis=1)
            keep_lo = lane >= off
            out_ref[e, blk] = jnp.where(keep_lo, rolled, out_ref[e, blk])
            out_ref[e, blk + 1] = jnp.where(keep_lo, out_ref[e, blk + 1], rolled)


def _route_compact(base_s, rank, aff, n_blk):
    nc = rank.shape[0]
    group = next(g for g in (4, 3, 2, 1) if nc % g == 0)
    blk = pl.BlockSpec((group, N_EXPERTS, CHUNK), lambda c, b: (c, 0, 0))
    return pl.pallas_call(
        functools.partial(_route_compact_kernel, group=group),
        grid_spec=pltpu.PrefetchScalarGridSpec(
            num_scalar_prefetch=1, grid=(nc // group,),
            in_specs=[blk, blk],
            out_specs=pl.BlockSpec((N_EXPERTS, n_blk, LIST_ROWS, CHUNK), lambda c, b: (0, 0, 0, 0))),
        out_shape=jax.ShapeDtypeStruct((N_EXPERTS, n_blk, LIST_ROWS, CHUNK), F32),
        compiler_params=_cparams(("arbitrary",)),
        name="route_compact",
    )(base_s, rank, aff)


def _ffn_kernel(idx_ref, idxn_ref, gate_ref, xp_hbm, wg_ref, wu_ref, wd_ref, ye_ref,
                rows_ref, xp_ref, hid_ref, sem, *, tm, rs, nf, tf):
    e, s, j = pl.program_id(0), pl.program_id(1), pl.program_id(2)
    tile = e * pl.num_programs(1) + s
    n_sub = tm // rs
    per_slot = tm // (nf * n_sub)

    def relayout():
        rb = tm // RELAYOUT_BLOCKS

        def body(r, carry):
            sl = pl.ds(pl.multiple_of(r * rb, SUBLANES), rb)
            y = pltpu.einshape("tcl->ctl", rows_ref[sl])
            for c in range(PACK_SUB):
                xp_ref[sl, c * LANES:(c + 1) * LANES] = y[c]
            return carry

        lax.fori_loop(0, RELAYOUT_BLOCKS, body, 0)

    def row_copy(iref, r):
        return pltpu.make_async_copy(xp_hbm.at[pl.ds(iref[0, 0, r], 1)], rows_ref.at[pl.ds(r, 1)], sem.at[0])

    def wait_rows():
        pltpu.make_async_copy(xp_hbm.at[pl.ds(0, tm)], rows_ref, sem.at[0]).wait()

    def prefetch_next(r):
        first = (j * n_sub + r) * per_slot
        for k in range(per_slot):
            row_copy(idxn_ref, first + k).start()

    def rows(r):
        return pl.ds(pl.multiple_of(r * rs, BF16_ROWS), rs)

    @pl.when((tile == 0) & (j == 0))
    def _():
        def body(r, carry):
            row_copy(idx_ref, r).start()
            return carry

        lax.fori_loop(0, tm, body, 0)
        wait_rows()
        relayout()

    @pl.when(j < nf)
    def _():
        wg = wg_ref[0].astype(BF16)
        wu = wu_ref[0].astype(BF16)

        def body(r, carry):
            prefetch_next(r)
            lo, hi = _unpack_bf16_pairs(xp_ref[rows(r), :])
            x = jnp.concatenate([lo, hi], axis=1)
            g = _dot(x, wg)
            u = _dot(x, wu)
            hid_ref[j, rows(r), :] = (g * jax.nn.sigmoid(g) * u).astype(BF16)
            return carry

        lax.fori_loop(0, n_sub, body, 0)

    @pl.when(j == nf)
    def _():
        wait_rows()
        relayout()

    @pl.when(j >= nf)
    def _():
        wd = wd_ref[0].astype(BF16)

        def body(r, carry):
            acc = _dot(hid_ref[0, rows(r), :], wd[0:tf])
            for f in range(1, nf):
                acc = acc + _dot(hid_ref[f, rows(r), :], wd[f * tf:(f + 1) * tf])
            ye_ref[0, rows(r), :] = (acc * gate_ref[0, rows(r), :]).astype(ye_ref.dtype)
            return carry

        lax.fori_loop(0, n_sub, body, 0)


def _expert_ffn(idx3, gate3, xp, wg, wu, wd, cap_p, tm, tf):
    ns = cap_p // tm
    nf = EXPERT_FF // tf
    n_tiles = N_EXPERTS * ns
    rs = tm // FFN_ROW_SPLIT
    assert rs % BF16_ROWS == 0 and D_MODEL // tf == nf and tm % (nf * (tm // rs)) == 0
    assert tm % (SUBLANES * RELAYOUT_BLOCKS) == 0
    return pl.pallas_call(
        functools.partial(_ffn_kernel, tm=tm, rs=rs, nf=nf, tf=tf),
        grid=(N_EXPERTS, ns, 2 * nf),
        in_specs=[
            pl.BlockSpec((1, 1, tm), lambda e, s, j: (e * ns + s, 0, 0), memory_space=pltpu.SMEM),
            pl.BlockSpec((1, 1, tm), lambda e, s, j: (jnp.minimum(e * ns + s + 1, n_tiles - 1), 0, 0),
                         memory_space=pltpu.SMEM),
            pl.BlockSpec((1, tm, 1), lambda e, s, j: (e, s, 0)),
            pl.BlockSpec(memory_space=pl.ANY),
            pl.BlockSpec((1, D_MODEL, tf), lambda e, s, j: (e, 0, jnp.minimum(j, nf - 1))),
            pl.BlockSpec((1, D_MODEL, tf), lambda e, s, j: (e, 0, jnp.minimum(j, nf - 1))),
            pl.BlockSpec((1, EXPERT_FF, tf), lambda e, s, j: (e, 0, jnp.maximum(j - nf, 0))),
        ],
        out_specs=pl.BlockSpec((1, tm, tf), lambda e, s, j: (e, s, jnp.maximum(j - nf, 0))),
        out_shape=jax.ShapeDtypeStruct((N_EXPERTS, cap_p, D_MODEL), BF16),
        scratch_shapes=[
            pltpu.VMEM((tm, PACK_SUB, LANES), I32),
            pltpu.VMEM((tm, HALF_D), I32),
            pltpu.VMEM((nf, tm, tf), BF16),
            pltpu.SemaphoreType.DMA((1,)),
        ],
        compiler_params=_cparams(("arbitrary", "arbitrary", "arbitrary")),
        name="expert_ffn",
    )(idx3, idx3, gate3, xp, wg, wu, wd)


COMB_ROWS = CHUNK + BF16_ROWS


COMB_WIN = 48


def _combine_kernel(base_ref, x1_ref, rkt_ref, ye_hbm, g2_ref, b2_ref, y_ref, buf, big, acc_ref, sem, bsem,
                    *, n_tiles, cap_p):
    i = pl.program_id(0)

    def window(c, e):
        b = base_ref[c, e]
        st = jnp.minimum((b // BF16_ROWS) * BF16_ROWS, cap_p - COMB_WIN)
        fits = base_ref[c + 1, e] - st <= COMB_WIN
        return b, st, fits

    def copy(c, e, slot):
        st = pl.multiple_of(window(c, e)[1], BF16_ROWS)
        return pltpu.make_async_copy(ye_hbm.at[e, pl.ds(st, COMB_WIN), :],
                                     buf.at[slot, pl.ds(e * COMB_WIN, COMB_WIN), :], sem.at[slot])

    @pl.when(i == 0)
    def _():
        for e in range(N_EXPERTS):
            copy(0, e, 0).start()

    @pl.when(i + 1 < n_tiles)
    def _():
        for e in range(N_EXPERTS):
            copy(i + 1, e, (i + 1) % 2).start()

    slot = i % 2
    rkt = rkt_ref[0]
    lane = lax.broadcasted_iota(I32, (1, LANES), 1)
    pos = []
    for e in range(N_EXPERTS):
        b, st, fits = window(i, e)
        rk = rkt[:, e:e + 1]
        pos.append(jnp.where((rk >= 0) & fits, rk + (b - st + e * COMB_WIN), -1))
    pieces = []
    for p in range(N_EXPERTS * COMB_WIN // LANES):
        lo, hi = p * LANES, (p + 1) * LANES
        target = None
        for e in range(N_EXPERTS):
            e_lo, e_hi = max(e * COMB_WIN, lo), min((e + 1) * COMB_WIN, hi)
            if e_lo >= e_hi:
                continue
            cand = pos[e] - lo
            if target is None:
                target = cand
            else:
                target = jnp.where((lane >= e_lo - lo) & (lane < e_hi - lo), cand, target)
        pieces.append(jnp.where(target == lane, 1.0, 0.0).astype(BF16))
    onehot = jnp.concatenate(pieces, axis=1)
    pltpu.make_async_copy(buf.at[slot], buf.at[slot], sem.at[slot]).wait()
    acc_ref[...] = ALPHA * x1_ref[...] + _dot(onehot, buf[slot])

    r_io = lax.broadcasted_iota(I32, (1, COMB_ROWS), 1)
    for e in range(N_EXPERTS):
        b, _, fits = window(i, e)

        @pl.when(jnp.logical_not(fits))
        def _():
            st = pl.multiple_of(jnp.minimum((b // BF16_ROWS) * BF16_ROWS, cap_p - COMB_ROWS), BF16_ROWS)
            cp = pltpu.make_async_copy(ye_hbm.at[e, pl.ds(st, COMB_ROWS), :], big, bsem.at[0])
            cp.start()
            cp.wait()
            rk = rkt[:, e:e + 1]
            srel = jnp.where(rk >= 0, rk + (b - st), -1)
            acc_ref[...] += _dot(jnp.where(srel == r_io, 1.0, 0.0).astype(BF16), big[...])

    y_ref[...] = _ln(acc_ref[...], g2_ref[...], b2_ref[...])


def _combine(base_s, x1, rank_t, ye, g2, b2, cap_p):
    assert base_s.shape[0] > x1.shape[0] // CHUNK
    t = x1.shape[0]
    n_tiles = t // CHUNK
    const = lambda i, b: (0, 0)
    return pl.pallas_call(
        functools.partial(_combine_kernel, n_tiles=n_tiles, cap_p=cap_p),
        grid_spec=pltpu.PrefetchScalarGridSpec(
            num_scalar_prefetch=1, grid=(n_tiles,),
            in_specs=[
                pl.BlockSpec((CHUNK, D_MODEL), lambda i, b: (i, 0)),
                pl.BlockSpec((1, CHUNK, N_EXPERTS), lambda i, b: (i, 0, 0)),
                pl.BlockSpec(memory_space=pl.ANY),
                pl.BlockSpec((1, D_MODEL), const), pl.BlockSpec((1, D_MODEL), const),
            ],
            out_specs=pl.BlockSpec((CHUNK, D_MODEL), lambda i, b: (i, 0)),
            scratch_shapes=[
                pltpu.VMEM((2, N_EXPERTS * COMB_WIN, D_MODEL), BF16),
                pltpu.VMEM((COMB_ROWS, D_MODEL), BF16),
                pltpu.VMEM((CHUNK, D_MODEL), F32),
                pltpu.SemaphoreType.DMA((2,)),
                pltpu.SemaphoreType.DMA((1,)),
            ]),
        out_shape=jax.ShapeDtypeStruct((t, D_MODEL), F32),
        compiler_params=_cparams(("arbitrary",)),
        name="moe_combine_ln2",
    )(base_s, x1, rank_t, ye, g2, b2)


def _row(v):
    return v.reshape(1, -1).astype(F32)


def _attn_bias_variants(bias_main, bias_meta0):
    half = CHUNK // 2
    dead = jnp.full((AT_HEADS, CHUNK, CHUNK), NEG, F32)
    dead_pad = jnp.full((AT_HEADS, half, half - N_META), NEG, F32)

    def halves(meta, prev, cur, nxt):
        top = jnp.concatenate([prev[:, :half], cur[:, :half], nxt[:, :half, :half], meta[:, :half], dead_pad], axis=2)
        bot = jnp.concatenate([prev[:, half:, half:], meta[:, half:], dead_pad, cur[:, half:], nxt[:, half:]], axis=2)
        return jnp.stack([top, bot])

    meta, prev = bias_main[:, :, :N_META], bias_main[:, :, CHUNK:2 * CHUNK]
    cur, nxt = bias_main[:, :, 2 * CHUNK:3 * CHUNK], bias_main[:, :, 3 * CHUNK:]
    tabs = jnp.stack([halves(meta, prev, cur, nxt),
                      halves(bias_meta0[:, :, :N_META], dead, cur, nxt),
                      halves(meta, prev, cur, dead)]) * LOG2E
    tabs = tabs.reshape(3, 2, AT_KV_HEADS, AT_GROUP * half, 3 * CHUNK).transpose(0, 2, 1, 3, 4)
    return tabs.reshape(3, AT_KV_HEADS, 2 * AT_GROUP * half, 3 * CHUNK)


def _pick_tile(n, pref):
    tm = pref
    while n % tm:
        tm //= 2
    return tm


def _trunk(x, p, meta):
    batch, seq, _ = x.shape
    nb = seq // CHUNK
    t = batch * seq
    bm = batch * N_META
    x2 = x.reshape(t, D_MODEL)
    pm_m, kt_m, gc_m, gt_m = meta
    pm, kt, gc, gt = _ln_proj(x2, p["g0"], p["b0"], p["w_main"], p["w_ktg"], p["w_g"], p["bg"], p["bgt"],
                              _pick_tile(t, PM_TM))
    hb, hb_m = _mlstm(pm, kt, gc, gt, pm_m, kt_m, gc_m, gt_m, batch, nb, reverse=True)
    ha, ha_m = _mlstm(pm, kt, gc, gt, pm_m, kt_m, gc_m, gt_m, batch, nb, reverse=False,
                      merged=(hb, hb_m, p["ng"]))
    at = _attn_real(pm, pm_m, p["bias3"], p["sink_col"], batch, nb)
    at_m = _attn_meta(pm, pm_m, p["bias_mq"], p["sink_b"], batch, nb)

    mix = _mix(ha, at, pm, p["wa"], p["wb"], _pick_tile(t, 512), PM_GA // MIX_TN, PM_GB // MIX_TN)
    gates_m = jnp.tile(pm_m[:, PM_GA:PM_MQ], (batch, 1))
    mix_m = _mix(ha_m, at_m, gates_m, p["wa"], p["wb"], bm, 0, D_MODEL // MIX_TN)

    ln_args = (ALPHA * p["g0"], ALPHA * p["b0"], p["g1"], p["b1"], p["wo"], p["wrt"])
    x1, xp, aff = _out_ln_router(mix, x2, *ln_args, _pick_tile(t, OUT_TM), bm)
    xm = jnp.tile(p["meta_tokens"], (batch, 1))
    xp, aff_m = _out_ln_router(mix_m, xm, *ln_args, bm, bm, packed=xp)

    n_tok = t + bm
    nc = -(-n_tok // CHUNK)
    aff_m = jnp.pad(aff_m, ((0, 0), (0, nc * CHUNK - n_tok)), constant_values=-1.0)
    aff_m = aff_m.reshape(N_EXPERTS, -1, CHUNK).transpose(1, 0, 2)
    aff_all = jnp.concatenate([aff, aff_m], axis=0)
    cap = CAPACITY_FACTOR * n_tok // N_EXPERTS
    cap_p = -(-cap // CHUNK) * CHUNK
    rank, base = _route_select(aff_all, cap)
    base_s = base[:, :, 0]
    n_blk = cap_p // CHUNK + 2
    lists = _route_compact(base_s, rank, aff_all, n_blk)
    lists = lists.transpose(0, 2, 1, 3).reshape(N_EXPERTS, LIST_ROWS, n_blk * CHUNK)[:, :, :cap_p]
    idx = (lists[:, 0] * float(1 << TOK_DIGIT_BITS) + lists[:, 1]).astype(I32)
    gate = lists[:, 2] + lists[:, 3] + lists[:, 4]
    ns = -(-cap_p // FFN_MAX_ROWS)
    tm = cap_p // ns
    ye = _expert_ffn(idx.reshape(N_EXPERTS * ns, 1, tm), gate.reshape(N_EXPERTS, cap_p, 1), xp,
                     p["wgate"], p["wup"], p["wdown"], cap_p, tm, FFN_TF)
    rank_t = rank[:t // CHUNK].transpose(0, 2, 1)
    y = _combine(base_s, x1, rank_t, ye, p["g2"], p["b2"], cap_p)
    return y.reshape(batch, seq, D_MODEL)


def kernel(x_prompt, x_sample, meta_tokens, ln0_g, ln0_b, rel_bias, attn_sink, w_in, b_gate, ml_norm_g,
           w_branch_a, w_branch_b, w_out, ln1_g, ln1_b, w_router, w_gate, w_up, w_down, ln2_g, ln2_b):
    assert w_in.shape[0] == DEPTH
    w = w_in[0]
    sizes = (ML_HEADS * ML_DQK, ML_HEADS * ML_DQK, ML_HEADS * ML_DV, ML_HEADS * ML_DV, 4 * ML_HEADS,
             AT_HEADS * AT_DH, AT_KV_HEADS * AT_DH, AT_KV_HEADS * AT_DH, D_MODEL, D_MODEL)
    offs = np.concatenate([[0], np.cumsum(sizes)])
    mq, mk, mv, mo, mg, aq, ak, av, ga, gb = [w[:, offs[i]:offs[i + 1]] for i in range(10)]
    w_g = jnp.pad(mg, ((0, 0), (0, LANES - 4 * ML_HEADS))).astype(BF16)
    bg = jnp.pad(b_gate[0].astype(F32), (0, LANES - 4 * ML_HEADS))
    nb_max = max(x_prompt.shape[1], x_sample.shape[1]) // CHUNK
    bk_main, bk_meta0, bk_mq = _bucket_tables(nb_max)
    rb = rel_bias.astype(F32)
    p = {
        "g0": _row(ln0_g), "b0": _row(ln0_b), "g1": _row(ln1_g[0]), "b1": _row(ln1_b[0]),
        "g2": _row(ln2_g[0]), "b2": _row(ln2_b[0]), "ng": _row(ml_norm_g[0]),
        "w_main": jnp.concatenate([mv, mo, aq, ga, gb, mq, ak, av], axis=1).astype(BF16),
        "w_ktg": jnp.concatenate([mk.T.astype(BF16), w_g.T], axis=0), "w_g": w_g,
        "bg": bg.reshape(1, LANES), "bgt": bg.reshape(LANES, 1),
        "wa": w_branch_a[0].astype(BF16), "wb": w_branch_b[0].astype(BF16), "wo": w_out[0].astype(BF16),
        "wrt": w_router[0].T.astype(BF16),
        "wgate": w_gate[0], "wup": w_up[0], "wdown": w_down[0],
        "bias3": _attn_bias_variants(_bias_table(rb, bk_main), _bias_table(rb, bk_meta0)),
        "sink_col": jnp.broadcast_to(
            jnp.tile(jnp.repeat(attn_sink[0].astype(F32).reshape(AT_KV_HEADS, AT_GROUP), CHUNK // 2, axis=1),
                     (1, 2))[:, :, None] * LOG2E, (AT_KV_HEADS, AT_GROUP * CHUNK, LANES)),
        "bias_mq": _bias_table(rb, bk_mq),
        "sink_b": jnp.broadcast_to(attn_sink[0].astype(F32)[:, None], (AT_HEADS, LANES)),
        "meta_tokens": meta_tokens.astype(F32),
    }
    meta = _ln_proj(p["meta_tokens"], p["g0"], p["b0"], p["w_main"], p["w_ktg"], p["w_g"],
                    p["bg"], p["bgt"], N_META)
    return (_trunk(x_prompt, p, meta), _trunk(x_sample, p, meta))
```

```python
import functools
import math

import numpy as np
import jax
import jax.numpy as jnp
from jax import lax
from jax.experimental import pallas as pl
from jax.experimental.pallas import tpu as pltpu

F32 = jnp.float32
BF16 = jnp.bfloat16
I32 = jnp.int32

D_MODEL = 2048
N_META = 16
CHUNK = 128
ML_HEADS = 8
ML_DV = D_MODEL // ML_HEADS
ML_DQK = ML_DV // 2
AT_DH = 128
AT_HEADS = D_MODEL // AT_DH
AT_KV_HEADS = AT_HEADS // 4
AT_GROUP = AT_HEADS // AT_KV_HEADS
WINDOW = 128
REL_BUCKETS = 32
REL_MAX_DIST = 128
N_EXPERTS = 16
EXPERT_FF = D_MODEL
CAPACITY_FACTOR = 2
DEPTH = 1
ALPHA = (2.0 * DEPTH) ** 0.25
LN_EPS = 1e-5
M_INIT = -1e30
NEG = -1e30

LANES = 128
SUBLANES = 8
TOK_DIGIT_BITS = 8
BF16_ROWS = 16
VMEM_LIMIT = 56 * 1024 * 1024

PM_MV, PM_MO, PM_AQ, PM_GA, PM_GB, PM_MQ, PM_AK, PM_AV = 0, 2048, 4096, 6144, 8192, 10240, 11264, 11776
PM_WIDTH = 12288
PM_TN = 2048
PM_TM = 512
LN_ROWS = 256
MIX_TN = 1024
OUT_TM = 512
CT_W = ML_DV + LANES

NT_DIMS = (((1,), (1,)), ((), ()))


def _cparams(sem, vmem=VMEM_LIMIT):
    return pltpu.CompilerParams(dimension_semantics=sem, vmem_limit_bytes=vmem)


def _dot(a, b):
    return jnp.dot(a, b, preferred_element_type=F32)


def _dot_nt(a, b):
    return lax.dot_general(a, b, NT_DIMS, preferred_element_type=F32)


def _ln(x, g, b):
    mu = jnp.mean(x, axis=-1, keepdims=True)
    xc = x - mu
    var = jnp.mean(xc * xc, axis=-1, keepdims=True)
    return xc * lax.rsqrt(var + LN_EPS) * g + b


def _split3(x):
    hi = x.astype(BF16)
    r1 = x - hi.astype(F32)
    mid = r1.astype(BF16)
    lo = (r1 - mid.astype(F32)).astype(BF16)
    return hi, mid, lo


def _bias_kernel(rb_ref, bk_ref, o_ref):
    h = pl.program_id(0)
    bk = bk_ref[...]
    acc = jnp.full(bk.shape, NEG, F32)
    for b in range(REL_BUCKETS):
        acc = jnp.where(bk == b, rb_ref[b, h], acc)
    o_ref[0] = acc


def _bias_table(rel_bias, bucket):
    r, c = bucket.shape
    return pl.pallas_call(
        _bias_kernel,
        grid_spec=pltpu.PrefetchScalarGridSpec(
            num_scalar_prefetch=1, grid=(AT_HEADS,),
            in_specs=[pl.BlockSpec((r, c), lambda h, rb: (0, 0))],
            out_specs=pl.BlockSpec((1, r, c), lambda h, rb: (h, 0, 0))),
        out_shape=jax.ShapeDtypeStruct((AT_HEADS, r, c), F32),
        compiler_params=_cparams(("arbitrary",)),
        name="bias_table",
    )(rel_bias, bucket)


def _t5_bucket(rel):
    half = REL_BUCKETS // 2
    max_exact = half // 2
    n = jnp.abs(rel)
    nf = jnp.maximum(n, 1).astype(jnp.float32)
    large = max_exact + (jnp.log(nf / max_exact) / math.log(REL_MAX_DIST / max_exact)
                         * (half - max_exact)).astype(jnp.int32)
    large = jnp.minimum(large, half - 1)
    return jnp.where(rel > 0, half, 0) + jnp.where(n < max_exact, n, large)


def _t5_bucket_np(rel):
    half = REL_BUCKETS // 2
    max_exact = half // 2
    n = np.abs(rel)
    nf = np.maximum(n, 1).astype(np.float64)
    large = max_exact + (np.log(nf / max_exact) / math.log(REL_MAX_DIST / max_exact) * (half - max_exact)).astype(np.int64)
    large = np.minimum(large, half - 1)
    return np.where(rel > 0, half, 0) + np.where(n < max_exact, n, large)


def _bucket_tables(nb_max):
    i = np.arange(CHUNK)[:, None]
    m = np.arange(N_META)[None, :]
    ref_tab = _t5_bucket_np(m - (N_META + CHUNK + i))
    for j in range(1, nb_max):
        assert np.array_equal(_t5_bucket_np(m - (N_META + j * CHUNK + i)), ref_tab)
    qi = jnp.arange(CHUNK, dtype=I32)[:, None]
    c = jnp.arange(4 * CHUNK, dtype=I32)[None, :]
    rel_meta = c - (N_META + CHUNK + qi)
    rel_nb = (c - CHUNK) - CHUNK - qi
    vis_nb = (c >= CHUNK) & (jnp.abs(rel_nb) <= WINDOW)
    main = jnp.where(c < N_META, _t5_bucket(rel_meta), jnp.where(vis_nb, _t5_bucket(rel_nb), -1))
    c1 = jnp.arange(CHUNK, dtype=I32)[None, :]
    meta0 = jnp.where(c1 < N_META, _t5_bucket(c1 - (N_META + qi)), -1)
    mi = jnp.arange(N_META, dtype=I32)[:, None]
    c2 = jnp.arange(2 * CHUNK, dtype=I32)[None, :]
    rel_r = N_META + (c2 - CHUNK) - mi
    vis_r = (c2 >= CHUNK) & (jnp.abs(rel_r) <= WINDOW)
    mq = jnp.where(c2 < N_META, _t5_bucket(c2 - mi), jnp.where(vis_r, _t5_bucket(rel_r), -1))
    return main.astype(I32), meta0.astype(I32), mq.astype(I32)


def _proj_kernel(x_ref, g0_ref, b0_ref, w_ref, wktg_ref, wg_ref, bg_ref, bgt_ref,
                 pm_ref, kt_ref, gc_ref, gt_ref, xs_ref):
    n = pl.program_id(1)

    @pl.when(n == 0)
    def _():
        tm = x_ref.shape[0]
        rb = min(tm, LN_ROWS)

        def ln_block(r, carry):
            sl = pl.ds(pl.multiple_of(r * rb, BF16_ROWS), rb)
            xs_ref[sl, :] = _ln(x_ref[sl, :], g0_ref[...], b0_ref[...]).astype(BF16)
            return carry

        lax.fori_loop(0, tm // rb, ln_block, 0)
        xn = xs_ref[...]
        nk = ML_HEADS * ML_DQK
        ktg = _dot_nt(wktg_ref[...], xn)
        kt_ref[...] = ktg[:nk].astype(BF16)
        gt = ktg[nk:] + bgt_ref[...]
        gt_ref[...] = gt
        if tm % LANES == 0:
            gc_ref[...] = gt.T
        else:
            gc_ref[...] = _dot(xn, wg_ref[...]) + bg_ref[...]

    y = _dot(xs_ref[...], w_ref[...])
    lo1, hi1 = PM_MO // PM_TN, PM_AQ // PM_TN
    lo2, hi2 = PM_GA // PM_TN, PM_MQ // PM_TN
    is_sig = ((n >= lo1) & (n < hi1)) | ((n >= lo2) & (n < hi2))
    pm_ref[...] = jnp.where(is_sig, jax.nn.sigmoid(y), y).astype(BF16)


def _ln_proj(x2, g0, b0, w_main, w_ktg, w_g, bg, bgt, tm):
    t = x2.shape[0]
    grid = (t // tm, PM_WIDTH // PM_TN)
    const = lambda i, n: (0, 0)
    return pl.pallas_call(
        _proj_kernel,
        grid=grid,
        in_specs=[
            pl.BlockSpec((tm, D_MODEL), lambda i, n: (i, 0)),
            pl.BlockSpec((1, D_MODEL), const),
            pl.BlockSpec((1, D_MODEL), const),
            pl.BlockSpec((D_MODEL, PM_TN), lambda i, n: (0, n)),
            pl.BlockSpec((ML_HEADS * ML_DQK + LANES, D_MODEL), const, pipeline_mode=pl.Buffered(1)),
            pl.BlockSpec((D_MODEL, LANES), const),
            pl.BlockSpec((1, LANES), const),
            pl.BlockSpec((LANES, 1), const),
        ],
        out_specs=[
            pl.BlockSpec((tm, PM_TN), lambda i, n: (i, n)),
            pl.BlockSpec((ML_HEADS * ML_DQK, tm), lambda i, n: (0, i)),
            pl.BlockSpec((tm, LANES), lambda i, n: (i, 0)),
            pl.BlockSpec((LANES, tm), lambda i, n: (0, i)),
        ],
        out_shape=[
            jax.ShapeDtypeStruct((t, PM_WIDTH), BF16),
            jax.ShapeDtypeStruct((ML_HEADS * ML_DQK, t), BF16),
            jax.ShapeDtypeStruct((t, LANES), F32),
            jax.ShapeDtypeStruct((LANES, t), F32),
        ],
        scratch_shapes=[pltpu.VMEM((tm, D_MODEL), BF16)],
        compiler_params=_cparams(("arbitrary", "arbitrary")),
        name="ln_proj",
    )(x2, g0, b0, w_main, w_ktg, w_g, bg, bgt)


def _mlstm_chunk(L, reverse, q_ref, kt_ref, v_ref, gc_ref, gt_ref, ct_ref, m_ref, finish):
    H = ML_HEADS
    row = lax.broadcasted_iota(I32, (L, L), 0)
    col = lax.broadcasted_iota(I32, (L, L), 1)
    if reverse:
        tri = row <= col
        tri_t = row >= col
    else:
        tri = row >= col
        tri_t = row <= col
    tri_b = jnp.where(tri, 1.0, 0.0).astype(BF16)
    tri_tb = jnp.where(tri_t, 1.0, 0.0).astype(BF16)
    gc = gc_ref[...] * LOG2E
    gt = gt_ref[...] * LOG2E
    lfc = jax.nn.log_sigmoid(gc_ref[...]) * LOG2E
    lft = jax.nn.log_sigmoid(gt_ref[...]) * LOG2E
    bc_all = sum(_dot(tri_b, p) for p in _split3(lfc))
    br_all = sum(_dot(p, tri_tb) for p in _split3(lft))
    end = 0 if reverse else L - 1
    ci0, cf0 = (2 * H, 3 * H) if reverse else (0, H)
    scale = ML_DQK ** -0.5

    def stack(parts):
        return jnp.concatenate(parts, axis=0)

    def cols(x_all, c0):
        return stack([jnp.broadcast_to(x_all[:, c0 + h:c0 + h + 1], (L, L)) for h in range(H)])

    def rows_b(x_t, c0):
        return stack([jnp.broadcast_to(x_t[c0 + h:c0 + h + 1, :], (L, L)) for h in range(H)])

    def per_head(vals):
        return stack([jnp.broadcast_to(x, (L, L)) for x in vals])

    def rep(x):
        return jnp.broadcast_to(x, (H * L, L))

    def wide(x, n):
        if L == LANES and n % LANES == 0:
            return jnp.concatenate([x] * (n // LANES), axis=1)
        return jnp.broadcast_to(x[:, 0:1], (H * L, n))

    def wide_row(x, n):
        if L == LANES and n % LANES == 0:
            return jnp.concatenate([x] * (n // LANES), axis=1)
        return jnp.broadcast_to(x[:, 0:1], (1, n))

    def head(x, h):
        return x[h * L:(h + 1) * L]

    q = [q_ref[:, h * ML_DQK:(h + 1) * ML_DQK] for h in range(H)]
    kt = [kt_ref[h * ML_DQK:(h + 1) * ML_DQK, :] for h in range(H)]
    v = [v_ref[:, h * ML_DV:(h + 1) * ML_DV] for h in range(H)]
    ct = [ct_ref[h] for h in range(H)]
    m_prev_h = [m_ref[h:h + 1, 0:L] for h in range(H)]
    btot_h = [jnp.broadcast_to(bc_all[end:end + 1, cf0 + h:cf0 + h + 1], (1, L)) for h in range(H)]

    bc = cols(bc_all, cf0)
    igc = cols(gc, ci0)
    m_prev = per_head(m_prev_h)
    btot = per_head(btot_h)
    mask = stack([tri] * H)
    a_rows = gt[ci0:ci0 + H, :] - br_all[cf0:cf0 + H, :]
    d = jnp.where(mask, bc + rows_b(a_rows, 0), NEG)
    inter = bc + m_prev
    m_comb = jnp.maximum(inter, rep(jnp.max(d, axis=1, keepdims=True)))
    w = jnp.exp2(d - m_comb)
    w_inter = jnp.exp2(inter - m_comb) * scale
    s = stack([_dot(q[h], kt[h]) for h in range(H)]) * w * scale
    qc = stack([_dot(q[h], ct[h].astype(BF16)) for h in range(H)])
    sb = s.astype(BF16)
    num = stack([_dot(head(sb, h), v[h]) for h in range(H)]) + wide(w_inter, ML_DV) * qc[:, :ML_DV]
    den = rep(jnp.sum(s, axis=1, keepdims=True)) + w_inter * qc[:, ML_DV:ML_DV + L]
    finish(num * wide(1.0 / jnp.maximum(jnp.abs(den), jnp.exp2(-m_comb)), ML_DV))

    dec = btot - bc + igc
    inter_end_h = [btot_h[h] + m_prev_h[h] for h in range(H)]
    m_new_h = [jnp.maximum(inter_end_h[h], jnp.max(head(dec, h), axis=0, keepdims=True)) for h in range(H)]
    w_end = jnp.exp2(dec - per_head(m_new_h))
    vf = stack([x.astype(F32) for x in v])
    wv = jnp.concatenate([(wide(w_end, ML_DV) * vf).astype(BF16), wide(w_end, LANES).astype(BF16)], axis=1)
    for h in range(H):
        w_prev = jnp.exp2(inter_end_h[h] - m_new_h[h])
        ct_ref[h] = wide_row(w_prev, CT_W) * ct[h] + _dot(kt[h], head(wv, h))
        m_ref[h:h + 1, :] = wide_row(m_new_h[h], LANES)


def _mlstm_kernel(*refs, reverse, nb, merge):
    (q_ref, kt_ref, v_ref, gc_ref, gt_ref, qm_ref, ktm_ref, vm_ref, gcm_ref, gtm_ref) = refs[:10]
    if merge:
        o_ref, om_ref, hb_ref, hbm_ref, ng_ref, out_ref, outm_ref, ct_ref, m_ref = refs[10:]
    else:
        out_ref, outm_ref, ct_ref, m_ref = refs[10:]
    j = pl.program_id(1)

    @pl.when(j == 0)
    def _():
        ct_ref[...] = jnp.zeros(ct_ref.shape, F32)
        m_ref[...] = jnp.full(m_ref.shape, M_INIT * LOG2E, F32)

    def make_emit(dst_ref, other_ref, gate_ref):
        def emit(hv):
            L = hv.shape[0] // ML_HEADS
            sls = [slice(h * ML_DV, (h + 1) * ML_DV) for h in range(ML_HEADS)]
            if merge:
                hs = hv + jnp.concatenate([other_ref[:, sl] for sl in sls], axis=0)
                mu = jnp.broadcast_to(jnp.mean(hs, axis=-1, keepdims=True), hs.shape)
                hc = hs - mu
                var = jnp.broadcast_to(jnp.mean(hc * hc, axis=-1, keepdims=True), hs.shape)
                ng = jnp.concatenate([jnp.broadcast_to(ng_ref[:, sl], (L, ML_DV)) for sl in sls], axis=0)
                og = jnp.concatenate([gate_ref[:, sl].astype(F32) for sl in sls], axis=0)
                hv = hc * lax.rsqrt(var + LN_EPS) * (ng * og)
            for h, sl in enumerate(sls):
                dst_ref[:, sl] = hv[h * L:(h + 1) * L].astype(dst_ref.dtype)
        return emit

    is_meta = (j == nb) if reverse else (j == 0)

    @pl.when(is_meta)
    def _():
        emit = make_emit(outm_ref, hbm_ref if merge else None, om_ref if merge else None)
        _mlstm_chunk(N_META, reverse, qm_ref, ktm_ref, vm_ref, gcm_ref, gtm_ref, ct_ref, m_ref, emit)

    @pl.when(jnp.logical_not(is_meta))
    def _():
        emit = make_emit(out_ref, hb_ref if merge else None, o_ref if merge else None)
        _mlstm_chunk(CHUNK, reverse, q_ref, kt_ref, v_ref, gc_ref, gt_ref, ct_ref, m_ref, emit)


def _mlstm(pm, kt, gc, gt, pm_m, kt_m, gc_m, gt_m, batch, nb, *, reverse, merged=None):
    t = pm.shape[0]
    if reverse:
        rblk = lambda b, j: b * nb + jnp.maximum(nb - 1 - j, 0)
    else:
        rblk = lambda b, j: b * nb + jnp.maximum(j - 1, 0)
    const = lambda b, j: (0, 0)
    in_specs = [
        pl.BlockSpec((CHUNK, ML_HEADS * ML_DQK), lambda b, j: (rblk(b, j), PM_MQ // (ML_HEADS * ML_DQK))),
        pl.BlockSpec((ML_HEADS * ML_DQK, CHUNK), lambda b, j: (0, rblk(b, j))),
        pl.BlockSpec((CHUNK, D_MODEL), lambda b, j: (rblk(b, j), PM_MV // D_MODEL)),
        pl.BlockSpec((CHUNK, LANES), lambda b, j: (rblk(b, j), 0)),
        pl.BlockSpec((LANES, CHUNK), lambda b, j: (0, rblk(b, j))),
        pl.BlockSpec((N_META, ML_HEADS * ML_DQK), lambda b, j: (0, PM_MQ // (ML_HEADS * ML_DQK))),
        pl.BlockSpec((ML_HEADS * ML_DQK, N_META), const),
        pl.BlockSpec((N_META, D_MODEL), lambda b, j: (0, PM_MV // D_MODEL)),
        pl.BlockSpec((N_META, LANES), const),
        pl.BlockSpec((LANES, N_META), const),
    ]
    args = [pm, kt, pm, gc, gt, pm_m, kt_m, pm_m, gc_m, gt_m]
    merge = merged is not None
    if merge:
        hb, hb_m, ng = merged
        in_specs += [
            pl.BlockSpec((CHUNK, D_MODEL), lambda b, j: (rblk(b, j), PM_MO // D_MODEL)),
            pl.BlockSpec((N_META, D_MODEL), lambda b, j: (0, PM_MO // D_MODEL)),
            pl.BlockSpec((CHUNK, D_MODEL), lambda b, j: (rblk(b, j), 0)),
            pl.BlockSpec((N_META, D_MODEL), lambda b, j: (b, 0)),
            pl.BlockSpec((1, D_MODEL), const),
        ]
        args += [pm, pm_m, hb, hb_m, ng]
    odt = BF16 if merge else F32
    return pl.pallas_call(
        functools.partial(_mlstm_kernel, reverse=reverse, nb=nb, merge=merge),
        grid=(batch, nb + 1),
        in_specs=in_specs,
        out_specs=[
            pl.BlockSpec((CHUNK, D_MODEL), lambda b, j: (rblk(b, j), 0)),
            pl.BlockSpec((N_META, D_MODEL), lambda b, j: (b, 0)),
        ],
        out_shape=[
            jax.ShapeDtypeStruct((t, D_MODEL), odt),
            jax.ShapeDtypeStruct((batch * N_META, D_MODEL), odt),
        ],
        scratch_shapes=[pltpu.VMEM((ML_HEADS, ML_DQK, CT_W), F32), pltpu.VMEM((ML_HEADS, LANES), F32)],
        compiler_params=_cparams(("arbitrary", "arbitrary")),
        name="mlstm_bwd" if reverse else "mlstm_fwd",
    )(*args)


def _softmax_pv(sg, snk, vcat):
    m = jnp.maximum(jnp.max(sg, axis=1, keepdims=True), snk)
    p = jnp.exp(sg - m)
    l = jnp.sum(p, axis=1, keepdims=True) + jnp.exp(snk - m)
    return _dot(p.astype(BF16), vcat) * (1.0 / l)


LOG2E = math.log2(math.e)


def _attn_kernel(q_ref, kp_ref, kc_ref, kn_ref, vp_ref, vc_ref, vn_ref, km_ref, vm_ref,
                 bias_ref, sink_ref, o_ref):
    half = CHUNK // 2
    pad = jnp.zeros((half - N_META, AT_DH), BF16)
    ones = jnp.ones((3 * CHUNK, LANES), BF16)
    scale2 = AT_DH ** -0.5 * LOG2E
    rows_u = AT_GROUP * half
    for c in range(AT_KV_HEADS):
        ks = slice(c * AT_DH, (c + 1) * AT_DH)

        def keys(u, m_ref, p_ref, c_ref, n_ref):
            if u == 0:
                return jnp.concatenate([p_ref[:, ks], c_ref[:, ks], n_ref[:half, ks], m_ref[:, ks], pad], axis=0)
            return jnp.concatenate([p_ref[half:, ks], m_ref[:, ks], pad, c_ref[:, ks], n_ref[:, ks]], axis=0)

        def queries(u):
            return jnp.concatenate(
                [q_ref[u * half:(u + 1) * half, (c * AT_GROUP + g) * AT_DH:(c * AT_GROUP + g + 1) * AT_DH]
                 for g in range(AT_GROUP)], axis=0)

        t = jnp.concatenate([_dot_nt(queries(u), keys(u, km_ref, kp_ref, kc_ref, kn_ref)) for u in range(2)],
                            axis=0) * scale2 + bias_ref[0, c]
        snk = sink_ref[c]
        m = jnp.maximum(jnp.broadcast_to(jnp.max(t, axis=1, keepdims=True), snk.shape), snk)
        p = jnp.exp2(t - jnp.concatenate([m] * 3, axis=1)).astype(BF16)
        ov = jnp.concatenate(
            [_dot(p[u * rows_u:(u + 1) * rows_u],
                  jnp.concatenate([keys(u, vm_ref, vp_ref, vc_ref, vn_ref), ones], axis=1)) for u in range(2)], axis=0)
        l = ov[:, AT_DH:] + jnp.exp2(snk - m)
        o = ov[:, :AT_DH] * (1.0 / l)
        for u in range(2):
            for g in range(AT_GROUP):
                h = c * AT_GROUP + g
                r0 = u * rows_u + g * half
                o_ref[u * half:(u + 1) * half, h * AT_DH:(h + 1) * AT_DH] = o[r0:r0 + half].astype(o_ref.dtype)


def _attn_real(pm, pm_m, bias3, sink_col, batch, nb):
    assert nb >= 2
    t = pm.shape[0]
    kvw = AT_KV_HEADS * AT_DH
    cur = lambda b, j: b * nb + j
    prv = lambda b, j: b * nb + jnp.maximum(j - 1, 0)
    nxt = lambda b, j: b * nb + jnp.minimum(j + 1, nb - 1)
    variant = lambda b, j: (jnp.where(j == 0, 1, jnp.where(j == nb - 1, 2, 0)), 0, 0, 0)
    kcol, vcol = PM_AK // kvw, PM_AV // kvw
    return pl.pallas_call(
        _attn_kernel,
        grid=(batch, nb),
        in_specs=[
            pl.BlockSpec((CHUNK, D_MODEL), lambda b, j: (cur(b, j), PM_AQ // D_MODEL)),
            pl.BlockSpec((CHUNK, kvw), lambda b, j: (prv(b, j), kcol)),
            pl.BlockSpec((CHUNK, kvw), lambda b, j: (cur(b, j), kcol)),
            pl.BlockSpec((CHUNK, kvw), lambda b, j: (nxt(b, j), kcol)),
            pl.BlockSpec((CHUNK, kvw), lambda b, j: (prv(b, j), vcol)),
            pl.BlockSpec((CHUNK, kvw), lambda b, j: (cur(b, j), vcol)),
            pl.BlockSpec((CHUNK, kvw), lambda b, j: (nxt(b, j), vcol)),
            pl.BlockSpec((N_META, kvw), lambda b, j: (0, kcol)),
            pl.BlockSpec((N_META, kvw), lambda b, j: (0, vcol)),
            pl.BlockSpec((1, AT_KV_HEADS, AT_GROUP * CHUNK, 3 * CHUNK), variant),
            pl.BlockSpec((AT_KV_HEADS, AT_GROUP * CHUNK, LANES), lambda b, j: (0, 0, 0)),
        ],
        out_specs=pl.BlockSpec((CHUNK, D_MODEL), lambda b, j: (cur(b, j), 0)),
        out_shape=jax.ShapeDtypeStruct((t, D_MODEL), BF16),
        compiler_params=_cparams(("arbitrary", "arbitrary")),
        name="attn_real",
    )(pm, pm, pm, pm, pm, pm, pm, pm_m, pm_m, bias3, sink_col)


def _attn_meta_kernel(q_ref, kr_ref, vr_ref, km_ref, vm_ref, bias_ref, sink_ref, o_ref):
    pad = jnp.zeros((CHUNK - N_META, AT_DH), BF16)
    scale = AT_DH ** -0.5
    for c in range(AT_KV_HEADS):
        ks = slice(c * AT_DH, (c + 1) * AT_DH)
        kcat = jnp.concatenate([km_ref[:, ks], pad, kr_ref[:, ks]], axis=0)
        vcat = jnp.concatenate([vm_ref[:, ks], pad, vr_ref[:, ks]], axis=0)
        qc = jnp.concatenate(
            [q_ref[:, (c * AT_GROUP + g) * AT_DH:(c * AT_GROUP + g + 1) * AT_DH] for g in range(AT_GROUP)], axis=0)
        s = _dot_nt(qc, kcat) * scale
        for g in range(AT_GROUP):
            h = c * AT_GROUP + g
            sg = s[g * N_META:(g + 1) * N_META] + bias_ref[h]
            o = _softmax_pv(sg, sink_ref[h:h + 1, 0:1], vcat)
            o_ref[:, h * AT_DH:(h + 1) * AT_DH] = o.astype(o_ref.dtype)


def _attn_meta(pm, pm_m, bias_mq, sink_b, batch, nb):
    kvw = AT_KV_HEADS * AT_DH
    kcol, vcol = PM_AK // kvw, PM_AV // kvw
    return pl.pallas_call(
        _attn_meta_kernel,
        grid=(batch,),
        in_specs=[
            pl.BlockSpec((N_META, D_MODEL), lambda b: (0, PM_AQ // D_MODEL)),
            pl.BlockSpec((CHUNK, kvw), lambda b: (b * nb, kcol)),
            pl.BlockSpec((CHUNK, kvw), lambda b: (b * nb, vcol)),
            pl.BlockSpec((N_META, kvw), lambda b: (0, kcol)),
            pl.BlockSpec((N_META, kvw), lambda b: (0, vcol)),
            pl.BlockSpec((AT_HEADS, N_META, 2 * CHUNK), lambda b: (0, 0, 0)),
            pl.BlockSpec((AT_HEADS, LANES), lambda b: (0, 0)),
        ],
        out_specs=pl.BlockSpec((N_META, D_MODEL), lambda b: (b, 0)),
        out_shape=jax.ShapeDtypeStruct((batch * N_META, D_MODEL), BF16),
        compiler_params=_cparams(("arbitrary",)),
        name="attn_meta",
    )(pm_m, pm, pm, pm_m, pm_m, bias_mq, sink_b)


def _mix_kernel(ha_ref, hb_ref, ga_ref, gb_ref, wa_ref, wb_ref, o_ref):
    a = _dot(ha_ref[...], wa_ref[...])
    b = _dot(hb_ref[...], wb_ref[...])
    o_ref[...] = (ga_ref[...].astype(F32) * a + gb_ref[...].astype(F32) * b).astype(o_ref.dtype)


def _mix(ha, hb, gates_src, wa, wb, tm, ga_col, gb_col):
    t = ha.shape[0]
    nt = D_MODEL // MIX_TN
    return pl.pallas_call(
        _mix_kernel,
        grid=(t // tm, nt),
        in_specs=[
            pl.BlockSpec((tm, D_MODEL), lambda i, n: (i, 0)),
            pl.BlockSpec((tm, D_MODEL), lambda i, n: (i, 0)),
            pl.BlockSpec((tm, MIX_TN), lambda i, n: (i, ga_col + n)),
            pl.BlockSpec((tm, MIX_TN), lambda i, n: (i, gb_col + n)),
            pl.BlockSpec((D_MODEL, MIX_TN), lambda i, n: (0, n)),
            pl.BlockSpec((D_MODEL, MIX_TN), lambda i, n: (0, n)),
        ],
        out_specs=pl.BlockSpec((tm, MIX_TN), lambda i, n: (i, n)),
        out_shape=jax.ShapeDtypeStruct((t, D_MODEL), BF16),
        compiler_params=_cparams(("arbitrary", "arbitrary")),
        name="branch_mix",
    )(ha, hb, gates_src, gates_src, wa, wb)


HALF_D = D_MODEL // 2
BF16_BITS = 16
HIGH_HALF = -(1 << BF16_BITS)


def _pack_bf16_pairs(x):
    lo = lax.bitcast_convert_type(x[:, :HALF_D].astype(BF16).astype(F32), I32)
    hi = lax.bitcast_convert_type(x[:, HALF_D:].astype(BF16).astype(F32), I32)
    return lax.shift_right_logical(lo, BF16_BITS) | (hi & jnp.int32(HIGH_HALF))


def _unpack_bf16_pairs(u):
    lo = lax.bitcast_convert_type(u << BF16_BITS, F32)
    hi = lax.bitcast_convert_type(u & jnp.int32(HIGH_HALF), F32)
    return lo.astype(BF16), hi.astype(BF16)


PACK_SUB = HALF_D // LANES


def _out_kernel(*refs, real):
    mix_ref, x_ref, g0_ref, b0_ref, g1_ref, b1_ref, wo_ref, wrt_ref = refs[:8]
    if real:
        x1_ref, x1p_ref, aff_ref = refs[8:]
    else:
        x1p_ref, aff_ref = refs[9:]

    def body():
        y = _dot(mix_ref[...], wo_ref[...])
        h0a = _ln(x_ref[...], g0_ref[...], b0_ref[...])
        x1 = _ln(h0a + y, g1_ref[...], b1_ref[...])
        x1p_ref[...] = pltpu.einshape("t(cl)->tcl", _pack_bf16_pairs(x1), c=PACK_SUB)
        lt = _dot_nt(wrt_ref[...], x1.astype(BF16))
        e = jnp.exp(lt - jnp.max(lt, axis=0, keepdims=True))
        aff = e / jnp.sum(e, axis=0, keepdims=True)
        if real:
            x1_ref[...] = x1
            for c in range(aff.shape[1] // CHUNK):
                aff_ref[c] = aff[:, c * CHUNK:(c + 1) * CHUNK]
        else:
            aff_ref[...] = aff

    if not real:
        body()
        return
    last = pl.program_id(0) == pl.num_programs(0) - 1
    pl.when(jnp.logical_not(last))(body)

    @pl.when(last)
    def _():
        x1p_ref[...] = jnp.zeros(x1p_ref.shape, I32)


def _out_ln_router(mix, x2, g0, b0, g1, b1, wo, wrt, tm, n_meta_rows, packed=None):
    t = mix.shape[0]
    real = packed is None
    n = t // tm
    const = lambda i: (0, 0)
    row = lambda i: (jnp.minimum(i, n - 1), 0)
    in_specs = [
        pl.BlockSpec((tm, D_MODEL), row),
        pl.BlockSpec((tm, D_MODEL), row),
        pl.BlockSpec((1, D_MODEL), const), pl.BlockSpec((1, D_MODEL), const),
        pl.BlockSpec((1, D_MODEL), const), pl.BlockSpec((1, D_MODEL), const),
        pl.BlockSpec((D_MODEL, D_MODEL), const, pipeline_mode=pl.Buffered(1)),
        pl.BlockSpec((N_EXPERTS, D_MODEL), const),
    ]
    args = [mix, x2, g0, b0, g1, b1, wo, wrt]
    if real:
        assert n_meta_rows <= tm
        out_specs = [
            pl.BlockSpec((tm, D_MODEL), row),
            pl.BlockSpec((tm, PACK_SUB, LANES), lambda i: (i, 0, 0)),
            pl.BlockSpec((tm // CHUNK, N_EXPERTS, CHUNK), lambda i: (jnp.minimum(i, n - 1), 0, 0)),
        ]
        out_shape = [
            jax.ShapeDtypeStruct((t, D_MODEL), F32),
            jax.ShapeDtypeStruct((t + n_meta_rows, PACK_SUB, LANES), I32),
            jax.ShapeDtypeStruct((t // CHUNK, N_EXPERTS, CHUNK), F32),
        ]
        aliases = {}
    else:
        n_real = packed.shape[0] - t
        assert tm == t and n_real % t == 0
        in_specs.append(pl.BlockSpec(memory_space=pl.ANY))
        args.append(packed)
        out_specs = [
            pl.BlockSpec((t, PACK_SUB, LANES), lambda i: (n_real // t, 0, 0)),
            pl.BlockSpec((N_EXPERTS, t), const),
        ]
        out_shape = [jax.ShapeDtypeStruct(packed.shape, I32), jax.ShapeDtypeStruct((N_EXPERTS, t), F32)]
        aliases = {8: 0}
    return pl.pallas_call(
        functools.partial(_out_kernel, real=real),
        grid=(n + 1 if real else n,),
        in_specs=in_specs,
        out_specs=out_specs,
        out_shape=out_shape,
        input_output_aliases=aliases,
        compiler_params=_cparams(("arbitrary",)),
        name="out_ln1_router" if real else "out_ln1_router_meta",
    )(*args)


def _route_select_kernel(aff_ref, rank_ref, base_ref, p_s, b_s, *, cap, nc, batch, nb):
    aff = aff_ref[...]
    bits = lax.bitcast_convert_type(aff, I32)

    def count(maskf):
        return jnp.sum(jnp.sum(maskf, axis=0), axis=1, keepdims=True)

    def search(it, lo):
        cand = lo | (jnp.int32(1) << (30 - it))
        cnt = count(jnp.where(bits >= cand[None], 1.0, 0.0))
        return jnp.where(cnt >= cap, cand, lo)

    thr = lax.fori_loop(0, 31, search, jnp.zeros((N_EXPERTS, 1), I32))
    gt = bits > thr[None]
    eq = bits == thr[None]
    need = cap - count(jnp.where(gt, 1.0, 0.0))
    r_i = lax.broadcasted_iota(I32, (CHUNK, CHUNK), 0)
    c_i = lax.broadcasted_iota(I32, (CHUNK, CHUNK), 1)
    upper = jnp.where(r_i <= c_i, 1.0, 0.0).astype(BF16)

    def prefix(maskb):
        p = _dot(jnp.where(maskb, 1.0, 0.0).astype(BF16).reshape(nc * N_EXPERTS, CHUNK), upper)
        p_s[...] = p.reshape(nc, N_EXPERTS, CHUNK)

        def step(c, carry):
            b_s[c] = jnp.broadcast_to(carry, (N_EXPERTS, CHUNK))
            return carry + p_s[c][:, CHUNK - 1:CHUNK]

        lax.fori_loop(0, nc, step, jnp.zeros((N_EXPERTS, 1), F32))

    prefix(eq)
    pe, be = p_s[...], b_s[...]
    n_real = batch * nb
    r_tot = be[n_real][:, 0:1]
    lane_seq = lax.broadcasted_iota(I32, (1, CHUNK), 1) // N_META
    pieces, meta_corr = [], {}
    for b in range(batch):
        cm = n_real + (b * N_META) // CHUNK
        last = (b * N_META) % CHUNK + N_META - 1
        meta_le = be[cm][:, 0:1] + pe[cm][:, last:last + 1] - r_tot
        real_lt = be[b * nb][:, 0:1]
        pieces.append(be[b * nb:(b + 1) * nb] + pe[b * nb:(b + 1) * nb] - 1.0 + meta_le[None])
        corr = jnp.where(lane_seq == (b * N_META % CHUNK) // N_META, real_lt - r_tot, 0.0)
        meta_corr[cm] = meta_corr[cm] + corr if cm in meta_corr else corr
    for cm in range(n_real, nc):
        pieces.append((be[cm] + pe[cm] - 1.0 + meta_corr[cm])[None])
    eq_rank = jnp.concatenate(pieces, axis=0)
    sel = gt | (eq & (eq_rank < need[None]))
    prefix(sel)
    rank_ref[...] = jnp.where(sel, p_s[...] - 1.0, -1.0).astype(I32)
    base_ref[...] = b_s[...].astype(I32)


def _route_select(aff, cap, batch, nb):
    nc = aff.shape[0]
    shp = (nc, N_EXPERTS, CHUNK)
    full = pl.BlockSpec(shp, lambda i: (0, 0, 0))
    return pl.pallas_call(
        functools.partial(_route_select_kernel, cap=cap, nc=nc, batch=batch, nb=nb),
        grid=(1,),
        in_specs=[full],
        out_specs=[full, full],
        out_shape=[jax.ShapeDtypeStruct(shp, I32), jax.ShapeDtypeStruct(shp, I32)],
        scratch_shapes=[pltpu.VMEM(shp, F32), pltpu.VMEM(shp, F32)],
        compiler_params=_cparams(("arbitrary",)),
        name="route_select",
    )(aff)


FFN_MAX_ROWS = 2304
FFN_TF = 256
FFN_ROW_SPLIT = 2
RELAYOUT_BLOCKS = 8
LIST_ROWS = 16


def _route_compact_kernel(base_ref, rank_ref, aff_ref, out_ref, *, group):
    step = pl.program_id(0)

    @pl.when(step == 0)
    def _():
        out_ref[...] = jnp.zeros(out_ref.shape, F32)

    lane = lax.broadcasted_iota(I32, (1, CHUNK), 1)
    r_io = lax.broadcasted_iota(I32, (CHUNK, CHUNK), 0)
    zeros = jnp.zeros((LIST_ROWS - 5, CHUNK), F32)
    for k in range(group):
        c = step * group + k
        tok = c * CHUNK + lane
        t_hi = (tok >> TOK_DIGIT_BITS).astype(F32)
        t_lo = (tok & ((1 << TOK_DIGIT_BITS) - 1)).astype(F32)
        for e in range(N_EXPERTS):
            rk = rank_ref[k, e:e + 1, :]
            onehot = jnp.where(r_io == rk, 1.0, 0.0).astype(BF16)
            a_hi, a_mid, a_lo = _split3(aff_ref[k, e:e + 1, :])
            vals = jnp.concatenate([t_hi, t_lo, a_hi.astype(F32), a_mid.astype(F32), a_lo.astype(F32), zeros],
                                   axis=0).astype(BF16)
            comp = _dot_nt(vals, onehot)
            base = base_ref[c, e]
            blk = base // CHUNK
            off = base - blk * CHUNK
            rolled = pltpu.roll(comp, off, axis=1)
            keep_lo = lane >= off
            out_ref[e, blk] = jnp.where(keep_lo, rolled, out_ref[e, blk])
            out_ref[e, blk + 1] = jnp.where(keep_lo, out_ref[e, blk + 1], rolled)


def _route_compact(base_s, rank, aff, n_blk):
    nc = rank.shape[0]
    group = next(g for g in (4, 3, 2, 1) if nc % g == 0)
    blk = pl.BlockSpec((group, N_EXPERTS, CHUNK), lambda c, b: (c, 0, 0))
    return pl.pallas_call(
        functools.partial(_route_compact_kernel, group=group),
        grid_spec=pltpu.PrefetchScalarGridSpec(
            num_scalar_prefetch=1, grid=(nc // group,),
            in_specs=[blk, blk],
            out_specs=pl.BlockSpec((N_EXPERTS, n_blk, LIST_ROWS, CHUNK), lambda c, b: (0, 0, 0, 0))),
        out_shape=jax.ShapeDtypeStruct((N_EXPERTS, n_blk, LIST_ROWS, CHUNK), F32),
        compiler_params=_cparams(("arbitrary",)),
        name="route_compact",
    )(base_s, rank, aff)


def _ffn_kernel(idx_ref, idxn_ref, gate_ref, xp_hbm, wg_ref, wu_ref, wd_ref, ye_ref,
                rows_ref, xp_ref, hid_ref, sem, *, tm, rs, nf, tf):
    e, s, j = pl.program_id(0), pl.program_id(1), pl.program_id(2)
    tile = e * pl.num_programs(1) + s
    n_sub = tm // rs
    per_slot = tm // (nf * n_sub)

    def relayout():
        rb = tm // RELAYOUT_BLOCKS

        def body(r, carry):
            sl = pl.ds(pl.multiple_of(r * rb, SUBLANES), rb)
            y = pltpu.einshape("tcl->ctl", rows_ref[sl])
            for c in range(PACK_SUB):
                xp_ref[sl, c * LANES:(c + 1) * LANES] = y[c]
            return carry

        lax.fori_loop(0, RELAYOUT_BLOCKS, body, 0)

    def row_copy(iref, r):
        return pltpu.make_async_copy(xp_hbm.at[pl.ds(iref[0, 0, r], 1)], rows_ref.at[pl.ds(r, 1)], sem.at[0])

    def wait_rows():
        pltpu.make_async_copy(xp_hbm.at[pl.ds(0, tm)], rows_ref, sem.at[0]).wait()

    def prefetch_next(r):
        first = (j * n_sub + r) * per_slot
        for k in range(per_slot):
            row_copy(idxn_ref, first + k).start()

    def rows(r):
        return pl.ds(pl.multiple_of(r * rs, BF16_ROWS), rs)

    @pl.when((tile == 0) & (j == 0))
    def _():
        def body(r, carry):
            row_copy(idx_ref, r).start()
            return carry

        lax.fori_loop(0, tm, body, 0)
        wait_rows()
        relayout()

    @pl.when(j < nf)
    def _():
        wg = wg_ref[0].astype(BF16)
        wu = wu_ref[0].astype(BF16)

        def body(r, carry):
            prefetch_next(r)
            lo, hi = _unpack_bf16_pairs(xp_ref[rows(r), :])
            x = jnp.concatenate([lo, hi], axis=1)
            g = _dot(x, wg)
            u = _dot(x, wu)
            hid_ref[j, rows(r), :] = (g * jax.nn.sigmoid(g) * u).astype(BF16)
            return carry

        lax.fori_loop(0, n_sub, body, 0)

    @pl.when(j == nf)
    def _():
        wait_rows()
        relayout()

    @pl.when(j >= nf)
    def _():
        wd = wd_ref[0].astype(BF16)

        def body(r, carry):
            acc = _dot(hid_ref[0, rows(r), :], wd[0:tf])
            for f in range(1, nf):
                acc = acc + _dot(hid_ref[f, rows(r), :], wd[f * tf:(f + 1) * tf])
            ye_ref[0, rows(r), :] = (acc * gate_ref[0, rows(r), :]).astype(ye_ref.dtype)
            return carry

        lax.fori_loop(0, n_sub, body, 0)


def _expert_ffn(idx3, gate3, xp, wg, wu, wd, cap_p, tm, tf):
    ns = cap_p // tm
    nf = EXPERT_FF // tf
    n_tiles = N_EXPERTS * ns
    rs = tm // FFN_ROW_SPLIT
    assert rs % BF16_ROWS == 0 and D_MODEL // tf == nf and tm % (nf * (tm // rs)) == 0
    assert tm % (SUBLANES * RELAYOUT_BLOCKS) == 0
    return pl.pallas_call(
        functools.partial(_ffn_kernel, tm=tm, rs=rs, nf=nf, tf=tf),
        grid=(N_EXPERTS, ns, 2 * nf),
        in_specs=[
            pl.BlockSpec((1, 1, tm), lambda e, s, j: (e * ns + s, 0, 0), memory_space=pltpu.SMEM),
            pl.BlockSpec((1, 1, tm), lambda e, s, j: (jnp.minimum(e * ns + s + 1, n_tiles - 1), 0, 0),
                         memory_space=pltpu.SMEM),
            pl.BlockSpec((1, tm, 1), lambda e, s, j: (e, s, 0)),
            pl.BlockSpec(memory_space=pl.ANY),
            pl.BlockSpec((1, D_MODEL, tf), lambda e, s, j: (e, 0, jnp.minimum(j, nf - 1))),
            pl.BlockSpec((1, D_MODEL, tf), lambda e, s, j: (e, 0, jnp.minimum(j, nf - 1))),
            pl.BlockSpec((1, EXPERT_FF, tf), lambda e, s, j: (e, 0, jnp.maximum(j - nf, 0))),
        ],
        out_specs=pl.BlockSpec((1, tm, tf), lambda e, s, j: (e, s, jnp.maximum(j - nf, 0))),
        out_shape=jax.ShapeDtypeStruct((N_EXPERTS, cap_p, D_MODEL), BF16),
        scratch_shapes=[
            pltpu.VMEM((tm, PACK_SUB, LANES), I32),
            pltpu.VMEM((tm, HALF_D), I32),
            pltpu.VMEM((nf, tm, tf), BF16),
            pltpu.SemaphoreType.DMA((1,)),
        ],
        compiler_params=_cparams(("arbitrary", "arbitrary", "arbitrary")),
        name="expert_ffn",
    )(idx3, idx3, gate3, xp, wg, wu, wd)


COMB_ROWS = CHUNK + BF16_ROWS


COMB_WIN = 48


def _combine_kernel(base_ref, x1_ref, rkt_ref, ye_hbm, g2_ref, b2_ref, y_ref, buf, big, acc_ref, sem, bsem,
                    *, n_tiles, cap_p):
    i = pl.program_id(0)

    def window(c, e):
        b = base_ref[c, e]
        st = jnp.minimum((b // BF16_ROWS) * BF16_ROWS, cap_p - COMB_WIN)
        fits = base_ref[c + 1, e] - st <= COMB_WIN
        return b, st, fits

    def copy(c, e, slot):
        st = pl.multiple_of(window(c, e)[1], BF16_ROWS)
        return pltpu.make_async_copy(ye_hbm.at[e, pl.ds(st, COMB_WIN), :],
                                     buf.at[slot, pl.ds(e * COMB_WIN, COMB_WIN), :], sem.at[slot])

    @pl.when(i == 0)
    def _():
        for e in range(N_EXPERTS):
            copy(0, e, 0).start()

    @pl.when(i + 1 < n_tiles)
    def _():
        for e in range(N_EXPERTS):
            copy(i + 1, e, (i + 1) % 2).start()

    slot = i % 2
    rkt = rkt_ref[0]
    lane = lax.broadcasted_iota(I32, (1, LANES), 1)
    pos = []
    for e in range(N_EXPERTS):
        b, st, fits = window(i, e)
        rk = rkt[:, e:e + 1]
        pos.append(jnp.where((rk >= 0) & fits, rk + (b - st + e * COMB_WIN), -1))
    pieces = []
    for p in range(N_EXPERTS * COMB_WIN // LANES):
        lo, hi = p * LANES, (p + 1) * LANES
        target = None
        for e in range(N_EXPERTS):
            e_lo, e_hi = max(e * COMB_WIN, lo), min((e + 1) * COMB_WIN, hi)
            if e_lo >= e_hi:
                continue
            cand = pos[e] - lo
            if target is None:
                target = cand
            else:
                target = jnp.where((lane >= e_lo - lo) & (lane < e_hi - lo), cand, target)
        pieces.append(jnp.where(target == lane, 1.0, 0.0).astype(BF16))
    onehot = jnp.concatenate(pieces, axis=1)
    pltpu.make_async_copy(buf.at[slot], buf.at[slot], sem.at[slot]).wait()
    acc_ref[...] = ALPHA * x1_ref[...] + _dot(onehot, buf[slot])

    r_io = lax.broadcasted_iota(I32, (1, COMB_ROWS), 1)
    for e in range(N_EXPERTS):
        b, _, fits = window(i, e)

        @pl.when(jnp.logical_not(fits))
        def _():
            st = pl.multiple_of(jnp.minimum((b // BF16_ROWS) * BF16_ROWS, cap_p - COMB_ROWS), BF16_ROWS)
            cp = pltpu.make_async_copy(ye_hbm.at[e, pl.ds(st, COMB_ROWS), :], big, bsem.at[0])
            cp.start()
            cp.wait()
            rk = rkt[:, e:e + 1]
            srel = jnp.where(rk >= 0, rk + (b - st), -1)
            acc_ref[...] += _dot(jnp.where(srel == r_io, 1.0, 0.0).astype(BF16), big[...])

    y_ref[...] = _ln(acc_ref[...], g2_ref[...], b2_ref[...])


def _combine(base_s, x1, rank_t, ye, g2, b2, cap_p):
    assert base_s.shape[0] > x1.shape[0] // CHUNK
    t = x1.shape[0]
    n_tiles = t // CHUNK
    const = lambda i, b: (0, 0)
    return pl.pallas_call(
        functools.partial(_combine_kernel, n_tiles=n_tiles, cap_p=cap_p),
        grid_spec=pltpu.PrefetchScalarGridSpec(
            num_scalar_prefetch=1, grid=(n_tiles,),
            in_specs=[
                pl.BlockSpec((CHUNK, D_MODEL), lambda i, b: (i, 0)),
                pl.BlockSpec((1, CHUNK, N_EXPERTS), lambda i, b: (i, 0, 0)),
                pl.BlockSpec(memory_space=pl.ANY),
                pl.BlockSpec((1, D_MODEL), const), pl.BlockSpec((1, D_MODEL), const),
            ],
            out_specs=pl.BlockSpec((CHUNK, D_MODEL), lambda i, b: (i, 0)),
            scratch_shapes=[
                pltpu.VMEM((2, N_EXPERTS * COMB_WIN, D_MODEL), BF16),
                pltpu.VMEM((COMB_ROWS, D_MODEL), BF16),
                pltpu.VMEM((CHUNK, D_MODEL), F32),
                pltpu.SemaphoreType.DMA((2,)),
                pltpu.SemaphoreType.DMA((1,)),
            ]),
        out_shape=jax.ShapeDtypeStruct((t, D_MODEL), F32),
        compiler_params=_cparams(("arbitrary",)),
        name="moe_combine_ln2",
    )(base_s, x1, rank_t, ye, g2, b2)


def _row(v):
    return v.reshape(1, -1).astype(F32)


def _attn_bias_variants(bias_main, bias_meta0):
    half = CHUNK // 2
    dead = jnp.full((AT_HEADS, CHUNK, CHUNK), NEG, F32)
    dead_pad = jnp.full((AT_HEADS, half, half - N_META), NEG, F32)

    def halves(meta, prev, cur, nxt):
        top = jnp.concatenate([prev[:, :half], cur[:, :half], nxt[:, :half, :half], meta[:, :half], dead_pad], axis=2)
        bot = jnp.concatenate([prev[:, half:, half:], meta[:, half:], dead_pad, cur[:, half:], nxt[:, half:]], axis=2)
        return jnp.stack([top, bot])

    meta, prev = bias_main[:, :, :N_META], bias_main[:, :, CHUNK:2 * CHUNK]
    cur, nxt = bias_main[:, :, 2 * CHUNK:3 * CHUNK], bias_main[:, :, 3 * CHUNK:]
    tabs = jnp.stack([halves(meta, prev, cur, nxt),
                      halves(bias_meta0[:, :, :N_META], dead, cur, nxt),
                      halves(meta, prev, cur, dead)]) * LOG2E
    tabs = tabs.reshape(3, 2, AT_KV_HEADS, AT_GROUP * half, 3 * CHUNK).transpose(0, 2, 1, 3, 4)
    return tabs.reshape(3, AT_KV_HEADS, 2 * AT_GROUP * half, 3 * CHUNK)


def _pick_tile(n, pref):
    tm = pref
    while n % tm:
        tm //= 2
    return tm


def _trunk(x, p, meta):
    batch, seq, _ = x.shape
    nb = seq // CHUNK
    t = batch * seq
    bm = batch * N_META
    x2 = x.reshape(t, D_MODEL)
    pm_m, kt_m, gc_m, gt_m = meta
    pm, kt, gc, gt = _ln_proj(x2, p["g0"], p["b0"], p["w_main"], p["w_ktg"], p["w_g"], p["bg"], p["bgt"],
                              _pick_tile(t, PM_TM))
    hb, hb_m = _mlstm(pm, kt, gc, gt, pm_m, kt_m, gc_m, gt_m, batch, nb, reverse=True)
    ha, ha_m = _mlstm(pm, kt, gc, gt, pm_m, kt_m, gc_m, gt_m, batch, nb, reverse=False,
                      merged=(hb, hb_m, p["ng"]))
    at = _attn_real(pm, pm_m, p["bias3"], p["sink_col"], batch, nb)
    at_m = _attn_meta(pm, pm_m, p["bias_mq"], p["sink_b"], batch, nb)

    mix = _mix(ha, at, pm, p["wa"], p["wb"], _pick_tile(t, 512), PM_GA // MIX_TN, PM_GB // MIX_TN)
    gates_m = jnp.tile(pm_m[:, PM_GA:PM_MQ], (batch, 1))
    mix_m = _mix(ha_m, at_m, gates_m, p["wa"], p["wb"], bm, 0, D_MODEL // MIX_TN)

    ln_args = (ALPHA * p["g0"], ALPHA * p["b0"], p["g1"], p["b1"], p["wo"], p["wrt"])
    x1, xp, aff = _out_ln_router(mix, x2, *ln_args, _pick_tile(t, OUT_TM), bm)
    xm = jnp.tile(p["meta_tokens"], (batch, 1))
    xp, aff_m = _out_ln_router(mix_m, xm, *ln_args, bm, bm, packed=xp)

    n_tok = t + bm
    nc = -(-n_tok // CHUNK)
    aff_m = jnp.pad(aff_m, ((0, 0), (0, nc * CHUNK - n_tok)), constant_values=-1.0)
    aff_m = aff_m.reshape(N_EXPERTS, -1, CHUNK).transpose(1, 0, 2)
    aff_all = jnp.concatenate([aff, aff_m], axis=0)
    cap = CAPACITY_FACTOR * n_tok // N_EXPERTS
    cap_p = -(-cap // CHUNK) * CHUNK
    rank, base = _route_select(aff_all, cap, batch, nb)
    base_s = base[:, :, 0]
    n_blk = cap_p // CHUNK + 2
    lists = _route_compact(base_s, rank, aff_all, n_blk)
    lists = lists.transpose(0, 2, 1, 3).reshape(N_EXPERTS, LIST_ROWS, n_blk * CHUNK)[:, :, :cap_p]
    idx = (lists[:, 0] * float(1 << TOK_DIGIT_BITS) + lists[:, 1]).astype(I32)
    gate = lists[:, 2] + lists[:, 3] + lists[:, 4]
    ns = -(-cap_p // FFN_MAX_ROWS)
    tm = cap_p // ns
    ye = _expert_ffn(idx.reshape(N_EXPERTS * ns, 1, tm), gate.reshape(N_EXPERTS, cap_p, 1), xp,
                     p["wgate"], p["wup"], p["wdown"], cap_p, tm, FFN_TF)
    rank_t = rank[:t // CHUNK].transpose(0, 2, 1)
    y = _combine(base_s, x1, rank_t, ye, p["g2"], p["b2"], cap_p)
    return y.reshape(batch, seq, D_MODEL)


def kernel(x_prompt, x_sample, meta_tokens, ln0_g, ln0_b, rel_bias, attn_sink, w_in, b_gate, ml_norm_g,
           w_branch_a, w_branch_b, w_out, ln1_g, ln1_b, w_router, w_gate, w_up, w_down, ln2_g, ln2_b):
    assert w_in.shape[0] == DEPTH
    w = w_in[0]
    sizes = (ML_HEADS * ML_DQK, ML_HEADS * ML_DQK, ML_HEADS * ML_DV, ML_HEADS * ML_DV, 4 * ML_HEADS,
             AT_HEADS * AT_DH, AT_KV_HEADS * AT_DH, AT_KV_HEADS * AT_DH, D_MODEL, D_MODEL)
    offs = np.concatenate([[0], np.cumsum(sizes)])
    mq, mk, mv, mo, mg, aq, ak, av, ga, gb = [w[:, offs[i]:offs[i + 1]] for i in range(10)]
    w_g = jnp.pad(mg, ((0, 0), (0, LANES - 4 * ML_HEADS))).astype(BF16)
    bg = jnp.pad(b_gate[0].astype(F32), (0, LANES - 4 * ML_HEADS))
    nb_max = max(x_prompt.shape[1], x_sample.shape[1]) // CHUNK
    bk_main, bk_meta0, bk_mq = _bucket_tables(nb_max)
    rb = rel_bias.astype(F32)
    p = {
        "g0": _row(ln0_g), "b0": _row(ln0_b), "g1": _row(ln1_g[0]), "b1": _row(ln1_b[0]),
        "g2": _row(ln2_g[0]), "b2": _row(ln2_b[0]), "ng": _row(ml_norm_g[0]),
        "w_main": jnp.concatenate([mv, mo, aq, ga, gb, mq, ak, av], axis=1).astype(BF16),
        "w_ktg": jnp.concatenate([mk.T.astype(BF16), w_g.T], axis=0), "w_g": w_g,
        "bg": bg.reshape(1, LANES), "bgt": bg.reshape(LANES, 1),
        "wa": w_branch_a[0].astype(BF16), "wb": w_branch_b[0].astype(BF16), "wo": w_out[0].astype(BF16),
        "wrt": w_router[0].T.astype(BF16),
        "wgate": w_gate[0], "wup": w_up[0], "wdown": w_down[0],
        "bias3": _attn_bias_variants(_bias_table(rb, bk_main), _bias_table(rb, bk_meta0)),
        "sink_col": jnp.broadcast_to(
            jnp.tile(jnp.repeat(attn_sink[0].astype(F32).reshape(AT_KV_HEADS, AT_GROUP), CHUNK // 2, axis=1),
                     (1, 2))[:, :, None] * LOG2E, (AT_KV_HEADS, AT_GROUP * CHUNK, LANES)),
        "bias_mq": _bias_table(rb, bk_mq),
        "sink_b": jnp.broadcast_to(attn_sink[0].astype(F32)[:, None], (AT_HEADS, LANES)),
        "meta_tokens": meta_tokens.astype(F32),
    }
    meta = _ln_proj(p["meta_tokens"], p["g0"], p["b0"], p["w_main"], p["w_ktg"], p["w_g"],
                    p["bg"], p["bgt"], N_META)
    return (_trunk(x_prompt, p, meta), _trunk(x_sample, p, meta))
```

```python
import functools
import math

import numpy as np
import jax
import jax.numpy as jnp
from jax import lax
from jax.experimental import pallas as pl
from jax.experimental.pallas import tpu as pltpu

F32 = jnp.float32
BF16 = jnp.bfloat16
I32 = jnp.int32

D_MODEL = 2048
N_META = 16
CHUNK = 128
ML_HEADS = 8
ML_DV = D_MODEL // ML_HEADS
ML_DQK = ML_DV // 2
AT_DH = 128
AT_HEADS = D_MODEL // AT_DH
AT_KV_HEADS = AT_HEADS // 4
AT_GROUP = AT_HEADS // AT_KV_HEADS
WINDOW = 128
REL_BUCKETS = 32
REL_MAX_DIST = 128
N_EXPERTS = 16
EXPERT_FF = D_MODEL
CAPACITY_FACTOR = 2
DEPTH = 1
ALPHA = (2.0 * DEPTH) ** 0.25
LN_EPS = 1e-5
M_INIT = -1e30
NEG = -1e30

LANES = 128
SUBLANES = 8
TOK_DIGIT_BITS = 8
BF16_ROWS = 16
VMEM_LIMIT = 56 * 1024 * 1024

PM_MV, PM_MO, PM_AQ, PM_GA, PM_GB, PM_MQ, PM_AK, PM_AV = 0, 2048, 4096, 6144, 8192, 10240, 11264, 11776
PM_WIDTH = 12288
PM_TN = 2048
PM_TM = 512
LN_ROWS = 256
MIX_TN = 1024
OUT_TM = 512
CT_W = ML_DV + LANES

NT_DIMS = (((1,), (1,)), ((), ()))


def _cparams(sem, vmem=VMEM_LIMIT):
    return pltpu.CompilerParams(dimension_semantics=sem, vmem_limit_bytes=vmem)


def _dot(a, b):
    return jnp.dot(a, b, preferred_element_type=F32)


def _dot_nt(a, b):
    return lax.dot_general(a, b, NT_DIMS, preferred_element_type=F32)


def _ln(x, g, b):
    mu = jnp.mean(x, axis=-1, keepdims=True)
    xc = x - mu
    var = jnp.mean(xc * xc, axis=-1, keepdims=True)
    return xc * lax.rsqrt(var + LN_EPS) * g + b


def _split3(x):
    hi = x.astype(BF16)
    r1 = x - hi.astype(F32)
    mid = r1.astype(BF16)
    lo = (r1 - mid.astype(F32)).astype(BF16)
    return hi, mid, lo


def _bias_kernel(rb_ref, bk_ref, o_ref):
    h = pl.program_id(0)
    bk = bk_ref[...]
    acc = jnp.full(bk.shape, NEG, F32)
    for b in range(REL_BUCKETS):
        acc = jnp.where(bk == b, rb_ref[b, h], acc)
    o_ref[0] = acc


def _bias_table(rel_bias, bucket):
    r, c = bucket.shape
    return pl.pallas_call(
        _bias_kernel,
        grid_spec=pltpu.PrefetchScalarGridSpec(
            num_scalar_prefetch=1, grid=(AT_HEADS,),
            in_specs=[pl.BlockSpec((r, c), lambda h, rb: (0, 0))],
            out_specs=pl.BlockSpec((1, r, c), lambda h, rb: (h, 0, 0))),
        out_shape=jax.ShapeDtypeStruct((AT_HEADS, r, c), F32),
        compiler_params=_cparams(("arbitrary",)),
        name="bias_table",
    )(rel_bias, bucket)


def _t5_bucket(rel):
    half = REL_BUCKETS // 2
    max_exact = half // 2
    n = jnp.abs(rel)
    nf = jnp.maximum(n, 1).astype(jnp.float32)
    large = max_exact + (jnp.log(nf / max_exact) / math.log(REL_MAX_DIST / max_exact)
                         * (half - max_exact)).astype(jnp.int32)
    large = jnp.minimum(large, half - 1)
    return jnp.where(rel > 0, half, 0) + jnp.where(n < max_exact, n, large)


def _t5_bucket_np(rel):
    half = REL_BUCKETS // 2
    max_exact = half // 2
    n = np.abs(rel)
    nf = np.maximum(n, 1).astype(np.float64)
    large = max_exact + (np.log(nf / max_exact) / math.log(REL_MAX_DIST / max_exact) * (half - max_exact)).astype(np.int64)
    large = np.minimum(large, half - 1)
    return np.where(rel > 0, half, 0) + np.where(n < max_exact, n, large)


def _bucket_tables(nb_max):
    i = np.arange(CHUNK)[:, None]
    m = np.arange(N_META)[None, :]
    ref_tab = _t5_bucket_np(m - (N_META + CHUNK + i))
    for j in range(1, nb_max):
        assert np.array_equal(_t5_bucket_np(m - (N_META + j * CHUNK + i)), ref_tab)
    qi = jnp.arange(CHUNK, dtype=I32)[:, None]
    c = jnp.arange(4 * CHUNK, dtype=I32)[None, :]
    rel_meta = c - (N_META + CHUNK + qi)
    rel_nb = (c - CHUNK) - CHUNK - qi
    vis_nb = (c >= CHUNK) & (jnp.abs(rel_nb) <= WINDOW)
    main = jnp.where(c < N_META, _t5_bucket(rel_meta), jnp.where(vis_nb, _t5_bucket(rel_nb), -1))
    c1 = jnp.arange(CHUNK, dtype=I32)[None, :]
    meta0 = jnp.where(c1 < N_META, _t5_bucket(c1 - (N_META + qi)), -1)
    mi = jnp.arange(N_META, dtype=I32)[:, None]
    c2 = jnp.arange(2 * CHUNK, dtype=I32)[None, :]
    rel_r = N_META + (c2 - CHUNK) - mi
    vis_r = (c2 >= CHUNK) & (jnp.abs(rel_r) <= WINDOW)
    mq = jnp.where(c2 < N_META, _t5_bucket(c2 - mi), jnp.where(vis_r, _t5_bucket(rel_r), -1))
    return main.astype(I32), meta0.astype(I32), mq.astype(I32)


def _proj_kernel(x_ref, g0_ref, b0_ref, w_ref, wktg_ref, wg_ref, bg_ref, bgt_ref,
                 pm_ref, kt_ref, gc_ref, gt_ref, xs_ref):
    n = pl.program_id(1)

    @pl.when(n == 0)
    def _():
        tm = x_ref.shape[0]
        rb = min(tm, LN_ROWS)

        def ln_block(r, carry):
            sl = pl.ds(pl.multiple_of(r * rb, BF16_ROWS), rb)
            xs_ref[sl, :] = _ln(x_ref[sl, :], g0_ref[...], b0_ref[...]).astype(BF16)
            return carry

        lax.fori_loop(0, tm // rb, ln_block, 0)
        xn = xs_ref[...]
        nk = ML_HEADS * ML_DQK
        ktg = _dot_nt(wktg_ref[...], xn)
        kt_ref[...] = ktg[:nk].astype(BF16)
        gt = ktg[nk:] + bgt_ref[...]
        gt_ref[...] = gt
        if tm % LANES == 0:
            gc_ref[...] = gt.T
        else:
            gc_ref[...] = _dot(xn, wg_ref[...]) + bg_ref[...]

    y = _dot(xs_ref[...], w_ref[...])
    lo1, hi1 = PM_MO // PM_TN, PM_AQ // PM_TN
    lo2, hi2 = PM_GA // PM_TN, PM_MQ // PM_TN
    is_sig = ((n >= lo1) & (n < hi1)) | ((n >= lo2) & (n < hi2))
    pm_ref[...] = jnp.where(is_sig, jax.nn.sigmoid(y), y).astype(BF16)


def _ln_proj(x2, g0, b0, w_main, w_ktg, w_g, bg, bgt, tm):
    t = x2.shape[0]
    grid = (t // tm, PM_WIDTH // PM_TN)
    const = lambda i, n: (0, 0)
    return pl.pallas_call(
        _proj_kernel,
        grid=grid,
        in_specs=[
            pl.BlockSpec((tm, D_MODEL), lambda i, n: (i, 0)),
            pl.BlockSpec((1, D_MODEL), const),
            pl.BlockSpec((1, D_MODEL), const),
            pl.BlockSpec((D_MODEL, PM_TN), lambda i, n: (0, n)),
            pl.BlockSpec((ML_HEADS * ML_DQK + LANES, D_MODEL), const, pipeline_mode=pl.Buffered(1)),
            pl.BlockSpec((D_MODEL, LANES), const),
            pl.BlockSpec((1, LANES), const),
            pl.BlockSpec((LANES, 1), const),
        ],
        out_specs=[
            pl.BlockSpec((tm, PM_TN), lambda i, n: (i, n)),
            pl.BlockSpec((ML_HEADS * ML_DQK, tm), lambda i, n: (0, i)),
            pl.BlockSpec((tm, LANES), lambda i, n: (i, 0)),
            pl.BlockSpec((LANES, tm), lambda i, n: (0, i)),
        ],
        out_shape=[
            jax.ShapeDtypeStruct((t, PM_WIDTH), BF16),
            jax.ShapeDtypeStruct((ML_HEADS * ML_DQK, t), BF16),
            jax.ShapeDtypeStruct((t, LANES), F32),
            jax.ShapeDtypeStruct((LANES, t), F32),
        ],
        scratch_shapes=[pltpu.VMEM((tm, D_MODEL), BF16)],
        compiler_params=_cparams(("arbitrary", "arbitrary")),
        name="ln_proj",
    )(x2, g0, b0, w_main, w_ktg, w_g, bg, bgt)


def _mlstm_chunk(L, reverse, q_ref, kt_ref, v_ref, gc_ref, gt_ref, ct_ref, m_ref, finish):
    H = ML_HEADS
    row = lax.broadcasted_iota(I32, (L, L), 0)
    col = lax.broadcasted_iota(I32, (L, L), 1)
    if reverse:
        tri = row <= col
        tri_t = row >= col
    else:
        tri = row >= col
        tri_t = row <= col
    tri_b = jnp.where(tri, 1.0, 0.0).astype(BF16)
    tri_tb = jnp.where(tri_t, 1.0, 0.0).astype(BF16)
    gc = gc_ref[...] * LOG2E
    gt = gt_ref[...] * LOG2E
    lfc = jax.nn.log_sigmoid(gc_ref[...]) * LOG2E
    lft = jax.nn.log_sigmoid(gt_ref[...]) * LOG2E
    bc_all = sum(_dot(tri_b, p) for p in _split3(lfc))
    br_all = sum(_dot(p, tri_tb) for p in _split3(lft))
    end = 0 if reverse else L - 1
    ci0, cf0 = (2 * H, 3 * H) if reverse else (0, H)
    scale = ML_DQK ** -0.5

    def stack(parts):
        return jnp.concatenate(parts, axis=0)

    def cols(x_all, c0):
        return stack([jnp.broadcast_to(x_all[:, c0 + h:c0 + h + 1], (L, L)) for h in range(H)])

    def rows_b(x_t, c0):
        return stack([jnp.broadcast_to(x_t[c0 + h:c0 + h + 1, :], (L, L)) for h in range(H)])

    def per_head(vals):
        return stack([jnp.broadcast_to(x, (L, L)) for x in vals])

    def rep(x):
        return jnp.broadcast_to(x, (H * L, L))

    def wide(x, n):
        if L == LANES and n % LANES == 0:
            return jnp.concatenate([x] * (n // LANES), axis=1)
        return jnp.broadcast_to(x[:, 0:1], (H * L, n))

    def wide_row(x, n):
        if L == LANES and n % LANES == 0:
            return jnp.concatenate([x] * (n // LANES), axis=1)
        return jnp.broadcast_to(x[:, 0:1], (1, n))

    def head(x, h):
        return x[h * L:(h + 1) * L]

    q = [q_ref[:, h * ML_DQK:(h + 1) * ML_DQK] for h in range(H)]
    kt = [kt_ref[h * ML_DQK:(h + 1) * ML_DQK, :] for h in range(H)]
    v = [v_ref[:, h * ML_DV:(h + 1) * ML_DV] for h in range(H)]
    ct = [ct_ref[h] for h in range(H)]
    m_prev_h = [m_ref[h:h + 1, 0:L] for h in range(H)]
    btot_h = [jnp.broadcast_to(bc_all[end:end + 1, cf0 + h:cf0 + h + 1], (1, L)) for h in range(H)]

    bc = cols(bc_all, cf0)
    igc = cols(gc, ci0)
    m_prev = per_head(m_prev_h)
    btot = per_head(btot_h)
    mask = stack([tri] * H)
    a_rows = gt[ci0:ci0 + H, :] - br_all[cf0:cf0 + H, :]
    d = jnp.where(mask, bc + rows_b(a_rows, 0), NEG)
    inter = bc + m_prev
    m_comb = jnp.maximum(inter, rep(jnp.max(d, axis=1, keepdims=True)))
    w = jnp.exp2(d - m_comb)
    w_inter = jnp.exp2(inter - m_comb) * scale
    s = stack([_dot(q[h], kt[h]) for h in range(H)]) * w * scale
    qc = stack([_dot(q[h], ct[h].astype(BF16)) for h in range(H)])
    sb = s.astype(BF16)
    num = stack([_dot(head(sb, h), v[h]) for h in range(H)]) + wide(w_inter, ML_DV) * qc[:, :ML_DV]
    den = rep(jnp.sum(s, axis=1, keepdims=True)) + w_inter * qc[:, ML_DV:ML_DV + L]
    finish(num * wide(1.0 / jnp.maximum(jnp.abs(den), jnp.exp2(-m_comb)), ML_DV))

    dec = btot - bc + igc
    inter_end_h = [btot_h[h] + m_prev_h[h] for h in range(H)]
    m_new_h = [jnp.maximum(inter_end_h[h], jnp.max(head(dec, h), axis=0, keepdims=True)) for h in range(H)]
    w_end = jnp.exp2(dec - per_head(m_new_h))
    vf = stack([x.astype(F32) for x in v])
    wv = jnp.concatenate([(wide(w_end, ML_DV) * vf).astype(BF16), wide(w_end, LANES).astype(BF16)], axis=1)
    for h in range(H):
        w_prev = jnp.exp2(inter_end_h[h] - m_new_h[h])
        ct_ref[h] = wide_row(w_prev, CT_W) * ct[h] + _dot(kt[h], head(wv, h))
        m_ref[h:h + 1, :] = wide_row(m_new_h[h], LANES)


def _mlstm_kernel(*refs, reverse, nb, merge):
    (q_ref, kt_ref, v_ref, gc_ref, gt_ref, qm_ref, ktm_ref, vm_ref, gcm_ref, gtm_ref) = refs[:10]
    if merge:
        o_ref, om_ref, hb_ref, hbm_ref, ng_ref, out_ref, outm_ref, ct_ref, m_ref = refs[10:]
    else:
        out_ref, outm_ref, ct_ref, m_ref = refs[10:]
    j = pl.program_id(1)

    @pl.when(j == 0)
    def _():
        ct_ref[...] = jnp.zeros(ct_ref.shape, F32)
        m_ref[...] = jnp.full(m_ref.shape, M_INIT * LOG2E, F32)

    def make_emit(dst_ref, other_ref, gate_ref):
        def emit(hv):
            L = hv.shape[0] // ML_HEADS
            sls = [slice(h * ML_DV, (h + 1) * ML_DV) for h in range(ML_HEADS)]
            if merge:
                hs = hv + jnp.concatenate([other_ref[:, sl] for sl in sls], axis=0)
                mu = jnp.broadcast_to(jnp.mean(hs, axis=-1, keepdims=True), hs.shape)
                hc = hs - mu
                var = jnp.broadcast_to(jnp.mean(hc * hc, axis=-1, keepdims=True), hs.shape)
                ng = jnp.concatenate([jnp.broadcast_to(ng_ref[:, sl], (L, ML_DV)) for sl in sls], axis=0)
                og = jnp.concatenate([gate_ref[:, sl].astype(F32) for sl in sls], axis=0)
                hv = hc * lax.rsqrt(var + LN_EPS) * (ng * og)
            for h, sl in enumerate(sls):
                dst_ref[:, sl] = hv[h * L:(h + 1) * L].astype(dst_ref.dtype)
        return emit

    is_meta = (j == nb) if reverse else (j == 0)

    @pl.when(is_meta)
    def _():
        emit = make_emit(outm_ref, hbm_ref if merge else None, om_ref if merge else None)
        _mlstm_chunk(N_META, reverse, qm_ref, ktm_ref, vm_ref, gcm_ref, gtm_ref, ct_ref, m_ref, emit)

    @pl.when(jnp.logical_not(is_meta))
    def _():
        emit = make_emit(out_ref, hb_ref if merge else None, o_ref if merge else None)
        _mlstm_chunk(CHUNK, reverse, q_ref, kt_ref, v_ref, gc_ref, gt_ref, ct_ref, m_ref, emit)


def _mlstm(pm, kt, gc, gt, pm_m, kt_m, gc_m, gt_m, batch, nb, *, reverse, merged=None):
    t = pm.shape[0]
    if reverse:
        rblk = lambda b, j: b * nb + jnp.maximum(nb - 1 - j, 0)
    else:
        rblk = lambda b, j: b * nb + jnp.maximum(j - 1, 0)
    const = lambda b, j: (0, 0)
    in_specs = [
        pl.BlockSpec((CHUNK, ML_HEADS * ML_DQK), lambda b, j: (rblk(b, j), PM_MQ // (ML_HEADS * ML_DQK))),
        pl.BlockSpec((ML_HEADS * ML_DQK, CHUNK), lambda b, j: (0, rblk(b, j))),
        pl.BlockSpec((CHUNK, D_MODEL), lambda b, j: (rblk(b, j), PM_MV // D_MODEL)),
        pl.BlockSpec((CHUNK, LANES), lambda b, j: (rblk(b, j), 0)),
        pl.BlockSpec((LANES, CHUNK), lambda b, j: (0, rblk(b, j))),
        pl.BlockSpec((N_META, ML_HEADS * ML_DQK), lambda b, j: (0, PM_MQ // (ML_HEADS * ML_DQK))),
        pl.BlockSpec((ML_HEADS * ML_DQK, N_META), const),
        pl.BlockSpec((N_META, D_MODEL), lambda b, j: (0, PM_MV // D_MODEL)),
        pl.BlockSpec((N_META, LANES), const),
        pl.BlockSpec((LANES, N_META), const),
    ]
    args = [pm, kt, pm, gc, gt, pm_m, kt_m, pm_m, gc_m, gt_m]
    merge = merged is not None
    if merge:
        hb, hb_m, ng = merged
        in_specs += [
            pl.BlockSpec((CHUNK, D_MODEL), lambda b, j: (rblk(b, j), PM_MO // D_MODEL)),
            pl.BlockSpec((N_META, D_MODEL), lambda b, j: (0, PM_MO // D_MODEL)),
            pl.BlockSpec((CHUNK, D_MODEL), lambda b, j: (rblk(b, j), 0)),
            pl.BlockSpec((N_META, D_MODEL), lambda b, j: (b, 0)),
            pl.BlockSpec((1, D_MODEL), const),
        ]
        args += [pm, pm_m, hb, hb_m, ng]
    odt = BF16 if merge else F32
    return pl.pallas_call(
        functools.partial(_mlstm_kernel, reverse=reverse, nb=nb, merge=merge),
        grid=(batch, nb + 1),
        in_specs=in_specs,
        out_specs=[
            pl.BlockSpec((CHUNK, D_MODEL), lambda b, j: (rblk(b, j), 0)),
            pl.BlockSpec((N_META, D_MODEL), lambda b, j: (b, 0)),
        ],
        out_shape=[
            jax.ShapeDtypeStruct((t, D_MODEL), odt),
            jax.ShapeDtypeStruct((batch * N_META, D_MODEL), odt),
        ],
        scratch_shapes=[pltpu.VMEM((ML_HEADS, ML_DQK, CT_W), F32), pltpu.VMEM((ML_HEADS, LANES), F32)],
        compiler_params=_cparams(("arbitrary", "arbitrary")),
        name="mlstm_bwd" if reverse else "mlstm_fwd",
    )(*args)


def _softmax_pv(sg, snk, vcat):
    m = jnp.maximum(jnp.max(sg, axis=1, keepdims=True), snk)
    p = jnp.exp(sg - m)
    l = jnp.sum(p, axis=1, keepdims=True) + jnp.exp(snk - m)
    return _dot(p.astype(BF16), vcat) * (1.0 / l)


LOG2E = math.log2(math.e)


def _attn_kernel(q_ref, kp_ref, kc_ref, kn_ref, vp_ref, vc_ref, vn_ref, km_ref, vm_ref,
                 bias_ref, sink_ref, o_ref):
    half = CHUNK // 2
    pad = jnp.zeros((half - N_META, AT_DH), BF16)
    ones = jnp.ones((3 * CHUNK, LANES), BF16)
    scale2 = AT_DH ** -0.5 * LOG2E
    rows_u = AT_GROUP * half
    for c in range(AT_KV_HEADS):
        ks = slice(c * AT_DH, (c + 1) * AT_DH)

        def keys(u, m_ref, p_ref, c_ref, n_ref):
            if u == 0:
                return jnp.concatenate([p_ref[:, ks], c_ref[:, ks], n_ref[:half, ks], m_ref[:, ks], pad], axis=0)
            return jnp.concatenate([p_ref[half:, ks], m_ref[:, ks], pad, c_ref[:, ks], n_ref[:, ks]], axis=0)

        def queries(u):
            return jnp.concatenate(
                [q_ref[u * half:(u + 1) * half, (c * AT_GROUP + g) * AT_DH:(c * AT_GROUP + g + 1) * AT_DH]
                 for g in range(AT_GROUP)], axis=0)

        t = jnp.concatenate([_dot_nt(queries(u), keys(u, km_ref, kp_ref, kc_ref, kn_ref)) for u in range(2)],
                            axis=0) * scale2 + bias_ref[0, c]
        snk = sink_ref[c]
        m = jnp.maximum(jnp.broadcast_to(jnp.max(t, axis=1, keepdims=True), snk.shape), snk)
        p = jnp.exp2(t - jnp.concatenate([m] * 3, axis=1)).astype(BF16)
        ov = jnp.concatenate(
            [_dot(p[u * rows_u:(u + 1) * rows_u],
                  jnp.concatenate([keys(u, vm_ref, vp_ref, vc_ref, vn_ref), ones], axis=1)) for u in range(2)], axis=0)
        l = ov[:, AT_DH:] + jnp.exp2(snk - m)
        o = ov[:, :AT_DH] * (1.0 / l)
        for u in range(2):
            for g in range(AT_GROUP):
                h = c * AT_GROUP + g
                r0 = u * rows_u + g * half
                o_ref[u * half:(u + 1) * half, h * AT_DH:(h + 1) * AT_DH] = o[r0:r0 + half].astype(o_ref.dtype)


def _attn_real(pm, pm_m, bias3, sink_col, batch, nb):
    assert nb >= 2
    t = pm.shape[0]
    kvw = AT_KV_HEADS * AT_DH
    cur = lambda b, j: b * nb + j
    prv = lambda b, j: b * nb + jnp.maximum(j - 1, 0)
    nxt = lambda b, j: b * nb + jnp.minimum(j + 1, nb - 1)
    variant = lambda b, j: (jnp.where(j == 0, 1, jnp.where(j == nb - 1, 2, 0)), 0, 0, 0)
    kcol, vcol = PM_AK // kvw, PM_AV // kvw
    return pl.pallas_call(
        _attn_kernel,
        grid=(batch, nb),
        in_specs=[
            pl.BlockSpec((CHUNK, D_MODEL), lambda b, j: (cur(b, j), PM_AQ // D_MODEL)),
            pl.BlockSpec((CHUNK, kvw), lambda b, j: (prv(b, j), kcol)),
            pl.BlockSpec((CHUNK, kvw), lambda b, j: (cur(b, j), kcol)),
            pl.BlockSpec((CHUNK, kvw), lambda b, j: (nxt(b, j), kcol)),
            pl.BlockSpec((CHUNK, kvw), lambda b, j: (prv(b, j), vcol)),
            pl.BlockSpec((CHUNK, kvw), lambda b, j: (cur(b, j), vcol)),
            pl.BlockSpec((CHUNK, kvw), lambda b, j: (nxt(b, j), vcol)),
            pl.BlockSpec((N_META, kvw), lambda b, j: (0, kcol)),
            pl.BlockSpec((N_META, kvw), lambda b, j: (0, vcol)),
            pl.BlockSpec((1, AT_KV_HEADS, AT_GROUP * CHUNK, 3 * CHUNK), variant),
            pl.BlockSpec((AT_KV_HEADS, AT_GROUP * CHUNK, LANES), lambda b, j: (0, 0, 0)),
        ],
        out_specs=pl.BlockSpec((CHUNK, D_MODEL), lambda b, j: (cur(b, j), 0)),
        out_shape=jax.ShapeDtypeStruct((t, D_MODEL), BF16),
        compiler_params=_cparams(("arbitrary", "arbitrary")),
        name="attn_real",
    )(pm, pm, pm, pm, pm, pm, pm, pm_m, pm_m, bias3, sink_col)


def _attn_meta_kernel(q_ref, kr_ref, vr_ref, km_ref, vm_ref, bias_ref, sink_ref, o_ref):
    pad = jnp.zeros((CHUNK - N_META, AT_DH), BF16)
    scale = AT_DH ** -0.5
    for c in range(AT_KV_HEADS):
        ks = slice(c * AT_DH, (c + 1) * AT_DH)
        kcat = jnp.concatenate([km_ref[:, ks], pad, kr_ref[:, ks]], axis=0)
        vcat = jnp.concatenate([vm_ref[:, ks], pad, vr_ref[:, ks]], axis=0)
        qc = jnp.concatenate(
            [q_ref[:, (c * AT_GROUP + g) * AT_DH:(c * AT_GROUP + g + 1) * AT_DH] for g in range(AT_GROUP)], axis=0)
        s = _dot_nt(qc, kcat) * scale
        for g in range(AT_GROUP):
            h = c * AT_GROUP + g
            sg = s[g * N_META:(g + 1) * N_META] + bias_ref[h]
            o = _softmax_pv(sg, sink_ref[h:h + 1, 0:1], vcat)
            o_ref[:, h * AT_DH:(h + 1) * AT_DH] = o.astype(o_ref.dtype)


def _attn_meta(pm, pm_m, bias_mq, sink_b, batch, nb):
    kvw = AT_KV_HEADS * AT_DH
    kcol, vcol = PM_AK // kvw, PM_AV // kvw
    return pl.pallas_call(
        _attn_meta_kernel,
        grid=(batch,),
        in_specs=[
            pl.BlockSpec((N_META, D_MODEL), lambda b: (0, PM_AQ // D_MODEL)),
            pl.BlockSpec((CHUNK, kvw), lambda b: (b * nb, kcol)),
            pl.BlockSpec((CHUNK, kvw), lambda b: (b * nb, vcol)),
            pl.BlockSpec((N_META, kvw), lambda b: (0, kcol)),
            pl.BlockSpec((N_META, kvw), lambda b: (0, vcol)),
            pl.BlockSpec((AT_HEADS, N_META, 2 * CHUNK), lambda b: (0, 0, 0)),
            pl.BlockSpec((AT_HEADS, LANES), lambda b: (0, 0)),
        ],
        out_specs=pl.BlockSpec((N_META, D_MODEL), lambda b: (b, 0)),
        out_shape=jax.ShapeDtypeStruct((batch * N_META, D_MODEL), BF16),
        compiler_params=_cparams(("arbitrary",)),
        name="attn_meta",
    )(pm_m, pm, pm, pm_m, pm_m, bias_mq, sink_b)


def _mix_kernel(ha_ref, hb_ref, ga_ref, gb_ref, wa_ref, wb_ref, o_ref):
    a = _dot(ha_ref[...], wa_ref[...])
    b = _dot(hb_ref[...], wb_ref[...])
    o_ref[...] = (ga_ref[...].astype(F32) * a + gb_ref[...].astype(F32) * b).astype(o_ref.dtype)


def _mix(ha, hb, gates_src, wa, wb, tm, ga_col, gb_col):
    t = ha.shape[0]
    nt = D_MODEL // MIX_TN
    return pl.pallas_call(
        _mix_kernel,
        grid=(t // tm, nt),
        in_specs=[
            pl.BlockSpec((tm, D_MODEL), lambda i, n: (i, 0)),
            pl.BlockSpec((tm, D_MODEL), lambda i, n: (i, 0)),
            pl.BlockSpec((tm, MIX_TN), lambda i, n: (i, ga_col + n)),
            pl.BlockSpec((tm, MIX_TN), lambda i, n: (i, gb_col + n)),
            pl.BlockSpec((D_MODEL, MIX_TN), lambda i, n: (0, n)),
            pl.BlockSpec((D_MODEL, MIX_TN), lambda i, n: (0, n)),
        ],
        out_specs=pl.BlockSpec((tm, MIX_TN), lambda i, n: (i, n)),
        out_shape=jax.ShapeDtypeStruct((t, D_MODEL), BF16),
        compiler_params=_cparams(("arbitrary", "arbitrary")),
        name="branch_mix",
    )(ha, hb, gates_src, gates_src, wa, wb)


HALF_D = D_MODEL // 2
BF16_BITS = 16
HIGH_HALF = -(1 << BF16_BITS)


def _pack_bf16_pairs(x):
    lo = lax.bitcast_convert_type(x[:, :HALF_D].astype(BF16).astype(F32), I32)
    hi = lax.bitcast_convert_type(x[:, HALF_D:].astype(BF16).astype(F32), I32)
    return lax.shift_right_logical(lo, BF16_BITS) | (hi & jnp.int32(HIGH_HALF))


def _unpack_bf16_pairs(u):
    lo = lax.bitcast_convert_type(u << BF16_BITS, F32)
    hi = lax.bitcast_convert_type(u & jnp.int32(HIGH_HALF), F32)
    return lo.astype(BF16), hi.astype(BF16)


PACK_SUB = HALF_D // LANES


def _out_kernel(*refs, real):
    mix_ref, x_ref, g0_ref, b0_ref, g1_ref, b1_ref, wo_ref, wrt_ref = refs[:8]
    if real:
        x1_ref, x1p_ref, aff_ref = refs[8:]
    else:
        x1p_ref, aff_ref = refs[9:]

    def body():
        y = _dot(mix_ref[...], wo_ref[...])
        h0a = _ln(x_ref[...], g0_ref[...], b0_ref[...])
        x1 = _ln(h0a + y, g1_ref[...], b1_ref[...])
        x1p_ref[...] = pltpu.einshape("t(cl)->tcl", _pack_bf16_pairs(x1), c=PACK_SUB)
        lt = _dot_nt(wrt_ref[...], x1.astype(BF16))
        e = jnp.exp(lt - jnp.max(lt, axis=0, keepdims=True))
        aff = e / jnp.sum(e, axis=0, keepdims=True)
        if real:
            x1_ref[...] = x1
            for c in range(aff.shape[1] // CHUNK):
                aff_ref[c] = aff[:, c * CHUNK:(c + 1) * CHUNK]
        else:
            aff_ref[...] = aff

    if not real:
        body()
        return
    last = pl.program_id(0) == pl.num_programs(0) - 1
    pl.when(jnp.logical_not(last))(body)

    @pl.when(last)
    def _():
        x1p_ref[...] = jnp.zeros(x1p_ref.shape, I32)


def _out_ln_router(mix, x2, g0, b0, g1, b1, wo, wrt, tm, n_meta_rows, packed=None):
    t = mix.shape[0]
    real = packed is None
    n = t // tm
    const = lambda i: (0, 0)
    row = lambda i: (jnp.minimum(i, n - 1), 0)
    in_specs = [
        pl.BlockSpec((tm, D_MODEL), row),
        pl.BlockSpec((tm, D_MODEL), row),
        pl.BlockSpec((1, D_MODEL), const), pl.BlockSpec((1, D_MODEL), const),
        pl.BlockSpec((1, D_MODEL), const), pl.BlockSpec((1, D_MODEL), const),
        pl.BlockSpec((D_MODEL, D_MODEL), const, pipeline_mode=pl.Buffered(1)),
        pl.BlockSpec((N_EXPERTS, D_MODEL), const),
    ]
    args = [mix, x2, g0, b0, g1, b1, wo, wrt]
    if real:
        assert n_meta_rows <= tm
        out_specs = [
            pl.BlockSpec((tm, D_MODEL), row),
            pl.BlockSpec((tm, PACK_SUB, LANES), lambda i: (i, 0, 0)),
            pl.BlockSpec((tm // CHUNK, N_EXPERTS, CHUNK), lambda i: (jnp.minimum(i, n - 1), 0, 0)),
        ]
        out_shape = [
            jax.ShapeDtypeStruct((t, D_MODEL), F32),
            jax.ShapeDtypeStruct((t + n_meta_rows, PACK_SUB, LANES), I32),
            jax.ShapeDtypeStruct((t // CHUNK, N_EXPERTS, CHUNK), F32),
        ]
        aliases = {}
    else:
        n_real = packed.shape[0] - t
        assert tm == t and n_real % t == 0
        in_specs.append(pl.BlockSpec(memory_space=pl.ANY))
        args.append(packed)
        out_specs = [
            pl.BlockSpec((t, PACK_SUB, LANES), lambda i: (n_real // t, 0, 0)),
            pl.BlockSpec((N_EXPERTS, t), const),
        ]
        out_shape = [jax.ShapeDtypeStruct(packed.shape, I32), jax.ShapeDtypeStruct((N_EXPERTS, t), F32)]
        aliases = {8: 0}
    return pl.pallas_call(
        functools.partial(_out_kernel, real=real),
        grid=(n + 1 if real else n,),
        in_specs=in_specs,
        out_specs=out_specs,
        out_shape=out_shape,
        input_output_aliases=aliases,
        compiler_params=_cparams(("arbitrary",)),
        name="out_ln1_router" if real else "out_ln1_router_meta",
    )(*args)


def _route_select_kernel(aff_ref, rank_ref, base_ref, p_s, b_s, *, cap, nc, batch, nb):
    aff = aff_ref[...]
    bits = lax.bitcast_convert_type(aff, I32)

    def count(maskf):
        return jnp.sum(jnp.sum(maskf, axis=0), axis=1, keepdims=True)

    def search(it, lo):
        cand = lo | (jnp.int32(1) << (30 - it))
        cnt = count(jnp.where(bits >= cand[None], 1.0, 0.0))
        return jnp.where(cnt >= cap, cand, lo)

    thr = lax.fori_loop(0, 31, search, jnp.zeros((N_EXPERTS, 1), I32))
    gt = bits > thr[None]
    eq = bits == thr[None]
    need = cap - count(jnp.where(gt, 1.0, 0.0))
    r_i = lax.broadcasted_iota(I32, (CHUNK, CHUNK), 0)
    c_i = lax.broadcasted_iota(I32, (CHUNK, CHUNK), 1)
    upper = jnp.where(r_i <= c_i, 1.0, 0.0).astype(BF16)

    def prefix(maskb):
        p = _dot(jnp.where(maskb, 1.0, 0.0).astype(BF16).reshape(nc * N_EXPERTS, CHUNK), upper)
        p_s[...] = p.reshape(nc, N_EXPERTS, CHUNK)

        def step(c, carry):
            b_s[c] = jnp.broadcast_to(carry, (N_EXPERTS, CHUNK))
            return carry + p_s[c][:, CHUNK - 1:CHUNK]

        lax.fori_loop(0, nc, step, jnp.zeros((N_EXPERTS, 1), F32))

    prefix(eq)
    pe, be = p_s[...], b_s[...]
    n_real = batch * nb
    r_tot = be[n_real][:, 0:1]
    lane_seq = lax.broadcasted_iota(I32, (1, CHUNK), 1) // N_META
    pieces, meta_corr = [], {}
    for b in range(batch):
        cm = n_real + (b * N_META) // CHUNK
        last = (b * N_META) % CHUNK + N_META - 1
        meta_le = be[cm][:, 0:1] + pe[cm][:, last:last + 1] - r_tot
        real_lt = be[b * nb][:, 0:1]
        pieces.append(be[b * nb:(b + 1) * nb] + pe[b * nb:(b + 1) * nb] - 1.0 + meta_le[None])
        corr = jnp.where(lane_seq == (b * N_META % CHUNK) // N_META, real_lt - r_tot, 0.0)
        meta_corr[cm] = meta_corr[cm] + corr if cm in meta_corr else corr
    for cm in range(n_real, nc):
        pieces.append((be[cm] + pe[cm] - 1.0 + meta_corr[cm])[None])
    eq_rank = jnp.concatenate(pieces, axis=0)
    sel = gt | (eq & (eq_rank < need[None]))
    prefix(sel)
    rank_ref[...] = jnp.where(sel, p_s[...] - 1.0, -1.0).astype(I32)
    base_ref[...] = b_s[...].astype(I32)


def _route_select(aff, cap, batch, nb):
    nc = aff.shape[0]
    shp = (nc, N_EXPERTS, CHUNK)
    full = pl.BlockSpec(shp, lambda i: (0, 0, 0))
    return pl.pallas_call(
        functools.partial(_route_select_kernel, cap=cap, nc=nc, batch=batch, nb=nb),
        grid=(1,),
        in_specs=[full],
        out_specs=[full, full],
        out_shape=[jax.ShapeDtypeStruct(shp, I32), jax.ShapeDtypeStruct(shp, I32)],
        scratch_shapes=[pltpu.VMEM(shp, F32), pltpu.VMEM(shp, F32)],
        compiler_params=_cparams(("arbitrary",)),
        name="route_select",
    )(aff)


FFN_MAX_ROWS = 2304
FFN_TF = 256
FFN_ROW_SPLIT = 2
RELAYOUT_BLOCKS = 8
LIST_ROWS = 16


def _route_compact_kernel(base_ref, rank_ref, aff_ref, out_ref, *, group):
    step = pl.program_id(0)

    @pl.when(step == 0)
    def _():
        out_ref[...] = jnp.zeros(out_ref.shape, F32)

    lane = lax.broadcasted_iota(I32, (1, CHUNK), 1)
    r_io = lax.broadcasted_iota(I32, (CHUNK, CHUNK), 0)
    zeros = jnp.zeros((LIST_ROWS - 5, CHUNK), F32)
    for k in range(group):
        c = step * group + k
        tok = c * CHUNK + lane
        t_hi = (tok >> TOK_DIGIT_BITS).astype(F32)
        t_lo = (tok & ((1 << TOK_DIGIT_BITS) - 1)).astype(F32)
        for e in range(N_EXPERTS):
            rk = rank_ref[k, e:e + 1, :]
            onehot = jnp.where(r_io == rk, 1.0, 0.0).astype(BF16)
            a_hi, a_mid, a_lo = _split3(aff_ref[k, e:e + 1, :])
            vals = jnp.concatenate([t_hi, t_lo, a_hi.astype(F32), a_mid.astype(F32), a_lo.astype(F32), zeros],
                                   axis=0).astype(BF16)
            comp = _dot_nt(vals, onehot)
            base = base_ref[c, e]
            blk = base // CHUNK
            off = base - blk * CHUNK
            rolled = pltpu.roll(comp, off, axis=1)
            keep_lo = lane >= off
            out_ref[e, blk] = jnp.where(keep_lo, rolled, out_ref[e, blk])
            out_ref[e, blk + 1] = jnp.where(keep_lo, out_ref[e, blk + 1], rolled)


def _route_compact(base_s, rank, aff, n_blk):
    nc = rank.shape[0]
    group = next(g for g in (4, 3, 2, 1) if nc % g == 0)
    blk = pl.BlockSpec((group, N_EXPERTS, CHUNK), lambda c, b: (c, 0, 0))
    return pl.pallas_call(
        functools.partial(_route_compact_kernel, group=group),
        grid_spec=pltpu.PrefetchScalarGridSpec(
            num_scalar_prefetch=1, grid=(nc // group,),
            in_specs=[blk, blk],
            out_specs=pl.BlockSpec((N_EXPERTS, n_blk, LIST_ROWS, CHUNK), lambda c, b: (0, 0, 0, 0))),
        out_shape=jax.ShapeDtypeStruct((N_EXPERTS, n_blk, LIST_ROWS, CHUNK), F32),
        compiler_params=_cparams(("arbitrary",)),
        name="route_compact",
    )(base_s, rank, aff)


def _ffn_kernel(idx_ref, idxn_ref, gate_ref, xp_hbm, wg_ref, wu_ref, wd_ref, ye_ref,
                rows_ref, xp_ref, hid_ref, sem, *, tm, rs, nf, tf):
    e, s, j = pl.program_id(0), pl.program_id(1), pl.program_id(2)
    tile = e * pl.num_programs(1) + s
    n_sub = tm // rs
    per_slot = tm // (nf * n_sub)

    def relayout():
        rb = tm // RELAYOUT_BLOCKS

        def body(r, carry):
            sl = pl.ds(pl.multiple_of(r * rb, SUBLANES), rb)
            y = pltpu.einshape("tcl->ctl", rows_ref[sl])
            for c in range(PACK_SUB):
                xp_ref[sl, c * LANES:(c + 1) * LANES] = y[c]
            return carry

        lax.fori_loop(0, RELAYOUT_BLOCKS, body, 0)

    def row_copy(iref, r):
        return pltpu.make_async_copy(xp_hbm.at[pl.ds(iref[0, 0, r], 1)], rows_ref.at[pl.ds(r, 1)], sem.at[0])

    def wait_rows():
        pltpu.make_async_copy(xp_hbm.at[pl.ds(0, tm)], rows_ref, sem.at[0]).wait()

    def prefetch_next(r):
        first = (j * n_sub + r) * per_slot
        for k in range(per_slot):
            row_copy(idxn_ref, first + k).start()

    def rows(r):
        return slice(r * rs, (r + 1) * rs)

    @pl.when((tile == 0) & (j == 0))
    def _():
        def body(r, carry):
            row_copy(idx_ref, r).start()
            return carry

        lax.fori_loop(0, tm, body, 0)
        wait_rows()
        relayout()

    @pl.when(j < nf)
    def _():
        wg = wg_ref[0].astype(BF16)
        wu = wu_ref[0].astype(BF16)

        def body(r, carry):
            prefetch_next(r)
            lo, hi = _unpack_bf16_pairs(xp_ref[rows(r), :])
            x = jnp.concatenate([lo, hi], axis=1)
            g = _dot(x, wg)
            u = _dot(x, wu)
            hid_ref[j, rows(r), :] = (g * jax.nn.sigmoid(g) * u).astype(BF16)
            return carry

        for r in range(n_sub):
            body(r, 0)

    @pl.when(j == nf)
    def _():
        wait_rows()
        relayout()

    @pl.when(j >= nf)
    def _():
        wd = wd_ref[0].astype(BF16)

        def body(r, carry):
            acc = _dot(hid_ref[0, rows(r), :], wd[0:tf])
            for f in range(1, nf):
                acc = acc + _dot(hid_ref[f, rows(r), :], wd[f * tf:(f + 1) * tf])
            ye_ref[0, rows(r), :] = (acc * gate_ref[0, rows(r), :]).astype(ye_ref.dtype)
            return carry

        for r in range(n_sub):
            body(r, 0)


def _expert_ffn(idx3, gate3, xp, wg, wu, wd, cap_p, tm, tf):
    ns = cap_p // tm
    nf = EXPERT_FF // tf
    n_tiles = N_EXPERTS * ns
    rs = tm // FFN_ROW_SPLIT
    assert rs % BF16_ROWS == 0 and D_MODEL // tf == nf and tm % (nf * (tm // rs)) == 0
    assert tm % (SUBLANES * RELAYOUT_BLOCKS) == 0
    return pl.pallas_call(
        functools.partial(_ffn_kernel, tm=tm, rs=rs, nf=nf, tf=tf),
        grid=(N_EXPERTS, ns, 2 * nf),
        in_specs=[
            pl.BlockSpec((1, 1, tm), lambda e, s, j: (e * ns + s, 0, 0), memory_space=pltpu.SMEM),
            pl.BlockSpec((1, 1, tm), lambda e, s, j: (jnp.minimum(e * ns + s + 1, n_tiles - 1), 0, 0),
                         memory_space=pltpu.SMEM),
            pl.BlockSpec((1, tm, 1), lambda e, s, j: (e, s, 0)),
            pl.BlockSpec(memory_space=pl.ANY),
            pl.BlockSpec((1, D_MODEL, tf), lambda e, s, j: (e, 0, jnp.minimum(j, nf - 1))),
            pl.BlockSpec((1, D_MODEL, tf), lambda e, s, j: (e, 0, jnp.minimum(j, nf - 1))),
            pl.BlockSpec((1, EXPERT_FF, tf), lambda e, s, j: (e, 0, jnp.maximum(j - nf, 0))),
        ],
        out_specs=pl.BlockSpec((1, tm, tf), lambda e, s, j: (e, s, jnp.maximum(j - nf, 0))),
        out_shape=jax.ShapeDtypeStruct((N_EXPERTS, cap_p, D_MODEL), BF16),
        scratch_shapes=[
            pltpu.VMEM((tm, PACK_SUB, LANES), I32),
            pltpu.VMEM((tm, HALF_D), I32),
            pltpu.VMEM((nf, tm, tf), BF16),
            pltpu.SemaphoreType.DMA((1,)),
        ],
        compiler_params=_cparams(("arbitrary", "arbitrary", "arbitrary")),
        name="expert_ffn",
    )(idx3, idx3, gate3, xp, wg, wu, wd)


COMB_ROWS = CHUNK + BF16_ROWS


COMB_WIN = 48


def _combine_kernel(base_ref, x1_ref, rkt_ref, ye_hbm, g2_ref, b2_ref, y_ref, buf, big, acc_ref, sem, bsem,
                    *, n_tiles, cap_p):
    i = pl.program_id(0)

    def window(c, e):
        b = base_ref[c, e]
        st = jnp.minimum((b // BF16_ROWS) * BF16_ROWS, cap_p - COMB_WIN)
        fits = base_ref[c + 1, e] - st <= COMB_WIN
        return b, st, fits

    def copy(c, e, slot):
        st = pl.multiple_of(window(c, e)[1], BF16_ROWS)
        return pltpu.make_async_copy(ye_hbm.at[e, pl.ds(st, COMB_WIN), :],
                                     buf.at[slot, pl.ds(e * COMB_WIN, COMB_WIN), :], sem.at[slot])

    @pl.when(i == 0)
    def _():
        for e in range(N_EXPERTS):
            copy(0, e, 0).start()

    @pl.when(i + 1 < n_tiles)
    def _():
        for e in range(N_EXPERTS):
            copy(i + 1, e, (i + 1) % 2).start()

    slot = i % 2
    rkt = rkt_ref[0]
    lane = lax.broadcasted_iota(I32, (1, LANES), 1)
    pos = []
    for e in range(N_EXPERTS):
        b, st, fits = window(i, e)
        rk = rkt[:, e:e + 1]
        pos.append(jnp.where((rk >= 0) & fits, rk + (b - st + e * COMB_WIN), -1))
    pieces = []
    for p in range(N_EXPERTS * COMB_WIN // LANES):
        lo, hi = p * LANES, (p + 1) * LANES
        target = None
        for e in range(N_EXPERTS):
            e_lo, e_hi = max(e * COMB_WIN, lo), min((e + 1) * COMB_WIN, hi)
            if e_lo >= e_hi:
                continue
            cand = pos[e] - lo
            if target is None:
                target = cand
            else:
                target = jnp.where((lane >= e_lo - lo) & (lane < e_hi - lo), cand, target)
        pieces.append(jnp.where(target == lane, 1.0, 0.0).astype(BF16))
    onehot = jnp.concatenate(pieces, axis=1)
    pltpu.make_async_copy(buf.at[slot], buf.at[slot], sem.at[slot]).wait()
    acc_ref[...] = ALPHA * x1_ref[...] + _dot(onehot, buf[slot])

    r_io = lax.broadcasted_iota(I32, (1, COMB_ROWS), 1)
    for e in range(N_EXPERTS):
        b, _, fits = window(i, e)

        @pl.when(jnp.logical_not(fits))
        def _():
            st = pl.multiple_of(jnp.minimum((b // BF16_ROWS) * BF16_ROWS, cap_p - COMB_ROWS), BF16_ROWS)
            cp = pltpu.make_async_copy(ye_hbm.at[e, pl.ds(st, COMB_ROWS), :], big, bsem.at[0])
            cp.start()
            cp.wait()
            rk = rkt[:, e:e + 1]
            srel = jnp.where(rk >= 0, rk + (b - st), -1)
            acc_ref[...] += _dot(jnp.where(srel == r_io, 1.0, 0.0).astype(BF16), big[...])

    y_ref[...] = _ln(acc_ref[...], g2_ref[...], b2_ref[...])


def _combine(base_s, x1, rank_t, ye, g2, b2, cap_p):
    assert base_s.shape[0] > x1.shape[0] // CHUNK
    t = x1.shape[0]
    n_tiles = t // CHUNK
    const = lambda i, b: (0, 0)
    return pl.pallas_call(
        functools.partial(_combine_kernel, n_tiles=n_tiles, cap_p=cap_p),
        grid_spec=pltpu.PrefetchScalarGridSpec(
            num_scalar_prefetch=1, grid=(n_tiles,),
            in_specs=[
                pl.BlockSpec((CHUNK, D_MODEL), lambda i, b: (i, 0)),
                pl.BlockSpec((1, CHUNK, N_EXPERTS), lambda i, b: (i, 0, 0)),
                pl.BlockSpec(memory_space=pl.ANY),
                pl.BlockSpec((1, D_MODEL), const), pl.BlockSpec((1, D_MODEL), const),
            ],
            out_specs=pl.BlockSpec((CHUNK, D_MODEL), lambda i, b: (i, 0)),
            scratch_shapes=[
                pltpu.VMEM((2, N_EXPERTS * COMB_WIN, D_MODEL), BF16),
                pltpu.VMEM((COMB_ROWS, D_MODEL), BF16),
                pltpu.VMEM((CHUNK, D_MODEL), F32),
                pltpu.SemaphoreType.DMA((2,)),
                pltpu.SemaphoreType.DMA((1,)),
            ]),
        out_shape=jax.ShapeDtypeStruct((t, D_MODEL), F32),
        compiler_params=_cparams(("arbitrary",)),
        name="moe_combine_ln2",
    )(base_s, x1, rank_t, ye, g2, b2)


def _row(v):
    return v.reshape(1, -1).astype(F32)


def _attn_bias_variants(bias_main, bias_meta0):
    half = CHUNK // 2
    dead = jnp.full((AT_HEADS, CHUNK, CHUNK), NEG, F32)
    dead_pad = jnp.full((AT_HEADS, half, half - N_META), NEG, F32)

    def halves(meta, prev, cur, nxt):
        top = jnp.concatenate([prev[:, :half], cur[:, :half], nxt[:, :half, :half], meta[:, :half], dead_pad], axis=2)
        bot = jnp.concatenate([prev[:, half:, half:], meta[:, half:], dead_pad, cur[:, half:], nxt[:, half:]], axis=2)
        return jnp.stack([top, bot])

    meta, prev = bias_main[:, :, :N_META], bias_main[:, :, CHUNK:2 * CHUNK]
    cur, nxt = bias_main[:, :, 2 * CHUNK:3 * CHUNK], bias_main[:, :, 3 * CHUNK:]
    tabs = jnp.stack([halves(meta, prev, cur, nxt),
                      halves(bias_meta0[:, :, :N_META], dead, cur, nxt),
                      halves(meta, prev, cur, dead)]) * LOG2E
    tabs = tabs.reshape(3, 2, AT_KV_HEADS, AT_GROUP * half, 3 * CHUNK).transpose(0, 2, 1, 3, 4)
    return tabs.reshape(3, AT_KV_HEADS, 2 * AT_GROUP * half, 3 * CHUNK)


def _pick_tile(n, pref):
    tm = pref
    while n % tm:
        tm //= 2
    return tm


def _trunk(x, p, meta):
    batch, seq, _ = x.shape
    nb = seq // CHUNK
    t = batch * seq
    bm = batch * N_META
    x2 = x.reshape(t, D_MODEL)
    pm_m, kt_m, gc_m, gt_m = meta
    pm, kt, gc, gt = _ln_proj(x2, p["g0"], p["b0"], p["w_main"], p["w_ktg"], p["w_g"], p["bg"], p["bgt"],
                              _pick_tile(t, PM_TM))
    hb, hb_m = _mlstm(pm, kt, gc, gt, pm_m, kt_m, gc_m, gt_m, batch, nb, reverse=True)
    ha, ha_m = _mlstm(pm, kt, gc, gt, pm_m, kt_m, gc_m, gt_m, batch, nb, reverse=False,
                      merged=(hb, hb_m, p["ng"]))
    at = _attn_real(pm, pm_m, p["bias3"], p["sink_col"], batch, nb)
    at_m = _attn_meta(pm, pm_m, p["bias_mq"], p["sink_b"], batch, nb)

    mix = _mix(ha, at, pm, p["wa"], p["wb"], _pick_tile(t, 512), PM_GA // MIX_TN, PM_GB // MIX_TN)
    gates_m = jnp.tile(pm_m[:, PM_GA:PM_MQ], (batch, 1))
    mix_m = _mix(ha_m, at_m, gates_m, p["wa"], p["wb"], bm, 0, D_MODEL // MIX_TN)

    ln_args = (ALPHA * p["g0"], ALPHA * p["b0"], p["g1"], p["b1"], p["wo"], p["wrt"])
    x1, xp, aff = _out_ln_router(mix, x2, *ln_args, _pick_tile(t, OUT_TM), bm)
    xm = jnp.tile(p["meta_tokens"], (batch, 1))
    xp, aff_m = _out_ln_router(mix_m, xm, *ln_args, bm, bm, packed=xp)

    n_tok = t + bm
    nc = -(-n_tok // CHUNK)
    aff_m = jnp.pad(aff_m, ((0, 0), (0, nc * CHUNK - n_tok)), constant_values=-1.0)
    aff_m = aff_m.reshape(N_EXPERTS, -1, CHUNK).transpose(1, 0, 2)
    aff_all = jnp.concatenate([aff, aff_m], axis=0)
    cap = CAPACITY_FACTOR * n_tok // N_EXPERTS
    cap_p = -(-cap // CHUNK) * CHUNK
    rank, base = _route_select(aff_all, cap, batch, nb)
    base_s = base[:, :, 0]
    n_blk = cap_p // CHUNK + 2
    lists = _route_compact(base_s, rank, aff_all, n_blk)
    lists = lists.transpose(0, 2, 1, 3).reshape(N_EXPERTS, LIST_ROWS, n_blk * CHUNK)[:, :, :cap_p]
    idx = (lists[:, 0] * float(1 << TOK_DIGIT_BITS) + lists[:, 1]).astype(I32)
    gate = lists[:, 2] + lists[:, 3] + lists[:, 4]
    ns = -(-cap_p // FFN_MAX_ROWS)
    tm = cap_p // ns
    ye = _expert_ffn(idx.reshape(N_EXPERTS * ns, 1, tm), gate.reshape(N_EXPERTS, cap_p, 1), xp,
                     p["wgate"], p["wup"], p["wdown"], cap_p, tm, FFN_TF)
    rank_t = rank[:t // CHUNK].transpose(0, 2, 1)
    y = _combine(base_s, x1, rank_t, ye, p["g2"], p["b2"], cap_p)
    return y.reshape(batch, seq, D_MODEL)


def kernel(x_prompt, x_sample, meta_tokens, ln0_g, ln0_b, rel_bias, attn_sink, w_in, b_gate, ml_norm_g,
           w_branch_a, w_branch_b, w_out, ln1_g, ln1_b, w_router, w_gate, w_up, w_down, ln2_g, ln2_b):
    assert w_in.shape[0] == DEPTH
    w = w_in[0]
    sizes = (ML_HEADS * ML_DQK, ML_HEADS * ML_DQK, ML_HEADS * ML_DV, ML_HEADS * ML_DV, 4 * ML_HEADS,
             AT_HEADS * AT_DH, AT_KV_HEADS * AT_DH, AT_KV_HEADS * AT_DH, D_MODEL, D_MODEL)
    offs = np.concatenate([[0], np.cumsum(sizes)])
    mq, mk, mv, mo, mg, aq, ak, av, ga, gb = [w[:, offs[i]:offs[i + 1]] for i in range(10)]
    w_g = jnp.pad(mg, ((0, 0), (0, LANES - 4 * ML_HEADS))).astype(BF16)
    bg = jnp.pad(b_gate[0].astype(F32), (0, LANES - 4 * ML_HEADS))
    nb_max = max(x_prompt.shape[1], x_sample.shape[1]) // CHUNK
    bk_main, bk_meta0, bk_mq = _bucket_tables(nb_max)
    rb = rel_bias.astype(F32)
    p = {
        "g0": _row(ln0_g), "b0": _row(ln0_b), "g1": _row(ln1_g[0]), "b1": _row(ln1_b[0]),
        "g2": _row(ln2_g[0]), "b2": _row(ln2_b[0]), "ng": _row(ml_norm_g[0]),
        "w_main": jnp.concatenate([mv, mo, aq, ga, gb, mq, ak, av], axis=1).astype(BF16),
        "w_ktg": jnp.concatenate([mk.T.astype(BF16), w_g.T], axis=0), "w_g": w_g,
        "bg": bg.reshape(1, LANES), "bgt": bg.reshape(LANES, 1),
        "wa": w_branch_a[0].astype(BF16), "wb": w_branch_b[0].astype(BF16), "wo": w_out[0].astype(BF16),
        "wrt": w_router[0].T.astype(BF16),
        "wgate": w_gate[0], "wup": w_up[0], "wdown": w_down[0],
        "bias3": _attn_bias_variants(_bias_table(rb, bk_main), _bias_table(rb, bk_meta0)),
        "sink_col": jnp.broadcast_to(
            jnp.tile(jnp.repeat(attn_sink[0].astype(F32).reshape(AT_KV_HEADS, AT_GROUP), CHUNK // 2, axis=1),
                     (1, 2))[:, :, None] * LOG2E, (AT_KV_HEADS, AT_GROUP * CHUNK, LANES)),
        "bias_mq": _bias_table(rb, bk_mq),
        "sink_b": jnp.broadcast_to(attn_sink[0].astype(F32)[:, None], (AT_HEADS, LANES)),
        "meta_tokens": meta_tokens.astype(F32),
    }
    meta = _ln_proj(p["meta_tokens"], p["g0"], p["b0"], p["w_main"], p["w_ktg"], p["w_g"],
                    p["bg"], p["bgt"], N_META)
    return (_trunk(x_prompt, p, meta), _trunk(x_sample, p, meta))
```
